```python
import jax, jax.numpy as jnp
from jax import lax
import numpy as np

D_MODEL = 1024
BATCH = 8
SEQ = 2048
DEPTH = 1
DEC_BATCH = 128
DEC_SEQ = 1
PAST_LEN = 16384
PAGE_SIZE = 128

MIX_WIDTH = D_MODEL
GM_WIDTH = MIX_WIDTH // 2
CV_WIDTH = MIX_WIDTH - GM_WIDTH
GM_HEADS = 4
GM_HEAD_DIM = GM_WIDTH // GM_HEADS
CHUNK = 128
CV_GROUPS = 4
CONV_WIDTH = 31
IN_COLS = 2 * GM_WIDTH + 2 * CV_WIDTH
N_MEM = 256
XA_HEADS = 4
XA_HEAD_DIM = D_MODEL // XA_HEADS
N_GROUPS = 4
EXPERTS_PER_GROUP = 8
TOP_K_IN_GROUP = 2
EXPERT_FF = D_MODEL // 4
EPS = 1e-6

kernel_name = "hymba_gmlp_conformer_memxattn_hmoe_step"


def rms_norm(x, g):
    xf = x.astype(jnp.float32)
    y = xf * lax.rsqrt(jnp.mean(xf * xf, axis=-1, keepdims=True) + EPS)
    return (y * g.astype(jnp.float32)).astype(x.dtype)


def layer_norm(x, g, b):
    xf = x.astype(jnp.float32)
    mu = jnp.mean(xf, axis=-1, keepdims=True)
    var = jnp.mean(jnp.square(xf - mu), axis=-1, keepdims=True)
    y = (xf - mu) * lax.rsqrt(var + EPS)
    return (y * g.astype(jnp.float32) + b.astype(jnp.float32)).astype(x.dtype)


def chunk_spatial_mix(v, ws, bs):
    n, t, h, p = v.shape
    n_chunks = -(-t // CHUNK)
    pad = n_chunks * CHUNK - t
    vc = jnp.pad(v, ((0, 0), (0, pad), (0, 0), (0, 0))).reshape(n, n_chunks, CHUNK, h, p)
    mask = jnp.tril(jnp.ones((CHUNK, CHUNK), dtype=bool))
    w = jnp.where(mask[None], ws, 0).astype(v.dtype)
    mixed = jnp.einsum('hts,ncshp->ncthp', w, vc) + bs.T.astype(v.dtype)[None, None, :, :, None]
    return mixed.reshape(n, n_chunks * CHUNK, h, p)[:, :t]


def mixer_block(x, conv_past, norm_g, w_in, gm_ln_g, gm_ln_b, gm_ws, gm_bs,
                conv_w, conv_b, cv_ln_g, cv_ln_b, w_out):
    n, t, _ = x.shape
    z = rms_norm(x, norm_g) @ w_in
    gm_u, gm_v, cv_a, cv_b = jnp.split(z, [GM_WIDTH, 2 * GM_WIDTH, 2 * GM_WIDTH + CV_WIDTH], axis=-1)
    u = jax.nn.gelu(gm_u).reshape(n, t, GM_HEADS, GM_HEAD_DIM)
    v = layer_norm(jax.nn.gelu(gm_v).reshape(n, t, GM_HEADS, GM_HEAD_DIM), gm_ln_g, gm_ln_b)
    a_out = (u * chunk_spatial_mix(v, gm_ws, gm_bs)).reshape(n, t, GM_WIDTH)
    glu = cv_a * jax.nn.sigmoid(cv_b)
    full = jnp.concatenate([conv_past.astype(glu.dtype), glu], axis=1)
    conv = lax.conv_general_dilated(
        full, conv_w.astype(full.dtype)[:, None, :], window_strides=(1,), padding='VALID',
        dimension_numbers=('NWC', 'WIO', 'NWC'), feature_group_count=CV_WIDTH) + conv_b.astype(full.dtype)
    b_out = jax.nn.silu(layer_norm(conv, cv_ln_g, cv_ln_b))
    y = x + jnp.concatenate([a_out, b_out], axis=-1) @ w_out
    start = ((t - 1) // CHUNK) * CHUNK
    v_rows = v[:, start:].reshape(n, t - start, GM_WIDTH)
    return y, full[:, -(CONV_WIDTH - 1):], v_rows


def memory_kv(mem, norm_g, wk, wv):
    n = mem.shape[0]
    mn = rms_norm(mem, norm_g)
    k = (mn @ wk).reshape(n, N_MEM, XA_HEADS, XA_HEAD_DIM)
    v = (mn @ wv).reshape(n, N_MEM, XA_HEADS, XA_HEAD_DIM)
    return k, v


def memory_cross_attention(x, mem_k, mem_v, norm_g, wq, wo):
    n, t, _ = x.shape
    q = (rms_norm(x, norm_g) @ wq).reshape(n, t, XA_HEADS, XA_HEAD_DIM)
    s = jnp.einsum('nthd,nmhd->nhtm', q, mem_k.astype(q.dtype)).astype(jnp.float32) * (XA_HEAD_DIM ** -0.5)
    p = jax.nn.softmax(s, axis=-1).astype(x.dtype)
    o = jnp.einsum('nhtm,nmhd->nthd', p, mem_v.astype(x.dtype)).reshape(n, t, D_MODEL)
    return x + o @ wo


def hier_moe(x, norm_g, router_g, router_g_b, router_e, router_e_b, exp_w1, exp_w3, exp_w2):
    n, t, _ = x.shape
    xt = rms_norm(x, norm_g).reshape(n * t, D_MODEL)
    g_logits = (xt @ router_g).astype(jnp.float32) + router_g_b.astype(jnp.float32)
    g_prob = jax.nn.softmax(g_logits, axis=-1)
    g_idx = jnp.argmax(g_logits, axis=-1)
    p_g = jnp.take_along_axis(g_prob, g_idx[:, None], axis=-1)[:, 0]
    e_logits = jnp.einsum('td,dge->tge', xt, router_e).astype(jnp.float32) + router_e_b.astype(jnp.float32)
    e_sel = jnp.take_along_axis(e_logits, g_idx[:, None, None], axis=1)[:, 0]
    top_v, top_i = lax.top_k(e_sel, TOP_K_IN_GROUP)
    top_w = jax.nn.softmax(top_v, axis=-1) * p_g[:, None]
    within = jnp.sum(jax.nn.one_hot(top_i, EXPERTS_PER_GROUP, dtype=jnp.float32) * top_w[..., None], axis=1)
    comb = (jax.nn.one_hot(g_idx, N_GROUPS, dtype=jnp.float32)[:, :, None] * within[:, None, :]).astype(xt.dtype)
    out = jnp.zeros_like(xt)
    for g in range(N_GROUPS):
        h1 = jnp.einsum('td,edf->tef', xt, exp_w1[g])
        h3 = jnp.einsum('td,edf->tef', xt, exp_w3[g])
        act = jax.nn.silu(h1) * h3 * comb[:, g, :, None]
        out = out + jnp.einsum('tef,efd->td', act, exp_w2[g])
    return x + out.reshape(n, t, D_MODEL)


def setup_inputs(seed: int = 0) -> dict:
    key = jax.random.key(seed)
    k = jax.random.split(key, 40)
    f32 = jnp.float32

    def nrm(kk, shape, scale):
        return jax.random.normal(kk, shape, f32) * scale

    L = DEPTH
    H, P = GM_HEADS, GM_HEAD_DIM
    G, E, F = N_GROUPS, EXPERTS_PER_GROUP, EXPERT_FF
    return {
        "x_prompt": nrm(k[0], (BATCH, SEQ, D_MODEL), 1.0),
        "x_sample": nrm(k[1], (DEC_BATCH, DEC_SEQ, D_MODEL), 1.0),
        "mem_prompt": nrm(k[2], (BATCH, N_MEM, D_MODEL), 1.0),
        "cache_conv": nrm(k[3], (L, DEC_BATCH, CONV_WIDTH - 1, CV_WIDTH), 0.5),
        "cache_mem_k": nrm(k[4], (L, DEC_BATCH, N_MEM, XA_HEADS, XA_HEAD_DIM), 1.0),
        "cache_mem_v": nrm(k[5], (L, DEC_BATCH, N_MEM, XA_HEADS, XA_HEAD_DIM), 1.0),
        "norm_mix_g": 1.0 + nrm(k[6], (L, D_MODEL), 0.02),
        "w_in": nrm(k[7], (L, D_MODEL, IN_COLS), D_MODEL ** -0.5),
        "gm_ln_g": 1.0 + nrm(k[8], (L, H, P), 0.02),
        "gm_ln_b": nrm(k[9], (L, H, P), 0.02),
        "gm_ws": nrm(k[10], (L, H, CHUNK, CHUNK), CHUNK ** -0.5),
        "gm_bs": 1.0 + nrm(k[11], (L, H, CHUNK), 0.02),
        "conv_w": nrm(k[12], (L, CONV_WIDTH, CV_WIDTH), CONV_WIDTH ** -0.5),
        "conv_b": nrm(k[13], (L, CV_WIDTH), 0.02),
        "cv_ln_g": 1.0 + nrm(k[14], (L, CV_WIDTH), 0.02),
        "cv_ln_b": nrm(k[15], (L, CV_WIDTH), 0.02),
        "w_out": nrm(k[16], (L, MIX_WIDTH, D_MODEL), MIX_WIDTH ** -0.5),
        "norm_mem_g": 1.0 + nrm(k[17], (L, D_MODEL), 0.02),
        "norm_xa_g": 1.0 + nrm(k[18], (L, D_MODEL), 0.02),
        "xa_wq": nrm(k[19], (L, D_MODEL, D_MODEL), D_MODEL ** -0.5),
        "xa_wk": nrm(k[20], (L, D_MODEL, D_MODEL), D_MODEL ** -0.5),
        "xa_wv": nrm(k[21], (L, D_MODEL, D_MODEL), D_MODEL ** -0.5),
        "xa_wo": nrm(k[22], (L, D_MODEL, D_MODEL), D_MODEL ** -0.5),
        "norm_ffn_g": 1.0 + nrm(k[23], (L, D_MODEL), 0.02),
        "router_g": nrm(k[24], (L, D_MODEL, G), D_MODEL ** -0.5),
        "router_g_b": nrm(k[25], (L, G), 0.01),
        "router_e": nrm(k[26], (L, D_MODEL, G, E), D_MODEL ** -0.5),
        "router_e_b": nrm(k[27], (L, G, E), 0.01),
        "exp_w1": nrm(k[28], (L, G, E, D_MODEL, F), D_MODEL ** -0.5),
        "exp_w3": nrm(k[29], (L, G, E, D_MODEL, F), D_MODEL ** -0.5),
        "exp_w2": nrm(k[30], (L, G, E, F, D_MODEL), F ** -0.5),
        "final_norm_g": 1.0 + nrm(k[31], (D_MODEL,), 0.02),
    }


def reference(x_prompt, x_sample, mem_prompt, cache_conv, cache_mem_k, cache_mem_v,
              norm_mix_g, w_in, gm_ln_g, gm_ln_b, gm_ws, gm_bs, conv_w, conv_b, cv_ln_g, cv_ln_b, w_out,
              norm_mem_g, norm_xa_g, xa_wq, xa_wk, xa_wv, xa_wo,
              norm_ffn_g, router_g, router_g_b, router_e, router_e_b, exp_w1, exp_w3, exp_w2,
              final_norm_g):
    hp, hs = x_prompt, x_sample
    conv_p, conv_s, gmv_p, gmv_s, mk_p, mv_p = [], [], [], [], [], []
    for l in range(DEPTH):
        mk, mv = memory_kv(mem_prompt, norm_mem_g[l], xa_wk[l], xa_wv[l])
        zero_past = jnp.zeros((hp.shape[0], CONV_WIDTH - 1, CV_WIDTH), hp.dtype)
        hp, cp, vp = mixer_block(hp, zero_past, norm_mix_g[l], w_in[l], gm_ln_g[l], gm_ln_b[l], gm_ws[l], gm_bs[l],
                                 conv_w[l], conv_b[l], cv_ln_g[l], cv_ln_b[l], w_out[l])
        hs, cs, vs = mixer_block(hs, cache_conv[l], norm_mix_g[l], w_in[l], gm_ln_g[l], gm_ln_b[l], gm_ws[l], gm_bs[l],
                                 conv_w[l], conv_b[l], cv_ln_g[l], cv_ln_b[l], w_out[l])
        hp = memory_cross_attention(hp, mk, mv, norm_xa_g[l], xa_wq[l], xa_wo[l])
        hs = memory_cross_attention(hs, cache_mem_k[l], cache_mem_v[l], norm_xa_g[l], xa_wq[l], xa_wo[l])
        hp = hier_moe(hp, norm_ffn_g[l], router_g[l], router_g_b[l], router_e[l], router_e_b[l],
                      exp_w1[l], exp_w3[l], exp_w2[l])
        hs = hier_moe(hs, norm_ffn_g[l], router_g[l], router_g_b[l], router_e[l], router_e_b[l],
                      exp_w1[l], exp_w3[l], exp_w2[l])
        conv_p.append(cp)
        conv_s.append(cs)
        gmv_p.append(vp)
        gmv_s.append(vs)
        mk_p.append(mk)
        mv_p.append(mv)
    y_prompt = rms_norm(hp, final_norm_g)
    y_sample = rms_norm(hs, final_norm_g)
    return (y_prompt, y_sample, jnp.stack(conv_p), jnp.stack(conv_s), jnp.stack(gmv_p), jnp.stack(gmv_s),
            jnp.stack(mk_p), jnp.stack(mv_p))
```

```python
import functools

import jax
import jax.numpy as jnp
from jax import lax
from jax.experimental import pallas as pl
from jax.experimental.pallas import tpu as pltpu

F32 = jnp.float32
BF16 = jnp.bfloat16

D_MODEL = 1024
GM_WIDTH = 512
CV_WIDTH = 512
GM_HEADS = 4
GM_HEAD_DIM = 128
CHUNK = 128
CONV_WIDTH = 31
IN_COLS = 2 * GM_WIDTH + 2 * CV_WIDTH
N_MEM = 256
XA_HEADS = 4
XA_HEAD_DIM = 256
N_GROUPS = 4
EXPERTS_PER_GROUP = 8
EXPERT_FF = 256
EPS = 1e-6

LANES = 128
SUBLANES = 8
HALO = 32
PROMPT_TILE = 512
MOE_TILE = 1024
MOE_SUB = 128
SAMPLE_ATTN_BLOCK = 8
ROUTER_ROWS = 128
VMEM_LIMIT = 56 * 1024 * 1024


def _rms(x, g):
    return x * lax.rsqrt(jnp.mean(x * x, axis=-1, keepdims=True) + EPS) * g


def _ln(x, g, b):
    mu = jnp.mean(x, axis=-1, keepdims=True)
    xc = x - mu
    var = jnp.mean(xc * xc, axis=-1, keepdims=True)
    return xc * lax.rsqrt(var + EPS) * g + b


def _dot(a, b):
    return jnp.dot(a, b, preferred_element_type=F32)


def _dot_nt(a, b):
    return lax.dot_general(a, b, (((1,), (1,)), ((), ())), preferred_element_type=F32)


def _dot_tn(a, b):
    return lax.dot_general(a, b, (((0,), (0,)), ((), ())), preferred_element_type=F32)


def _memkv_kernel(mem_ref, g_ref, wk_ref, wv_ref, k_ref, v_ref, kb_ref, vb_ref):
    mn = _rms(mem_ref[...], g_ref[...]).astype(BF16)
    k = _dot(mn, wk_ref[...])
    v = _dot(mn, wv_ref[...])
    k_ref[...] = k
    v_ref[...] = v
    kb_ref[...] = k.astype(BF16)
    vb_ref[...] = v.astype(BF16)


def _memkv(mem2d, g, wk, wv):
    rows = mem2d.shape[0]
    tile = 512
    row_spec = pl.BlockSpec((tile, D_MODEL), lambda i: (i, 0))
    full = lambda shape: pl.BlockSpec(shape, lambda i: (0,) * len(shape))
    return pl.pallas_call(
        _memkv_kernel,
        grid=(rows // tile,),
        in_specs=[row_spec, full((1, D_MODEL)), full((D_MODEL, D_MODEL)), full((D_MODEL, D_MODEL))],
        out_specs=[row_spec, row_spec, row_spec, row_spec],
        out_shape=[jax.ShapeDtypeStruct((rows, D_MODEL), F32), jax.ShapeDtypeStruct((rows, D_MODEL), F32),
                   jax.ShapeDtypeStruct((rows, D_MODEL), BF16), jax.ShapeDtypeStruct((rows, D_MODEL), BF16)],
        compiler_params=pltpu.CompilerParams(dimension_semantics=("arbitrary",), vmem_limit_bytes=VMEM_LIMIT),
        name="memkv",
    )(mem2d, g, wk, wv)


def _prompt_kernel(x_ref, gmix_ref, win_ref, glng_ref, glnb_ref, ws_ref, bst_ref, cw_ref, cb_ref,
                   clng_ref, clnb_ref, wout_ref, gxa_ref, wq_ref, kb_ref, vb_ref, wo_ref,
                   h_ref, conv_ref, gmv_ref, z_s, g_s, c_s, ab_s):
    t = pl.program_id(1)
    tb = x_ref.shape[0]
    n_chunks = tb // CHUNK

    @pl.when(t == 0)
    def _():
        g_s[0:HALO, :] = jnp.zeros((HALO, CV_WIDTH), F32)

    xn = _rms(x_ref[...], gmix_ref[...]).astype(BF16)
    z_s[...] = _dot(xn, win_ref[...])

    tri = (lax.broadcasted_iota(jnp.int32, (CHUNK, CHUNK), 0)
           >= lax.broadcasted_iota(jnp.int32, (CHUNK, CHUNK), 1))
    wm = [jnp.where(tri, ws_ref[h], 0.0).astype(BF16) for h in range(GM_HEADS)]

    for c in range(n_chunks):
        rows = slice(c * CHUNK, (c + 1) * CHUNK)
        for h in range(GM_HEADS):
            cu = slice(h * GM_HEAD_DIM, (h + 1) * GM_HEAD_DIM)
            cv = slice(GM_WIDTH + h * GM_HEAD_DIM, GM_WIDTH + (h + 1) * GM_HEAD_DIM)
            v = _ln(jax.nn.gelu(z_s[rows, cv]), glng_ref[:, cu], glnb_ref[:, cu])
            if c == n_chunks - 1:
                gmv_ref[:, cu] = v
            mixed = _dot(wm[h], v.astype(BF16)) + bst_ref[:, h:h + 1]
            ab_s[rows, cu] = (jax.nn.gelu(z_s[rows, cu]) * mixed).astype(BF16)
        ca = slice(2 * GM_WIDTH, 2 * GM_WIDTH + CV_WIDTH)
        cg = slice(2 * GM_WIDTH + CV_WIDTH, IN_COLS)
        g_s[HALO + c * CHUNK:HALO + (c + 1) * CHUNK, :] = z_s[rows, ca] * jax.nn.sigmoid(z_s[rows, cg])

    off = HALO - (CONV_WIDTH - 1)
    for rb in range(n_chunks):
        for cb in range(CV_WIDTH // LANES):
            lanes = slice(cb * LANES, (cb + 1) * LANES)
            acc = jnp.zeros((CHUNK, LANES), F32)
            for k in range(CONV_WIDTH):
                acc = acc + g_s[rb * CHUNK + off + k:rb * CHUNK + off + k + CHUNK, lanes] * cw_ref[k:k + 1, lanes]
            c_s[rb * CHUNK:(rb + 1) * CHUNK, lanes] = acc + cb_ref[:, lanes]

    conv_ref[...] = g_s[tb:tb + HALO, :]
    g_s[0:HALO, :] = g_s[tb:tb + HALO, :]

    for rb in range(n_chunks):
        rows = slice(rb * CHUNK, (rb + 1) * CHUNK)
        b = _ln(c_s[rows, :], clng_ref[...], clnb_ref[...])
        ab_s[rows, GM_WIDTH:] = (b * jax.nn.sigmoid(b)).astype(BF16)

    h_ref[...] = x_ref[...] + _dot(ab_s[...], wout_ref[...])

    qn = _rms(h_ref[...], gxa_ref[...]).astype(BF16)
    q = (_dot(qn, wq_ref[...]) * (XA_HEAD_DIM ** -0.5)).astype(BF16)
    for h in range(XA_HEADS):
        cols = slice(h * XA_HEAD_DIM, (h + 1) * XA_HEAD_DIM)
        s = _dot_nt(q[:, cols], kb_ref[:, cols])
        e = jnp.exp(s - jnp.max(s, axis=-1, keepdims=True))
        p = (e / jnp.sum(e, axis=-1, keepdims=True)).astype(BF16)
        ab_s[:, cols] = _dot(p, vb_ref[:, cols]).astype(BF16)
    h_ref[...] = h_ref[...] + _dot(ab_s[...], wo_ref[...])


def _prompt_mix_attn(x, gmix, win, glng, glnb, ws, bst, cw, cb, clng, clnb, wout, gxa, wq, kb, vb, wo):
    nb, seq, _ = x.shape
    tb = PROMPT_TILE
    full = lambda shape: pl.BlockSpec(shape, lambda n, t: (0,) * len(shape))
    row_spec = pl.BlockSpec((None, tb, D_MODEL), lambda n, t: (n, t, 0))
    per_seq = lambda r, c: pl.BlockSpec((None, r, c), lambda n, t: (n, 0, 0))
    return pl.pallas_call(
        _prompt_kernel,
        grid=(nb, seq // tb),
        in_specs=[row_spec, full((1, D_MODEL)), full((D_MODEL, IN_COLS)), full((1, GM_WIDTH)), full((1, GM_WIDTH)),
                  full((GM_HEADS, CHUNK, CHUNK)), full((CHUNK, GM_HEADS)), full((CONV_WIDTH, CV_WIDTH)),
                  full((1, CV_WIDTH)), full((1, CV_WIDTH)), full((1, CV_WIDTH)), full((D_MODEL, D_MODEL)),
                  full((1, D_MODEL)), full((D_MODEL, D_MODEL)), per_seq(N_MEM, D_MODEL), per_seq(N_MEM, D_MODEL),
                  full((D_MODEL, D_MODEL))],
        out_specs=[row_spec, per_seq(HALO, CV_WIDTH), per_seq(CHUNK, GM_WIDTH)],
        out_shape=[jax.ShapeDtypeStruct((nb, seq, D_MODEL), F32),
                   jax.ShapeDtypeStruct((nb, HALO, CV_WIDTH), F32),
                   jax.ShapeDtypeStruct((nb, CHUNK, GM_WIDTH), F32)],
        scratch_shapes=[pltpu.VMEM((tb, IN_COLS), F32), pltpu.VMEM((HALO + tb, CV_WIDTH), F32),
                        pltpu.VMEM((tb, CV_WIDTH), F32), pltpu.VMEM((tb, D_MODEL), BF16)],
        compiler_params=pltpu.CompilerParams(dimension_semantics=("arbitrary", "arbitrary"),
                                             vmem_limit_bytes=VMEM_LIMIT),
        name="prompt_mix_attn",
    )(x, gmix, win, glng, glnb, ws, bst, cw, cb, clng, clnb, wout, gxa, wq, kb, vb, wo)


def _sample_mix_kernel(x_ref, gmix_ref, win_ref, glng_ref, glnb_ref, ws0_ref, bs0_ref, cache_ref, cw_ref, cb_ref,
                       clng_ref, clnb_ref, wout_ref, gxa_ref, wq_ref,
                       y_ref, glu_ref, v_ref, q_ref, ab_s):
    x = x_ref[...]
    z = _dot(_rms(x, gmix_ref[...]).astype(BF16), win_ref[...])
    for h in range(GM_HEADS):
        cu = slice(h * GM_HEAD_DIM, (h + 1) * GM_HEAD_DIM)
        cv = slice(GM_WIDTH + h * GM_HEAD_DIM, GM_WIDTH + (h + 1) * GM_HEAD_DIM)
        v = _ln(jax.nn.gelu(z[:, cv]), glng_ref[:, cu], glnb_ref[:, cu])
        v_ref[:, cu] = v
        ab_s[:, cu] = (jax.nn.gelu(z[:, cu]) * (v * ws0_ref[:, cu] + bs0_ref[:, cu])).astype(BF16)
    glu = z[:, 2 * GM_WIDTH:2 * GM_WIDTH + CV_WIDTH] * jax.nn.sigmoid(z[:, 2 * GM_WIDTH + CV_WIDTH:])
    glu_ref[...] = glu
    conv = glu * cw_ref[CONV_WIDTH - 1:CONV_WIDTH, :] + cb_ref[...]
    for k in range(CONV_WIDTH - 1):
        conv = conv + cache_ref[:, k * CV_WIDTH:(k + 1) * CV_WIDTH] * cw_ref[k:k + 1, :]
    b = _ln(conv, clng_ref[...], clnb_ref[...])
    ab_s[:, GM_WIDTH:] = (b * jax.nn.sigmoid(b)).astype(BF16)
    y = x + _dot(ab_s[...], wout_ref[...])
    y_ref[...] = y
    q_ref[...] = _dot(_rms(y, gxa_ref[...]).astype(BF16), wq_ref[...]) * (XA_HEAD_DIM ** -0.5)


def _sample_mix(x, gmix, win, glng, glnb, ws0, bs0, cache2d, cw, cb, clng, clnb, wout, gxa, wq):
    ns = x.shape[0]
    return pl.pallas_call(
        _sample_mix_kernel,
        out_shape=[jax.ShapeDtypeStruct((ns, D_MODEL), F32), jax.ShapeDtypeStruct((ns, CV_WIDTH), F32),
                   jax.ShapeDtypeStruct((ns, GM_WIDTH), F32), jax.ShapeDtypeStruct((ns, D_MODEL), F32)],
        scratch_shapes=[pltpu.VMEM((ns, D_MODEL), BF16)],
        compiler_params=pltpu.CompilerParams(vmem_limit_bytes=VMEM_LIMIT),
        name="sample_mix",
    )(x, gmix, win, glng, glnb, ws0, bs0, cache2d, cw, cb, clng, clnb, wout, gxa, wq)


def _sample_attn_kernel(q_ref, k_ref, v_ref, o_ref):
    nb = q_ref.shape[0]
    row = lax.broadcasted_iota(jnp.int32, (SUBLANES, D_MODEL), 0)
    col = lax.broadcasted_iota(jnp.int32, (SUBLANES, D_MODEL), 1)
    head_mask = (col // XA_HEAD_DIM == row).astype(F32)
    for i in range(nb):
        qrows = (q_ref[i:i + 1, :] * head_mask).astype(BF16)
        s = _dot_nt(qrows, k_ref[i].astype(BF16))
        e = jnp.exp(s - jnp.max(s, axis=-1, keepdims=True))
        p = (e / jnp.sum(e, axis=-1, keepdims=True)).astype(BF16)
        o8 = _dot(p, v_ref[i].astype(BF16)) * head_mask
        o_ref[i:i + 1, :] = jnp.sum(o8, axis=0, keepdims=True)


def _sample_attn(q, k, v):
    ns = q.shape[0]
    nb = SAMPLE_ATTN_BLOCK
    return pl.pallas_call(
        _sample_attn_kernel,
        grid=(ns // nb,),
        in_specs=[pl.BlockSpec((nb, D_MODEL), lambda i: (i, 0)),
                  pl.BlockSpec((nb, N_MEM, D_MODEL), lambda i: (i, 0, 0)),
                  pl.BlockSpec((nb, N_MEM, D_MODEL), lambda i: (i, 0, 0))],
        out_specs=pl.BlockSpec((nb, D_MODEL), lambda i: (i, 0)),
        out_shape=jax.ShapeDtypeStruct((ns, D_MODEL), F32),
        compiler_params=pltpu.CompilerParams(dimension_semantics=("arbitrary",), vmem_limit_bytes=VMEM_LIMIT),
        name="sample_attn",
    )(q, k, v)


def _sample_proj_kernel(y_ref, o_ref, wo_ref, h_ref):
    h_ref[...] = y_ref[...] + _dot(o_ref[...].astype(BF16), wo_ref[...])


def _sample_proj(y, o, wo):
    return pl.pallas_call(
        _sample_proj_kernel,
        out_shape=jax.ShapeDtypeStruct(y.shape, F32),
        compiler_params=pltpu.CompilerParams(vmem_limit_bytes=VMEM_LIMIT),
        name="sample_proj",
    )(y, o, wo)


def _moe_kernel(h_ref, gffn_ref, wrt_ref, rb_ref, w1_ref, w3_ref, w2_ref, gfin_ref,
                o_ref, xt_s, u_s, gidx_s, slot_s, comb_s, act_s, cnt_s):
    i = pl.program_id(0)
    g = pl.program_id(1)
    rt = h_ref.shape[0]
    sb = act_s.shape[0]

    @pl.when(jnp.logical_and(i == 0, g == 0))
    def _():
        u_s[...] = (lax.broadcasted_iota(jnp.int32, (rt, rt), 0)
                    < lax.broadcasted_iota(jnp.int32, (rt, rt), 1)).astype(F32).astype(BF16)

    @pl.when(g == 0)
    def _route():
        h = h_ref[...]
        o_ref[...] = h
        xt = _rms(h, gffn_ref[...]).astype(BF16)
        xt_s[...] = xt
        lt = _dot_nt(wrt_ref[...], xt) + rb_ref[...]
        gl = [lt[k:k + 1, :] for k in range(N_GROUPS)]
        gmax = jnp.maximum(jnp.maximum(gl[0], gl[1]), jnp.maximum(gl[2], gl[3]))
        gidx = jnp.where(gl[0] == gmax, 0, jnp.where(gl[1] == gmax, 1, jnp.where(gl[2] == gmax, 2, 3)))
        gidx = gidx.astype(jnp.int32)
        sumexp = (jnp.exp(gl[0] - gmax) + jnp.exp(gl[1] - gmax)) + (jnp.exp(gl[2] - gmax) + jnp.exp(gl[3] - gmax))
        p_g = 1.0 / sumexp
        esel = lt[SUBLANES + 3 * EXPERTS_PER_GROUP:SUBLANES + 4 * EXPERTS_PER_GROUP, :]
        for k in (2, 1, 0):
            esel = jnp.where(gidx == k, lt[SUBLANES + k * EXPERTS_PER_GROUP:SUBLANES + (k + 1) * EXPERTS_PER_GROUP, :],
                             esel)
        eidx = lax.broadcasted_iota(jnp.int32, (EXPERTS_PER_GROUP, rt), 0)
        m1 = jnp.max(esel, axis=0, keepdims=True)
        i1 = jnp.min(jnp.where(esel == m1, eidx, EXPERTS_PER_GROUP), axis=0, keepdims=True)
        rest = jnp.where(eidx == i1, -jnp.inf, esel)
        m2 = jnp.max(rest, axis=0, keepdims=True)
        i2 = jnp.min(jnp.where(rest == m2, eidx, EXPERTS_PER_GROUP), axis=0, keepdims=True)
        t2 = jnp.exp(m2 - m1)
        den = 1.0 + t2
        w_top1 = (1.0 / den) * p_g
        w_top2 = (t2 / den) * p_g
        within = jnp.where(eidx == i1, w_top1, 0.0) + jnp.where(eidx == i2, w_top2, 0.0)
        c_hi = within.astype(BF16).astype(F32)
        r1 = within - c_hi
        c_mid = r1.astype(BF16).astype(F32)
        c_lo = (r1 - c_mid).astype(BF16).astype(F32)
        comb_s[0:8, :] = c_hi
        comb_s[8:16, :] = c_mid
        comb_s[16:24, :] = c_lo
        comb_s[24:32, :] = jnp.zeros((8, rt), F32)
        onehot = (eidx == gidx).astype(F32)
        rank = _dot(onehot.astype(BF16), u_s[...])
        slot_s[...] = jnp.sum(onehot * rank, axis=0, keepdims=True).astype(jnp.int32)
        gidx_s[...] = gidx
        for k in range(N_GROUPS):
            cnt_s[k] = jnp.sum(onehot[k:k + 1, :]).astype(jnp.int32)

    n_blk = (cnt_s[g] + sb - 1) // sb

    def body(j, carry):
        rows = lax.broadcasted_iota(jnp.int32, (sb, rt), 0) + j * sb
        hit = jnp.logical_and(rows == slot_s[...], gidx_s[...] == g)
        p = jnp.where(hit, 1.0, 0.0).astype(BF16)
        xc = _dot(p, xt_s[...]).astype(BF16)
        cexp = _dot_nt(p, comb_s[...].astype(BF16))
        cw = (cexp[:, 0:8] + cexp[:, 8:16]) + cexp[:, 16:24]
        for e in range(EXPERTS_PER_GROUP):
            h1 = _dot(xc, w1_ref[e])
            h3 = _dot(xc, w3_ref[e])
            a = (h1 * jax.nn.sigmoid(h1)) * h3 * cw[:, e:e + 1]
            act_s[:, e * EXPERT_FF:(e + 1) * EXPERT_FF] = a.astype(BF16)
        y = _dot(act_s[...], w2_ref[...])
        o_ref[...] += _dot_tn(p, y.astype(BF16))
        return carry

    lax.fori_loop(0, n_blk, body, 0)

    @pl.when(g == N_GROUPS - 1)
    def _():
        o_ref[...] = _rms(o_ref[...], gfin_ref[...])


def _moe(h, gffn, wrt, rbias, w1, w3, w2, gfin, *, tile):
    tokens = h.shape[0]
    sb = min(MOE_SUB, tile)
    ff = EXPERTS_PER_GROUP * EXPERT_FF
    full = lambda shape: pl.BlockSpec(shape, lambda i, g: (0,) * len(shape))
    row_spec = pl.BlockSpec((tile, D_MODEL), lambda i, g: (i, 0))
    return pl.pallas_call(
        _moe_kernel,
        grid=(tokens // tile, N_GROUPS),
        in_specs=[row_spec, full((1, D_MODEL)), full((ROUTER_ROWS, D_MODEL)), full((ROUTER_ROWS, 1)),
                  pl.BlockSpec((None, EXPERTS_PER_GROUP, D_MODEL, EXPERT_FF), lambda i, g: (g, 0, 0, 0)),
                  pl.BlockSpec((None, EXPERTS_PER_GROUP, D_MODEL, EXPERT_FF), lambda i, g: (g, 0, 0, 0)),
                  pl.BlockSpec((None, ff, D_MODEL), lambda i, g: (g, 0, 0)),
                  full((1, D_MODEL))],
        out_specs=row_spec,
        out_shape=jax.ShapeDtypeStruct((tokens, D_MODEL), F32),
        scratch_shapes=[pltpu.VMEM((tile, D_MODEL), BF16), pltpu.VMEM((tile, tile), BF16),
                        pltpu.VMEM((1, tile), jnp.int32), pltpu.VMEM((1, tile), jnp.int32),
                        pltpu.VMEM((4 * SUBLANES, tile), F32), pltpu.VMEM((sb, ff), BF16),
                        pltpu.SMEM((N_GROUPS,), jnp.int32)],
        compiler_params=pltpu.CompilerParams(dimension_semantics=("arbitrary", "arbitrary"),
                                             vmem_limit_bytes=VMEM_LIMIT),
        name="moe",
    )(h, gffn, wrt, rbias, w1, w3, w2, gfin)


def kernel(x_prompt, x_sample, mem_prompt, cache_conv, cache_mem_k, cache_mem_v, norm_mix_g, w_in, gm_ln_g, gm_ln_b, gm_ws, gm_bs, conv_w, conv_b, cv_ln_g, cv_ln_b, w_out, norm_mem_g, norm_xa_g, xa_wq, xa_wk, xa_wv, xa_wo, norm_ffn_g, router_g, router_g_b, router_e, router_e_b, exp_w1, exp_w3, exp_w2, final_norm_g):
    depth = w_in.shape[0]
    assert depth == 1, "single-layer trunk"
    nb, seq, _ = x_prompt.shape
    ns = x_sample.shape[0]
    row = lambda a: a.reshape(1, -1)

    l = 0
    gmix, gxa, gffn, gmem = row(norm_mix_g[l]), row(norm_xa_g[l]), row(norm_ffn_g[l]), row(norm_mem_g[l])
    gfin = row(final_norm_g)
    win, wout = w_in[l].astype(BF16), w_out[l].astype(BF16)
    wq, wk, wv, wo = (w[l].astype(BF16) for w in (xa_wq, xa_wk, xa_wv, xa_wo))
    glng, glnb = row(gm_ln_g[l]), row(gm_ln_b[l])
    cw, cb, clng, clnb = conv_w[l], row(conv_b[l]), row(cv_ln_g[l]), row(cv_ln_b[l])
    ws, bst = gm_ws[l], gm_bs[l].T
    ws0 = jnp.repeat(gm_ws[l][:, 0, 0], GM_HEAD_DIM).reshape(1, GM_WIDTH)
    bs0 = jnp.repeat(gm_bs[l][:, 0], GM_HEAD_DIM).reshape(1, GM_WIDTH)

    n_exp = N_GROUPS * EXPERTS_PER_GROUP
    pad_g = SUBLANES - N_GROUPS
    pad_t = ROUTER_ROWS - SUBLANES - n_exp
    wrt = jnp.concatenate([router_g[l].T, jnp.zeros((pad_g, D_MODEL), F32),
                           router_e[l].reshape(D_MODEL, n_exp).T, jnp.zeros((pad_t, D_MODEL), F32)], axis=0).astype(BF16)
    rbias = jnp.concatenate([router_g_b[l], jnp.zeros((pad_g,), F32), router_e_b[l].reshape(n_exp),
                             jnp.zeros((pad_t,), F32)]).reshape(ROUTER_ROWS, 1)
    w1, w3 = exp_w1[l].astype(BF16), exp_w3[l].astype(BF16)
    w2 = exp_w2[l].astype(BF16).reshape(N_GROUPS, EXPERTS_PER_GROUP * EXPERT_FF, D_MODEL)

    mk, mv, kb, vb = _memkv(mem_prompt.reshape(nb * N_MEM, D_MODEL), gmem, wk, wv)
    kb, vb = kb.reshape(nb, N_MEM, D_MODEL), vb.reshape(nb, N_MEM, D_MODEL)
    hp, conv_tail, gmv_p = _prompt_mix_attn(x_prompt, gmix, win, glng, glnb, ws, bst, cw, cb, clng, clnb, wout,
                                            gxa, wq, kb, vb, wo)
    y_prompt = _moe(hp.reshape(nb * seq, D_MODEL), gffn, wrt, rbias, w1, w3, w2, gfin, tile=MOE_TILE)

    cache = cache_conv[l]
    ys, glu_s, gmv_s, q_s = _sample_mix(x_sample.reshape(ns, D_MODEL), gmix, win, glng, glnb, ws0, bs0,
                                        cache.reshape(ns, (CONV_WIDTH - 1) * CV_WIDTH), cw, cb, clng, clnb, wout,
                                        gxa, wq)
    o_s = _sample_attn(q_s, cache_mem_k[l].reshape(ns, N_MEM, D_MODEL), cache_mem_v[l].reshape(ns, N_MEM, D_MODEL))
    hs = _sample_proj(ys, o_s, wo)
    y_sample = _moe(hs, gffn, wrt, rbias, w1, w3, w2, gfin, tile=ns)

    conv_prompt = conv_tail[:, HALO - (CONV_WIDTH - 1):, :][None]
    conv_sample = jnp.concatenate([cache[:, 1:, :], glu_s[:, None, :]], axis=1)[None]
    return (y_prompt.reshape(nb, seq, D_MODEL), y_sample.reshape(ns, 1, D_MODEL), conv_prompt, conv_sample,
            gmv_p[None], gmv_s.reshape(1, ns, 1, GM_WIDTH),
            mk.reshape(1, nb, N_MEM, XA_HEADS, XA_HEAD_DIM), mv.reshape(1, nb, N_MEM, XA_HEADS, XA_HEAD_DIM))
```

```python
import functools

import jax
import jax.numpy as jnp
from jax import lax
from jax.experimental import pallas as pl
from jax.experimental.pallas import tpu as pltpu

F32 = jnp.float32
BF16 = jnp.bfloat16

D_MODEL = 1024
GM_WIDTH = 512
CV_WIDTH = 512
GM_HEADS = 4
GM_HEAD_DIM = 128
CHUNK = 128
CONV_WIDTH = 31
IN_COLS = 2 * GM_WIDTH + 2 * CV_WIDTH
N_MEM = 256
XA_HEADS = 4
XA_HEAD_DIM = 256
N_GROUPS = 4
EXPERTS_PER_GROUP = 8
EXPERT_FF = 256
EPS = 1e-6

LANES = 128
SUBLANES = 8
HALO = 32
PROMPT_TILE = 512
MOE_TILE = 1024
MOE_SUB = 128
SAMPLE_ATTN_BLOCK = 8
ROUTER_ROWS = 128
VMEM_LIMIT = 56 * 1024 * 1024


def _rms(x, g):
    return x * lax.rsqrt(jnp.mean(x * x, axis=-1, keepdims=True) + EPS) * g


def _ln(x, g, b):
    mu = jnp.mean(x, axis=-1, keepdims=True)
    xc = x - mu
    var = jnp.mean(xc * xc, axis=-1, keepdims=True)
    return xc * lax.rsqrt(var + EPS) * g + b


def _dot(a, b):
    return jnp.dot(a, b, preferred_element_type=F32)


def _dot_nt(a, b):
    return lax.dot_general(a, b, (((1,), (1,)), ((), ())), preferred_element_type=F32)


def _dot_tn(a, b):
    return lax.dot_general(a, b, (((0,), (0,)), ((), ())), preferred_element_type=F32)


def _memkv_kernel(mem_ref, g_ref, wk_ref, wv_ref, k_ref, v_ref, kb_ref, vb_ref):
    mn = _rms(mem_ref[...], g_ref[...]).astype(BF16)
    k = _dot(mn, wk_ref[...])
    v = _dot(mn, wv_ref[...])
    for h in range(XA_HEADS):
        cols = slice(h * XA_HEAD_DIM, (h + 1) * XA_HEAD_DIM)
        k_ref[:, h, :] = k[:, cols]
        v_ref[:, h, :] = v[:, cols]
    kb_ref[...] = k.astype(BF16)
    vb_ref[...] = v.astype(BF16)


def _memkv(mem2d, g, wk, wv):
    rows = mem2d.shape[0]
    tile = 512
    row_spec = pl.BlockSpec((tile, D_MODEL), lambda i: (i, 0))
    head_spec = pl.BlockSpec((tile, XA_HEADS, XA_HEAD_DIM), lambda i: (i, 0, 0))
    full = lambda shape: pl.BlockSpec(shape, lambda i: (0,) * len(shape))
    return pl.pallas_call(
        _memkv_kernel,
        grid=(rows // tile,),
        in_specs=[row_spec, full((1, D_MODEL)), full((D_MODEL, D_MODEL)), full((D_MODEL, D_MODEL))],
        out_specs=[head_spec, head_spec, row_spec, row_spec],
        out_shape=[jax.ShapeDtypeStruct((rows, XA_HEADS, XA_HEAD_DIM), F32),
                   jax.ShapeDtypeStruct((rows, XA_HEADS, XA_HEAD_DIM), F32),
                   jax.ShapeDtypeStruct((rows, D_MODEL), BF16), jax.ShapeDtypeStruct((rows, D_MODEL), BF16)],
        compiler_params=pltpu.CompilerParams(dimension_semantics=("arbitrary",), vmem_limit_bytes=VMEM_LIMIT),
        name="memkv",
    )(mem2d, g, wk, wv)


def _prompt_kernel(x_ref, gmix_ref, win_ref, glng_ref, glnb_ref, ws_ref, bst_ref, cw_ref, cb_ref,
                   clng_ref, clnb_ref, wout_ref, gxa_ref, wq_ref, kb_ref, vb_ref, wo_ref,
                   h_ref, conv_ref, gmv_ref, z_s, g_s, c_s, ab_s):
    t = pl.program_id(1)
    tb = x_ref.shape[0]
    n_chunks = tb // CHUNK

    @pl.when(t == 0)
    def _():
        g_s[0:HALO, :] = jnp.zeros((HALO, CV_WIDTH), F32)

    xn = _rms(x_ref[...], gmix_ref[...]).astype(BF16)
    z_s[...] = _dot(xn, win_ref[...])

    tri = (lax.broadcasted_iota(jnp.int32, (CHUNK, CHUNK), 0)
           >= lax.broadcasted_iota(jnp.int32, (CHUNK, CHUNK), 1))
    wm = [jnp.where(tri, ws_ref[h], 0.0).astype(BF16) for h in range(GM_HEADS)]

    for c in range(n_chunks):
        rows = slice(c * CHUNK, (c + 1) * CHUNK)
        for h in range(GM_HEADS):
            cu = slice(h * GM_HEAD_DIM, (h + 1) * GM_HEAD_DIM)
            cv = slice(GM_WIDTH + h * GM_HEAD_DIM, GM_WIDTH + (h + 1) * GM_HEAD_DIM)
            v = _ln(jax.nn.gelu(z_s[rows, cv]), glng_ref[:, cu], glnb_ref[:, cu])
            if c == n_chunks - 1:
                gmv_ref[:, cu] = v
            mixed = _dot(wm[h], v.astype(BF16)) + bst_ref[:, h:h + 1]
            ab_s[rows, cu] = (jax.nn.gelu(z_s[rows, cu]) * mixed).astype(BF16)
        ca = slice(2 * GM_WIDTH, 2 * GM_WIDTH + CV_WIDTH)
        cg = slice(2 * GM_WIDTH + CV_WIDTH, IN_COLS)
        g_s[HALO + c * CHUNK:HALO + (c + 1) * CHUNK, :] = z_s[rows, ca] * jax.nn.sigmoid(z_s[rows, cg])

    off = HALO - (CONV_WIDTH - 1)
    for rb in range(n_chunks):
        for cb in range(CV_WIDTH // LANES):
            lanes = slice(cb * LANES, (cb + 1) * LANES)
            acc = jnp.zeros((CHUNK, LANES), F32)
            for k in range(CONV_WIDTH):
                acc = acc + g_s[rb * CHUNK + off + k:rb * CHUNK + off + k + CHUNK, lanes] * cw_ref[k:k + 1, lanes]
            c_s[rb * CHUNK:(rb + 1) * CHUNK, lanes] = acc + cb_ref[:, lanes]

    conv_ref[...] = g_s[tb:tb + HALO, :]
    g_s[0:HALO, :] = g_s[tb:tb + HALO, :]

    for rb in range(n_chunks):
        rows = slice(rb * CHUNK, (rb + 1) * CHUNK)
        b = _ln(c_s[rows, :], clng_ref[...], clnb_ref[...])
        ab_s[rows, GM_WIDTH:] = (b * jax.nn.sigmoid(b)).astype(BF16)

    h_ref[...] = x_ref[...] + _dot(ab_s[...], wout_ref[...])

    qn = _rms(h_ref[...], gxa_ref[...]).astype(BF16)
    q = (_dot(qn, wq_ref[...]) * (XA_HEAD_DIM ** -0.5)).astype(BF16)
    for h in range(XA_HEADS):
        cols = slice(h * XA_HEAD_DIM, (h + 1) * XA_HEAD_DIM)
        s = _dot_nt(q[:, cols], kb_ref[:, cols])
        e = jnp.exp(s - jnp.max(s, axis=-1, keepdims=True))
        p = (e / jnp.sum(e, axis=-1, keepdims=True)).astype(BF16)
        ab_s[:, cols] = _dot(p, vb_ref[:, cols]).astype(BF16)
    h_ref[...] = h_ref[...] + _dot(ab_s[...], wo_ref[...])


def _prompt_mix_attn(x, gmix, win, glng, glnb, ws, bst, cw, cb, clng, clnb, wout, gxa, wq, kb, vb, wo):
    nb, seq, _ = x.shape
    tb = PROMPT_TILE
    full = lambda shape: pl.BlockSpec(shape, lambda n, t: (0,) * len(shape))
    row_spec = pl.BlockSpec((None, tb, D_MODEL), lambda n, t: (n, t, 0))
    per_seq = lambda r, c: pl.BlockSpec((None, r, c), lambda n, t: (n, 0, 0))
    return pl.pallas_call(
        _prompt_kernel,
        grid=(nb, seq // tb),
        in_specs=[row_spec, full((1, D_MODEL)), full((D_MODEL, IN_COLS)), full((1, GM_WIDTH)), full((1, GM_WIDTH)),
                  full((GM_HEADS, CHUNK, CHUNK)), full((CHUNK, GM_HEADS)), full((CONV_WIDTH, CV_WIDTH)),
                  full((1, CV_WIDTH)), full((1, CV_WIDTH)), full((1, CV_WIDTH)), full((D_MODEL, D_MODEL)),
                  full((1, D_MODEL)), full((D_MODEL, D_MODEL)), per_seq(N_MEM, D_MODEL), per_seq(N_MEM, D_MODEL),
                  full((D_MODEL, D_MODEL))],
        out_specs=[row_spec, per_seq(HALO, CV_WIDTH), per_seq(CHUNK, GM_WIDTH)],
        out_shape=[jax.ShapeDtypeStruct((nb, seq, D_MODEL), F32),
                   jax.ShapeDtypeStruct((nb, HALO, CV_WIDTH), F32),
                   jax.ShapeDtypeStruct((nb, CHUNK, GM_WIDTH), F32)],
        scratch_shapes=[pltpu.VMEM((tb, IN_COLS), F32), pltpu.VMEM((HALO + tb, CV_WIDTH), F32),
                        pltpu.VMEM((tb, CV_WIDTH), F32), pltpu.VMEM((tb, D_MODEL), BF16)],
        compiler_params=pltpu.CompilerParams(dimension_semantics=("arbitrary", "arbitrary"),
                                             vmem_limit_bytes=VMEM_LIMIT),
        name="prompt_mix_attn",
    )(x, gmix, win, glng, glnb, ws, bst, cw, cb, clng, clnb, wout, gxa, wq, kb, vb, wo)


def _sample_mix_kernel(x_ref, gmix_ref, win_ref, glng_ref, glnb_ref, ws0_ref, bs0_ref, cache_ref, cw_ref, cb_ref,
                       clng_ref, clnb_ref, wout_ref, gxa_ref, wq_ref,
                       y_ref, glu_ref, v_ref, q_ref, ab_s):
    x = x_ref[...]
    z = _dot(_rms(x, gmix_ref[...]).astype(BF16), win_ref[...])
    for h in range(GM_HEADS):
        cu = slice(h * GM_HEAD_DIM, (h + 1) * GM_HEAD_DIM)
        cv = slice(GM_WIDTH + h * GM_HEAD_DIM, GM_WIDTH + (h + 1) * GM_HEAD_DIM)
        v = _ln(jax.nn.gelu(z[:, cv]), glng_ref[:, cu], glnb_ref[:, cu])
        v_ref[:, cu] = v
        ab_s[:, cu] = (jax.nn.gelu(z[:, cu]) * (v * ws0_ref[:, cu] + bs0_ref[:, cu])).astype(BF16)
    glu = z[:, 2 * GM_WIDTH:2 * GM_WIDTH + CV_WIDTH] * jax.nn.sigmoid(z[:, 2 * GM_WIDTH + CV_WIDTH:])
    glu_ref[...] = glu
    conv = glu * cw_ref[CONV_WIDTH - 1:CONV_WIDTH, :] + cb_ref[...]
    for k in range(CONV_WIDTH - 1):
        conv = conv + cache_ref[k] * cw_ref[k:k + 1, :]
    b = _ln(conv, clng_ref[...], clnb_ref[...])
    ab_s[:, GM_WIDTH:] = (b * jax.nn.sigmoid(b)).astype(BF16)
    y = x + _dot(ab_s[...], wout_ref[...])
    y_ref[...] = y
    q_ref[...] = _dot(_rms(y, gxa_ref[...]).astype(BF16), wq_ref[...]) * (XA_HEAD_DIM ** -0.5)


def _sample_mix(x, gmix, win, glng, glnb, ws0, bs0, cache2d, cw, cb, clng, clnb, wout, gxa, wq):
    ns = x.shape[0]
    return pl.pallas_call(
        _sample_mix_kernel,
        out_shape=[jax.ShapeDtypeStruct((ns, D_MODEL), F32), jax.ShapeDtypeStruct((ns, CV_WIDTH), F32),
                   jax.ShapeDtypeStruct((ns, GM_WIDTH), F32), jax.ShapeDtypeStruct((ns, D_MODEL), F32)],
        scratch_shapes=[pltpu.VMEM((ns, D_MODEL), BF16)],
        compiler_params=pltpu.CompilerParams(vmem_limit_bytes=VMEM_LIMIT),
        name="sample_mix",
    )(x, gmix, win, glng, glnb, ws0, bs0, cache2d, cw, cb, clng, clnb, wout, gxa, wq)


def _sample_attn_kernel(q_ref, k_ref, v_ref, o_ref):
    for i in range(q_ref.shape[0]):
        s = jnp.sum(k_ref[i] * q_ref[i][None], axis=-1, keepdims=True)
        e = jnp.exp(s - jnp.max(s, axis=0, keepdims=True))
        p = e / jnp.sum(e, axis=0, keepdims=True)
        o_ref[i] = jnp.sum(p * v_ref[i], axis=0)


def _sample_attn(q, k, v):
    ns = q.shape[0]
    nb = SAMPLE_ATTN_BLOCK
    return pl.pallas_call(
        _sample_attn_kernel,
        grid=(ns // nb,),
        in_specs=[pl.BlockSpec((nb, XA_HEADS, XA_HEAD_DIM), lambda i: (i, 0, 0)),
                  pl.BlockSpec((nb, N_MEM, XA_HEADS, XA_HEAD_DIM), lambda i: (i, 0, 0, 0)),
                  pl.BlockSpec((nb, N_MEM, XA_HEADS, XA_HEAD_DIM), lambda i: (i, 0, 0, 0))],
        out_specs=pl.BlockSpec((nb, XA_HEADS, XA_HEAD_DIM), lambda i: (i, 0, 0)),
        out_shape=jax.ShapeDtypeStruct((ns, XA_HEADS, XA_HEAD_DIM), F32),
        compiler_params=pltpu.CompilerParams(dimension_semantics=("arbitrary",), vmem_limit_bytes=VMEM_LIMIT),
        name="sample_attn",
    )(q, k, v)


def _sample_proj_kernel(y_ref, o_ref, wo_ref, h_ref):
    h_ref[...] = y_ref[...] + _dot(o_ref[...].astype(BF16), wo_ref[...])


def _sample_proj(y, o, wo):
    return pl.pallas_call(
        _sample_proj_kernel,
        out_shape=jax.ShapeDtypeStruct(y.shape, F32),
        compiler_params=pltpu.CompilerParams(vmem_limit_bytes=VMEM_LIMIT),
        name="sample_proj",
    )(y, o, wo)


def _moe_kernel(h_ref, gffn_ref, wrt_ref, rb_ref, w1_ref, w3_ref, w2_ref, gfin_ref,
                o_ref, xt_s, u_s, gidx_s, slot_s, comb_s, act_s, cnt_s):
    i = pl.program_id(0)
    g = pl.program_id(1)
    rt = h_ref.shape[0]
    sb = act_s.shape[0]

    @pl.when(jnp.logical_and(i == 0, g == 0))
    def _():
        u_s[...] = (lax.broadcasted_iota(jnp.int32, (rt, rt), 0)
                    < lax.broadcasted_iota(jnp.int32, (rt, rt), 1)).astype(F32).astype(BF16)

    @pl.when(g == 0)
    def _route():
        h = h_ref[...]
        o_ref[...] = h
        xt = _rms(h, gffn_ref[...]).astype(BF16)
        xt_s[...] = xt
        lt = _dot_nt(wrt_ref[...], xt) + rb_ref[...]
        gl = [lt[k:k + 1, :] for k in range(N_GROUPS)]
        gmax = jnp.maximum(jnp.maximum(gl[0], gl[1]), jnp.maximum(gl[2], gl[3]))
        gidx = jnp.where(gl[0] == gmax, 0, jnp.where(gl[1] == gmax, 1, jnp.where(gl[2] == gmax, 2, 3)))
        gidx = gidx.astype(jnp.int32)
        sumexp = (jnp.exp(gl[0] - gmax) + jnp.exp(gl[1] - gmax)) + (jnp.exp(gl[2] - gmax) + jnp.exp(gl[3] - gmax))
        p_g = 1.0 / sumexp
        esel = lt[SUBLANES + 3 * EXPERTS_PER_GROUP:SUBLANES + 4 * EXPERTS_PER_GROUP, :]
        for k in (2, 1, 0):
            esel = jnp.where(gidx == k, lt[SUBLANES + k * EXPERTS_PER_GROUP:SUBLANES + (k + 1) * EXPERTS_PER_GROUP, :],
                             esel)
        eidx = lax.broadcasted_iota(jnp.int32, (EXPERTS_PER_GROUP, rt), 0)
        m1 = jnp.max(esel, axis=0, keepdims=True)
        i1 = jnp.min(jnp.where(esel == m1, eidx, EXPERTS_PER_GROUP), axis=0, keepdims=True)
        rest = jnp.where(eidx == i1, -jnp.inf, esel)
        m2 = jnp.max(rest, axis=0, keepdims=True)
        i2 = jnp.min(jnp.where(rest == m2, eidx, EXPERTS_PER_GROUP), axis=0, keepdims=True)
        t2 = jnp.exp(m2 - m1)
        den = 1.0 + t2
        w_top1 = (1.0 / den) * p_g
        w_top2 = (t2 / den) * p_g
        within = jnp.where(eidx == i1, w_top1, 0.0) + jnp.where(eidx == i2, w_top2, 0.0)
        c_hi = within.astype(BF16).astype(F32)
        r1 = within - c_hi
        c_mid = r1.astype(BF16).astype(F32)
        c_lo = (r1 - c_mid).astype(BF16).astype(F32)
        comb_s[0:8, :] = c_hi
        comb_s[8:16, :] = c_mid
        comb_s[16:24, :] = c_lo
        comb_s[24:32, :] = jnp.zeros((8, rt), F32)
        onehot = (eidx == gidx).astype(F32)
        rank = _dot(onehot.astype(BF16), u_s[...])
        slot_s[...] = jnp.sum(onehot * rank, axis=0, keepdims=True).astype(jnp.int32)
        gidx_s[...] = gidx
        for k in range(N_GROUPS):
            cnt_s[k] = jnp.sum(onehot[k:k + 1, :]).astype(jnp.int32)

    n_blk = (cnt_s[g] + sb - 1) // sb

    def body(j, carry):
        rows = lax.broadcasted_iota(jnp.int32, (sb, rt), 0) + j * sb
        hit = jnp.logical_and(rows == slot_s[...], gidx_s[...] == g)
        p = jnp.where(hit, 1.0, 0.0).astype(BF16)
        xc = _dot(p, xt_s[...]).astype(BF16)
        cexp = _dot_nt(p, comb_s[...].astype(BF16))
        cw = (cexp[:, 0:8] + cexp[:, 8:16]) + cexp[:, 16:24]
        for e in range(EXPERTS_PER_GROUP):
            h1 = _dot(xc, w1_ref[e])
            h3 = _dot(xc, w3_ref[e])
            a = (h1 * jax.nn.sigmoid(h1)) * h3 * cw[:, e:e + 1]
            act_s[:, e * EXPERT_FF:(e + 1) * EXPERT_FF] = a.astype(BF16)
        y = _dot(act_s[...], w2_ref[...])
        o_ref[...] += _dot_tn(p, y.astype(BF16))
        return carry

    lax.fori_loop(0, n_blk, body, 0)

    @pl.when(g == N_GROUPS - 1)
    def _():
        o_ref[...] = _rms(o_ref[...], gfin_ref[...])


def _moe(h, gffn, wrt, rbias, w1, w3, w2, gfin, *, tile):
    tokens = h.shape[0]
    sb = min(MOE_SUB, tile)
    ff = EXPERTS_PER_GROUP * EXPERT_FF
    full = lambda shape: pl.BlockSpec(shape, lambda i, g: (0,) * len(shape))
    row_spec = pl.BlockSpec((tile, D_MODEL), lambda i, g: (i, 0))
    return pl.pallas_call(
        _moe_kernel,
        grid=(tokens // tile, N_GROUPS),
        in_specs=[row_spec, full((1, D_MODEL)), full((ROUTER_ROWS, D_MODEL)), full((ROUTER_ROWS, 1)),
                  pl.BlockSpec((None, EXPERTS_PER_GROUP, D_MODEL, EXPERT_FF), lambda i, g: (g, 0, 0, 0)),
                  pl.BlockSpec((None, EXPERTS_PER_GROUP, D_MODEL, EXPERT_FF), lambda i, g: (g, 0, 0, 0)),
                  pl.BlockSpec((None, ff, D_MODEL), lambda i, g: (g, 0, 0)),
                  full((1, D_MODEL))],
        out_specs=row_spec,
        out_shape=jax.ShapeDtypeStruct((tokens, D_MODEL), F32),
        scratch_shapes=[pltpu.VMEM((tile, D_MODEL), BF16), pltpu.VMEM((tile, tile), BF16),
                        pltpu.VMEM((1, tile), jnp.int32), pltpu.VMEM((1, tile), jnp.int32),
                        pltpu.VMEM((4 * SUBLANES, tile), F32), pltpu.VMEM((sb, ff), BF16),
                        pltpu.SMEM((N_GROUPS,), jnp.int32)],
        compiler_params=pltpu.CompilerParams(dimension_semantics=("arbitrary", "arbitrary"),
                                             vmem_limit_bytes=VMEM_LIMIT),
        name="moe",
    )(h, gffn, wrt, rbias, w1, w3, w2, gfin)


def kernel(x_prompt, x_sample, mem_prompt, cache_conv, cache_mem_k, cache_mem_v, norm_mix_g, w_in, gm_ln_g, gm_ln_b, gm_ws, gm_bs, conv_w, conv_b, cv_ln_g, cv_ln_b, w_out, norm_mem_g, norm_xa_g, xa_wq, xa_wk, xa_wv, xa_wo, norm_ffn_g, router_g, router_g_b, router_e, router_e_b, exp_w1, exp_w3, exp_w2, final_norm_g):
    depth = w_in.shape[0]
    assert depth == 1, "single-layer trunk"
    nb, seq, _ = x_prompt.shape
    ns = x_sample.shape[0]
    row = lambda a: a.reshape(1, -1)

    l = 0
    gmix, gxa, gffn, gmem = row(norm_mix_g[l]), row(norm_xa_g[l]), row(norm_ffn_g[l]), row(norm_mem_g[l])
    gfin = row(final_norm_g)
    win, wout = w_in[l].astype(BF16), w_out[l].astype(BF16)
    wq, wk, wv, wo = (w[l].astype(BF16) for w in (xa_wq, xa_wk, xa_wv, xa_wo))
    glng, glnb = row(gm_ln_g[l]), row(gm_ln_b[l])
    cw, cb, clng, clnb = conv_w[l], row(conv_b[l]), row(cv_ln_g[l]), row(cv_ln_b[l])
    ws, bst = gm_ws[l], gm_bs[l].T
    ws0 = jnp.repeat(gm_ws[l][:, 0, 0], GM_HEAD_DIM).reshape(1, GM_WIDTH)
    bs0 = jnp.repeat(gm_bs[l][:, 0], GM_HEAD_DIM).reshape(1, GM_WIDTH)

    n_exp = N_GROUPS * EXPERTS_PER_GROUP
    pad_g = SUBLANES - N_GROUPS
    pad_t = ROUTER_ROWS - SUBLANES - n_exp
    wrt = jnp.concatenate([router_g[l].T, jnp.zeros((pad_g, D_MODEL), F32),
                           router_e[l].reshape(D_MODEL, n_exp).T, jnp.zeros((pad_t, D_MODEL), F32)], axis=0).astype(BF16)
    rbias = jnp.concatenate([router_g_b[l], jnp.zeros((pad_g,), F32), router_e_b[l].reshape(n_exp),
                             jnp.zeros((pad_t,), F32)]).reshape(ROUTER_ROWS, 1)
    w1, w3 = exp_w1[l].astype(BF16), exp_w3[l].astype(BF16)
    w2 = exp_w2[l].astype(BF16).reshape(N_GROUPS, EXPERTS_PER_GROUP * EXPERT_FF, D_MODEL)

    mk, mv, kb, vb = _memkv(mem_prompt.reshape(nb * N_MEM, D_MODEL), gmem, wk, wv)
    kb, vb = kb.reshape(nb, N_MEM, D_MODEL), vb.reshape(nb, N_MEM, D_MODEL)
    hp, conv_tail, gmv_p = _prompt_mix_attn(x_prompt, gmix, win, glng, glnb, ws, bst, cw, cb, clng, clnb, wout,
                                            gxa, wq, kb, vb, wo)
    y_prompt = _moe(hp.reshape(nb * seq, D_MODEL), gffn, wrt, rbias, w1, w3, w2, gfin, tile=MOE_TILE)

    cache_t = jnp.transpose(cache_conv[l], (1, 0, 2))
    ys, glu_s, gmv_s, q_s = _sample_mix(x_sample.reshape(ns, D_MODEL), gmix, win, glng, glnb, ws0, bs0,
                                        cache_t, cw, cb, clng, clnb, wout, gxa, wq)
    o_s = _sample_attn(q_s.reshape(ns, XA_HEADS, XA_HEAD_DIM), cache_mem_k[l], cache_mem_v[l])
    hs = _sample_proj(ys, o_s.reshape(ns, D_MODEL), wo)
    y_sample = _moe(hs, gffn, wrt, rbias, w1, w3, w2, gfin, tile=ns)

    conv_prompt = conv_tail[:, HALO - (CONV_WIDTH - 1):, :][None]
    conv_sample = jnp.transpose(jnp.concatenate([cache_t[1:], glu_s[None]], axis=0), (1, 0, 2))[None]
    return (y_prompt.reshape(nb, seq, D_MODEL), y_sample.reshape(ns, 1, D_MODEL), conv_prompt, conv_sample,
            gmv_p[None], gmv_s.reshape(1, ns, 1, GM_WIDTH),
            mk.reshape(1, nb, N_MEM, XA_HEADS, XA_HEAD_DIM), mv.reshape(1, nb, N_MEM, XA_HEADS, XA_HEAD_DIM))
```

```python
import functools

import jax
import jax.numpy as jnp
from jax import lax
from jax.experimental import pallas as pl
from jax.experimental.pallas import tpu as pltpu

F32 = jnp.float32
BF16 = jnp.bfloat16

D_MODEL = 1024
GM_WIDTH = 512
CV_WIDTH = 512
GM_HEADS = 4
GM_HEAD_DIM = 128
CHUNK = 128
CONV_WIDTH = 31
IN_COLS = 2 * GM_WIDTH + 2 * CV_WIDTH
N_MEM = 256
XA_HEADS = 4
XA_HEAD_DIM = 256
N_GROUPS = 4
EXPERTS_PER_GROUP = 8
EXPERT_FF = 256
EPS = 1e-6

LANES = 128
SUBLANES = 8
HALO = 32
PROMPT_TILE = 512
MOE_TILE = 1024
MOE_SUB = 128
SAMPLE_ATTN_BLOCK = 8
ROUTER_ROWS = 128
VMEM_LIMIT = 56 * 1024 * 1024


def _rms(x, g):
    return x * lax.rsqrt(jnp.mean(x * x, axis=-1, keepdims=True) + EPS) * g


def _ln(x, g, b):
    mu = jnp.mean(x, axis=-1, keepdims=True)
    xc = x - mu
    var = jnp.mean(xc * xc, axis=-1, keepdims=True)
    return xc * lax.rsqrt(var + EPS) * g + b


def _dot(a, b):
    return jnp.dot(a, b, preferred_element_type=F32)


def _dot_nt(a, b):
    return lax.dot_general(a, b, (((1,), (1,)), ((), ())), preferred_element_type=F32)


def _dot_tn(a, b):
    return lax.dot_general(a, b, (((0,), (0,)), ((), ())), preferred_element_type=F32)


def _memkv_kernel(mem_ref, g_ref, wk_ref, wv_ref, k_ref, v_ref, kb_ref, vb_ref):
    mn = _rms(mem_ref[...], g_ref[...]).astype(BF16)
    k = _dot(mn, wk_ref[...])
    v = _dot(mn, wv_ref[...])
    for h in range(XA_HEADS):
        cols = slice(h * XA_HEAD_DIM, (h + 1) * XA_HEAD_DIM)
        k_ref[:, h, :] = k[:, cols]
        v_ref[:, h, :] = v[:, cols]
    kb_ref[...] = k.astype(BF16)
    vb_ref[...] = v.astype(BF16)


def _memkv(mem2d, g, wk, wv):
    rows = mem2d.shape[0]
    tile = 512
    row_spec = pl.BlockSpec((tile, D_MODEL), lambda i: (i, 0))
    head_spec = pl.BlockSpec((tile, XA_HEADS, XA_HEAD_DIM), lambda i: (i, 0, 0))
    full = lambda shape: pl.BlockSpec(shape, lambda i: (0,) * len(shape))
    return pl.pallas_call(
        _memkv_kernel,
        grid=(rows // tile,),
        in_specs=[row_spec, full((1, D_MODEL)), full((D_MODEL, D_MODEL)), full((D_MODEL, D_MODEL))],
        out_specs=[head_spec, head_spec, row_spec, row_spec],
        out_shape=[jax.ShapeDtypeStruct((rows, XA_HEADS, XA_HEAD_DIM), F32),
                   jax.ShapeDtypeStruct((rows, XA_HEADS, XA_HEAD_DIM), F32),
                   jax.ShapeDtypeStruct((rows, D_MODEL), BF16), jax.ShapeDtypeStruct((rows, D_MODEL), BF16)],
        compiler_params=pltpu.CompilerParams(dimension_semantics=("arbitrary",), vmem_limit_bytes=VMEM_LIMIT),
        name="memkv",
    )(mem2d, g, wk, wv)


def _conv_block(g_s, cw_ref, start, lanes):
    assert start % SUBLANES == 0
    rows = CHUNK + HALO
    win = g_s[start:start + rows, lanes]
    off = HALO - (CONV_WIDTH - 1)
    acc = None
    for b in range(SUBLANES):
        shifted = pltpu.roll(win, rows - (off + b), axis=0) if off + b else win
        for k in range(b, CONV_WIDTH, SUBLANES):
            assert k - b + CHUNK + off + b <= rows
            term = shifted[k - b:k - b + CHUNK] * cw_ref[k:k + 1, lanes]
            acc = term if acc is None else acc + term
    return acc


def _prompt_kernel(x_ref, gmix_ref, win_ref, glng_ref, glnb_ref, ws_ref, bst_ref, cw_ref, cb_ref,
                   clng_ref, clnb_ref, wout_ref, gxa_ref, wq_ref, kb_ref, vb_ref, wo_ref,
                   h_ref, conv_ref, gmv_ref, z_s, g_s, c_s, ab_s, o_s, y_prev, y_cur, *, tiles_per_seq):
    s = pl.program_id(0)
    tb = x_ref.shape[0]
    n_chunks = tb // CHUNK

    @pl.when(s == 0)
    def _():
        y_prev[...] = jnp.zeros(y_prev.shape, F32)

    @pl.when(s % tiles_per_seq == 0)
    def _():
        g_s[0:HALO, :] = jnp.zeros((HALO, CV_WIDTH), F32)

    z_s[...] = _dot(_rms(x_ref[...], gmix_ref[...]).astype(BF16), win_ref[...])

    y = y_prev[...]
    qn = _rms(y, gxa_ref[...]).astype(BF16)
    q = (_dot(qn, wq_ref[...]) * (XA_HEAD_DIM ** -0.5)).astype(BF16)
    for h in range(XA_HEADS):
        cols = slice(h * XA_HEAD_DIM, (h + 1) * XA_HEAD_DIM)
        sc = _dot_nt(q[:, cols], kb_ref[:, cols])
        e = jnp.exp(sc - jnp.max(sc, axis=-1, keepdims=True))
        p = (e / jnp.sum(e, axis=-1, keepdims=True)).astype(BF16)
        o_s[:, cols] = _dot(p, vb_ref[:, cols]).astype(BF16)
    h_ref[...] = y + _dot(o_s[...], wo_ref[...])

    tri = (lax.broadcasted_iota(jnp.int32, (CHUNK, CHUNK), 0)
           >= lax.broadcasted_iota(jnp.int32, (CHUNK, CHUNK), 1))
    wm = [jnp.where(tri, ws_ref[h], 0.0).astype(BF16) for h in range(GM_HEADS)]

    for c in range(n_chunks):
        rows = slice(c * CHUNK, (c + 1) * CHUNK)
        for h in range(GM_HEADS):
            cu = slice(h * GM_HEAD_DIM, (h + 1) * GM_HEAD_DIM)
            cv = slice(GM_WIDTH + h * GM_HEAD_DIM, GM_WIDTH + (h + 1) * GM_HEAD_DIM)
            v = _ln(jax.nn.gelu(z_s[rows, cv]), glng_ref[:, cu], glnb_ref[:, cu])
            if c == n_chunks - 1:
                gmv_ref[:, cu] = v
            mixed = _dot(wm[h], v.astype(BF16)) + bst_ref[:, h:h + 1]
            ab_s[rows, cu] = (jax.nn.gelu(z_s[rows, cu]) * mixed).astype(BF16)
        ca = slice(2 * GM_WIDTH, 2 * GM_WIDTH + CV_WIDTH)
        cg = slice(2 * GM_WIDTH + CV_WIDTH, IN_COLS)
        g_s[HALO + c * CHUNK:HALO + (c + 1) * CHUNK, :] = z_s[rows, ca] * jax.nn.sigmoid(z_s[rows, cg])

    for c in range(n_chunks):
        rows = slice(c * CHUNK, (c + 1) * CHUNK)
        for cb in range(CV_WIDTH // LANES):
            lanes = slice(cb * LANES, (cb + 1) * LANES)
            c_s[rows, lanes] = _conv_block(g_s, cw_ref, c * CHUNK, lanes) + cb_ref[:, lanes]
        b = _ln(c_s[rows, :], clng_ref[...], clnb_ref[...])
        ab_s[rows, GM_WIDTH:] = (b * jax.nn.sigmoid(b)).astype(BF16)

    conv_ref[...] = g_s[tb:tb + HALO, :]
    g_s[0:HALO, :] = g_s[tb:tb + HALO, :]
    y_cur[...] = x_ref[...] + _dot(ab_s[...], wout_ref[...])
    y_prev[...] = y_cur[...]


def _prompt_mix_attn(x, gmix, win, glng, glnb, ws, bst, cw, cb, clng, clnb, wout, gxa, wq, kb, vb, wo):
    nb, seq, _ = x.shape
    tb = PROMPT_TILE
    tps = seq // tb
    last = nb * tps - 1
    tile = lambda s, lag: jnp.clip(s - lag, 0, last)
    full = lambda shape: pl.BlockSpec(shape, lambda s: (0,) * len(shape))
    rows = lambda lag: pl.BlockSpec((None, tb, D_MODEL), lambda s: (tile(s, lag) // tps, tile(s, lag) % tps, 0))
    kv_spec = pl.BlockSpec((None, N_MEM, D_MODEL), lambda s: (tile(s, 1) // tps, 0, 0))
    seq_out = lambda r, c: pl.BlockSpec((None, r, c), lambda s: (tile(s, 0) // tps, 0, 0))
    return pl.pallas_call(
        functools.partial(_prompt_kernel, tiles_per_seq=tps),
        grid=(nb * tps + 1,),
        in_specs=[rows(0), full((1, D_MODEL)), full((D_MODEL, IN_COLS)), full((1, GM_WIDTH)), full((1, GM_WIDTH)),
                  full((GM_HEADS, CHUNK, CHUNK)), full((CHUNK, GM_HEADS)), full((CONV_WIDTH, CV_WIDTH)),
                  full((1, CV_WIDTH)), full((1, CV_WIDTH)), full((1, CV_WIDTH)), full((D_MODEL, D_MODEL)),
                  full((1, D_MODEL)), full((D_MODEL, D_MODEL)), kv_spec, kv_spec, full((D_MODEL, D_MODEL))],
        out_specs=[rows(1), seq_out(HALO, CV_WIDTH), seq_out(CHUNK, GM_WIDTH)],
        out_shape=[jax.ShapeDtypeStruct((nb, seq, D_MODEL), F32),
                   jax.ShapeDtypeStruct((nb, HALO, CV_WIDTH), F32),
                   jax.ShapeDtypeStruct((nb, CHUNK, GM_WIDTH), F32)],
        scratch_shapes=[pltpu.VMEM((tb, IN_COLS), F32), pltpu.VMEM((HALO + tb, CV_WIDTH), F32),
                        pltpu.VMEM((tb, CV_WIDTH), F32), pltpu.VMEM((tb, D_MODEL), BF16),
                        pltpu.VMEM((tb, D_MODEL), BF16), pltpu.VMEM((tb, D_MODEL), F32),
                        pltpu.VMEM((tb, D_MODEL), F32)],
        compiler_params=pltpu.CompilerParams(dimension_semantics=("arbitrary",), vmem_limit_bytes=VMEM_LIMIT),
        name="prompt_mix_attn",
    )(x, gmix, win, glng, glnb, ws, bst, cw, cb, clng, clnb, wout, gxa, wq, kb, vb, wo)


def _sample_mix_kernel(x_ref, gmix_ref, win_ref, glng_ref, glnb_ref, ws0_ref, bs0_ref, cache_ref, cw_ref, cb_ref,
                       clng_ref, clnb_ref, wout_ref, gxa_ref, wq_ref,
                       y_ref, glu_ref, v_ref, q_ref, ab_s):
    x = x_ref[...]
    z = _dot(_rms(x, gmix_ref[...]).astype(BF16), win_ref[...])
    for h in range(GM_HEADS):
        cu = slice(h * GM_HEAD_DIM, (h + 1) * GM_HEAD_DIM)
        cv = slice(GM_WIDTH + h * GM_HEAD_DIM, GM_WIDTH + (h + 1) * GM_HEAD_DIM)
        v = _ln(jax.nn.gelu(z[:, cv]), glng_ref[:, cu], glnb_ref[:, cu])
        v_ref[:, cu] = v
        ab_s[:, cu] = (jax.nn.gelu(z[:, cu]) * (v * ws0_ref[:, cu] + bs0_ref[:, cu])).astype(BF16)
    glu = z[:, 2 * GM_WIDTH:2 * GM_WIDTH + CV_WIDTH] * jax.nn.sigmoid(z[:, 2 * GM_WIDTH + CV_WIDTH:])
    glu_ref[...] = glu
    conv = glu * cw_ref[CONV_WIDTH - 1:CONV_WIDTH, :] + cb_ref[...]
    for k in range(CONV_WIDTH - 1):
        conv = conv + cache_ref[k] * cw_ref[k:k + 1, :]
    b = _ln(conv, clng_ref[...], clnb_ref[...])
    ab_s[:, GM_WIDTH:] = (b * jax.nn.sigmoid(b)).astype(BF16)
    y = x + _dot(ab_s[...], wout_ref[...])
    y_ref[...] = y
    q_ref[...] = _dot(_rms(y, gxa_ref[...]).astype(BF16), wq_ref[...]) * (XA_HEAD_DIM ** -0.5)


def _sample_mix(x, gmix, win, glng, glnb, ws0, bs0, cache_t, cw, cb, clng, clnb, wout, gxa, wq):
    ns = x.shape[0]
    return pl.pallas_call(
        _sample_mix_kernel,
        out_shape=[jax.ShapeDtypeStruct((ns, D_MODEL), F32), jax.ShapeDtypeStruct((ns, CV_WIDTH), F32),
                   jax.ShapeDtypeStruct((ns, GM_WIDTH), F32), jax.ShapeDtypeStruct((ns, D_MODEL), F32)],
        scratch_shapes=[pltpu.VMEM((ns, D_MODEL), BF16)],
        compiler_params=pltpu.CompilerParams(vmem_limit_bytes=VMEM_LIMIT),
        name="sample_mix",
    )(x, gmix, win, glng, glnb, ws0, bs0, cache_t, cw, cb, clng, clnb, wout, gxa, wq)


def _sample_attn_kernel(q_ref, k_ref, v_ref, o_ref):
    ones = jnp.ones((LANES, LANES), BF16)
    rows = N_MEM * SUBLANES
    for i in range(q_ref.shape[0]):
        prod = (k_ref[i] * q_ref[i][None]).reshape(rows, LANES).astype(BF16)
        part = _dot(prod, ones).reshape(N_MEM, SUBLANES, LANES)
        s = part + pltpu.roll(part, XA_HEADS, axis=1)
        e = jnp.exp(s - jnp.max(s, axis=0, keepdims=True))
        p = e / jnp.sum(e, axis=0, keepdims=True)
        o_ref[i] = jnp.sum(p * v_ref[i], axis=0)


def _sample_attn(q, k, v):
    ns = q.shape[0]
    nb = SAMPLE_ATTN_BLOCK
    halves = XA_HEAD_DIM // LANES
    assert halves * XA_HEADS == SUBLANES

    def split(a):
        lead = a.shape[:-2]
        a = a.reshape(*lead, XA_HEADS, halves, LANES)
        return jnp.swapaxes(a, -3, -2).reshape(*lead, SUBLANES, LANES)

    o = pl.pallas_call(
        _sample_attn_kernel,
        grid=(ns // nb,),
        in_specs=[pl.BlockSpec((nb, SUBLANES, LANES), lambda i: (i, 0, 0)),
                  pl.BlockSpec((nb, N_MEM, SUBLANES, LANES), lambda i: (i, 0, 0, 0)),
                  pl.BlockSpec((nb, N_MEM, SUBLANES, LANES), lambda i: (i, 0, 0, 0))],
        out_specs=pl.BlockSpec((nb, SUBLANES, LANES), lambda i: (i, 0, 0)),
        out_shape=jax.ShapeDtypeStruct((ns, SUBLANES, LANES), F32),
        compiler_params=pltpu.CompilerParams(dimension_semantics=("arbitrary",), vmem_limit_bytes=VMEM_LIMIT),
        name="sample_attn",
    )(split(q), split(k), split(v))
    o = jnp.swapaxes(o.reshape(ns, halves, XA_HEADS, LANES), 1, 2)
    return o.reshape(ns, XA_HEADS * XA_HEAD_DIM)


def _sample_proj_kernel(y_ref, o_ref, wo_ref, h_ref):
    h_ref[...] = y_ref[...] + _dot(o_ref[...].astype(BF16), wo_ref[...])


def _sample_proj(y, o, wo):
    return pl.pallas_call(
        _sample_proj_kernel,
        out_shape=jax.ShapeDtypeStruct(y.shape, F32),
        compiler_params=pltpu.CompilerParams(vmem_limit_bytes=VMEM_LIMIT),
        name="sample_proj",
    )(y, o, wo)


def _moe_kernel(h_ref, gffn_ref, wrt_ref, rb_ref, w1_ref, w3_ref, w2_ref, gfin_ref,
                o_ref, xt_s, u_s, gidx_s, slot_s, comb_s, act_s, cnt_s):
    i = pl.program_id(0)
    g = pl.program_id(1)
    rt = h_ref.shape[0]
    sb = act_s.shape[0]

    @pl.when(jnp.logical_and(i == 0, g == 0))
    def _():
        u_s[...] = (lax.broadcasted_iota(jnp.int32, (rt, rt), 0)
                    < lax.broadcasted_iota(jnp.int32, (rt, rt), 1)).astype(F32).astype(BF16)

    @pl.when(g == 0)
    def _route():
        h = h_ref[...]
        o_ref[...] = h
        xt = _rms(h, gffn_ref[...]).astype(BF16)
        xt_s[...] = xt
        lt = _dot_nt(wrt_ref[...], xt) + rb_ref[...]
        gl = [lt[k:k + 1, :] for k in range(N_GROUPS)]
        gmax = jnp.maximum(jnp.maximum(gl[0], gl[1]), jnp.maximum(gl[2], gl[3]))
        gidx = jnp.where(gl[0] == gmax, 0, jnp.where(gl[1] == gmax, 1, jnp.where(gl[2] == gmax, 2, 3)))
        gidx = gidx.astype(jnp.int32)
        sumexp = (jnp.exp(gl[0] - gmax) + jnp.exp(gl[1] - gmax)) + (jnp.exp(gl[2] - gmax) + jnp.exp(gl[3] - gmax))
        p_g = 1.0 / sumexp
        esel = lt[SUBLANES + 3 * EXPERTS_PER_GROUP:SUBLANES + 4 * EXPERTS_PER_GROUP, :]
        for k in (2, 1, 0):
            esel = jnp.where(gidx == k, lt[SUBLANES + k * EXPERTS_PER_GROUP:SUBLANES + (k + 1) * EXPERTS_PER_GROUP, :],
                             esel)
        eidx = lax.broadcasted_iota(jnp.int32, (EXPERTS_PER_GROUP, rt), 0)
        m1 = jnp.max(esel, axis=0, keepdims=True)
        i1 = jnp.min(jnp.where(esel == m1, eidx, EXPERTS_PER_GROUP), axis=0, keepdims=True)
        rest = jnp.where(eidx == i1, -jnp.inf, esel)
        m2 = jnp.max(rest, axis=0, keepdims=True)
        i2 = jnp.min(jnp.where(rest == m2, eidx, EXPERTS_PER_GROUP), axis=0, keepdims=True)
        t2 = jnp.exp(m2 - m1)
        den = 1.0 + t2
        w_top1 = (1.0 / den) * p_g
        w_top2 = (t2 / den) * p_g
        within = jnp.where(eidx == i1, w_top1, 0.0) + jnp.where(eidx == i2, w_top2, 0.0)
        c_hi = within.astype(BF16).astype(F32)
        r1 = within - c_hi
        c_mid = r1.astype(BF16).astype(F32)
        c_lo = (r1 - c_mid).astype(BF16).astype(F32)
        comb_s[0:8, :] = c_hi
        comb_s[8:16, :] = c_mid
        comb_s[16:24, :] = c_lo
        comb_s[24:32, :] = jnp.zeros((8, rt), F32)
        onehot = (eidx == gidx).astype(F32)
        rank = _dot(onehot.astype(BF16), u_s[...])
        slot_s[...] = jnp.sum(onehot * rank, axis=0, keepdims=True).astype(jnp.int32)
        gidx_s[...] = gidx
        for k in range(N_GROUPS):
            cnt_s[k] = jnp.sum(onehot[k:k + 1, :]).astype(jnp.int32)

    n_blk = (cnt_s[g] + sb - 1) // sb

    def body(j, carry):
        rows = lax.broadcasted_iota(jnp.int32, (sb, rt), 0) + j * sb
        hit = jnp.logical_and(rows == slot_s[...], gidx_s[...] == g)
        p = jnp.where(hit, 1.0, 0.0).astype(BF16)
        xc = _dot(p, xt_s[...]).astype(BF16)
        cexp = _dot_nt(p, comb_s[...].astype(BF16))
        cw = (cexp[:, 0:8] + cexp[:, 8:16]) + cexp[:, 16:24]
        for e in range(EXPERTS_PER_GROUP):
            h1 = _dot(xc, w1_ref[e])
            h3 = _dot(xc, w3_ref[e])
            a = (h1 * jax.nn.sigmoid(h1)) * h3 * cw[:, e:e + 1]
            act_s[:, e * EXPERT_FF:(e + 1) * EXPERT_FF] = a.astype(BF16)
        y = _dot(act_s[...], w2_ref[...])
        o_ref[...] += _dot_tn(p, y.astype(BF16))
        return carry

    lax.fori_loop(0, n_blk, body, 0)

    @pl.when(g == N_GROUPS - 1)
    def _():
        o_ref[...] = _rms(o_ref[...], gfin_ref[...])


def _moe(h, gffn, wrt, rbias, w1, w3, w2, gfin, *, tile):
    tokens = h.shape[0]
    sb = min(MOE_SUB, tile)
    ff = EXPERTS_PER_GROUP * EXPERT_FF
    full = lambda shape: pl.BlockSpec(shape, lambda i, g: (0,) * len(shape))
    row_spec = pl.BlockSpec((tile, D_MODEL), lambda i, g: (i, 0))
    return pl.pallas_call(
        _moe_kernel,
        grid=(tokens // tile, N_GROUPS),
        in_specs=[row_spec, full((1, D_MODEL)), full((ROUTER_ROWS, D_MODEL)), full((ROUTER_ROWS, 1)),
                  pl.BlockSpec((None, EXPERTS_PER_GROUP, D_MODEL, EXPERT_FF), lambda i, g: (g, 0, 0, 0)),
                  pl.BlockSpec((None, EXPERTS_PER_GROUP, D_MODEL, EXPERT_FF), lambda i, g: (g, 0, 0, 0)),
                  pl.BlockSpec((None, ff, D_MODEL), lambda i, g: (g, 0, 0)),
                  full((1, D_MODEL))],
        out_specs=row_spec,
        out_shape=jax.ShapeDtypeStruct((tokens, D_MODEL), F32),
        scratch_shapes=[pltpu.VMEM((tile, D_MODEL), BF16), pltpu.VMEM((tile, tile), BF16),
                        pltpu.VMEM((1, tile), jnp.int32), pltpu.VMEM((1, tile), jnp.int32),
                        pltpu.VMEM((4 * SUBLANES, tile), F32), pltpu.VMEM((sb, ff), BF16),
                        pltpu.SMEM((N_GROUPS,), jnp.int32)],
        compiler_params=pltpu.CompilerParams(dimension_semantics=("arbitrary", "arbitrary"),
                                             vmem_limit_bytes=VMEM_LIMIT),
        name="moe",
    )(h, gffn, wrt, rbias, w1, w3, w2, gfin)


def kernel(x_prompt, x_sample, mem_prompt, cache_conv, cache_mem_k, cache_mem_v, norm_mix_g, w_in, gm_ln_g, gm_ln_b, gm_ws, gm_bs, conv_w, conv_b, cv_ln_g, cv_ln_b, w_out, norm_mem_g, norm_xa_g, xa_wq, xa_wk, xa_wv, xa_wo, norm_ffn_g, router_g, router_g_b, router_e, router_e_b, exp_w1, exp_w3, exp_w2, final_norm_g):
    depth = w_in.shape[0]
    assert depth == 1, "single-layer trunk"
    nb, seq, _ = x_prompt.shape
    ns = x_sample.shape[0]
    row = lambda a: a.reshape(1, -1)

    l = 0
    gmix, gxa, gffn, gmem = row(norm_mix_g[l]), row(norm_xa_g[l]), row(norm_ffn_g[l]), row(norm_mem_g[l])
    gfin = row(final_norm_g)
    win, wout = w_in[l].astype(BF16), w_out[l].astype(BF16)
    wq, wk, wv, wo = (w[l].astype(BF16) for w in (xa_wq, xa_wk, xa_wv, xa_wo))
    glng, glnb = row(gm_ln_g[l]), row(gm_ln_b[l])
    cw, cb, clng, clnb = conv_w[l], row(conv_b[l]), row(cv_ln_g[l]), row(cv_ln_b[l])
    ws, bst = gm_ws[l], gm_bs[l].T
    ws0 = jnp.repeat(gm_ws[l][:, 0, 0], GM_HEAD_DIM).reshape(1, GM_WIDTH)
    bs0 = jnp.repeat(gm_bs[l][:, 0], GM_HEAD_DIM).reshape(1, GM_WIDTH)

    n_exp = N_GROUPS * EXPERTS_PER_GROUP
    pad_g = SUBLANES - N_GROUPS
    pad_t = ROUTER_ROWS - SUBLANES - n_exp
    wrt = jnp.concatenate([router_g[l].T, jnp.zeros((pad_g, D_MODEL), F32),
                           router_e[l].reshape(D_MODEL, n_exp).T, jnp.zeros((pad_t, D_MODEL), F32)], axis=0).astype(BF16)
    rbias = jnp.concatenate([router_g_b[l], jnp.zeros((pad_g,), F32), router_e_b[l].reshape(n_exp),
                             jnp.zeros((pad_t,), F32)]).reshape(ROUTER_ROWS, 1)
    w1, w3 = exp_w1[l].astype(BF16), exp_w3[l].astype(BF16)
    w2 = exp_w2[l].astype(BF16).reshape(N_GROUPS, EXPERTS_PER_GROUP * EXPERT_FF, D_MODEL)

    mk, mv, kb, vb = _memkv(mem_prompt.reshape(nb * N_MEM, D_MODEL), gmem, wk, wv)
    kb, vb = kb.reshape(nb, N_MEM, D_MODEL), vb.reshape(nb, N_MEM, D_MODEL)
    hp, conv_tail, gmv_p = _prompt_mix_attn(x_prompt, gmix, win, glng, glnb, ws, bst, cw, cb, clng, clnb, wout,
                                            gxa, wq, kb, vb, wo)
    y_prompt = _moe(hp.reshape(nb * seq, D_MODEL), gffn, wrt, rbias, w1, w3, w2, gfin, tile=MOE_TILE)

    cache_t = jnp.transpose(cache_conv[l], (1, 0, 2))
    ys, glu_s, gmv_s, q_s = _sample_mix(x_sample.reshape(ns, D_MODEL), gmix, win, glng, glnb, ws0, bs0,
                                        cache_t, cw, cb, clng, clnb, wout, gxa, wq)
    o_s = _sample_attn(q_s.reshape(ns, XA_HEADS, XA_HEAD_DIM), cache_mem_k[l], cache_mem_v[l])
    hs = _sample_proj(ys, o_s, wo)
    y_sample = _moe(hs, gffn, wrt, rbias, w1, w3, w2, gfin, tile=ns)

    conv_prompt = conv_tail[:, HALO - (CONV_WIDTH - 1):, :][None]
    conv_sample = jnp.transpose(jnp.concatenate([cache_t[1:], glu_s[None]], axis=0), (1, 0, 2))[None]
    return (y_prompt.reshape(nb, seq, D_MODEL), y_sample.reshape(ns, 1, D_MODEL), conv_prompt, conv_sample,
            gmv_p[None], gmv_s.reshape(1, ns, 1, GM_WIDTH),
            mk.reshape(1, nb, N_MEM, XA_HEADS, XA_HEAD_DIM), mv.reshape(1, nb, N_MEM, XA_HEADS, XA_HEAD_DIM))
```

```python
import functools

import jax
import jax.numpy as jnp
from jax import lax
from jax.experimental import pallas as pl
from jax.experimental.pallas import tpu as pltpu

F32 = jnp.float32
BF16 = jnp.bfloat16

D_MODEL = 1024
GM_WIDTH = 512
CV_WIDTH = 512
GM_HEADS = 4
GM_HEAD_DIM = 128
CHUNK = 128
CONV_WIDTH = 31
IN_COLS = 2 * GM_WIDTH + 2 * CV_WIDTH
N_MEM = 256
XA_HEADS = 4
XA_HEAD_DIM = 256
N_GROUPS = 4
EXPERTS_PER_GROUP = 8
EXPERT_FF = 256
EPS = 1e-6

LANES = 128
SUBLANES = 8
HALO = 32
PROMPT_TILE = 512
MOE_TILE = 1024
MOE_SUB = 128
SAMPLE_ATTN_BLOCK = 8
ROUTER_ROWS = 128
VMEM_LIMIT = 56 * 1024 * 1024


def _rms(x, g):
    return x * lax.rsqrt(jnp.mean(x * x, axis=-1, keepdims=True) + EPS) * g


def _ln(x, g, b):
    mu = jnp.mean(x, axis=-1, keepdims=True)
    xc = x - mu
    var = jnp.mean(xc * xc, axis=-1, keepdims=True)
    return xc * lax.rsqrt(var + EPS) * g + b


def _dot(a, b):
    return jnp.dot(a, b, preferred_element_type=F32)


def _dot_nt(a, b):
    return lax.dot_general(a, b, (((1,), (1,)), ((), ())), preferred_element_type=F32)


def _dot_tn(a, b):
    return lax.dot_general(a, b, (((0,), (0,)), ((), ())), preferred_element_type=F32)


def _memkv_kernel(mem_ref, g_ref, wk_ref, wv_ref, k_ref, v_ref, kb_ref, vb_ref):
    mn = _rms(mem_ref[...], g_ref[...]).astype(BF16)
    k = _dot(mn, wk_ref[...])
    v = _dot(mn, wv_ref[...])
    for h in range(XA_HEADS):
        cols = slice(h * XA_HEAD_DIM, (h + 1) * XA_HEAD_DIM)
        k_ref[:, h, :] = k[:, cols]
        v_ref[:, h, :] = v[:, cols]
    kb_ref[...] = k.astype(BF16)
    vb_ref[...] = v.astype(BF16)


def _memkv(mem2d, g, wk, wv):
    rows = mem2d.shape[0]
    tile = 512
    row_spec = pl.BlockSpec((tile, D_MODEL), lambda i: (i, 0))
    head_spec = pl.BlockSpec((tile, XA_HEADS, XA_HEAD_DIM), lambda i: (i, 0, 0))
    full = lambda shape: pl.BlockSpec(shape, lambda i: (0,) * len(shape))
    return pl.pallas_call(
        _memkv_kernel,
        grid=(rows // tile,),
        in_specs=[row_spec, full((1, D_MODEL)), full((D_MODEL, D_MODEL)), full((D_MODEL, D_MODEL))],
        out_specs=[head_spec, head_spec, row_spec, row_spec],
        out_shape=[jax.ShapeDtypeStruct((rows, XA_HEADS, XA_HEAD_DIM), F32),
                   jax.ShapeDtypeStruct((rows, XA_HEADS, XA_HEAD_DIM), F32),
                   jax.ShapeDtypeStruct((rows, D_MODEL), BF16), jax.ShapeDtypeStruct((rows, D_MODEL), BF16)],
        compiler_params=pltpu.CompilerParams(dimension_semantics=("arbitrary",), vmem_limit_bytes=VMEM_LIMIT),
        name="memkv",
    )(mem2d, g, wk, wv)


def _conv_block(g_s, cw_ref, start, lanes):
    assert start % SUBLANES == 0
    rows = CHUNK + HALO
    win = g_s[start:start + rows, lanes]
    off = HALO - (CONV_WIDTH - 1)
    acc = None
    for b in range(SUBLANES):
        shifted = pltpu.roll(win, rows - (off + b), axis=0) if off + b else win
        for k in range(b, CONV_WIDTH, SUBLANES):
            assert k - b + CHUNK + off + b <= rows
            term = shifted[k - b:k - b + CHUNK] * cw_ref[k:k + 1, lanes]
            acc = term if acc is None else acc + term
    return acc


def _cast_copies(src_refs, dst_refs, in_bufs, out_bufs, sem_in, sem_out, chunk, slot):
    copies_in, copies_out = [], []
    for i, (src, dst, ibuf, obuf) in enumerate(zip(src_refs, dst_refs, in_bufs, out_bufs)):
        rows = obuf.shape[0]
        start = pl.multiple_of(chunk * rows, rows)
        copies_in.append(pltpu.make_async_copy(src.at[pl.ds(start, rows), :], ibuf.at[slot], sem_in.at[slot, i]))
        copies_out.append(pltpu.make_async_copy(obuf, dst.at[pl.ds(start, rows), :], sem_out.at[i]))
    return copies_in, copies_out


def _prompt_kernel(x_ref, gmix_ref, win_ref, glng_ref, glnb_ref, ws_ref, bst_ref, cw_ref, cb_ref,
                   clng_ref, clnb_ref, wout_ref, gxa_ref, wq_ref, kb_ref, vb_ref, wo_ref,
                   w1f_ref, w3f_ref, w2f_ref,
                   h_ref, conv_ref, gmv_ref, w1b_ref, w3b_ref, w2b_ref,
                   z_s, g_s, c_s, ab_s, o_s, y_prev, y_cur, in1, in3, in2, st1, st3, st2, sem_in, sem_out,
                   *, tiles_per_seq, n_cast_chunks):
    s = pl.program_id(0)
    n_steps = pl.num_programs(0)
    tb = x_ref.shape[0]
    n_chunks = tb // CHUNK

    srcs, dsts = (w1f_ref, w3f_ref, w2f_ref), (w1b_ref, w3b_ref, w2b_ref)
    ins, sts = (in1, in3, in2), (st1, st3, st2)
    slot = s % 2
    chunk_of = lambda step: jnp.minimum(step, n_cast_chunks - 1)

    @pl.when(s == 0)
    def _():
        for c in _cast_copies(srcs, dsts, ins, sts, sem_in, sem_out, chunk_of(s), slot)[0]:
            c.start()

    @pl.when(s + 1 < n_steps)
    def _():
        for c in _cast_copies(srcs, dsts, ins, sts, sem_in, sem_out, chunk_of(s + 1), 1 - slot)[0]:
            c.start()

    cast_in, cast_out = _cast_copies(srcs, dsts, ins, sts, sem_in, sem_out, chunk_of(s), slot)
    for c in cast_in:
        c.wait()
    for ibuf, obuf in zip(ins, sts):
        obuf[...] = ibuf[slot].astype(BF16)
    for c in cast_out:
        c.start()

    @pl.when(s == 0)
    def _():
        y_prev[...] = jnp.zeros(y_prev.shape, F32)

    @pl.when(s % tiles_per_seq == 0)
    def _():
        g_s[0:HALO, :] = jnp.zeros((HALO, CV_WIDTH), F32)

    z_s[...] = _dot(_rms(x_ref[...], gmix_ref[...]).astype(BF16), win_ref[...])

    y = y_prev[...]
    qn = _rms(y, gxa_ref[...]).astype(BF16)
    q = (_dot(qn, wq_ref[...]) * (XA_HEAD_DIM ** -0.5)).astype(BF16)
    for h in range(XA_HEADS):
        cols = slice(h * XA_HEAD_DIM, (h + 1) * XA_HEAD_DIM)
        sc = _dot_nt(q[:, cols], kb_ref[:, cols])
        e = jnp.exp(sc - jnp.max(sc, axis=-1, keepdims=True))
        p = (e / jnp.sum(e, axis=-1, keepdims=True)).astype(BF16)
        o_s[:, cols] = _dot(p, vb_ref[:, cols]).astype(BF16)
    h_ref[...] = y + _dot(o_s[...], wo_ref[...])

    tri = (lax.broadcasted_iota(jnp.int32, (CHUNK, CHUNK), 0)
           >= lax.broadcasted_iota(jnp.int32, (CHUNK, CHUNK), 1))
    wm = [jnp.where(tri, ws_ref[h], 0.0).astype(BF16) for h in range(GM_HEADS)]

    for c in range(n_chunks):
        rows = slice(c * CHUNK, (c + 1) * CHUNK)
        for h in range(GM_HEADS):
            cu = slice(h * GM_HEAD_DIM, (h + 1) * GM_HEAD_DIM)
            cv = slice(GM_WIDTH + h * GM_HEAD_DIM, GM_WIDTH + (h + 1) * GM_HEAD_DIM)
            v = _ln(jax.nn.gelu(z_s[rows, cv]), glng_ref[:, cu], glnb_ref[:, cu])
            if c == n_chunks - 1:
                gmv_ref[:, cu] = v
            mixed = _dot(wm[h], v.astype(BF16)) + bst_ref[:, h:h + 1]
            ab_s[rows, cu] = (jax.nn.gelu(z_s[rows, cu]) * mixed).astype(BF16)
        ca = slice(2 * GM_WIDTH, 2 * GM_WIDTH + CV_WIDTH)
        cg = slice(2 * GM_WIDTH + CV_WIDTH, IN_COLS)
        g_s[HALO + c * CHUNK:HALO + (c + 1) * CHUNK, :] = z_s[rows, ca] * jax.nn.sigmoid(z_s[rows, cg])

    for c in range(n_chunks):
        rows = slice(c * CHUNK, (c + 1) * CHUNK)
        for cb in range(CV_WIDTH // LANES):
            lanes = slice(cb * LANES, (cb + 1) * LANES)
            c_s[rows, lanes] = _conv_block(g_s, cw_ref, c * CHUNK, lanes) + cb_ref[:, lanes]
        b = _ln(c_s[rows, :], clng_ref[...], clnb_ref[...])
        ab_s[rows, GM_WIDTH:] = (b * jax.nn.sigmoid(b)).astype(BF16)

    conv_ref[...] = g_s[tb:tb + HALO, :]
    g_s[0:HALO, :] = g_s[tb:tb + HALO, :]
    y_cur[...] = x_ref[...] + _dot(ab_s[...], wout_ref[...])
    y_prev[...] = y_cur[...]
    for c in cast_out:
        c.wait()


def _prompt_mix_attn(x, gmix, win, glng, glnb, ws, bst, cw, cb, clng, clnb, wout, gxa, wq, kb, vb, wo,
                     w1f, w3f, w2f):
    nb, seq, _ = x.shape
    tb = PROMPT_TILE
    tps = seq // tb
    last = nb * tps - 1
    n_cast = nb * tps
    r13, r2 = w1f.shape[0] // n_cast, w2f.shape[0] // n_cast
    assert w1f.shape == w3f.shape and w1f.shape[0] % n_cast == 0 and w2f.shape[0] % n_cast == 0
    assert r13 % 16 == 0 and r2 % 16 == 0
    hbm = pl.BlockSpec(memory_space=pl.ANY)
    tile = lambda s, lag: jnp.clip(s - lag, 0, last)
    full = lambda shape: pl.BlockSpec(shape, lambda s: (0,) * len(shape))
    rows = lambda lag: pl.BlockSpec((None, tb, D_MODEL), lambda s: (tile(s, lag) // tps, tile(s, lag) % tps, 0))
    kv_spec = pl.BlockSpec((None, N_MEM, D_MODEL), lambda s: (tile(s, 1) // tps, 0, 0))
    seq_out = lambda r, c: pl.BlockSpec((None, r, c), lambda s: (tile(s, 0) // tps, 0, 0))
    return pl.pallas_call(
        functools.partial(_prompt_kernel, tiles_per_seq=tps, n_cast_chunks=n_cast),
        grid=(nb * tps + 1,),
        in_specs=[rows(0), full((1, D_MODEL)), full((D_MODEL, IN_COLS)), full((1, GM_WIDTH)), full((1, GM_WIDTH)),
                  full((GM_HEADS, CHUNK, CHUNK)), full((CHUNK, GM_HEADS)), full((CONV_WIDTH, CV_WIDTH)),
                  full((1, CV_WIDTH)), full((1, CV_WIDTH)), full((1, CV_WIDTH)), full((D_MODEL, D_MODEL)),
                  full((1, D_MODEL)), full((D_MODEL, D_MODEL)), kv_spec, kv_spec, full((D_MODEL, D_MODEL)),
                  hbm, hbm, hbm],
        out_specs=[rows(1), seq_out(HALO, CV_WIDTH), seq_out(CHUNK, GM_WIDTH), hbm, hbm, hbm],
        out_shape=[jax.ShapeDtypeStruct((nb, seq, D_MODEL), F32),
                   jax.ShapeDtypeStruct((nb, HALO, CV_WIDTH), F32),
                   jax.ShapeDtypeStruct((nb, CHUNK, GM_WIDTH), F32),
                   jax.ShapeDtypeStruct(w1f.shape, BF16), jax.ShapeDtypeStruct(w3f.shape, BF16),
                   jax.ShapeDtypeStruct(w2f.shape, BF16)],
        scratch_shapes=[pltpu.VMEM((tb, IN_COLS), F32), pltpu.VMEM((HALO + tb, CV_WIDTH), F32),
                        pltpu.VMEM((tb, CV_WIDTH), F32), pltpu.VMEM((tb, D_MODEL), BF16),
                        pltpu.VMEM((tb, D_MODEL), BF16), pltpu.VMEM((tb, D_MODEL), F32),
                        pltpu.VMEM((tb, D_MODEL), F32),
                        pltpu.VMEM((2, r13, EXPERT_FF), F32), pltpu.VMEM((2, r13, EXPERT_FF), F32),
                        pltpu.VMEM((2, r2, D_MODEL), F32),
                        pltpu.VMEM((r13, EXPERT_FF), BF16), pltpu.VMEM((r13, EXPERT_FF), BF16),
                        pltpu.VMEM((r2, D_MODEL), BF16),
                        pltpu.SemaphoreType.DMA((2, 3)), pltpu.SemaphoreType.DMA((3,))],
        compiler_params=pltpu.CompilerParams(dimension_semantics=("arbitrary",), vmem_limit_bytes=VMEM_LIMIT),
        name="prompt_mix_attn",
    )(x, gmix, win, glng, glnb, ws, bst, cw, cb, clng, clnb, wout, gxa, wq, kb, vb, wo, w1f, w3f, w2f)


def _sample_mix_kernel(x_ref, gmix_ref, win_ref, glng_ref, glnb_ref, ws0_ref, bs0_ref, cache_ref, cw_ref, cb_ref,
                       clng_ref, clnb_ref, wout_ref, gxa_ref, wq_ref,
                       y_ref, glu_ref, v_ref, q_ref, ab_s):
    x = x_ref[...]
    z = _dot(_rms(x, gmix_ref[...]).astype(BF16), win_ref[...])
    for h in range(GM_HEADS):
        cu = slice(h * GM_HEAD_DIM, (h + 1) * GM_HEAD_DIM)
        cv = slice(GM_WIDTH + h * GM_HEAD_DIM, GM_WIDTH + (h + 1) * GM_HEAD_DIM)
        v = _ln(jax.nn.gelu(z[:, cv]), glng_ref[:, cu], glnb_ref[:, cu])
        v_ref[:, cu] = v
        ab_s[:, cu] = (jax.nn.gelu(z[:, cu]) * (v * ws0_ref[:, cu] + bs0_ref[:, cu])).astype(BF16)
    glu = z[:, 2 * GM_WIDTH:2 * GM_WIDTH + CV_WIDTH] * jax.nn.sigmoid(z[:, 2 * GM_WIDTH + CV_WIDTH:])
    glu_ref[...] = glu
    conv = glu * cw_ref[CONV_WIDTH - 1:CONV_WIDTH, :] + cb_ref[...]
    for k in range(CONV_WIDTH - 1):
        conv = conv + cache_ref[k] * cw_ref[k:k + 1, :]
    b = _ln(conv, clng_ref[...], clnb_ref[...])
    ab_s[:, GM_WIDTH:] = (b * jax.nn.sigmoid(b)).astype(BF16)
    y = x + _dot(ab_s[...], wout_ref[...])
    y_ref[...] = y
    q_ref[...] = _dot(_rms(y, gxa_ref[...]).astype(BF16), wq_ref[...]) * (XA_HEAD_DIM ** -0.5)


def _sample_mix(x, gmix, win, glng, glnb, ws0, bs0, cache_t, cw, cb, clng, clnb, wout, gxa, wq):
    ns = x.shape[0]
    return pl.pallas_call(
        _sample_mix_kernel,
        out_shape=[jax.ShapeDtypeStruct((ns, D_MODEL), F32), jax.ShapeDtypeStruct((ns, CV_WIDTH), F32),
                   jax.ShapeDtypeStruct((ns, GM_WIDTH), F32), jax.ShapeDtypeStruct((ns, D_MODEL), F32)],
        scratch_shapes=[pltpu.VMEM((ns, D_MODEL), BF16)],
        compiler_params=pltpu.CompilerParams(vmem_limit_bytes=VMEM_LIMIT),
        name="sample_mix",
    )(x, gmix, win, glng, glnb, ws0, bs0, cache_t, cw, cb, clng, clnb, wout, gxa, wq)


def _sample_attn_kernel(q_ref, k_ref, v_ref, o_ref):
    ones = jnp.ones((LANES, LANES), BF16)
    rows = N_MEM * SUBLANES
    for i in range(q_ref.shape[0]):
        prod = (k_ref[i] * q_ref[i][None]).reshape(rows, LANES).astype(BF16)
        part = _dot(prod, ones).reshape(N_MEM, SUBLANES, LANES)
        s = part + pltpu.roll(part, XA_HEADS, axis=1)
        e = jnp.exp(s - jnp.max(s, axis=0, keepdims=True))
        p = e / jnp.sum(e, axis=0, keepdims=True)
        o_ref[i] = jnp.sum(p * v_ref[i], axis=0)


def _sample_attn(q, k, v):
    ns = q.shape[0]
    nb = SAMPLE_ATTN_BLOCK
    halves = XA_HEAD_DIM // LANES
    assert halves * XA_HEADS == SUBLANES

    def split(a):
        lead = a.shape[:-2]
        a = a.reshape(*lead, XA_HEADS, halves, LANES)
        return jnp.swapaxes(a, -3, -2).reshape(*lead, SUBLANES, LANES)

    o = pl.pallas_call(
        _sample_attn_kernel,
        grid=(ns // nb,),
        in_specs=[pl.BlockSpec((nb, SUBLANES, LANES), lambda i: (i, 0, 0)),
                  pl.BlockSpec((nb, N_MEM, SUBLANES, LANES), lambda i: (i, 0, 0, 0)),
                  pl.BlockSpec((nb, N_MEM, SUBLANES, LANES), lambda i: (i, 0, 0, 0))],
        out_specs=pl.BlockSpec((nb, SUBLANES, LANES), lambda i: (i, 0, 0)),
        out_shape=jax.ShapeDtypeStruct((ns, SUBLANES, LANES), F32),
        compiler_params=pltpu.CompilerParams(dimension_semantics=("arbitrary",), vmem_limit_bytes=VMEM_LIMIT),
        name="sample_attn",
    )(split(q), split(k), split(v))
    o = jnp.swapaxes(o.reshape(ns, halves, XA_HEADS, LANES), 1, 2)
    return o.reshape(ns, XA_HEADS * XA_HEAD_DIM)


def _sample_proj_kernel(y_ref, o_ref, wo_ref, h_ref):
    h_ref[...] = y_ref[...] + _dot(o_ref[...].astype(BF16), wo_ref[...])


def _sample_proj(y, o, wo):
    return pl.pallas_call(
        _sample_proj_kernel,
        out_shape=jax.ShapeDtypeStruct(y.shape, F32),
        compiler_params=pltpu.CompilerParams(vmem_limit_bytes=VMEM_LIMIT),
        name="sample_proj",
    )(y, o, wo)


def _moe_kernel(h_ref, gffn_ref, wrt_ref, rb_ref, w1_ref, w3_ref, w2_ref, gfin_ref,
                o_ref, xt_s, u_s, gidx_s, slot_s, comb_s, act_s, p2_s, y2_s, cnt_s):
    i = pl.program_id(0)
    g = pl.program_id(1)
    rt = h_ref.shape[0]
    sb = act_s.shape[0]

    @pl.when(jnp.logical_and(i == 0, g == 0))
    def _():
        u_s[...] = (lax.broadcasted_iota(jnp.int32, (rt, rt), 0)
                    < lax.broadcasted_iota(jnp.int32, (rt, rt), 1)).astype(F32).astype(BF16)

    @pl.when(g == 0)
    def _route():
        h = h_ref[...]
        o_ref[...] = h
        xt = _rms(h, gffn_ref[...]).astype(BF16)
        xt_s[...] = xt
        lt = _dot_nt(wrt_ref[...], xt) + rb_ref[...]
        gl = [lt[k:k + 1, :] for k in range(N_GROUPS)]
        gmax = jnp.maximum(jnp.maximum(gl[0], gl[1]), jnp.maximum(gl[2], gl[3]))
        gidx = jnp.where(gl[0] == gmax, 0, jnp.where(gl[1] == gmax, 1, jnp.where(gl[2] == gmax, 2, 3)))
        gidx = gidx.astype(jnp.int32)
        sumexp = (jnp.exp(gl[0] - gmax) + jnp.exp(gl[1] - gmax)) + (jnp.exp(gl[2] - gmax) + jnp.exp(gl[3] - gmax))
        p_g = 1.0 / sumexp
        esel = lt[SUBLANES + 3 * EXPERTS_PER_GROUP:SUBLANES + 4 * EXPERTS_PER_GROUP, :]
        for k in (2, 1, 0):
            esel = jnp.where(gidx == k, lt[SUBLANES + k * EXPERTS_PER_GROUP:SUBLANES + (k + 1) * EXPERTS_PER_GROUP, :],
                             esel)
        eidx = lax.broadcasted_iota(jnp.int32, (EXPERTS_PER_GROUP, rt), 0)
        m1 = jnp.max(esel, axis=0, keepdims=True)
        i1 = jnp.min(jnp.where(esel == m1, eidx, EXPERTS_PER_GROUP), axis=0, keepdims=True)
        rest = jnp.where(eidx == i1, -jnp.inf, esel)
        m2 = jnp.max(rest, axis=0, keepdims=True)
        i2 = jnp.min(jnp.where(rest == m2, eidx, EXPERTS_PER_GROUP), axis=0, keepdims=True)
        t2 = jnp.exp(m2 - m1)
        den = 1.0 + t2
        w_top1 = (1.0 / den) * p_g
        w_top2 = (t2 / den) * p_g
        within = jnp.where(eidx == i1, w_top1, 0.0) + jnp.where(eidx == i2, w_top2, 0.0)
        c_hi = within.astype(BF16).astype(F32)
        r1 = within - c_hi
        c_mid = r1.astype(BF16).astype(F32)
        c_lo = (r1 - c_mid).astype(BF16).astype(F32)
        comb_s[0:8, :] = c_hi
        comb_s[8:16, :] = c_mid
        comb_s[16:24, :] = c_lo
        comb_s[24:32, :] = jnp.zeros((8, rt), F32)
        onehot = (eidx == gidx).astype(F32)
        rank = _dot(onehot.astype(BF16), u_s[...])
        slot_s[...] = jnp.sum(onehot * rank, axis=0, keepdims=True).astype(jnp.int32)
        gidx_s[...] = gidx
        for k in range(N_GROUPS):
            cnt_s[k] = jnp.sum(onehot[k:k + 1, :]).astype(jnp.int32)

    n_blk = (cnt_s[g] + sb - 1) // sb

    def sub_block(j, half):
        half_rows = slice(half * sb, (half + 1) * sb)
        rows = lax.broadcasted_iota(jnp.int32, (sb, rt), 0) + j * sb
        hit = jnp.logical_and(rows == slot_s[...], gidx_s[...] == g)
        p = jnp.where(hit, 1.0, 0.0).astype(BF16)
        p2_s[half_rows, :] = p
        xc = _dot(p, xt_s[...]).astype(BF16)
        cexp = _dot_nt(p, comb_s[...].astype(BF16))
        cw = (cexp[:, 0:8] + cexp[:, 8:16]) + cexp[:, 16:24]
        for e in range(EXPERTS_PER_GROUP):
            h1 = _dot(xc, w1_ref[e])
            h3 = _dot(xc, w3_ref[e])
            a = (h1 * jax.nn.sigmoid(h1)) * h3 * cw[:, e:e + 1]
            act_s[:, e * EXPERT_FF:(e + 1) * EXPERT_FF] = a.astype(BF16)
        y2_s[half_rows, :] = _dot(act_s[...], w2_ref[...]).astype(BF16)

    def body(jj, carry):
        sub_block(2 * jj, 0)

        @pl.when(2 * jj + 1 < n_blk)
        def _():
            sub_block(2 * jj + 1, 1)

        @pl.when(2 * jj + 1 >= n_blk)
        def _():
            p2_s[sb:, :] = jnp.zeros((sb, rt), BF16)
            y2_s[sb:, :] = jnp.zeros((sb, D_MODEL), BF16)

        o_ref[...] += _dot_tn(p2_s[...], y2_s[...])
        return carry

    lax.fori_loop(0, (n_blk + 1) // 2, body, 0)

    @pl.when(g == N_GROUPS - 1)
    def _():
        o_ref[...] = _rms(o_ref[...], gfin_ref[...])


def _moe(h, gffn, wrt, rbias, w1, w3, w2, gfin, *, tile):
    tokens = h.shape[0]
    sb = min(MOE_SUB, tile)
    ff = EXPERTS_PER_GROUP * EXPERT_FF
    full = lambda shape: pl.BlockSpec(shape, lambda i, g: (0,) * len(shape))
    row_spec = pl.BlockSpec((tile, D_MODEL), lambda i, g: (i, 0))
    return pl.pallas_call(
        _moe_kernel,
        grid=(tokens // tile, N_GROUPS),
        in_specs=[row_spec, full((1, D_MODEL)), full((ROUTER_ROWS, D_MODEL)), full((ROUTER_ROWS, 1)),
                  pl.BlockSpec((None, EXPERTS_PER_GROUP, D_MODEL, EXPERT_FF), lambda i, g: (g, 0, 0, 0)),
                  pl.BlockSpec((None, EXPERTS_PER_GROUP, D_MODEL, EXPERT_FF), lambda i, g: (g, 0, 0, 0)),
                  pl.BlockSpec((None, ff, D_MODEL), lambda i, g: (g, 0, 0)),
                  full((1, D_MODEL))],
        out_specs=row_spec,
        out_shape=jax.ShapeDtypeStruct((tokens, D_MODEL), F32),
        scratch_shapes=[pltpu.VMEM((tile, D_MODEL), BF16), pltpu.VMEM((tile, tile), BF16),
                        pltpu.VMEM((1, tile), jnp.int32), pltpu.VMEM((1, tile), jnp.int32),
                        pltpu.VMEM((4 * SUBLANES, tile), F32), pltpu.VMEM((sb, ff), BF16),
                        pltpu.VMEM((2 * sb, tile), BF16), pltpu.VMEM((2 * sb, D_MODEL), BF16),
                        pltpu.SMEM((N_GROUPS,), jnp.int32)],
        compiler_params=pltpu.CompilerParams(dimension_semantics=("arbitrary", "arbitrary"),
                                             vmem_limit_bytes=VMEM_LIMIT),
        name="moe",
    )(h, gffn, wrt, rbias, w1, w3, w2, gfin)


def kernel(x_prompt, x_sample, mem_prompt, cache_conv, cache_mem_k, cache_mem_v, norm_mix_g, w_in, gm_ln_g, gm_ln_b, gm_ws, gm_bs, conv_w, conv_b, cv_ln_g, cv_ln_b, w_out, norm_mem_g, norm_xa_g, xa_wq, xa_wk, xa_wv, xa_wo, norm_ffn_g, router_g, router_g_b, router_e, router_e_b, exp_w1, exp_w3, exp_w2, final_norm_g):
    depth = w_in.shape[0]
    assert depth == 1, "single-layer trunk"
    nb, seq, _ = x_prompt.shape
    ns = x_sample.shape[0]
    row = lambda a: a.reshape(1, -1)

    l = 0
    gmix, gxa, gffn, gmem = row(norm_mix_g[l]), row(norm_xa_g[l]), row(norm_ffn_g[l]), row(norm_mem_g[l])
    gfin = row(final_norm_g)
    win, wout = w_in[l].astype(BF16), w_out[l].astype(BF16)
    wq, wk, wv, wo = (w[l].astype(BF16) for w in (xa_wq, xa_wk, xa_wv, xa_wo))
    glng, glnb = row(gm_ln_g[l]), row(gm_ln_b[l])
    cw, cb, clng, clnb = conv_w[l], row(conv_b[l]), row(cv_ln_g[l]), row(cv_ln_b[l])
    ws, bst = gm_ws[l], gm_bs[l].T
    ws0 = jnp.repeat(gm_ws[l][:, 0, 0], GM_HEAD_DIM).reshape(1, GM_WIDTH)
    bs0 = jnp.repeat(gm_bs[l][:, 0], GM_HEAD_DIM).reshape(1, GM_WIDTH)

    n_exp = N_GROUPS * EXPERTS_PER_GROUP
    pad_g = SUBLANES - N_GROUPS
    pad_t = ROUTER_ROWS - SUBLANES - n_exp
    wrt = jnp.concatenate([router_g[l].T, jnp.zeros((pad_g, D_MODEL), F32),
                           router_e[l].reshape(D_MODEL, n_exp).T, jnp.zeros((pad_t, D_MODEL), F32)], axis=0).astype(BF16)
    rbias = jnp.concatenate([router_g_b[l], jnp.zeros((pad_g,), F32), router_e_b[l].reshape(n_exp),
                             jnp.zeros((pad_t,), F32)]).reshape(ROUTER_ROWS, 1)

    mk, mv, kb, vb = _memkv(mem_prompt.reshape(nb * N_MEM, D_MODEL), gmem, wk, wv)
    kb, vb = kb.reshape(nb, N_MEM, D_MODEL), vb.reshape(nb, N_MEM, D_MODEL)
    hp, conv_tail, gmv_p, w1, w3, w2 = _prompt_mix_attn(
        x_prompt, gmix, win, glng, glnb, ws, bst, cw, cb, clng, clnb, wout, gxa, wq, kb, vb, wo,
        exp_w1[l].reshape(-1, EXPERT_FF), exp_w3[l].reshape(-1, EXPERT_FF), exp_w2[l].reshape(-1, D_MODEL))
    w1 = w1.reshape(N_GROUPS, EXPERTS_PER_GROUP, D_MODEL, EXPERT_FF)
    w3 = w3.reshape(N_GROUPS, EXPERTS_PER_GROUP, D_MODEL, EXPERT_FF)
    w2 = w2.reshape(N_GROUPS, EXPERTS_PER_GROUP * EXPERT_FF, D_MODEL)
    y_prompt = _moe(hp.reshape(nb * seq, D_MODEL), gffn, wrt, rbias, w1, w3, w2, gfin, tile=MOE_TILE)

    cache_t = jnp.transpose(cache_conv[l], (1, 0, 2))
    ys, glu_s, gmv_s, q_s = _sample_mix(x_sample.reshape(ns, D_MODEL), gmix, win, glng, glnb, ws0, bs0,
                                        cache_t, cw, cb, clng, clnb, wout, gxa, wq)
    o_s = _sample_attn(q_s.reshape(ns, XA_HEADS, XA_HEAD_DIM), cache_mem_k[l], cache_mem_v[l])
    hs = _sample_proj(ys, o_s, wo)
    y_sample = _moe(hs, gffn, wrt, rbias, w1, w3, w2, gfin, tile=ns)

    conv_prompt = conv_tail[:, HALO - (CONV_WIDTH - 1):, :][None]
    conv_sample = jnp.transpose(jnp.concatenate([cache_t[1:], glu_s[None]], axis=0), (1, 0, 2))[None]
    return (y_prompt.reshape(nb, seq, D_MODEL), y_sample.reshape(ns, 1, D_MODEL), conv_prompt, conv_sample,
            gmv_p[None], gmv_s.reshape(1, ns, 1, GM_WIDTH),
            mk.reshape(1, nb, N_MEM, XA_HEADS, XA_HEAD_DIM), mv.reshape(1, nb, N_MEM, XA_HEADS, XA_HEAD_DIM))
```

```python
import functools

import jax
import jax.numpy as jnp
from jax import lax
from jax.experimental import pallas as pl
from jax.experimental.pallas import tpu as pltpu
from jax.experimental.pallas import tpu_sc as plsc

F32 = jnp.float32
BF16 = jnp.bfloat16

D_MODEL = 1024
GM_WIDTH = 512
CV_WIDTH = 512
GM_HEADS = 4
GM_HEAD_DIM = 128
CHUNK = 128
CONV_WIDTH = 31
IN_COLS = 2 * GM_WIDTH + 2 * CV_WIDTH
N_MEM = 256
XA_HEADS = 4
XA_HEAD_DIM = 256
N_GROUPS = 4
EXPERTS_PER_GROUP = 8
EXPERT_FF = 256
EPS = 1e-6

LANES = 128
SUBLANES = 8
HALO = 32
PROMPT_TILE = 512
MOE_TILE = 1024
MOE_SUB = 128
FFN_BLOCK = 256
SC_WINDOW = 128
SAMPLE_ATTN_BLOCK = 8
ROUTER_ROWS = 128
VMEM_LIMIT = 56 * 1024 * 1024


def _rms(x, g):
    return x * lax.rsqrt(jnp.mean(x * x, axis=-1, keepdims=True) + EPS) * g


def _ln(x, g, b):
    mu = jnp.mean(x, axis=-1, keepdims=True)
    xc = x - mu
    var = jnp.mean(xc * xc, axis=-1, keepdims=True)
    return xc * lax.rsqrt(var + EPS) * g + b


def _dot(a, b):
    return jnp.dot(a, b, preferred_element_type=F32)


def _dot_nt(a, b):
    return lax.dot_general(a, b, (((1,), (1,)), ((), ())), preferred_element_type=F32)


def _dot_tn(a, b):
    return lax.dot_general(a, b, (((0,), (0,)), ((), ())), preferred_element_type=F32)


def _memkv_kernel(mem_ref, g_ref, wk_ref, wv_ref, k_ref, v_ref, kb_ref, vb_ref):
    mn = _rms(mem_ref[...], g_ref[...]).astype(BF16)
    k = _dot(mn, wk_ref[...])
    v = _dot(mn, wv_ref[...])
    for h in range(XA_HEADS):
        cols = slice(h * XA_HEAD_DIM, (h + 1) * XA_HEAD_DIM)
        k_ref[:, h, :] = k[:, cols]
        v_ref[:, h, :] = v[:, cols]
    kb_ref[...] = k.astype(BF16)
    vb_ref[...] = v.astype(BF16)


def _memkv(mem2d, g, wk, wv):
    rows = mem2d.shape[0]
    tile = 512
    row_spec = pl.BlockSpec((tile, D_MODEL), lambda i: (i, 0))
    head_spec = pl.BlockSpec((tile, XA_HEADS, XA_HEAD_DIM), lambda i: (i, 0, 0))
    full = lambda shape: pl.BlockSpec(shape, lambda i: (0,) * len(shape))
    return pl.pallas_call(
        _memkv_kernel,
        grid=(rows // tile,),
        in_specs=[row_spec, full((1, D_MODEL)), full((D_MODEL, D_MODEL)), full((D_MODEL, D_MODEL))],
        out_specs=[head_spec, head_spec, row_spec, row_spec],
        out_shape=[jax.ShapeDtypeStruct((rows, XA_HEADS, XA_HEAD_DIM), F32),
                   jax.ShapeDtypeStruct((rows, XA_HEADS, XA_HEAD_DIM), F32),
                   jax.ShapeDtypeStruct((rows, D_MODEL), BF16), jax.ShapeDtypeStruct((rows, D_MODEL), BF16)],
        compiler_params=pltpu.CompilerParams(dimension_semantics=("arbitrary",), vmem_limit_bytes=VMEM_LIMIT),
        name="memkv",
    )(mem2d, g, wk, wv)


def _conv_block(g_s, cw_ref, start, lanes):
    assert start % SUBLANES == 0
    rows = CHUNK + HALO
    win = g_s[start:start + rows, lanes]
    off = HALO - (CONV_WIDTH - 1)
    acc = None
    for b in range(SUBLANES):
        shifted = pltpu.roll(win, rows - (off + b), axis=0) if off + b else win
        for k in range(b, CONV_WIDTH, SUBLANES):
            assert k - b + CHUNK + off + b <= rows
            term = shifted[k - b:k - b + CHUNK] * cw_ref[k:k + 1, lanes]
            acc = term if acc is None else acc + term
    return acc


def _cast_copies(src_refs, dst_refs, in_bufs, out_bufs, sem_in, sem_out, chunk, slot):
    copies_in, copies_out = [], []
    for i, (src, dst, ibuf, obuf) in enumerate(zip(src_refs, dst_refs, in_bufs, out_bufs)):
        rows = obuf.shape[0]
        start = pl.multiple_of(chunk * rows, rows)
        copies_in.append(pltpu.make_async_copy(src.at[pl.ds(start, rows), :], ibuf.at[slot], sem_in.at[slot, i]))
        copies_out.append(pltpu.make_async_copy(obuf, dst.at[pl.ds(start, rows), :], sem_out.at[i]))
    return copies_in, copies_out


def _prompt_kernel(x_ref, gmix_ref, win_ref, glng_ref, glnb_ref, ws_ref, bst_ref, cw_ref, cb_ref,
                   clng_ref, clnb_ref, wout_ref, gxa_ref, wq_ref, kb_ref, vb_ref, wo_ref,
                   w1f_ref, w3f_ref, w2f_ref,
                   h_ref, conv_ref, gmv_ref, w1b_ref, w3b_ref, w2b_ref,
                   z_s, g_s, c_s, ab_s, o_s, y_prev, y_cur, in1, in3, in2, st1, st3, st2, sem_in, sem_out,
                   *, tiles_per_seq, n_cast_chunks):
    s = pl.program_id(0)
    n_steps = pl.num_programs(0)
    tb = x_ref.shape[0]
    n_chunks = tb // CHUNK

    srcs, dsts = (w1f_ref, w3f_ref, w2f_ref), (w1b_ref, w3b_ref, w2b_ref)
    ins, sts = (in1, in3, in2), (st1, st3, st2)
    slot = s % 2
    chunk_of = lambda step: jnp.minimum(step, n_cast_chunks - 1)

    @pl.when(s == 0)
    def _():
        for c in _cast_copies(srcs, dsts, ins, sts, sem_in, sem_out, chunk_of(s), slot)[0]:
            c.start()

    @pl.when(s + 1 < n_steps)
    def _():
        for c in _cast_copies(srcs, dsts, ins, sts, sem_in, sem_out, chunk_of(s + 1), 1 - slot)[0]:
            c.start()

    cast_in, cast_out = _cast_copies(srcs, dsts, ins, sts, sem_in, sem_out, chunk_of(s), slot)
    for c in cast_in:
        c.wait()
    for ibuf, obuf in zip(ins, sts):
        obuf[...] = ibuf[slot].astype(BF16)
    for c in cast_out:
        c.start()

    @pl.when(s == 0)
    def _():
        y_prev[...] = jnp.zeros(y_prev.shape, F32)

    @pl.when(s % tiles_per_seq == 0)
    def _():
        g_s[0:HALO, :] = jnp.zeros((HALO, CV_WIDTH), F32)

    z_s[...] = _dot(_rms(x_ref[...], gmix_ref[...]).astype(BF16), win_ref[...])

    y = y_prev[...]
    qn = _rms(y, gxa_ref[...]).astype(BF16)
    q = (_dot(qn, wq_ref[...]) * (XA_HEAD_DIM ** -0.5)).astype(BF16)
    for h in range(XA_HEADS):
        cols = slice(h * XA_HEAD_DIM, (h + 1) * XA_HEAD_DIM)
        sc = _dot_nt(q[:, cols], kb_ref[:, cols])
        e = jnp.exp(sc - jnp.max(sc, axis=-1, keepdims=True))
        p = (e / jnp.sum(e, axis=-1, keepdims=True)).astype(BF16)
        o_s[:, cols] = _dot(p, vb_ref[:, cols]).astype(BF16)
    h_ref[...] = y + _dot(o_s[...], wo_ref[...])

    tri = (lax.broadcasted_iota(jnp.int32, (CHUNK, CHUNK), 0)
           >= lax.broadcasted_iota(jnp.int32, (CHUNK, CHUNK), 1))
    wm = [jnp.where(tri, ws_ref[h], 0.0).astype(BF16) for h in range(GM_HEADS)]

    for c in range(n_chunks):
        rows = slice(c * CHUNK, (c + 1) * CHUNK)
        for h in range(GM_HEADS):
            cu = slice(h * GM_HEAD_DIM, (h + 1) * GM_HEAD_DIM)
            cv = slice(GM_WIDTH + h * GM_HEAD_DIM, GM_WIDTH + (h + 1) * GM_HEAD_DIM)
            v = _ln(jax.nn.gelu(z_s[rows, cv]), glng_ref[:, cu], glnb_ref[:, cu])
            if c == n_chunks - 1:
                gmv_ref[:, cu] = v
            mixed = _dot(wm[h], v.astype(BF16)) + bst_ref[:, h:h + 1]
            ab_s[rows, cu] = (jax.nn.gelu(z_s[rows, cu]) * mixed).astype(BF16)
        ca = slice(2 * GM_WIDTH, 2 * GM_WIDTH + CV_WIDTH)
        cg = slice(2 * GM_WIDTH + CV_WIDTH, IN_COLS)
        g_s[HALO + c * CHUNK:HALO + (c + 1) * CHUNK, :] = z_s[rows, ca] * jax.nn.sigmoid(z_s[rows, cg])

    for c in range(n_chunks):
        rows = slice(c * CHUNK, (c + 1) * CHUNK)
        for cb in range(CV_WIDTH // LANES):
            lanes = slice(cb * LANES, (cb + 1) * LANES)
            c_s[rows, lanes] = _conv_block(g_s, cw_ref, c * CHUNK, lanes) + cb_ref[:, lanes]
        b = _ln(c_s[rows, :], clng_ref[...], clnb_ref[...])
        ab_s[rows, GM_WIDTH:] = (b * jax.nn.sigmoid(b)).astype(BF16)

    conv_ref[...] = g_s[tb:tb + HALO, :]
    g_s[0:HALO, :] = g_s[tb:tb + HALO, :]
    y_cur[...] = x_ref[...] + _dot(ab_s[...], wout_ref[...])
    y_prev[...] = y_cur[...]
    for c in cast_out:
        c.wait()


def _prompt_mix_attn(x, gmix, win, glng, glnb, ws, bst, cw, cb, clng, clnb, wout, gxa, wq, kb, vb, wo,
                     w1f, w3f, w2f):
    nb, seq, _ = x.shape
    tb = PROMPT_TILE
    tps = seq // tb
    last = nb * tps - 1
    n_cast = nb * tps
    r13, r2 = w1f.shape[0] // n_cast, w2f.shape[0] // n_cast
    assert w1f.shape == w3f.shape and w1f.shape[0] % n_cast == 0 and w2f.shape[0] % n_cast == 0
    assert r13 % 16 == 0 and r2 % 16 == 0
    hbm = pl.BlockSpec(memory_space=pl.ANY)
    tile = lambda s, lag: jnp.clip(s - lag, 0, last)
    full = lambda shape: pl.BlockSpec(shape, lambda s: (0,) * len(shape))
    rows = lambda lag: pl.BlockSpec((None, tb, D_MODEL), lambda s: (tile(s, lag) // tps, tile(s, lag) % tps, 0))
    kv_spec = pl.BlockSpec((None, N_MEM, D_MODEL), lambda s: (tile(s, 1) // tps, 0, 0))
    seq_out = lambda r, c: pl.BlockSpec((None, r, c), lambda s: (tile(s, 0) // tps, 0, 0))
    return pl.pallas_call(
        functools.partial(_prompt_kernel, tiles_per_seq=tps, n_cast_chunks=n_cast),
        grid=(nb * tps + 1,),
        in_specs=[rows(0), full((1, D_MODEL)), full((D_MODEL, IN_COLS)), full((1, GM_WIDTH)), full((1, GM_WIDTH)),
                  full((GM_HEADS, CHUNK, CHUNK)), full((CHUNK, GM_HEADS)), full((CONV_WIDTH, CV_WIDTH)),
                  full((1, CV_WIDTH)), full((1, CV_WIDTH)), full((1, CV_WIDTH)), full((D_MODEL, D_MODEL)),
                  full((1, D_MODEL)), full((D_MODEL, D_MODEL)), kv_spec, kv_spec, full((D_MODEL, D_MODEL)),
                  hbm, hbm, hbm],
        out_specs=[rows(1), seq_out(HALO, CV_WIDTH), seq_out(CHUNK, GM_WIDTH), hbm, hbm, hbm],
        out_shape=[jax.ShapeDtypeStruct((nb, seq, D_MODEL), F32),
                   jax.ShapeDtypeStruct((nb, HALO, CV_WIDTH), F32),
                   jax.ShapeDtypeStruct((nb, CHUNK, GM_WIDTH), F32),
                   jax.ShapeDtypeStruct(w1f.shape, BF16), jax.ShapeDtypeStruct(w3f.shape, BF16),
                   jax.ShapeDtypeStruct(w2f.shape, BF16)],
        scratch_shapes=[pltpu.VMEM((tb, IN_COLS), F32), pltpu.VMEM((HALO + tb, CV_WIDTH), F32),
                        pltpu.VMEM((tb, CV_WIDTH), F32), pltpu.VMEM((tb, D_MODEL), BF16),
                        pltpu.VMEM((tb, D_MODEL), BF16), pltpu.VMEM((tb, D_MODEL), F32),
                        pltpu.VMEM((tb, D_MODEL), F32),
                        pltpu.VMEM((2, r13, EXPERT_FF), F32), pltpu.VMEM((2, r13, EXPERT_FF), F32),
                        pltpu.VMEM((2, r2, D_MODEL), F32),
                        pltpu.VMEM((r13, EXPERT_FF), BF16), pltpu.VMEM((r13, EXPERT_FF), BF16),
                        pltpu.VMEM((r2, D_MODEL), BF16),
                        pltpu.SemaphoreType.DMA((2, 3)), pltpu.SemaphoreType.DMA((3,))],
        compiler_params=pltpu.CompilerParams(dimension_semantics=("arbitrary",), vmem_limit_bytes=VMEM_LIMIT),
        name="prompt_mix_attn",
    )(x, gmix, win, glng, glnb, ws, bst, cw, cb, clng, clnb, wout, gxa, wq, kb, vb, wo, w1f, w3f, w2f)


def _sample_mix_kernel(x_ref, gmix_ref, win_ref, glng_ref, glnb_ref, ws0_ref, bs0_ref, cache_ref, cw_ref, cb_ref,
                       clng_ref, clnb_ref, wout_ref, gxa_ref, wq_ref,
                       y_ref, glu_ref, v_ref, q_ref, ab_s):
    x = x_ref[...]
    z = _dot(_rms(x, gmix_ref[...]).astype(BF16), win_ref[...])
    for h in range(GM_HEADS):
        cu = slice(h * GM_HEAD_DIM, (h + 1) * GM_HEAD_DIM)
        cv = slice(GM_WIDTH + h * GM_HEAD_DIM, GM_WIDTH + (h + 1) * GM_HEAD_DIM)
        v = _ln(jax.nn.gelu(z[:, cv]), glng_ref[:, cu], glnb_ref[:, cu])
        v_ref[:, cu] = v
        ab_s[:, cu] = (jax.nn.gelu(z[:, cu]) * (v * ws0_ref[:, cu] + bs0_ref[:, cu])).astype(BF16)
    glu = z[:, 2 * GM_WIDTH:2 * GM_WIDTH + CV_WIDTH] * jax.nn.sigmoid(z[:, 2 * GM_WIDTH + CV_WIDTH:])
    glu_ref[...] = glu
    conv = glu * cw_ref[CONV_WIDTH - 1:CONV_WIDTH, :] + cb_ref[...]
    for k in range(CONV_WIDTH - 1):
        conv = conv + cache_ref[k] * cw_ref[k:k + 1, :]
    b = _ln(conv, clng_ref[...], clnb_ref[...])
    ab_s[:, GM_WIDTH:] = (b * jax.nn.sigmoid(b)).astype(BF16)
    y = x + _dot(ab_s[...], wout_ref[...])
    y_ref[...] = y
    q_ref[...] = _dot(_rms(y, gxa_ref[...]).astype(BF16), wq_ref[...]) * (XA_HEAD_DIM ** -0.5)


def _sample_mix(x, gmix, win, glng, glnb, ws0, bs0, cache_t, cw, cb, clng, clnb, wout, gxa, wq):
    ns = x.shape[0]
    return pl.pallas_call(
        _sample_mix_kernel,
        out_shape=[jax.ShapeDtypeStruct((ns, D_MODEL), F32), jax.ShapeDtypeStruct((ns, CV_WIDTH), F32),
                   jax.ShapeDtypeStruct((ns, GM_WIDTH), F32), jax.ShapeDtypeStruct((ns, D_MODEL), F32)],
        scratch_shapes=[pltpu.VMEM((ns, D_MODEL), BF16)],
        compiler_params=pltpu.CompilerParams(vmem_limit_bytes=VMEM_LIMIT),
        name="sample_mix",
    )(x, gmix, win, glng, glnb, ws0, bs0, cache_t, cw, cb, clng, clnb, wout, gxa, wq)


def _sample_attn_kernel(q_ref, k_ref, v_ref, o_ref):
    ones = jnp.ones((LANES, LANES), BF16)
    rows = N_MEM * SUBLANES
    for i in range(q_ref.shape[0]):
        prod = (k_ref[i] * q_ref[i][None]).reshape(rows, LANES).astype(BF16)
        part = _dot(prod, ones).reshape(N_MEM, SUBLANES, LANES)
        s = part + pltpu.roll(part, XA_HEADS, axis=1)
        e = jnp.exp(s - jnp.max(s, axis=0, keepdims=True))
        p = e / jnp.sum(e, axis=0, keepdims=True)
        o_ref[i] = jnp.sum(p * v_ref[i], axis=0)


def _sample_attn(q, k, v):
    ns = q.shape[0]
    nb = SAMPLE_ATTN_BLOCK
    halves = XA_HEAD_DIM // LANES
    assert halves * XA_HEADS == SUBLANES

    def split(a):
        lead = a.shape[:-2]
        a = a.reshape(*lead, XA_HEADS, halves, LANES)
        return jnp.swapaxes(a, -3, -2).reshape(*lead, SUBLANES, LANES)

    o = pl.pallas_call(
        _sample_attn_kernel,
        grid=(ns // nb,),
        in_specs=[pl.BlockSpec((nb, SUBLANES, LANES), lambda i: (i, 0, 0)),
                  pl.BlockSpec((nb, N_MEM, SUBLANES, LANES), lambda i: (i, 0, 0, 0)),
                  pl.BlockSpec((nb, N_MEM, SUBLANES, LANES), lambda i: (i, 0, 0, 0))],
        out_specs=pl.BlockSpec((nb, SUBLANES, LANES), lambda i: (i, 0, 0)),
        out_shape=jax.ShapeDtypeStruct((ns, SUBLANES, LANES), F32),
        compiler_params=pltpu.CompilerParams(dimension_semantics=("arbitrary",), vmem_limit_bytes=VMEM_LIMIT),
        name="sample_attn",
    )(split(q), split(k), split(v))
    o = jnp.swapaxes(o.reshape(ns, halves, XA_HEADS, LANES), 1, 2)
    return o.reshape(ns, XA_HEADS * XA_HEAD_DIM)


def _sample_proj_kernel(y_ref, o_ref, wo_ref, h_ref):
    h_ref[...] = y_ref[...] + _dot(o_ref[...].astype(BF16), wo_ref[...])


def _sample_proj(y, o, wo):
    return pl.pallas_call(
        _sample_proj_kernel,
        out_shape=jax.ShapeDtypeStruct(y.shape, F32),
        compiler_params=pltpu.CompilerParams(vmem_limit_bytes=VMEM_LIMIT),
        name="sample_proj",
    )(y, o, wo)


def _moe_kernel(h_ref, gffn_ref, wrt_ref, rb_ref, w1_ref, w3_ref, w2_ref, gfin_ref,
                o_ref, xt_s, u_s, gidx_s, slot_s, comb_s, act_s, p2_s, y2_s, cnt_s):
    i = pl.program_id(0)
    g = pl.program_id(1)
    rt = h_ref.shape[0]
    sb = act_s.shape[0]

    @pl.when(jnp.logical_and(i == 0, g == 0))
    def _():
        u_s[...] = (lax.broadcasted_iota(jnp.int32, (rt, rt), 0)
                    < lax.broadcasted_iota(jnp.int32, (rt, rt), 1)).astype(F32).astype(BF16)

    @pl.when(g == 0)
    def _route():
        h = h_ref[...]
        o_ref[...] = h
        xt = _rms(h, gffn_ref[...]).astype(BF16)
        xt_s[...] = xt
        lt = _dot_nt(wrt_ref[...], xt) + rb_ref[...]
        gl = [lt[k:k + 1, :] for k in range(N_GROUPS)]
        gmax = jnp.maximum(jnp.maximum(gl[0], gl[1]), jnp.maximum(gl[2], gl[3]))
        gidx = jnp.where(gl[0] == gmax, 0, jnp.where(gl[1] == gmax, 1, jnp.where(gl[2] == gmax, 2, 3)))
        gidx = gidx.astype(jnp.int32)
        sumexp = (jnp.exp(gl[0] - gmax) + jnp.exp(gl[1] - gmax)) + (jnp.exp(gl[2] - gmax) + jnp.exp(gl[3] - gmax))
        p_g = 1.0 / sumexp
        esel = lt[SUBLANES + 3 * EXPERTS_PER_GROUP:SUBLANES + 4 * EXPERTS_PER_GROUP, :]
        for k in (2, 1, 0):
            esel = jnp.where(gidx == k, lt[SUBLANES + k * EXPERTS_PER_GROUP:SUBLANES + (k + 1) * EXPERTS_PER_GROUP, :],
                             esel)
        eidx = lax.broadcasted_iota(jnp.int32, (EXPERTS_PER_GROUP, rt), 0)
        m1 = jnp.max(esel, axis=0, keepdims=True)
        i1 = jnp.min(jnp.where(esel == m1, eidx, EXPERTS_PER_GROUP), axis=0, keepdims=True)
        rest = jnp.where(eidx == i1, -jnp.inf, esel)
        m2 = jnp.max(rest, axis=0, keepdims=True)
        i2 = jnp.min(jnp.where(rest == m2, eidx, EXPERTS_PER_GROUP), axis=0, keepdims=True)
        t2 = jnp.exp(m2 - m1)
        den = 1.0 + t2
        w_top1 = (1.0 / den) * p_g
        w_top2 = (t2 / den) * p_g
        within = jnp.where(eidx == i1, w_top1, 0.0) + jnp.where(eidx == i2, w_top2, 0.0)
        c_hi = within.astype(BF16).astype(F32)
        r1 = within - c_hi
        c_mid = r1.astype(BF16).astype(F32)
        c_lo = (r1 - c_mid).astype(BF16).astype(F32)
        comb_s[0:8, :] = c_hi
        comb_s[8:16, :] = c_mid
        comb_s[16:24, :] = c_lo
        comb_s[24:32, :] = jnp.zeros((8, rt), F32)
        onehot = (eidx == gidx).astype(F32)
        rank = _dot(onehot.astype(BF16), u_s[...])
        slot_s[...] = jnp.sum(onehot * rank, axis=0, keepdims=True).astype(jnp.int32)
        gidx_s[...] = gidx
        for k in range(N_GROUPS):
            cnt_s[k] = jnp.sum(onehot[k:k + 1, :]).astype(jnp.int32)

    n_blk = (cnt_s[g] + sb - 1) // sb

    def sub_block(j, half):
        half_rows = slice(half * sb, (half + 1) * sb)
        rows = lax.broadcasted_iota(jnp.int32, (sb, rt), 0) + j * sb
        hit = jnp.logical_and(rows == slot_s[...], gidx_s[...] == g)
        p = jnp.where(hit, 1.0, 0.0).astype(BF16)
        p2_s[half_rows, :] = p
        xc = _dot(p, xt_s[...]).astype(BF16)
        cexp = _dot_nt(p, comb_s[...].astype(BF16))
        cw = (cexp[:, 0:8] + cexp[:, 8:16]) + cexp[:, 16:24]
        for e in range(EXPERTS_PER_GROUP):
            h1 = _dot(xc, w1_ref[e])
            h3 = _dot(xc, w3_ref[e])
            a = (h1 * jax.nn.sigmoid(h1)) * h3 * cw[:, e:e + 1]
            act_s[:, e * EXPERT_FF:(e + 1) * EXPERT_FF] = a.astype(BF16)
        y2_s[half_rows, :] = _dot(act_s[...], w2_ref[...]).astype(BF16)

    def body(jj, carry):
        sub_block(2 * jj, 0)

        @pl.when(2 * jj + 1 < n_blk)
        def _():
            sub_block(2 * jj + 1, 1)

        @pl.when(2 * jj + 1 >= n_blk)
        def _():
            p2_s[sb:, :] = jnp.zeros((sb, rt), BF16)
            y2_s[sb:, :] = jnp.zeros((sb, D_MODEL), BF16)

        o_ref[...] += _dot_tn(p2_s[...], y2_s[...])
        return carry

    lax.fori_loop(0, (n_blk + 1) // 2, body, 0)

    @pl.when(g == N_GROUPS - 1)
    def _():
        o_ref[...] = _rms(o_ref[...], gfin_ref[...])


def _moe(h, gffn, wrt, rbias, w1, w3, w2, gfin, *, tile):
    tokens = h.shape[0]
    sb = min(MOE_SUB, tile)
    ff = EXPERTS_PER_GROUP * EXPERT_FF
    full = lambda shape: pl.BlockSpec(shape, lambda i, g: (0,) * len(shape))
    row_spec = pl.BlockSpec((tile, D_MODEL), lambda i, g: (i, 0))
    return pl.pallas_call(
        _moe_kernel,
        grid=(tokens // tile, N_GROUPS),
        in_specs=[row_spec, full((1, D_MODEL)), full((ROUTER_ROWS, D_MODEL)), full((ROUTER_ROWS, 1)),
                  pl.BlockSpec((None, EXPERTS_PER_GROUP, D_MODEL, EXPERT_FF), lambda i, g: (g, 0, 0, 0)),
                  pl.BlockSpec((None, EXPERTS_PER_GROUP, D_MODEL, EXPERT_FF), lambda i, g: (g, 0, 0, 0)),
                  pl.BlockSpec((None, ff, D_MODEL), lambda i, g: (g, 0, 0)),
                  full((1, D_MODEL))],
        out_specs=row_spec,
        out_shape=jax.ShapeDtypeStruct((tokens, D_MODEL), F32),
        scratch_shapes=[pltpu.VMEM((tile, D_MODEL), BF16), pltpu.VMEM((tile, tile), BF16),
                        pltpu.VMEM((1, tile), jnp.int32), pltpu.VMEM((1, tile), jnp.int32),
                        pltpu.VMEM((4 * SUBLANES, tile), F32), pltpu.VMEM((sb, ff), BF16),
                        pltpu.VMEM((2 * sb, tile), BF16), pltpu.VMEM((2 * sb, D_MODEL), BF16),
                        pltpu.SMEM((N_GROUPS,), jnp.int32)],
        compiler_params=pltpu.CompilerParams(dimension_semantics=("arbitrary", "arbitrary"),
                                             vmem_limit_bytes=VMEM_LIMIT),
        name="moe",
    )(h, gffn, wrt, rbias, w1, w3, w2, gfin)


HALF = D_MODEL // 2
SUB_PER_ROW = HALF // LANES
N_EXPERTS = N_GROUPS * EXPERTS_PER_GROUP


def _pack_bf16_pairs(x):
    bits = pltpu.bitcast(x.astype(BF16).astype(F32), jnp.uint32)
    return (bits[:, HALF:] & jnp.uint32(0xFFFF0000)) | (bits[:, :HALF] >> 16)


def _unpack_bf16_pairs(w):
    lo = pltpu.bitcast(w << 16, F32)
    hi = pltpu.bitcast(w & jnp.uint32(0xFFFF0000), F32)
    return lo, hi


def _route_kernel(h_ref, gffn_ref, wrt_ref, rb_ref, xp_ref, e_ref, w_ref, r_ref, cnt_ref, u_s, run_s):
    i = pl.program_id(0)
    rt = h_ref.shape[0]

    @pl.when(i == 0)
    def _():
        u_s[...] = (lax.broadcasted_iota(jnp.int32, (rt, rt), 0)
                    < lax.broadcasted_iota(jnp.int32, (rt, rt), 1)).astype(F32).astype(BF16)
        run_s[...] = jnp.zeros(run_s.shape, F32)

    xt = _rms(h_ref[...], gffn_ref[...])
    xp_ref[...] = _pack_bf16_pairs(xt)
    lt = _dot_nt(wrt_ref[...], xt.astype(BF16)) + rb_ref[...]
    gl = [lt[k:k + 1, :] for k in range(N_GROUPS)]
    gmax = jnp.maximum(jnp.maximum(gl[0], gl[1]), jnp.maximum(gl[2], gl[3]))
    gidx = jnp.where(gl[0] == gmax, 0, jnp.where(gl[1] == gmax, 1, jnp.where(gl[2] == gmax, 2, 3)))
    gidx = gidx.astype(jnp.int32)
    sumexp = (jnp.exp(gl[0] - gmax) + jnp.exp(gl[1] - gmax)) + (jnp.exp(gl[2] - gmax) + jnp.exp(gl[3] - gmax))
    p_g = 1.0 / sumexp
    esel = lt[SUBLANES + 3 * EXPERTS_PER_GROUP:SUBLANES + 4 * EXPERTS_PER_GROUP, :]
    for k in (2, 1, 0):
        esel = jnp.where(gidx == k, lt[SUBLANES + k * EXPERTS_PER_GROUP:SUBLANES + (k + 1) * EXPERTS_PER_GROUP, :],
                         esel)
    eidx = lax.broadcasted_iota(jnp.int32, (EXPERTS_PER_GROUP, rt), 0)
    m1 = jnp.max(esel, axis=0, keepdims=True)
    i1 = jnp.min(jnp.where(esel == m1, eidx, EXPERTS_PER_GROUP), axis=0, keepdims=True)
    rest = jnp.where(eidx == i1, -jnp.inf, esel)
    m2 = jnp.max(rest, axis=0, keepdims=True)
    i2 = jnp.min(jnp.where(rest == m2, eidx, EXPERTS_PER_GROUP), axis=0, keepdims=True)
    t2 = jnp.exp(m2 - m1)
    den = 1.0 + t2
    w_ref[0:1, :] = (1.0 / den) * p_g
    w_ref[1:2, :] = (t2 / den) * p_g
    e1 = gidx * EXPERTS_PER_GROUP + i1
    e2 = gidx * EXPERTS_PER_GROUP + i2
    e_ref[0:1, :] = e1
    e_ref[1:2, :] = e2
    xid = lax.broadcasted_iota(jnp.int32, (N_EXPERTS, rt), 0)
    oh1 = (xid == e1).astype(F32)
    oh2 = (xid == e2).astype(F32)
    both = oh1 + oh2
    before = _dot(both.astype(BF16), u_s[...]) + run_s[:, 0:1]
    r_ref[0:1, :] = jnp.sum(oh1 * before, axis=0, keepdims=True).astype(jnp.int32)
    r_ref[1:2, :] = jnp.sum(oh2 * before, axis=0, keepdims=True).astype(jnp.int32)
    run_s[...] = run_s[...] + jnp.sum(both, axis=1, keepdims=True)
    cnt_ref[...] = run_s[...].astype(jnp.int32)


def _route(h, gffn, wrt, rbias, *, tile):
    tokens = h.shape[0]
    full = lambda shape: pl.BlockSpec(shape, lambda i: (0,) * len(shape))
    lanes = lambda rows: pl.BlockSpec((rows, tile), lambda i: (0, i))
    return pl.pallas_call(
        _route_kernel,
        grid=(tokens // tile,),
        in_specs=[pl.BlockSpec((tile, D_MODEL), lambda i: (i, 0)), full((1, D_MODEL)),
                  full((ROUTER_ROWS, D_MODEL)), full((ROUTER_ROWS, 1))],
        out_specs=[pl.BlockSpec((tile, HALF), lambda i: (i, 0)), lanes(2), lanes(2), lanes(2),
                   full((N_EXPERTS, LANES))],
        out_shape=[jax.ShapeDtypeStruct((tokens, HALF), jnp.uint32), jax.ShapeDtypeStruct((2, tokens), jnp.int32),
                   jax.ShapeDtypeStruct((2, tokens), F32), jax.ShapeDtypeStruct((2, tokens), jnp.int32),
                   jax.ShapeDtypeStruct((N_EXPERTS, LANES), jnp.int32)],
        scratch_shapes=[pltpu.VMEM((tile, tile), BF16), pltpu.VMEM((N_EXPERTS, LANES), F32)],
        compiler_params=pltpu.CompilerParams(dimension_semantics=("arbitrary",), vmem_limit_bytes=VMEM_LIMIT),
        name="moe_route",
    )(h, gffn, wrt, rbias)


def _sub_rows(a):
    rows = a.shape[0]
    a = a.reshape(rows // SUBLANES, SUBLANES, SUB_PER_ROW, LANES)
    return jnp.swapaxes(a, 1, 2).reshape(rows * SUB_PER_ROW, LANES)


def _from_sub_rows(a):
    rows = a.shape[0] // SUB_PER_ROW
    a = a.reshape(rows // SUBLANES, SUB_PER_ROW, SUBLANES, LANES)
    return jnp.swapaxes(a, 1, 2).reshape(rows, HALF)


def _sub_row_index(row_of_token):
    tokens = row_of_token.shape[0]
    r = jnp.broadcast_to(row_of_token.reshape(tokens // SUBLANES, 1, SUBLANES),
                         (tokens // SUBLANES, SUB_PER_ROW, SUBLANES))
    j = jnp.arange(SUB_PER_ROW, dtype=jnp.int32).reshape(1, SUB_PER_ROW, 1)
    idx = (r // SUBLANES) * (SUBLANES * SUB_PER_ROW) + j * SUBLANES + r % SUBLANES
    return idx.reshape(1, tokens * SUB_PER_ROW)


def _sc_mesh():
    return plsc.VectorSubcoreMesh(core_axis_name="core", subcore_axis_name="subcore")


def _sc_scatter_two(x_sub, idx_a, idx_b, n_out):
    n_in = x_sub.shape[0]

    @pl.kernel(out_type=jax.ShapeDtypeStruct((n_out, LANES), x_sub.dtype), mesh=_sc_mesh(), scratch_types=[])
    def scatter(x_hbm, a_hbm, b_hbm, o_hbm):
        def body(x_vmem, a_vmem, b_vmem):
            pltpu.sync_copy(x_vmem, o_hbm.at[a_vmem.at[0]])
            pltpu.sync_copy(x_vmem, o_hbm.at[b_vmem.at[0]])

        pltpu.emit_pipeline(
            body, grid=(n_in // SC_WINDOW,),
            in_specs=[pl.BlockSpec((SC_WINDOW, LANES), lambda i: (i, 0)),
                      pl.BlockSpec((1, SC_WINDOW), lambda i: (0, i)),
                      pl.BlockSpec((1, SC_WINDOW), lambda i: (0, i))],
            out_specs=[],
            core_axis_name=("core", "subcore"), dimension_semantics=(pltpu.PARALLEL,),
        )(x_hbm, a_hbm, b_hbm)

    return scatter(x_sub, idx_a, idx_b)


def _sc_gather(table, idx):
    n_out = idx.shape[1]

    @pl.kernel(out_type=jax.ShapeDtypeStruct((n_out, LANES), table.dtype), mesh=_sc_mesh())
    def gather(t_hbm, i_hbm, o_hbm):
        def body(i_vmem, o_vmem):
            pltpu.sync_copy(t_hbm.at[i_vmem.at[0]], o_vmem)

        pltpu.emit_pipeline(
            body, grid=(n_out // SC_WINDOW,),
            in_specs=[pl.BlockSpec((1, SC_WINDOW), lambda i: (0, i))],
            out_specs=[pl.BlockSpec((SC_WINDOW, LANES), lambda i: (i, 0))],
            core_axis_name=("core", "subcore"), dimension_semantics=(pltpu.PARALLEL,),
        )(i_hbm, o_hbm)

    return gather(table, idx)


def _expert_ffn_kernel(blk_e_ref, n_valid_ref, x_ref, w1_ref, w3_ref, w2_ref, y_ref):
    del blk_e_ref

    @pl.when(pl.program_id(0) < n_valid_ref[0])
    def _():
        lo, hi = _unpack_bf16_pairs(x_ref[...])
        xc = jnp.concatenate([lo.astype(BF16), hi.astype(BF16)], axis=1)
        h1 = _dot(xc, w1_ref[...])
        h3 = _dot(xc, w3_ref[...])
        act = ((h1 * jax.nn.sigmoid(h1)) * h3).astype(BF16)
        y_ref[...] = _pack_bf16_pairs(_dot(act, w2_ref[...]))


def _expert_ffn(xs, blk_e, n_valid, w1, w3, w2):
    rows = xs.shape[0]
    n_blocks = rows // FFN_BLOCK
    live = lambda b, be, nv: jnp.minimum(b, nv[0] - 1)
    grid_spec = pltpu.PrefetchScalarGridSpec(
        num_scalar_prefetch=2, grid=(n_blocks,),
        in_specs=[pl.BlockSpec((FFN_BLOCK, HALF), lambda b, be, nv: (live(b, be, nv), 0)),
                  pl.BlockSpec((None, D_MODEL, EXPERT_FF), lambda b, be, nv: (be[b], 0, 0)),
                  pl.BlockSpec((None, D_MODEL, EXPERT_FF), lambda b, be, nv: (be[b], 0, 0)),
                  pl.BlockSpec((None, EXPERT_FF, D_MODEL), lambda b, be, nv: (be[b], 0, 0))],
        out_specs=pl.BlockSpec((FFN_BLOCK, HALF), lambda b, be, nv: (live(b, be, nv), 0)))
    return pl.pallas_call(
        _expert_ffn_kernel, grid_spec=grid_spec,
        out_shape=jax.ShapeDtypeStruct((rows, HALF), jnp.uint32),
        compiler_params=pltpu.CompilerParams(dimension_semantics=("arbitrary",), vmem_limit_bytes=VMEM_LIMIT),
        name="moe_ffn",
    )(blk_e, n_valid, xs, w1, w3, w2)


def _combine_kernel(h_ref, ya_ref, yb_ref, wt_ref, gfin_ref, o_ref):
    a_lo, a_hi = _unpack_bf16_pairs(ya_ref[...])
    b_lo, b_hi = _unpack_bf16_pairs(yb_ref[...])
    wa = wt_ref[:, 0:1]
    wb = wt_ref[:, 1:2]
    o_ref[:, :HALF] = h_ref[:, :HALF] + (wa * a_lo + wb * b_lo)
    o_ref[:, HALF:] = h_ref[:, HALF:] + (wa * a_hi + wb * b_hi)
    o_ref[...] = _rms(o_ref[...], gfin_ref[...])


def _combine(h, ya, yb, wt, gfin, *, tile):
    tokens = h.shape[0]
    return pl.pallas_call(
        _combine_kernel,
        grid=(tokens // tile,),
        in_specs=[pl.BlockSpec((tile, D_MODEL), lambda i: (i, 0)), pl.BlockSpec((tile, HALF), lambda i: (i, 0)),
                  pl.BlockSpec((tile, HALF), lambda i: (i, 0)), pl.BlockSpec((tile, 2), lambda i: (i, 0)),
                  pl.BlockSpec((1, D_MODEL), lambda i: (0, 0))],
        out_specs=pl.BlockSpec((tile, D_MODEL), lambda i: (i, 0)),
        out_shape=jax.ShapeDtypeStruct((tokens, D_MODEL), F32),
        compiler_params=pltpu.CompilerParams(dimension_semantics=("arbitrary",), vmem_limit_bytes=VMEM_LIMIT),
        name="moe_combine",
    )(h, ya, yb, wt, gfin)


def _moe_sorted(h, gffn, wrt, rbias, w1, w3, w2, gfin):
    tokens = h.shape[0]
    xp, e12, w12, r12, cnt = _route(h, gffn, wrt, rbias, tile=MOE_TILE)
    count = cnt[:, 0]
    padded = (count + FFN_BLOCK - 1) // FFN_BLOCK * FFN_BLOCK
    end = jnp.cumsum(padded)
    start = end - padded
    n_rows = 2 * tokens + N_EXPERTS * FFN_BLOCK
    n_blocks = n_rows // FFN_BLOCK
    n_valid = (end[-1:] // FFN_BLOCK).astype(jnp.int32)
    first_row = jnp.minimum(jnp.arange(n_blocks, dtype=jnp.int32), n_valid - 1) * FFN_BLOCK
    blk_e = jnp.sum((end[None, :] <= first_row[:, None]).astype(jnp.int32), axis=1)
    experts = jnp.arange(N_EXPERTS, dtype=jnp.int32)
    start_of = jnp.sum(jnp.where(e12[:, :, None] == experts, start.astype(jnp.int32), 0), axis=-1)
    rows_ab = start_of + r12
    idx_a, idx_b = _sub_row_index(rows_ab[0]), _sub_row_index(rows_ab[1])
    xs = _from_sub_rows(_sc_scatter_two(_sub_rows(xp), idx_a, idx_b, n_rows * SUB_PER_ROW))
    ys = _sub_rows(_expert_ffn(xs, blk_e, n_valid, w1, w3, w2))
    ya = _from_sub_rows(_sc_gather(ys, idx_a))
    yb = _from_sub_rows(_sc_gather(ys, idx_b))
    return _combine(h, ya, yb, w12.T, gfin, tile=MOE_TILE)


def kernel(x_prompt, x_sample, mem_prompt, cache_conv, cache_mem_k, cache_mem_v, norm_mix_g, w_in, gm_ln_g, gm_ln_b, gm_ws, gm_bs, conv_w, conv_b, cv_ln_g, cv_ln_b, w_out, norm_mem_g, norm_xa_g, xa_wq, xa_wk, xa_wv, xa_wo, norm_ffn_g, router_g, router_g_b, router_e, router_e_b, exp_w1, exp_w3, exp_w2, final_norm_g):
    depth = w_in.shape[0]
    assert depth == 1, "single-layer trunk"
    nb, seq, _ = x_prompt.shape
    ns = x_sample.shape[0]
    row = lambda a: a.reshape(1, -1)

    l = 0
    gmix, gxa, gffn, gmem = row(norm_mix_g[l]), row(norm_xa_g[l]), row(norm_ffn_g[l]), row(norm_mem_g[l])
    gfin = row(final_norm_g)
    win, wout = w_in[l].astype(BF16), w_out[l].astype(BF16)
    wq, wk, wv, wo = (w[l].astype(BF16) for w in (xa_wq, xa_wk, xa_wv, xa_wo))
    glng, glnb = row(gm_ln_g[l]), row(gm_ln_b[l])
    cw, cb, clng, clnb = conv_w[l], row(conv_b[l]), row(cv_ln_g[l]), row(cv_ln_b[l])
    ws, bst = gm_ws[l], gm_bs[l].T
    ws0 = jnp.repeat(gm_ws[l][:, 0, 0], GM_HEAD_DIM).reshape(1, GM_WIDTH)
    bs0 = jnp.repeat(gm_bs[l][:, 0], GM_HEAD_DIM).reshape(1, GM_WIDTH)

    n_exp = N_GROUPS * EXPERTS_PER_GROUP
    pad_g = SUBLANES - N_GROUPS
    pad_t = ROUTER_ROWS - SUBLANES - n_exp
    wrt = jnp.concatenate([router_g[l].T, jnp.zeros((pad_g, D_MODEL), F32),
                           router_e[l].reshape(D_MODEL, n_exp).T, jnp.zeros((pad_t, D_MODEL), F32)], axis=0).astype(BF16)
    rbias = jnp.concatenate([router_g_b[l], jnp.zeros((pad_g,), F32), router_e_b[l].reshape(n_exp),
                             jnp.zeros((pad_t,), F32)]).reshape(ROUTER_ROWS, 1)

    mk, mv, kb, vb = _memkv(mem_prompt.reshape(nb * N_MEM, D_MODEL), gmem, wk, wv)
    kb, vb = kb.reshape(nb, N_MEM, D_MODEL), vb.reshape(nb, N_MEM, D_MODEL)
    hp, conv_tail, gmv_p, w1, w3, w2 = _prompt_mix_attn(
        x_prompt, gmix, win, glng, glnb, ws, bst, cw, cb, clng, clnb, wout, gxa, wq, kb, vb, wo,
        exp_w1[l].reshape(-1, EXPERT_FF), exp_w3[l].reshape(-1, EXPERT_FF), exp_w2[l].reshape(-1, D_MODEL))
    y_prompt = _moe_sorted(hp.reshape(nb * seq, D_MODEL), gffn, wrt, rbias,
                           w1.reshape(N_EXPERTS, D_MODEL, EXPERT_FF), w3.reshape(N_EXPERTS, D_MODEL, EXPERT_FF),
                           w2.reshape(N_EXPERTS, EXPERT_FF, D_MODEL), gfin)
    w1 = w1.reshape(N_GROUPS, EXPERTS_PER_GROUP, D_MODEL, EXPERT_FF)
    w3 = w3.reshape(N_GROUPS, EXPERTS_PER_GROUP, D_MODEL, EXPERT_FF)
    w2 = w2.reshape(N_GROUPS, EXPERTS_PER_GROUP * EXPERT_FF, D_MODEL)

    cache_t = jnp.transpose(cache_conv[l], (1, 0, 2))
    ys, glu_s, gmv_s, q_s = _sample_mix(x_sample.reshape(ns, D_MODEL), gmix, win, glng, glnb, ws0, bs0,
                                        cache_t, cw, cb, clng, clnb, wout, gxa, wq)
    o_s = _sample_attn(q_s.reshape(ns, XA_HEADS, XA_HEAD_DIM), cache_mem_k[l], cache_mem_v[l])
    hs = _sample_proj(ys, o_s, wo)
    y_sample = _moe(hs, gffn, wrt, rbias, w1, w3, w2, gfin, tile=ns)

    conv_prompt = conv_tail[:, HALO - (CONV_WIDTH - 1):, :][None]
    conv_sample = jnp.transpose(jnp.concatenate([cache_t[1:], glu_s[None]], axis=0), (1, 0, 2))[None]
    return (y_prompt.reshape(nb, seq, D_MODEL), y_sample.reshape(ns, 1, D_MODEL), conv_prompt, conv_sample,
            gmv_p[None], gmv_s.reshape(1, ns, 1, GM_WIDTH),
            mk.reshape(1, nb, N_MEM, XA_HEADS, XA_HEAD_DIM), mv.reshape(1, nb, N_MEM, XA_HEADS, XA_HEAD_DIM))
```

```python
import functools

import jax
import jax.numpy as jnp
from jax import lax
from jax.experimental import pallas as pl
from jax.experimental.pallas import tpu as pltpu
from jax.experimental.pallas import tpu_sc as plsc

F32 = jnp.float32
BF16 = jnp.bfloat16

D_MODEL = 1024
GM_WIDTH = 512
CV_WIDTH = 512
GM_HEADS = 4
GM_HEAD_DIM = 128
CHUNK = 128
CONV_WIDTH = 31
IN_COLS = 2 * GM_WIDTH + 2 * CV_WIDTH
N_MEM = 256
XA_HEADS = 4
XA_HEAD_DIM = 256
N_GROUPS = 4
EXPERTS_PER_GROUP = 8
EXPERT_FF = 256
EPS = 1e-6

LANES = 128
SUBLANES = 8
HALO = 32
PROMPT_TILE = 512
MOE_TILE = 1024
MOE_SUB = 128
FFN_BLOCK = 512
SC_WINDOW = 128
SAMPLE_ATTN_BLOCK = 8
ROUTER_ROWS = 128
VMEM_LIMIT = 56 * 1024 * 1024


def _rms(x, g):
    return x * lax.rsqrt(jnp.mean(x * x, axis=-1, keepdims=True) + EPS) * g


def _ln(x, g, b):
    mu = jnp.mean(x, axis=-1, keepdims=True)
    xc = x - mu
    var = jnp.mean(xc * xc, axis=-1, keepdims=True)
    return xc * lax.rsqrt(var + EPS) * g + b


def _dot(a, b):
    return jnp.dot(a, b, preferred_element_type=F32)


def _dot_nt(a, b):
    return lax.dot_general(a, b, (((1,), (1,)), ((), ())), preferred_element_type=F32)


def _dot_tn(a, b):
    return lax.dot_general(a, b, (((0,), (0,)), ((), ())), preferred_element_type=F32)


def _memkv_kernel(mem_ref, g_ref, wk_ref, wv_ref, k_ref, v_ref, kb_ref, vb_ref):
    mn = _rms(mem_ref[...], g_ref[...]).astype(BF16)
    k = _dot(mn, wk_ref[...])
    v = _dot(mn, wv_ref[...])
    for h in range(XA_HEADS):
        cols = slice(h * XA_HEAD_DIM, (h + 1) * XA_HEAD_DIM)
        k_ref[:, h, :] = k[:, cols]
        v_ref[:, h, :] = v[:, cols]
    kb_ref[...] = k.astype(BF16)
    vb_ref[...] = v.astype(BF16)


def _memkv(mem2d, g, wk, wv):
    rows = mem2d.shape[0]
    tile = 512
    row_spec = pl.BlockSpec((tile, D_MODEL), lambda i: (i, 0))
    head_spec = pl.BlockSpec((tile, XA_HEADS, XA_HEAD_DIM), lambda i: (i, 0, 0))
    full = lambda shape: pl.BlockSpec(shape, lambda i: (0,) * len(shape))
    return pl.pallas_call(
        _memkv_kernel,
        grid=(rows // tile,),
        in_specs=[row_spec, full((1, D_MODEL)), full((D_MODEL, D_MODEL)), full((D_MODEL, D_MODEL))],
        out_specs=[head_spec, head_spec, row_spec, row_spec],
        out_shape=[jax.ShapeDtypeStruct((rows, XA_HEADS, XA_HEAD_DIM), F32),
                   jax.ShapeDtypeStruct((rows, XA_HEADS, XA_HEAD_DIM), F32),
                   jax.ShapeDtypeStruct((rows, D_MODEL), BF16), jax.ShapeDtypeStruct((rows, D_MODEL), BF16)],
        compiler_params=pltpu.CompilerParams(dimension_semantics=("arbitrary",), vmem_limit_bytes=VMEM_LIMIT),
        name="memkv",
    )(mem2d, g, wk, wv)


def _conv_block(g_s, cw_ref, start, lanes):
    assert start % SUBLANES == 0
    rows = CHUNK + HALO
    win = g_s[start:start + rows, lanes]
    off = HALO - (CONV_WIDTH - 1)
    acc = None
    for b in range(SUBLANES):
        shifted = pltpu.roll(win, rows - (off + b), axis=0) if off + b else win
        for k in range(b, CONV_WIDTH, SUBLANES):
            assert k - b + CHUNK + off + b <= rows
            term = shifted[k - b:k - b + CHUNK] * cw_ref[k:k + 1, lanes]
            acc = term if acc is None else acc + term
    return acc


def _cast_copies(src_refs, dst_refs, in_bufs, out_bufs, sem_in, sem_out, chunk, slot):
    copies_in, copies_out = [], []
    for i, (src, dst, ibuf, obuf) in enumerate(zip(src_refs, dst_refs, in_bufs, out_bufs)):
        rows = obuf.shape[0]
        start = pl.multiple_of(chunk * rows, rows)
        copies_in.append(pltpu.make_async_copy(src.at[pl.ds(start, rows), :], ibuf.at[slot], sem_in.at[slot, i]))
        copies_out.append(pltpu.make_async_copy(obuf, dst.at[pl.ds(start, rows), :], sem_out.at[i]))
    return copies_in, copies_out


def _prompt_kernel(x_ref, gmix_ref, win_ref, glng_ref, glnb_ref, ws_ref, bst_ref, cw_ref, cb_ref,
                   clng_ref, clnb_ref, wout_ref, gxa_ref, wq_ref, kb_ref, vb_ref, wo_ref,
                   w1f_ref, w3f_ref, w2f_ref,
                   h_ref, conv_ref, gmv_ref, w1b_ref, w3b_ref, w2b_ref,
                   z_s, g_s, c_s, ab_s, o_s, y_prev, y_cur, in1, in3, in2, st1, st3, st2, sem_in, sem_out,
                   *, tiles_per_seq, n_cast_chunks):
    s = pl.program_id(0)
    n_steps = pl.num_programs(0)
    tb = x_ref.shape[0]
    n_chunks = tb // CHUNK

    srcs, dsts = (w1f_ref, w3f_ref, w2f_ref), (w1b_ref, w3b_ref, w2b_ref)
    ins, sts = (in1, in3, in2), (st1, st3, st2)
    slot = s % 2
    chunk_of = lambda step: jnp.minimum(step, n_cast_chunks - 1)

    @pl.when(s == 0)
    def _():
        for c in _cast_copies(srcs, dsts, ins, sts, sem_in, sem_out, chunk_of(s), slot)[0]:
            c.start()

    @pl.when(s + 1 < n_steps)
    def _():
        for c in _cast_copies(srcs, dsts, ins, sts, sem_in, sem_out, chunk_of(s + 1), 1 - slot)[0]:
            c.start()

    cast_in, cast_out = _cast_copies(srcs, dsts, ins, sts, sem_in, sem_out, chunk_of(s), slot)
    for c in cast_in:
        c.wait()
    for ibuf, obuf in zip(ins, sts):
        obuf[...] = ibuf[slot].astype(BF16)
    for c in cast_out:
        c.start()

    @pl.when(s == 0)
    def _():
        y_prev[...] = jnp.zeros(y_prev.shape, F32)

    @pl.when(s % tiles_per_seq == 0)
    def _():
        g_s[0:HALO, :] = jnp.zeros((HALO, CV_WIDTH), F32)

    z_s[...] = _dot(_rms(x_ref[...], gmix_ref[...]).astype(BF16), win_ref[...])

    y = y_prev[...]
    qn = _rms(y, gxa_ref[...]).astype(BF16)
    q = (_dot(qn, wq_ref[...]) * (XA_HEAD_DIM ** -0.5)).astype(BF16)
    for h in range(XA_HEADS):
        cols = slice(h * XA_HEAD_DIM, (h + 1) * XA_HEAD_DIM)
        sc = _dot_nt(q[:, cols], kb_ref[:, cols])
        e = jnp.exp(sc - jnp.max(sc, axis=-1, keepdims=True))
        p = (e / jnp.sum(e, axis=-1, keepdims=True)).astype(BF16)
        o_s[:, cols] = _dot(p, vb_ref[:, cols]).astype(BF16)
    h_ref[...] = y + _dot(o_s[...], wo_ref[...])

    tri = (lax.broadcasted_iota(jnp.int32, (CHUNK, CHUNK), 0)
           >= lax.broadcasted_iota(jnp.int32, (CHUNK, CHUNK), 1))
    wm = [jnp.where(tri, ws_ref[h], 0.0).astype(BF16) for h in range(GM_HEADS)]

    for c in range(n_chunks):
        rows = slice(c * CHUNK, (c + 1) * CHUNK)
        for h in range(GM_HEADS):
            cu = slice(h * GM_HEAD_DIM, (h + 1) * GM_HEAD_DIM)
            cv = slice(GM_WIDTH + h * GM_HEAD_DIM, GM_WIDTH + (h + 1) * GM_HEAD_DIM)
            v = _ln(jax.nn.gelu(z_s[rows, cv]), glng_ref[:, cu], glnb_ref[:, cu])
            if c == n_chunks - 1:
                gmv_ref[:, cu] = v
            mixed = _dot(wm[h], v.astype(BF16)) + bst_ref[:, h:h + 1]
            ab_s[rows, cu] = (jax.nn.gelu(z_s[rows, cu]) * mixed).astype(BF16)
        ca = slice(2 * GM_WIDTH, 2 * GM_WIDTH + CV_WIDTH)
        cg = slice(2 * GM_WIDTH + CV_WIDTH, IN_COLS)
        g_s[HALO + c * CHUNK:HALO + (c + 1) * CHUNK, :] = z_s[rows, ca] * jax.nn.sigmoid(z_s[rows, cg])

    for c in range(n_chunks):
        rows = slice(c * CHUNK, (c + 1) * CHUNK)
        for cb in range(CV_WIDTH // LANES):
            lanes = slice(cb * LANES, (cb + 1) * LANES)
            c_s[rows, lanes] = _conv_block(g_s, cw_ref, c * CHUNK, lanes) + cb_ref[:, lanes]
        b = _ln(c_s[rows, :], clng_ref[...], clnb_ref[...])
        ab_s[rows, GM_WIDTH:] = (b * jax.nn.sigmoid(b)).astype(BF16)

    conv_ref[...] = g_s[tb:tb + HALO, :]
    g_s[0:HALO, :] = g_s[tb:tb + HALO, :]
    y_cur[...] = x_ref[...] + _dot(ab_s[...], wout_ref[...])
    y_prev[...] = y_cur[...]
    for c in cast_out:
        c.wait()


def _prompt_mix_attn(x, gmix, win, glng, glnb, ws, bst, cw, cb, clng, clnb, wout, gxa, wq, kb, vb, wo,
                     w1f, w3f, w2f):
    nb, seq, _ = x.shape
    tb = PROMPT_TILE
    tps = seq // tb
    last = nb * tps - 1
    n_cast = nb * tps
    r13, r2 = w1f.shape[0] // n_cast, w2f.shape[0] // n_cast
    assert w1f.shape == w3f.shape and w1f.shape[0] % n_cast == 0 and w2f.shape[0] % n_cast == 0
    assert r13 % 16 == 0 and r2 % 16 == 0
    hbm = pl.BlockSpec(memory_space=pl.ANY)
    tile = lambda s, lag: jnp.clip(s - lag, 0, last)
    full = lambda shape: pl.BlockSpec(shape, lambda s: (0,) * len(shape))
    rows = lambda lag: pl.BlockSpec((None, tb, D_MODEL), lambda s: (tile(s, lag) // tps, tile(s, lag) % tps, 0))
    kv_spec = pl.BlockSpec((None, N_MEM, D_MODEL), lambda s: (tile(s, 1) // tps, 0, 0))
    seq_out = lambda r, c: pl.BlockSpec((None, r, c), lambda s: (tile(s, 0) // tps, 0, 0))
    return pl.pallas_call(
        functools.partial(_prompt_kernel, tiles_per_seq=tps, n_cast_chunks=n_cast),
        grid=(nb * tps + 1,),
        in_specs=[rows(0), full((1, D_MODEL)), full((D_MODEL, IN_COLS)), full((1, GM_WIDTH)), full((1, GM_WIDTH)),
                  full((GM_HEADS, CHUNK, CHUNK)), full((CHUNK, GM_HEADS)), full((CONV_WIDTH, CV_WIDTH)),
                  full((1, CV_WIDTH)), full((1, CV_WIDTH)), full((1, CV_WIDTH)), full((D_MODEL, D_MODEL)),
                  full((1, D_MODEL)), full((D_MODEL, D_MODEL)), kv_spec, kv_spec, full((D_MODEL, D_MODEL)),
                  hbm, hbm, hbm],
        out_specs=[rows(1), seq_out(HALO, CV_WIDTH), seq_out(CHUNK, GM_WIDTH), hbm, hbm, hbm],
        out_shape=[jax.ShapeDtypeStruct((nb, seq, D_MODEL), F32),
                   jax.ShapeDtypeStruct((nb, HALO, CV_WIDTH), F32),
                   jax.ShapeDtypeStruct((nb, CHUNK, GM_WIDTH), F32),
                   jax.ShapeDtypeStruct(w1f.shape, BF16), jax.ShapeDtypeStruct(w3f.shape, BF16),
                   jax.ShapeDtypeStruct(w2f.shape, BF16)],
        scratch_shapes=[pltpu.VMEM((tb, IN_COLS), F32), pltpu.VMEM((HALO + tb, CV_WIDTH), F32),
                        pltpu.VMEM((tb, CV_WIDTH), F32), pltpu.VMEM((tb, D_MODEL), BF16),
                        pltpu.VMEM((tb, D_MODEL), BF16), pltpu.VMEM((tb, D_MODEL), F32),
                        pltpu.VMEM((tb, D_MODEL), F32),
                        pltpu.VMEM((2, r13, EXPERT_FF), F32), pltpu.VMEM((2, r13, EXPERT_FF), F32),
                        pltpu.VMEM((2, r2, D_MODEL), F32),
                        pltpu.VMEM((r13, EXPERT_FF), BF16), pltpu.VMEM((r13, EXPERT_FF), BF16),
                        pltpu.VMEM((r2, D_MODEL), BF16),
                        pltpu.SemaphoreType.DMA((2, 3)), pltpu.SemaphoreType.DMA((3,))],
        compiler_params=pltpu.CompilerParams(dimension_semantics=("arbitrary",), vmem_limit_bytes=VMEM_LIMIT),
        name="prompt_mix_attn",
    )(x, gmix, win, glng, glnb, ws, bst, cw, cb, clng, clnb, wout, gxa, wq, kb, vb, wo, w1f, w3f, w2f)


def _sample_mix_kernel(x_ref, gmix_ref, win_ref, glng_ref, glnb_ref, ws0_ref, bs0_ref, cache_ref, cw_ref, cb_ref,
                       clng_ref, clnb_ref, wout_ref, gxa_ref, wq_ref,
                       y_ref, glu_ref, v_ref, q_ref, ab_s):
    x = x_ref[...]
    z = _dot(_rms(x, gmix_ref[...]).astype(BF16), win_ref[...])
    for h in range(GM_HEADS):
        cu = slice(h * GM_HEAD_DIM, (h + 1) * GM_HEAD_DIM)
        cv = slice(GM_WIDTH + h * GM_HEAD_DIM, GM_WIDTH + (h + 1) * GM_HEAD_DIM)
        v = _ln(jax.nn.gelu(z[:, cv]), glng_ref[:, cu], glnb_ref[:, cu])
        v_ref[:, cu] = v
        ab_s[:, cu] = (jax.nn.gelu(z[:, cu]) * (v * ws0_ref[:, cu] + bs0_ref[:, cu])).astype(BF16)
    glu = z[:, 2 * GM_WIDTH:2 * GM_WIDTH + CV_WIDTH] * jax.nn.sigmoid(z[:, 2 * GM_WIDTH + CV_WIDTH:])
    glu_ref[...] = glu
    conv = glu * cw_ref[CONV_WIDTH - 1:CONV_WIDTH, :] + cb_ref[...]
    for k in range(CONV_WIDTH - 1):
        conv = conv + cache_ref[k] * cw_ref[k:k + 1, :]
    b = _ln(conv, clng_ref[...], clnb_ref[...])
    ab_s[:, GM_WIDTH:] = (b * jax.nn.sigmoid(b)).astype(BF16)
    y = x + _dot(ab_s[...], wout_ref[...])
    y_ref[...] = y
    q_ref[...] = _dot(_rms(y, gxa_ref[...]).astype(BF16), wq_ref[...]) * (XA_HEAD_DIM ** -0.5)


def _sample_mix(x, gmix, win, glng, glnb, ws0, bs0, cache_t, cw, cb, clng, clnb, wout, gxa, wq):
    ns = x.shape[0]
    return pl.pallas_call(
        _sample_mix_kernel,
        out_shape=[jax.ShapeDtypeStruct((ns, D_MODEL), F32), jax.ShapeDtypeStruct((ns, CV_WIDTH), F32),
                   jax.ShapeDtypeStruct((ns, GM_WIDTH), F32), jax.ShapeDtypeStruct((ns, D_MODEL), F32)],
        scratch_shapes=[pltpu.VMEM((ns, D_MODEL), BF16)],
        compiler_params=pltpu.CompilerParams(vmem_limit_bytes=VMEM_LIMIT),
        name="sample_mix",
    )(x, gmix, win, glng, glnb, ws0, bs0, cache_t, cw, cb, clng, clnb, wout, gxa, wq)


def _sample_attn_kernel(q_ref, k_ref, v_ref, o_ref):
    ones = jnp.ones((LANES, LANES), BF16)
    rows = N_MEM * SUBLANES
    for i in range(q_ref.shape[0]):
        prod = (k_ref[i] * q_ref[i][None]).reshape(rows, LANES).astype(BF16)
        part = _dot(prod, ones).reshape(N_MEM, SUBLANES, LANES)
        s = part + pltpu.roll(part, XA_HEADS, axis=1)
        e = jnp.exp(s - jnp.max(s, axis=0, keepdims=True))
        p = e / jnp.sum(e, axis=0, keepdims=True)
        o_ref[i] = jnp.sum(p * v_ref[i], axis=0)


def _sample_attn(q, k, v):
    ns = q.shape[0]
    nb = SAMPLE_ATTN_BLOCK
    halves = XA_HEAD_DIM // LANES
    assert halves * XA_HEADS == SUBLANES

    def split(a):
        lead = a.shape[:-2]
        a = a.reshape(*lead, XA_HEADS, halves, LANES)
        return jnp.swapaxes(a, -3, -2).reshape(*lead, SUBLANES, LANES)

    o = pl.pallas_call(
        _sample_attn_kernel,
        grid=(ns // nb,),
        in_specs=[pl.BlockSpec((nb, SUBLANES, LANES), lambda i: (i, 0, 0)),
                  pl.BlockSpec((nb, N_MEM, SUBLANES, LANES), lambda i: (i, 0, 0, 0)),
                  pl.BlockSpec((nb, N_MEM, SUBLANES, LANES), lambda i: (i, 0, 0, 0))],
        out_specs=pl.BlockSpec((nb, SUBLANES, LANES), lambda i: (i, 0, 0)),
        out_shape=jax.ShapeDtypeStruct((ns, SUBLANES, LANES), F32),
        compiler_params=pltpu.CompilerParams(dimension_semantics=("arbitrary",), vmem_limit_bytes=VMEM_LIMIT),
        name="sample_attn",
    )(split(q), split(k), split(v))
    o = jnp.swapaxes(o.reshape(ns, halves, XA_HEADS, LANES), 1, 2)
    return o.reshape(ns, XA_HEADS * XA_HEAD_DIM)


def _sample_proj_kernel(y_ref, o_ref, wo_ref, h_ref):
    h_ref[...] = y_ref[...] + _dot(o_ref[...].astype(BF16), wo_ref[...])


def _sample_proj(y, o, wo):
    return pl.pallas_call(
        _sample_proj_kernel,
        out_shape=jax.ShapeDtypeStruct(y.shape, F32),
        compiler_params=pltpu.CompilerParams(vmem_limit_bytes=VMEM_LIMIT),
        name="sample_proj",
    )(y, o, wo)


def _moe_kernel(h_ref, gffn_ref, wrt_ref, rb_ref, w1_ref, w3_ref, w2_ref, gfin_ref,
                o_ref, xt_s, u_s, gidx_s, slot_s, comb_s, act_s, p2_s, y2_s, cnt_s):
    i = pl.program_id(0)
    g = pl.program_id(1)
    rt = h_ref.shape[0]
    sb = act_s.shape[0]

    @pl.when(jnp.logical_and(i == 0, g == 0))
    def _():
        u_s[...] = (lax.broadcasted_iota(jnp.int32, (rt, rt), 0)
                    < lax.broadcasted_iota(jnp.int32, (rt, rt), 1)).astype(F32).astype(BF16)

    @pl.when(g == 0)
    def _route():
        h = h_ref[...]
        o_ref[...] = h
        xt = _rms(h, gffn_ref[...]).astype(BF16)
        xt_s[...] = xt
        lt = _dot_nt(wrt_ref[...], xt) + rb_ref[...]
        gl = [lt[k:k + 1, :] for k in range(N_GROUPS)]
        gmax = jnp.maximum(jnp.maximum(gl[0], gl[1]), jnp.maximum(gl[2], gl[3]))
        gidx = jnp.where(gl[0] == gmax, 0, jnp.where(gl[1] == gmax, 1, jnp.where(gl[2] == gmax, 2, 3)))
        gidx = gidx.astype(jnp.int32)
        sumexp = (jnp.exp(gl[0] - gmax) + jnp.exp(gl[1] - gmax)) + (jnp.exp(gl[2] - gmax) + jnp.exp(gl[3] - gmax))
        p_g = 1.0 / sumexp
        esel = lt[SUBLANES + 3 * EXPERTS_PER_GROUP:SUBLANES + 4 * EXPERTS_PER_GROUP, :]
        for k in (2, 1, 0):
            esel = jnp.where(gidx == k, lt[SUBLANES + k * EXPERTS_PER_GROUP:SUBLANES + (k + 1) * EXPERTS_PER_GROUP, :],
                             esel)
        eidx = lax.broadcasted_iota(jnp.int32, (EXPERTS_PER_GROUP, rt), 0)
        m1 = jnp.max(esel, axis=0, keepdims=True)
        i1 = jnp.min(jnp.where(esel == m1, eidx, EXPERTS_PER_GROUP), axis=0, keepdims=True)
        rest = jnp.where(eidx == i1, -jnp.inf, esel)
        m2 = jnp.max(rest, axis=0, keepdims=True)
        i2 = jnp.min(jnp.where(rest == m2, eidx, EXPERTS_PER_GROUP), axis=0, keepdims=True)
        t2 = jnp.exp(m2 - m1)
        den = 1.0 + t2
        w_top1 = (1.0 / den) * p_g
        w_top2 = (t2 / den) * p_g
        within = jnp.where(eidx == i1, w_top1, 0.0) + jnp.where(eidx == i2, w_top2, 0.0)
        c_hi = within.astype(BF16).astype(F32)
        r1 = within - c_hi
        c_mid = r1.astype(BF16).astype(F32)
        c_lo = (r1 - c_mid).astype(BF16).astype(F32)
        comb_s[0:8, :] = c_hi
        comb_s[8:16, :] = c_mid
        comb_s[16:24, :] = c_lo
        comb_s[24:32, :] = jnp.zeros((8, rt), F32)
        onehot = (eidx == gidx).astype(F32)
        rank = _dot(onehot.astype(BF16), u_s[...])
        slot_s[...] = jnp.sum(onehot * rank, axis=0, keepdims=True).astype(jnp.int32)
        gidx_s[...] = gidx
        for k in range(N_GROUPS):
            cnt_s[k] = jnp.sum(onehot[k:k + 1, :]).astype(jnp.int32)

    n_blk = (cnt_s[g] + sb - 1) // sb

    def sub_block(j, half):
        half_rows = slice(half * sb, (half + 1) * sb)
        rows = lax.broadcasted_iota(jnp.int32, (sb, rt), 0) + j * sb
        hit = jnp.logical_and(rows == slot_s[...], gidx_s[...] == g)
        p = jnp.where(hit, 1.0, 0.0).astype(BF16)
        p2_s[half_rows, :] = p
        xc = _dot(p, xt_s[...]).astype(BF16)
        cexp = _dot_nt(p, comb_s[...].astype(BF16))
        cw = (cexp[:, 0:8] + cexp[:, 8:16]) + cexp[:, 16:24]
        for e in range(EXPERTS_PER_GROUP):
            h1 = _dot(xc, w1_ref[e])
            h3 = _dot(xc, w3_ref[e])
            a = (h1 * jax.nn.sigmoid(h1)) * h3 * cw[:, e:e + 1]
            act_s[:, e * EXPERT_FF:(e + 1) * EXPERT_FF] = a.astype(BF16)
        y2_s[half_rows, :] = _dot(act_s[...], w2_ref[...]).astype(BF16)

    def body(jj, carry):
        sub_block(2 * jj, 0)

        @pl.when(2 * jj + 1 < n_blk)
        def _():
            sub_block(2 * jj + 1, 1)

        @pl.when(2 * jj + 1 >= n_blk)
        def _():
            p2_s[sb:, :] = jnp.zeros((sb, rt), BF16)
            y2_s[sb:, :] = jnp.zeros((sb, D_MODEL), BF16)

        o_ref[...] += _dot_tn(p2_s[...], y2_s[...])
        return carry

    lax.fori_loop(0, (n_blk + 1) // 2, body, 0)

    @pl.when(g == N_GROUPS - 1)
    def _():
        o_ref[...] = _rms(o_ref[...], gfin_ref[...])


def _moe(h, gffn, wrt, rbias, w1, w3, w2, gfin, *, tile):
    tokens = h.shape[0]
    sb = min(MOE_SUB, tile)
    ff = EXPERTS_PER_GROUP * EXPERT_FF
    full = lambda shape: pl.BlockSpec(shape, lambda i, g: (0,) * len(shape))
    row_spec = pl.BlockSpec((tile, D_MODEL), lambda i, g: (i, 0))
    return pl.pallas_call(
        _moe_kernel,
        grid=(tokens // tile, N_GROUPS),
        in_specs=[row_spec, full((1, D_MODEL)), full((ROUTER_ROWS, D_MODEL)), full((ROUTER_ROWS, 1)),
                  pl.BlockSpec((None, EXPERTS_PER_GROUP, D_MODEL, EXPERT_FF), lambda i, g: (g, 0, 0, 0)),
                  pl.BlockSpec((None, EXPERTS_PER_GROUP, D_MODEL, EXPERT_FF), lambda i, g: (g, 0, 0, 0)),
                  pl.BlockSpec((None, ff, D_MODEL), lambda i, g: (g, 0, 0)),
                  full((1, D_MODEL))],
        out_specs=row_spec,
        out_shape=jax.ShapeDtypeStruct((tokens, D_MODEL), F32),
        scratch_shapes=[pltpu.VMEM((tile, D_MODEL), BF16), pltpu.VMEM((tile, tile), BF16),
                        pltpu.VMEM((1, tile), jnp.int32), pltpu.VMEM((1, tile), jnp.int32),
                        pltpu.VMEM((4 * SUBLANES, tile), F32), pltpu.VMEM((sb, ff), BF16),
                        pltpu.VMEM((2 * sb, tile), BF16), pltpu.VMEM((2 * sb, D_MODEL), BF16),
                        pltpu.SMEM((N_GROUPS,), jnp.int32)],
        compiler_params=pltpu.CompilerParams(dimension_semantics=("arbitrary", "arbitrary"),
                                             vmem_limit_bytes=VMEM_LIMIT),
        name="moe",
    )(h, gffn, wrt, rbias, w1, w3, w2, gfin)


HALF = D_MODEL // 2
SUB_PER_ROW = HALF // LANES
N_EXPERTS = N_GROUPS * EXPERTS_PER_GROUP


def _pack_bf16_pairs(x):
    bits = pltpu.bitcast(x.astype(BF16).astype(F32), jnp.uint32)
    return (bits[:, HALF:] & jnp.uint32(0xFFFF0000)) | (bits[:, :HALF] >> 16)


def _unpack_bf16_pairs(w):
    lo = pltpu.bitcast(w << 16, F32)
    hi = pltpu.bitcast(w & jnp.uint32(0xFFFF0000), F32)
    return lo, hi


def _route_kernel(h_ref, gffn_ref, wrt_ref, rb_ref, xp_ref, e_ref, w_ref, r_ref, cnt_ref, u_s, run_s):
    i = pl.program_id(0)
    rt = h_ref.shape[0]

    @pl.when(i == 0)
    def _():
        u_s[...] = (lax.broadcasted_iota(jnp.int32, (rt, rt), 0)
                    < lax.broadcasted_iota(jnp.int32, (rt, rt), 1)).astype(F32).astype(BF16)
        run_s[...] = jnp.zeros(run_s.shape, F32)

    xt = _rms(h_ref[...], gffn_ref[...])
    _store_planes(xp_ref, _pack_bf16_pairs(xt))
    lt = _dot_nt(wrt_ref[...], xt.astype(BF16)) + rb_ref[...]
    gl = [lt[k:k + 1, :] for k in range(N_GROUPS)]
    gmax = jnp.maximum(jnp.maximum(gl[0], gl[1]), jnp.maximum(gl[2], gl[3]))
    gidx = jnp.where(gl[0] == gmax, 0, jnp.where(gl[1] == gmax, 1, jnp.where(gl[2] == gmax, 2, 3)))
    gidx = gidx.astype(jnp.int32)
    sumexp = (jnp.exp(gl[0] - gmax) + jnp.exp(gl[1] - gmax)) + (jnp.exp(gl[2] - gmax) + jnp.exp(gl[3] - gmax))
    p_g = 1.0 / sumexp
    esel = lt[SUBLANES + 3 * EXPERTS_PER_GROUP:SUBLANES + 4 * EXPERTS_PER_GROUP, :]
    for k in (2, 1, 0):
        esel = jnp.where(gidx == k, lt[SUBLANES + k * EXPERTS_PER_GROUP:SUBLANES + (k + 1) * EXPERTS_PER_GROUP, :],
                         esel)
    eidx = lax.broadcasted_iota(jnp.int32, (EXPERTS_PER_GROUP, rt), 0)
    m1 = jnp.max(esel, axis=0, keepdims=True)
    i1 = jnp.min(jnp.where(esel == m1, eidx, EXPERTS_PER_GROUP), axis=0, keepdims=True)
    rest = jnp.where(eidx == i1, -jnp.inf, esel)
    m2 = jnp.max(rest, axis=0, keepdims=True)
    i2 = jnp.min(jnp.where(rest == m2, eidx, EXPERTS_PER_GROUP), axis=0, keepdims=True)
    t2 = jnp.exp(m2 - m1)
    den = 1.0 + t2
    w_ref[0:1, :] = (1.0 / den) * p_g
    w_ref[1:2, :] = (t2 / den) * p_g
    e1 = gidx * EXPERTS_PER_GROUP + i1
    e2 = gidx * EXPERTS_PER_GROUP + i2
    e_ref[0:1, :] = e1
    e_ref[1:2, :] = e2
    xid = lax.broadcasted_iota(jnp.int32, (N_EXPERTS, rt), 0)
    oh1 = (xid == e1).astype(F32)
    oh2 = (xid == e2).astype(F32)
    both = oh1 + oh2
    before = _dot(both.astype(BF16), u_s[...]) + run_s[:, 0:1]
    r_ref[0:1, :] = jnp.sum(oh1 * before, axis=0, keepdims=True).astype(jnp.int32)
    r_ref[1:2, :] = jnp.sum(oh2 * before, axis=0, keepdims=True).astype(jnp.int32)
    run_s[...] = run_s[...] + jnp.sum(both, axis=1, keepdims=True)
    cnt_ref[...] = run_s[...].astype(jnp.int32)


def _route(h, gffn, wrt, rbias, *, tile):
    tokens = h.shape[0]
    full = lambda shape: pl.BlockSpec(shape, lambda i: (0,) * len(shape))
    lanes = lambda rows: pl.BlockSpec((rows, tile), lambda i: (0, i))
    return pl.pallas_call(
        _route_kernel,
        grid=(tokens // tile,),
        in_specs=[pl.BlockSpec((tile, D_MODEL), lambda i: (i, 0)), full((1, D_MODEL)),
                  full((ROUTER_ROWS, D_MODEL)), full((ROUTER_ROWS, 1))],
        out_specs=[pl.BlockSpec((SUB_PER_ROW, tile, LANES), lambda i: (0, i, 0)), lanes(2), lanes(2), lanes(2),
                   full((N_EXPERTS, LANES))],
        out_shape=[jax.ShapeDtypeStruct((SUB_PER_ROW, tokens, LANES), jnp.uint32),
                   jax.ShapeDtypeStruct((2, tokens), jnp.int32),
                   jax.ShapeDtypeStruct((2, tokens), F32), jax.ShapeDtypeStruct((2, tokens), jnp.int32),
                   jax.ShapeDtypeStruct((N_EXPERTS, LANES), jnp.int32)],
        scratch_shapes=[pltpu.VMEM((tile, tile), BF16), pltpu.VMEM((N_EXPERTS, LANES), F32)],
        compiler_params=pltpu.CompilerParams(dimension_semantics=("arbitrary",), vmem_limit_bytes=VMEM_LIMIT),
        name="moe_route",
    )(h, gffn, wrt, rbias)


def _store_planes(ref, words):
    for j in range(SUB_PER_ROW):
        ref[j] = words[:, j * LANES:(j + 1) * LANES]


def _load_planes(ref):
    return jnp.concatenate([ref[j] for j in range(SUB_PER_ROW)], axis=1)


def _sub_row_index(row_of_token, n_rows):
    plane = jnp.arange(SUB_PER_ROW, dtype=jnp.int32)[:, None] * n_rows
    return (row_of_token[None, :] + plane).reshape(1, -1)


def _sc_mesh():
    return plsc.VectorSubcoreMesh(core_axis_name="core", subcore_axis_name="subcore")


def _sc_scatter_two(x_sub, idx_a, idx_b, n_out):
    n_in = x_sub.shape[0]

    @pl.kernel(out_type=jax.ShapeDtypeStruct((n_out, LANES), x_sub.dtype), mesh=_sc_mesh(), scratch_types=[])
    def scatter(x_hbm, a_hbm, b_hbm, o_hbm):
        def body(x_vmem, a_vmem, b_vmem):
            pltpu.sync_copy(x_vmem, o_hbm.at[a_vmem.at[0]])
            pltpu.sync_copy(x_vmem, o_hbm.at[b_vmem.at[0]])

        pltpu.emit_pipeline(
            body, grid=(n_in // SC_WINDOW,),
            in_specs=[pl.BlockSpec((SC_WINDOW, LANES), lambda i: (i, 0)),
                      pl.BlockSpec((1, SC_WINDOW), lambda i: (0, i)),
                      pl.BlockSpec((1, SC_WINDOW), lambda i: (0, i))],
            out_specs=[],
            core_axis_name=("core", "subcore"), dimension_semantics=(pltpu.PARALLEL,),
        )(x_hbm, a_hbm, b_hbm)

    return scatter(x_sub, idx_a, idx_b)


def _sc_gather(table, idx):
    n_out = idx.shape[1]

    @pl.kernel(out_type=jax.ShapeDtypeStruct((n_out, LANES), table.dtype), mesh=_sc_mesh())
    def gather(t_hbm, i_hbm, o_hbm):
        def body(i_vmem, o_vmem):
            pltpu.sync_copy(t_hbm.at[i_vmem.at[0]], o_vmem)

        pltpu.emit_pipeline(
            body, grid=(n_out // SC_WINDOW,),
            in_specs=[pl.BlockSpec((1, SC_WINDOW), lambda i: (0, i))],
            out_specs=[pl.BlockSpec((SC_WINDOW, LANES), lambda i: (i, 0))],
            core_axis_name=("core", "subcore"), dimension_semantics=(pltpu.PARALLEL,),
        )(i_hbm, o_hbm)

    return gather(table, idx)


def _expert_ffn_kernel(blk_e_ref, n_valid_ref, x_ref, w1_ref, w3_ref, w2_ref, y_ref):
    del blk_e_ref

    @pl.when(pl.program_id(0) < n_valid_ref[0])
    def _():
        lo, hi = _unpack_bf16_pairs(_load_planes(x_ref))
        xc = jnp.concatenate([lo.astype(BF16), hi.astype(BF16)], axis=1)
        h1 = _dot(xc, w1_ref[...])
        h3 = _dot(xc, w3_ref[...])
        act = ((h1 * jax.nn.sigmoid(h1)) * h3).astype(BF16)
        _store_planes(y_ref, _pack_bf16_pairs(_dot(act, w2_ref[...])))


def _expert_ffn(xs, blk_e, n_valid, w1, w3, w2):
    rows = xs.shape[1]
    n_blocks = rows // FFN_BLOCK
    live = lambda b, be, nv: jnp.minimum(b, nv[0] - 1)
    grid_spec = pltpu.PrefetchScalarGridSpec(
        num_scalar_prefetch=2, grid=(n_blocks,),
        in_specs=[pl.BlockSpec((SUB_PER_ROW, FFN_BLOCK, LANES), lambda b, be, nv: (0, live(b, be, nv), 0)),
                  pl.BlockSpec((None, D_MODEL, EXPERT_FF), lambda b, be, nv: (be[b], 0, 0)),
                  pl.BlockSpec((None, D_MODEL, EXPERT_FF), lambda b, be, nv: (be[b], 0, 0)),
                  pl.BlockSpec((None, EXPERT_FF, D_MODEL), lambda b, be, nv: (be[b], 0, 0))],
        out_specs=pl.BlockSpec((SUB_PER_ROW, FFN_BLOCK, LANES), lambda b, be, nv: (0, live(b, be, nv), 0)))
    return pl.pallas_call(
        _expert_ffn_kernel, grid_spec=grid_spec,
        out_shape=jax.ShapeDtypeStruct((SUB_PER_ROW, rows, LANES), jnp.uint32),
        compiler_params=pltpu.CompilerParams(dimension_semantics=("arbitrary",), vmem_limit_bytes=VMEM_LIMIT),
        name="moe_ffn",
    )(blk_e, n_valid, xs, w1, w3, w2)


def _combine_kernel(h_ref, ya_ref, yb_ref, wt_ref, gfin_ref, o_ref):
    a_lo, a_hi = _unpack_bf16_pairs(_load_planes(ya_ref))
    b_lo, b_hi = _unpack_bf16_pairs(_load_planes(yb_ref))
    wa = wt_ref[:, 0:1]
    wb = wt_ref[:, 1:2]
    o_ref[:, :HALF] = h_ref[:, :HALF] + (wa * a_lo + wb * b_lo)
    o_ref[:, HALF:] = h_ref[:, HALF:] + (wa * a_hi + wb * b_hi)
    o_ref[...] = _rms(o_ref[...], gfin_ref[...])


def _combine(h, ya, yb, wt, gfin, *, tile):
    tokens = h.shape[0]
    return pl.pallas_call(
        _combine_kernel,
        grid=(tokens // tile,),
        in_specs=[pl.BlockSpec((tile, D_MODEL), lambda i: (i, 0)),
                  pl.BlockSpec((SUB_PER_ROW, tile, LANES), lambda i: (0, i, 0)),
                  pl.BlockSpec((SUB_PER_ROW, tile, LANES), lambda i: (0, i, 0)), pl.BlockSpec((tile, 2), lambda i: (i, 0)),
                  pl.BlockSpec((1, D_MODEL), lambda i: (0, 0))],
        out_specs=pl.BlockSpec((tile, D_MODEL), lambda i: (i, 0)),
        out_shape=jax.ShapeDtypeStruct((tokens, D_MODEL), F32),
        compiler_params=pltpu.CompilerParams(dimension_semantics=("arbitrary",), vmem_limit_bytes=VMEM_LIMIT),
        name="moe_combine",
    )(h, ya, yb, wt, gfin)


def _moe_sorted(h, gffn, wrt, rbias, w1, w3, w2, gfin):
    tokens = h.shape[0]
    xp, e12, w12, r12, cnt = _route(h, gffn, wrt, rbias, tile=MOE_TILE)
    count = cnt[:, 0]
    padded = (count + FFN_BLOCK - 1) // FFN_BLOCK * FFN_BLOCK
    end = jnp.cumsum(padded)
    start = end - padded
    n_rows = 2 * tokens + N_EXPERTS * FFN_BLOCK
    n_blocks = n_rows // FFN_BLOCK
    n_valid = (end[-1:] // FFN_BLOCK).astype(jnp.int32)
    first_row = jnp.minimum(jnp.arange(n_blocks, dtype=jnp.int32), n_valid - 1) * FFN_BLOCK
    blk_e = jnp.sum((end[None, :] <= first_row[:, None]).astype(jnp.int32), axis=1)
    experts = jnp.arange(N_EXPERTS, dtype=jnp.int32)
    start_of = jnp.sum(jnp.where(e12[:, :, None] == experts, start.astype(jnp.int32), 0), axis=-1)
    rows_ab = start_of + r12
    idx_a, idx_b = _sub_row_index(rows_ab[0], n_rows), _sub_row_index(rows_ab[1], n_rows)
    flat = lambda a: a.reshape(-1, LANES)
    planes = lambda a: a.reshape(SUB_PER_ROW, -1, LANES)
    xs = planes(_sc_scatter_two(flat(xp), idx_a, idx_b, n_rows * SUB_PER_ROW))
    ys = flat(_expert_ffn(xs, blk_e, n_valid, w1, w3, w2))
    ya = planes(_sc_gather(ys, idx_a))
    yb = planes(_sc_gather(ys, idx_b))
    return _combine(h, ya, yb, w12.T, gfin, tile=MOE_TILE)


def kernel(x_prompt, x_sample, mem_prompt, cache_conv, cache_mem_k, cache_mem_v, norm_mix_g, w_in, gm_ln_g, gm_ln_b, gm_ws, gm_bs, conv_w, conv_b, cv_ln_g, cv_ln_b, w_out, norm_mem_g, norm_xa_g, xa_wq, xa_wk, xa_wv, xa_wo, norm_ffn_g, router_g, router_g_b, router_e, router_e_b, exp_w1, exp_w3, exp_w2, final_norm_g):
    depth = w_in.shape[0]
    assert depth == 1, "single-layer trunk"
    nb, seq, _ = x_prompt.shape
    ns = x_sample.shape[0]
    row = lambda a: a.reshape(1, -1)

    l = 0
    gmix, gxa, gffn, gmem = row(norm_mix_g[l]), row(norm_xa_g[l]), row(norm_ffn_g[l]), row(norm_mem_g[l])
    gfin = row(final_norm_g)
    win, wout = w_in[l].astype(BF16), w_out[l].astype(BF16)
    wq, wk, wv, wo = (w[l].astype(BF16) for w in (xa_wq, xa_wk, xa_wv, xa_wo))
    glng, glnb = row(gm_ln_g[l]), row(gm_ln_b[l])
    cw, cb, clng, clnb = conv_w[l], row(conv_b[l]), row(cv_ln_g[l]), row(cv_ln_b[l])
    ws, bst = gm_ws[l], gm_bs[l].T
    ws0 = jnp.repeat(gm_ws[l][:, 0, 0], GM_HEAD_DIM).reshape(1, GM_WIDTH)
    bs0 = jnp.repeat(gm_bs[l][:, 0], GM_HEAD_DIM).reshape(1, GM_WIDTH)

    n_exp = N_GROUPS * EXPERTS_PER_GROUP
    pad_g = SUBLANES - N_GROUPS
    pad_t = ROUTER_ROWS - SUBLANES - n_exp
    wrt = jnp.concatenate([router_g[l].T, jnp.zeros((pad_g, D_MODEL), F32),
                           router_e[l].reshape(D_MODEL, n_exp).T, jnp.zeros((pad_t, D_MODEL), F32)], axis=0).astype(BF16)
    rbias = jnp.concatenate([router_g_b[l], jnp.zeros((pad_g,), F32), router_e_b[l].reshape(n_exp),
                             jnp.zeros((pad_t,), F32)]).reshape(ROUTER_ROWS, 1)

    mk, mv, kb, vb = _memkv(mem_prompt.reshape(nb * N_MEM, D_MODEL), gmem, wk, wv)
    kb, vb = kb.reshape(nb, N_MEM, D_MODEL), vb.reshape(nb, N_MEM, D_MODEL)
    hp, conv_tail, gmv_p, w1, w3, w2 = _prompt_mix_attn(
        x_prompt, gmix, win, glng, glnb, ws, bst, cw, cb, clng, clnb, wout, gxa, wq, kb, vb, wo,
        exp_w1[l].reshape(-1, EXPERT_FF), exp_w3[l].reshape(-1, EXPERT_FF), exp_w2[l].reshape(-1, D_MODEL))
    y_prompt = _moe_sorted(hp.reshape(nb * seq, D_MODEL), gffn, wrt, rbias,
                           w1.reshape(N_EXPERTS, D_MODEL, EXPERT_FF), w3.reshape(N_EXPERTS, D_MODEL, EXPERT_FF),
                           w2.reshape(N_EXPERTS, EXPERT_FF, D_MODEL), gfin)
    w1 = w1.reshape(N_GROUPS, EXPERTS_PER_GROUP, D_MODEL, EXPERT_FF)
    w3 = w3.reshape(N_GROUPS, EXPERTS_PER_GROUP, D_MODEL, EXPERT_FF)
    w2 = w2.reshape(N_GROUPS, EXPERTS_PER_GROUP * EXPERT_FF, D_MODEL)

    cache_t = jnp.transpose(cache_conv[l], (1, 0, 2))
    ys, glu_s, gmv_s, q_s = _sample_mix(x_sample.reshape(ns, D_MODEL), gmix, win, glng, glnb, ws0, bs0,
                                        cache_t, cw, cb, clng, clnb, wout, gxa, wq)
    o_s = _sample_attn(q_s.reshape(ns, XA_HEADS, XA_HEAD_DIM), cache_mem_k[l], cache_mem_v[l])
    hs = _sample_proj(ys, o_s, wo)
    y_sample = _moe(hs, gffn, wrt, rbias, w1, w3, w2, gfin, tile=ns)

    conv_prompt = conv_tail[:, HALO - (CONV_WIDTH - 1):, :][None]
    conv_sample = jnp.transpose(jnp.concatenate([cache_t[1:], glu_s[None]], axis=0), (1, 0, 2))[None]
    return (y_prompt.reshape(nb, seq, D_MODEL), y_sample.reshape(ns, 1, D_MODEL), conv_prompt, conv_sample,
            gmv_p[None], gmv_s.reshape(1, ns, 1, GM_WIDTH),
            mk.reshape(1, nb, N_MEM, XA_HEADS, XA_HEAD_DIM), mv.reshape(1, nb, N_MEM, XA_HEADS, XA_HEAD_DIM))
```

```python
import functools

import jax
import jax.numpy as jnp
from jax import lax
from jax.experimental import pallas as pl
from jax.experimental.pallas import tpu as pltpu
from jax.experimental.pallas import tpu_sc as plsc

F32 = jnp.float32
BF16 = jnp.bfloat16

D_MODEL = 1024
GM_WIDTH = 512
CV_WIDTH = 512
GM_HEADS = 4
GM_HEAD_DIM = 128
CHUNK = 128
CONV_WIDTH = 31
IN_COLS = 2 * GM_WIDTH + 2 * CV_WIDTH
N_MEM = 256
XA_HEADS = 4
XA_HEAD_DIM = 256
N_GROUPS = 4
EXPERTS_PER_GROUP = 8
EXPERT_FF = 256
EPS = 1e-6

LANES = 128
SUBLANES = 8
HALO = 32
PROMPT_TILE = 512
MOE_TILE = 1024
MOE_SUB = 128
FFN_BLOCK = 512
SC_WINDOW = 128
SAMPLE_ATTN_BLOCK = 8
ROUTER_ROWS = 128
VMEM_LIMIT = 56 * 1024 * 1024


def _rms(x, g):
    return x * lax.rsqrt(jnp.mean(x * x, axis=-1, keepdims=True) + EPS) * g


def _ln(x, g, b):
    mu = jnp.mean(x, axis=-1, keepdims=True)
    xc = x - mu
    var = jnp.mean(xc * xc, axis=-1, keepdims=True)
    return xc * lax.rsqrt(var + EPS) * g + b


def _dot(a, b):
    return jnp.dot(a, b, preferred_element_type=F32)


def _dot_nt(a, b):
    return lax.dot_general(a, b, (((1,), (1,)), ((), ())), preferred_element_type=F32)


def _dot_tn(a, b):
    return lax.dot_general(a, b, (((0,), (0,)), ((), ())), preferred_element_type=F32)


def _memkv_kernel(mem_ref, g_ref, wk_ref, wv_ref, k_ref, v_ref, kb_ref, vb_ref):
    mn = _rms(mem_ref[...], g_ref[...]).astype(BF16)
    k = _dot(mn, wk_ref[...])
    v = _dot(mn, wv_ref[...])
    for h in range(XA_HEADS):
        cols = slice(h * XA_HEAD_DIM, (h + 1) * XA_HEAD_DIM)
        k_ref[:, h, :] = k[:, cols]
        v_ref[:, h, :] = v[:, cols]
    kb_ref[...] = k.astype(BF16)
    vb_ref[...] = v.astype(BF16)


def _memkv(mem2d, g, wk, wv):
    rows = mem2d.shape[0]
    tile = 512
    row_spec = pl.BlockSpec((tile, D_MODEL), lambda i: (i, 0))
    head_spec = pl.BlockSpec((tile, XA_HEADS, XA_HEAD_DIM), lambda i: (i, 0, 0))
    full = lambda shape: pl.BlockSpec(shape, lambda i: (0,) * len(shape))
    return pl.pallas_call(
        _memkv_kernel,
        grid=(rows // tile,),
        in_specs=[row_spec, full((1, D_MODEL)), full((D_MODEL, D_MODEL)), full((D_MODEL, D_MODEL))],
        out_specs=[head_spec, head_spec, row_spec, row_spec],
        out_shape=[jax.ShapeDtypeStruct((rows, XA_HEADS, XA_HEAD_DIM), F32),
                   jax.ShapeDtypeStruct((rows, XA_HEADS, XA_HEAD_DIM), F32),
                   jax.ShapeDtypeStruct((rows, D_MODEL), BF16), jax.ShapeDtypeStruct((rows, D_MODEL), BF16)],
        compiler_params=pltpu.CompilerParams(dimension_semantics=("arbitrary",), vmem_limit_bytes=VMEM_LIMIT),
        name="memkv",
    )(mem2d, g, wk, wv)


def _conv_block(g_s, cw_ref, start, lanes):
    assert start % SUBLANES == 0
    rows = CHUNK + HALO
    win = g_s[start:start + rows, lanes]
    off = HALO - (CONV_WIDTH - 1)
    acc = None
    for b in range(SUBLANES):
        shifted = pltpu.roll(win, rows - (off + b), axis=0) if off + b else win
        for k in range(b, CONV_WIDTH, SUBLANES):
            assert k - b + CHUNK + off + b <= rows
            term = shifted[k - b:k - b + CHUNK] * cw_ref[k:k + 1, lanes]
            acc = term if acc is None else acc + term
    return acc


def _cast_copies(src_refs, dst_refs, in_bufs, out_bufs, sem_in, sem_out, chunk, slot):
    copies_in, copies_out = [], []
    for i, (src, dst, ibuf, obuf) in enumerate(zip(src_refs, dst_refs, in_bufs, out_bufs)):
        rows = obuf.shape[0]
        start = pl.multiple_of(chunk * rows, rows)
        copies_in.append(pltpu.make_async_copy(src.at[pl.ds(start, rows), :], ibuf.at[slot], sem_in.at[slot, i]))
        copies_out.append(pltpu.make_async_copy(obuf, dst.at[pl.ds(start, rows), :], sem_out.at[i]))
    return copies_in, copies_out


def _prompt_kernel(x_ref, gmix_ref, win_ref, glng_ref, glnb_ref, ws_ref, bst_ref, cw_ref, cb_ref,
                   clng_ref, clnb_ref, wout_ref, gxa_ref, wq_ref, kb_ref, vb_ref, wo_ref,
                   w1f_ref, w3f_ref, w2f_ref,
                   h_ref, conv_ref, gmv_ref, w1b_ref, w3b_ref, w2b_ref,
                   z_s, g_s, c_s, ab_s, o_s, y_prev, y_cur, in1, in3, in2, st1, st3, st2, sem_in, sem_out,
                   *, tiles_per_seq, n_cast_chunks):
    s = pl.program_id(0)
    n_steps = pl.num_programs(0)
    tb = x_ref.shape[0]
    n_chunks = tb // CHUNK

    srcs, dsts = (w1f_ref, w3f_ref, w2f_ref), (w1b_ref, w3b_ref, w2b_ref)
    ins, sts = (in1, in3, in2), (st1, st3, st2)
    slot = s % 2
    chunk_of = lambda step: jnp.minimum(step, n_cast_chunks - 1)

    @pl.when(s == 0)
    def _():
        for c in _cast_copies(srcs, dsts, ins, sts, sem_in, sem_out, chunk_of(s), slot)[0]:
            c.start()

    @pl.when(s + 1 < n_steps)
    def _():
        for c in _cast_copies(srcs, dsts, ins, sts, sem_in, sem_out, chunk_of(s + 1), 1 - slot)[0]:
            c.start()

    cast_in, cast_out = _cast_copies(srcs, dsts, ins, sts, sem_in, sem_out, chunk_of(s), slot)
    for c in cast_in:
        c.wait()
    for ibuf, obuf in zip(ins, sts):
        obuf[...] = ibuf[slot].astype(BF16)
    for c in cast_out:
        c.start()

    @pl.when(s == 0)
    def _():
        y_prev[...] = jnp.zeros(y_prev.shape, F32)

    @pl.when(s % tiles_per_seq == 0)
    def _():
        g_s[0:HALO, :] = jnp.zeros((HALO, CV_WIDTH), F32)

    z_s[...] = _dot(_rms(x_ref[...], gmix_ref[...]).astype(BF16), win_ref[...])

    y = y_prev[...]
    qn = _rms(y, gxa_ref[...]).astype(BF16)
    q = (_dot(qn, wq_ref[...]) * (XA_HEAD_DIM ** -0.5)).astype(BF16)
    for h in range(XA_HEADS):
        cols = slice(h * XA_HEAD_DIM, (h + 1) * XA_HEAD_DIM)
        sc = _dot_nt(q[:, cols], kb_ref[:, cols])
        e = jnp.exp(sc - jnp.max(sc, axis=-1, keepdims=True))
        p = (e / jnp.sum(e, axis=-1, keepdims=True)).astype(BF16)
        o_s[:, cols] = _dot(p, vb_ref[:, cols]).astype(BF16)
    h_ref[...] = y + _dot(o_s[...], wo_ref[...])

    tri = (lax.broadcasted_iota(jnp.int32, (CHUNK, CHUNK), 0)
           >= lax.broadcasted_iota(jnp.int32, (CHUNK, CHUNK), 1))
    wm = [jnp.where(tri, ws_ref[h], 0.0).astype(BF16) for h in range(GM_HEADS)]

    for c in range(n_chunks):
        rows = slice(c * CHUNK, (c + 1) * CHUNK)
        for h in range(GM_HEADS):
            cu = slice(h * GM_HEAD_DIM, (h + 1) * GM_HEAD_DIM)
            cv = slice(GM_WIDTH + h * GM_HEAD_DIM, GM_WIDTH + (h + 1) * GM_HEAD_DIM)
            v = _ln(jax.nn.gelu(z_s[rows, cv]), glng_ref[:, cu], glnb_ref[:, cu])
            if c == n_chunks - 1:
                gmv_ref[:, cu] = v
            mixed = _dot(wm[h], v.astype(BF16)) + bst_ref[:, h:h + 1]
            ab_s[rows, cu] = (jax.nn.gelu(z_s[rows, cu]) * mixed).astype(BF16)
        ca = slice(2 * GM_WIDTH, 2 * GM_WIDTH + CV_WIDTH)
        cg = slice(2 * GM_WIDTH + CV_WIDTH, IN_COLS)
        g_s[HALO + c * CHUNK:HALO + (c + 1) * CHUNK, :] = z_s[rows, ca] * jax.nn.sigmoid(z_s[rows, cg])

    for c in range(n_chunks):
        rows = slice(c * CHUNK, (c + 1) * CHUNK)
        for cb in range(CV_WIDTH // LANES):
            lanes = slice(cb * LANES, (cb + 1) * LANES)
            c_s[rows, lanes] = _conv_block(g_s, cw_ref, c * CHUNK, lanes) + cb_ref[:, lanes]
        b = _ln(c_s[rows, :], clng_ref[...], clnb_ref[...])
        ab_s[rows, GM_WIDTH:] = (b * jax.nn.sigmoid(b)).astype(BF16)

    conv_ref[...] = g_s[tb:tb + HALO, :]
    g_s[0:HALO, :] = g_s[tb:tb + HALO, :]
    y_cur[...] = x_ref[...] + _dot(ab_s[...], wout_ref[...])
    y_prev[...] = y_cur[...]
    for c in cast_out:
        c.wait()


def _prompt_mix_attn(x, gmix, win, glng, glnb, ws, bst, cw, cb, clng, clnb, wout, gxa, wq, kb, vb, wo,
                     w1f, w3f, w2f):
    nb, seq, _ = x.shape
    tb = PROMPT_TILE
    tps = seq // tb
    last = nb * tps - 1
    n_cast = nb * tps
    r13, r2 = w1f.shape[0] // n_cast, w2f.shape[0] // n_cast
    assert w1f.shape == w3f.shape and w1f.shape[0] % n_cast == 0 and w2f.shape[0] % n_cast == 0
    assert r13 % 16 == 0 and r2 % 16 == 0
    hbm = pl.BlockSpec(memory_space=pl.ANY)
    tile = lambda s, lag: jnp.clip(s - lag, 0, last)
    full = lambda shape: pl.BlockSpec(shape, lambda s: (0,) * len(shape))
    rows = lambda lag: pl.BlockSpec((None, tb, D_MODEL), lambda s: (tile(s, lag) // tps, tile(s, lag) % tps, 0))
    kv_spec = pl.BlockSpec((None, N_MEM, D_MODEL), lambda s: (tile(s, 1) // tps, 0, 0))
    seq_out = lambda r, c: pl.BlockSpec((None, r, c), lambda s: (tile(s, 0) // tps, 0, 0))
    return pl.pallas_call(
        functools.partial(_prompt_kernel, tiles_per_seq=tps, n_cast_chunks=n_cast),
        grid=(nb * tps + 1,),
        in_specs=[rows(0), full((1, D_MODEL)), full((D_MODEL, IN_COLS)), full((1, GM_WIDTH)), full((1, GM_WIDTH)),
                  full((GM_HEADS, CHUNK, CHUNK)), full((CHUNK, GM_HEADS)), full((CONV_WIDTH, CV_WIDTH)),
                  full((1, CV_WIDTH)), full((1, CV_WIDTH)), full((1, CV_WIDTH)), full((D_MODEL, D_MODEL)),
                  full((1, D_MODEL)), full((D_MODEL, D_MODEL)), kv_spec, kv_spec, full((D_MODEL, D_MODEL)),
                  hbm, hbm, hbm],
        out_specs=[rows(1), seq_out(HALO, CV_WIDTH), seq_out(CHUNK, GM_WIDTH), hbm, hbm, hbm],
        out_shape=[jax.ShapeDtypeStruct((nb, seq, D_MODEL), F32),
                   jax.ShapeDtypeStruct((nb, HALO, CV_WIDTH), F32),
                   jax.ShapeDtypeStruct((nb, CHUNK, GM_WIDTH), F32),
                   jax.ShapeDtypeStruct(w1f.shape, BF16), jax.ShapeDtypeStruct(w3f.shape, BF16),
                   jax.ShapeDtypeStruct(w2f.shape, BF16)],
        scratch_shapes=[pltpu.VMEM((tb, IN_COLS), F32), pltpu.VMEM((HALO + tb, CV_WIDTH), F32),
                        pltpu.VMEM((tb, CV_WIDTH), F32), pltpu.VMEM((tb, D_MODEL), BF16),
                        pltpu.VMEM((tb, D_MODEL), BF16), pltpu.VMEM((tb, D_MODEL), F32),
                        pltpu.VMEM((tb, D_MODEL), F32),
                        pltpu.VMEM((2, r13, EXPERT_FF), F32), pltpu.VMEM((2, r13, EXPERT_FF), F32),
                        pltpu.VMEM((2, r2, D_MODEL), F32),
                        pltpu.VMEM((r13, EXPERT_FF), BF16), pltpu.VMEM((r13, EXPERT_FF), BF16),
                        pltpu.VMEM((r2, D_MODEL), BF16),
                        pltpu.SemaphoreType.DMA((2, 3)), pltpu.SemaphoreType.DMA((3,))],
        compiler_params=pltpu.CompilerParams(dimension_semantics=("arbitrary",), vmem_limit_bytes=VMEM_LIMIT),
        name="prompt_mix_attn",
    )(x, gmix, win, glng, glnb, ws, bst, cw, cb, clng, clnb, wout, gxa, wq, kb, vb, wo, w1f, w3f, w2f)


def _sample_mix_kernel(x_ref, gmix_ref, win_ref, glng_ref, glnb_ref, ws0_ref, bs0_ref, cache_ref, cw_ref, cb_ref,
                       clng_ref, clnb_ref, wout_ref, gxa_ref, wq_ref,
                       y_ref, glu_ref, v_ref, q_ref, ab_s):
    x = x_ref[...]
    z = _dot(_rms(x, gmix_ref[...]).astype(BF16), win_ref[...])
    for h in range(GM_HEADS):
        cu = slice(h * GM_HEAD_DIM, (h + 1) * GM_HEAD_DIM)
        cv = slice(GM_WIDTH + h * GM_HEAD_DIM, GM_WIDTH + (h + 1) * GM_HEAD_DIM)
        v = _ln(jax.nn.gelu(z[:, cv]), glng_ref[:, cu], glnb_ref[:, cu])
        v_ref[:, cu] = v
        ab_s[:, cu] = (jax.nn.gelu(z[:, cu]) * (v * ws0_ref[:, cu] + bs0_ref[:, cu])).astype(BF16)
    glu = z[:, 2 * GM_WIDTH:2 * GM_WIDTH + CV_WIDTH] * jax.nn.sigmoid(z[:, 2 * GM_WIDTH + CV_WIDTH:])
    glu_ref[...] = glu
    conv = glu * cw_ref[CONV_WIDTH - 1:CONV_WIDTH, :] + cb_ref[...]
    for k in range(CONV_WIDTH - 1):
        conv = conv + cache_ref[k] * cw_ref[k:k + 1, :]
    b = _ln(conv, clng_ref[...], clnb_ref[...])
    ab_s[:, GM_WIDTH:] = (b * jax.nn.sigmoid(b)).astype(BF16)
    y = x + _dot(ab_s[...], wout_ref[...])
    y_ref[...] = y
    q_ref[...] = _dot(_rms(y, gxa_ref[...]).astype(BF16), wq_ref[...]) * (XA_HEAD_DIM ** -0.5)


def _sample_mix(x, gmix, win, glng, glnb, ws0, bs0, cache_t, cw, cb, clng, clnb, wout, gxa, wq):
    ns = x.shape[0]
    return pl.pallas_call(
        _sample_mix_kernel,
        out_shape=[jax.ShapeDtypeStruct((ns, D_MODEL), F32), jax.ShapeDtypeStruct((ns, CV_WIDTH), F32),
                   jax.ShapeDtypeStruct((ns, GM_WIDTH), F32), jax.ShapeDtypeStruct((ns, D_MODEL), F32)],
        scratch_shapes=[pltpu.VMEM((ns, D_MODEL), BF16)],
        compiler_params=pltpu.CompilerParams(vmem_limit_bytes=VMEM_LIMIT),
        name="sample_mix",
    )(x, gmix, win, glng, glnb, ws0, bs0, cache_t, cw, cb, clng, clnb, wout, gxa, wq)


def _sample_attn_kernel(q_ref, k_ref, v_ref, o_ref):
    ones = jnp.ones((LANES, LANES), BF16)
    rows = N_MEM * SUBLANES
    for i in range(q_ref.shape[0]):
        prod = (k_ref[i] * q_ref[i][None]).reshape(rows, LANES).astype(BF16)
        part = _dot(prod, ones).reshape(N_MEM, SUBLANES, LANES)
        s = part + pltpu.roll(part, XA_HEADS, axis=1)
        e = jnp.exp(s - jnp.max(s, axis=0, keepdims=True))
        p = e / jnp.sum(e, axis=0, keepdims=True)
        o_ref[i] = jnp.sum(p * v_ref[i], axis=0)


def _split_heads(a):
    halves = XA_HEAD_DIM // LANES
    assert halves * XA_HEADS == SUBLANES
    lead = a.shape[:-2]
    a = a.reshape(*lead, XA_HEADS, halves, LANES)
    return jnp.swapaxes(a, -3, -2).reshape(*lead, SUBLANES, LANES)


def _merge_heads(o):
    ns = o.shape[0]
    o = jnp.swapaxes(o.reshape(ns, XA_HEAD_DIM // LANES, XA_HEADS, LANES), 1, 2)
    return o.reshape(ns, XA_HEADS * XA_HEAD_DIM)


def _sample_attn(q, k, v, first, count):
    nb = SAMPLE_ATTN_BLOCK
    assert first % nb == 0 and count % nb == 0
    off = first // nb
    return pl.pallas_call(
        _sample_attn_kernel,
        grid=(count // nb,),
        in_specs=[pl.BlockSpec((nb, SUBLANES, LANES), lambda i: (i + off, 0, 0)),
                  pl.BlockSpec((nb, N_MEM, SUBLANES, LANES), lambda i: (i + off, 0, 0, 0)),
                  pl.BlockSpec((nb, N_MEM, SUBLANES, LANES), lambda i: (i + off, 0, 0, 0))],
        out_specs=pl.BlockSpec((nb, SUBLANES, LANES), lambda i: (i, 0, 0)),
        out_shape=jax.ShapeDtypeStruct((count, SUBLANES, LANES), F32),
        compiler_params=pltpu.CompilerParams(dimension_semantics=("arbitrary",), vmem_limit_bytes=VMEM_LIMIT),
        name="sample_attn",
    )(q, k, v)


def _sample_proj_kernel(y_ref, o_ref, wo_ref, h_ref):
    h_ref[...] = y_ref[...] + _dot(o_ref[...].astype(BF16), wo_ref[...])


def _sample_proj(y, o, wo):
    return pl.pallas_call(
        _sample_proj_kernel,
        out_shape=jax.ShapeDtypeStruct(y.shape, F32),
        compiler_params=pltpu.CompilerParams(vmem_limit_bytes=VMEM_LIMIT),
        name="sample_proj",
    )(y, o, wo)


def _moe_kernel(h_ref, gffn_ref, wrt_ref, rb_ref, w1_ref, w3_ref, w2_ref, gfin_ref,
                o_ref, xt_s, u_s, gidx_s, slot_s, comb_s, act_s, p2_s, y2_s, cnt_s):
    i = pl.program_id(0)
    g = pl.program_id(1)
    rt = h_ref.shape[0]
    sb = act_s.shape[0]

    @pl.when(jnp.logical_and(i == 0, g == 0))
    def _():
        u_s[...] = (lax.broadcasted_iota(jnp.int32, (rt, rt), 0)
                    < lax.broadcasted_iota(jnp.int32, (rt, rt), 1)).astype(F32).astype(BF16)

    @pl.when(g == 0)
    def _route():
        h = h_ref[...]
        o_ref[...] = h
        xt = _rms(h, gffn_ref[...]).astype(BF16)
        xt_s[...] = xt
        lt = _dot_nt(wrt_ref[...], xt) + rb_ref[...]
        gl = [lt[k:k + 1, :] for k in range(N_GROUPS)]
        gmax = jnp.maximum(jnp.maximum(gl[0], gl[1]), jnp.maximum(gl[2], gl[3]))
        gidx = jnp.where(gl[0] == gmax, 0, jnp.where(gl[1] == gmax, 1, jnp.where(gl[2] == gmax, 2, 3)))
        gidx = gidx.astype(jnp.int32)
        sumexp = (jnp.exp(gl[0] - gmax) + jnp.exp(gl[1] - gmax)) + (jnp.exp(gl[2] - gmax) + jnp.exp(gl[3] - gmax))
        p_g = 1.0 / sumexp
        esel = lt[SUBLANES + 3 * EXPERTS_PER_GROUP:SUBLANES + 4 * EXPERTS_PER_GROUP, :]
        for k in (2, 1, 0):
            esel = jnp.where(gidx == k, lt[SUBLANES + k * EXPERTS_PER_GROUP:SUBLANES + (k + 1) * EXPERTS_PER_GROUP, :],
                             esel)
        eidx = lax.broadcasted_iota(jnp.int32, (EXPERTS_PER_GROUP, rt), 0)
        m1 = jnp.max(esel, axis=0, keepdims=True)
        i1 = jnp.min(jnp.where(esel == m1, eidx, EXPERTS_PER_GROUP), axis=0, keepdims=True)
        rest = jnp.where(eidx == i1, -jnp.inf, esel)
        m2 = jnp.max(rest, axis=0, keepdims=True)
        i2 = jnp.min(jnp.where(rest == m2, eidx, EXPERTS_PER_GROUP), axis=0, keepdims=True)
        t2 = jnp.exp(m2 - m1)
        den = 1.0 + t2
        w_top1 = (1.0 / den) * p_g
        w_top2 = (t2 / den) * p_g
        within = jnp.where(eidx == i1, w_top1, 0.0) + jnp.where(eidx == i2, w_top2, 0.0)
        c_hi = within.astype(BF16).astype(F32)
        r1 = within - c_hi
        c_mid = r1.astype(BF16).astype(F32)
        c_lo = (r1 - c_mid).astype(BF16).astype(F32)
        comb_s[0:8, :] = c_hi
        comb_s[8:16, :] = c_mid
        comb_s[16:24, :] = c_lo
        comb_s[24:32, :] = jnp.zeros((8, rt), F32)
        onehot = (eidx == gidx).astype(F32)
        rank = _dot(onehot.astype(BF16), u_s[...])
        slot_s[...] = jnp.sum(onehot * rank, axis=0, keepdims=True).astype(jnp.int32)
        gidx_s[...] = gidx
        for k in range(N_GROUPS):
            cnt_s[k] = jnp.sum(onehot[k:k + 1, :]).astype(jnp.int32)

    n_blk = (cnt_s[g] + sb - 1) // sb

    def sub_block(j, half):
        half_rows = slice(half * sb, (half + 1) * sb)
        rows = lax.broadcasted_iota(jnp.int32, (sb, rt), 0) + j * sb
        hit = jnp.logical_and(rows == slot_s[...], gidx_s[...] == g)
        p = jnp.where(hit, 1.0, 0.0).astype(BF16)
        p2_s[half_rows, :] = p
        xc = _dot(p, xt_s[...]).astype(BF16)
        cexp = _dot_nt(p, comb_s[...].astype(BF16))
        cw = (cexp[:, 0:8] + cexp[:, 8:16]) + cexp[:, 16:24]
        for e in range(EXPERTS_PER_GROUP):
            h1 = _dot(xc, w1_ref[e])
            h3 = _dot(xc, w3_ref[e])
            a = (h1 * jax.nn.sigmoid(h1)) * h3 * cw[:, e:e + 1]
            act_s[:, e * EXPERT_FF:(e + 1) * EXPERT_FF] = a.astype(BF16)
        y2_s[half_rows, :] = _dot(act_s[...], w2_ref[...]).astype(BF16)

    def body(jj, carry):
        sub_block(2 * jj, 0)

        @pl.when(2 * jj + 1 < n_blk)
        def _():
            sub_block(2 * jj + 1, 1)

        @pl.when(2 * jj + 1 >= n_blk)
        def _():
            p2_s[sb:, :] = jnp.zeros((sb, rt), BF16)
            y2_s[sb:, :] = jnp.zeros((sb, D_MODEL), BF16)

        o_ref[...] += _dot_tn(p2_s[...], y2_s[...])
        return carry

    lax.fori_loop(0, (n_blk + 1) // 2, body, 0)

    @pl.when(g == N_GROUPS - 1)
    def _():
        o_ref[...] = _rms(o_ref[...], gfin_ref[...])


def _moe(h, gffn, wrt, rbias, w1, w3, w2, gfin, *, tile):
    tokens = h.shape[0]
    sb = min(MOE_SUB, tile)
    ff = EXPERTS_PER_GROUP * EXPERT_FF
    full = lambda shape: pl.BlockSpec(shape, lambda i, g: (0,) * len(shape))
    row_spec = pl.BlockSpec((tile, D_MODEL), lambda i, g: (i, 0))
    return pl.pallas_call(
        _moe_kernel,
        grid=(tokens // tile, N_GROUPS),
        in_specs=[row_spec, full((1, D_MODEL)), full((ROUTER_ROWS, D_MODEL)), full((ROUTER_ROWS, 1)),
                  pl.BlockSpec((None, EXPERTS_PER_GROUP, D_MODEL, EXPERT_FF), lambda i, g: (g, 0, 0, 0)),
                  pl.BlockSpec((None, EXPERTS_PER_GROUP, D_MODEL, EXPERT_FF), lambda i, g: (g, 0, 0, 0)),
                  pl.BlockSpec((None, ff, D_MODEL), lambda i, g: (g, 0, 0)),
                  full((1, D_MODEL))],
        out_specs=row_spec,
        out_shape=jax.ShapeDtypeStruct((tokens, D_MODEL), F32),
        scratch_shapes=[pltpu.VMEM((tile, D_MODEL), BF16), pltpu.VMEM((tile, tile), BF16),
                        pltpu.VMEM((1, tile), jnp.int32), pltpu.VMEM((1, tile), jnp.int32),
                        pltpu.VMEM((4 * SUBLANES, tile), F32), pltpu.VMEM((sb, ff), BF16),
                        pltpu.VMEM((2 * sb, tile), BF16), pltpu.VMEM((2 * sb, D_MODEL), BF16),
                        pltpu.SMEM((N_GROUPS,), jnp.int32)],
        compiler_params=pltpu.CompilerParams(dimension_semantics=("arbitrary", "arbitrary"),
                                             vmem_limit_bytes=VMEM_LIMIT),
        name="moe",
    )(h, gffn, wrt, rbias, w1, w3, w2, gfin)


HALF = D_MODEL // 2
SUB_PER_ROW = HALF // LANES
N_EXPERTS = N_GROUPS * EXPERTS_PER_GROUP


def _pack_bf16_pairs(x):
    bits = pltpu.bitcast(x.astype(BF16).astype(F32), jnp.uint32)
    return (bits[:, HALF:] & jnp.uint32(0xFFFF0000)) | (bits[:, :HALF] >> 16)


def _unpack_bf16_pairs(w):
    lo = pltpu.bitcast(w << 16, F32)
    hi = pltpu.bitcast(w & jnp.uint32(0xFFFF0000), F32)
    return lo, hi


def _route_kernel(h_ref, gffn_ref, wrt_ref, rb_ref, xp_ref, e_ref, w_ref, r_ref, cnt_ref, u_s, run_s):
    i = pl.program_id(0)
    rt = h_ref.shape[0]

    @pl.when(i == 0)
    def _():
        u_s[...] = (lax.broadcasted_iota(jnp.int32, (rt, rt), 0)
                    < lax.broadcasted_iota(jnp.int32, (rt, rt), 1)).astype(F32).astype(BF16)
        run_s[...] = jnp.zeros(run_s.shape, F32)

    xt = _rms(h_ref[...], gffn_ref[...])
    _store_planes(xp_ref, _pack_bf16_pairs(xt))
    lt = _dot_nt(wrt_ref[...], xt.astype(BF16)) + rb_ref[...]
    gl = [lt[k:k + 1, :] for k in range(N_GROUPS)]
    gmax = jnp.maximum(jnp.maximum(gl[0], gl[1]), jnp.maximum(gl[2], gl[3]))
    gidx = jnp.where(gl[0] == gmax, 0, jnp.where(gl[1] == gmax, 1, jnp.where(gl[2] == gmax, 2, 3)))
    gidx = gidx.astype(jnp.int32)
    sumexp = (jnp.exp(gl[0] - gmax) + jnp.exp(gl[1] - gmax)) + (jnp.exp(gl[2] - gmax) + jnp.exp(gl[3] - gmax))
    p_g = 1.0 / sumexp
    esel = lt[SUBLANES + 3 * EXPERTS_PER_GROUP:SUBLANES + 4 * EXPERTS_PER_GROUP, :]
    for k in (2, 1, 0):
        esel = jnp.where(gidx == k, lt[SUBLANES + k * EXPERTS_PER_GROUP:SUBLANES + (k + 1) * EXPERTS_PER_GROUP, :],
                         esel)
    eidx = lax.broadcasted_iota(jnp.int32, (EXPERTS_PER_GROUP, rt), 0)
    m1 = jnp.max(esel, axis=0, keepdims=True)
    i1 = jnp.min(jnp.where(esel == m1, eidx, EXPERTS_PER_GROUP), axis=0, keepdims=True)
    rest = jnp.where(eidx == i1, -jnp.inf, esel)
    m2 = jnp.max(rest, axis=0, keepdims=True)
    i2 = jnp.min(jnp.where(rest == m2, eidx, EXPERTS_PER_GROUP), axis=0, keepdims=True)
    t2 = jnp.exp(m2 - m1)
    den = 1.0 + t2
    w_ref[0:1, :] = (1.0 / den) * p_g
    w_ref[1:2, :] = (t2 / den) * p_g
    e1 = gidx * EXPERTS_PER_GROUP + i1
    e2 = gidx * EXPERTS_PER_GROUP + i2
    e_ref[0:1, :] = e1
    e_ref[1:2, :] = e2
    xid = lax.broadcasted_iota(jnp.int32, (N_EXPERTS, rt), 0)
    oh1 = (xid == e1).astype(F32)
    oh2 = (xid == e2).astype(F32)
    both = oh1 + oh2
    before = _dot(both.astype(BF16), u_s[...]) + run_s[:, 0:1]
    r_ref[0:1, :] = jnp.sum(oh1 * before, axis=0, keepdims=True).astype(jnp.int32)
    r_ref[1:2, :] = jnp.sum(oh2 * before, axis=0, keepdims=True).astype(jnp.int32)
    run_s[...] = run_s[...] + jnp.sum(both, axis=1, keepdims=True)
    cnt_ref[...] = run_s[...].astype(jnp.int32)


def _route(h, gffn, wrt, rbias, *, tile):
    tokens = h.shape[0]
    full = lambda shape: pl.BlockSpec(shape, lambda i: (0,) * len(shape))
    lanes = lambda rows: pl.BlockSpec((rows, tile), lambda i: (0, i))
    return pl.pallas_call(
        _route_kernel,
        grid=(tokens // tile,),
        in_specs=[pl.BlockSpec((tile, D_MODEL), lambda i: (i, 0)), full((1, D_MODEL)),
                  full((ROUTER_ROWS, D_MODEL)), full((ROUTER_ROWS, 1))],
        out_specs=[pl.BlockSpec((SUB_PER_ROW, tile, LANES), lambda i: (0, i, 0)), lanes(2), lanes(2), lanes(2),
                   full((N_EXPERTS, LANES))],
        out_shape=[jax.ShapeDtypeStruct((SUB_PER_ROW, tokens, LANES), jnp.uint32),
                   jax.ShapeDtypeStruct((2, tokens), jnp.int32),
                   jax.ShapeDtypeStruct((2, tokens), F32), jax.ShapeDtypeStruct((2, tokens), jnp.int32),
                   jax.ShapeDtypeStruct((N_EXPERTS, LANES), jnp.int32)],
        scratch_shapes=[pltpu.VMEM((tile, tile), BF16), pltpu.VMEM((N_EXPERTS, LANES), F32)],
        compiler_params=pltpu.CompilerParams(dimension_semantics=("arbitrary",), vmem_limit_bytes=VMEM_LIMIT),
        name="moe_route",
    )(h, gffn, wrt, rbias)


def _store_planes(ref, words):
    for j in range(SUB_PER_ROW):
        ref[j] = words[:, j * LANES:(j + 1) * LANES]


def _load_planes(ref):
    return jnp.concatenate([ref[j] for j in range(SUB_PER_ROW)], axis=1)


def _sub_row_index(row_of_token, n_rows):
    plane = jnp.arange(SUB_PER_ROW, dtype=jnp.int32)[:, None] * n_rows
    return (row_of_token[None, :] + plane).reshape(1, -1)


def _sc_mesh():
    return plsc.VectorSubcoreMesh(core_axis_name="core", subcore_axis_name="subcore")


def _sc_scatter_two(x_sub, idx_a, idx_b, n_out):
    n_in = x_sub.shape[0]

    @pl.kernel(out_type=jax.ShapeDtypeStruct((n_out, LANES), x_sub.dtype), mesh=_sc_mesh(), scratch_types=[])
    def scatter(x_hbm, a_hbm, b_hbm, o_hbm):
        def body(x_vmem, a_vmem, b_vmem):
            pltpu.sync_copy(x_vmem, o_hbm.at[a_vmem.at[0]])
            pltpu.sync_copy(x_vmem, o_hbm.at[b_vmem.at[0]])

        pltpu.emit_pipeline(
            body, grid=(n_in // SC_WINDOW,),
            in_specs=[pl.BlockSpec((SC_WINDOW, LANES), lambda i: (i, 0)),
                      pl.BlockSpec((1, SC_WINDOW), lambda i: (0, i)),
                      pl.BlockSpec((1, SC_WINDOW), lambda i: (0, i))],
            out_specs=[],
            core_axis_name=("core", "subcore"), dimension_semantics=(pltpu.PARALLEL,),
        )(x_hbm, a_hbm, b_hbm)

    return scatter(x_sub, idx_a, idx_b)


def _sc_gather(table, idx):
    n_out = idx.shape[1]

    @pl.kernel(out_type=jax.ShapeDtypeStruct((n_out, LANES), table.dtype), mesh=_sc_mesh())
    def gather(t_hbm, i_hbm, o_hbm):
        def body(i_vmem, o_vmem):
            pltpu.sync_copy(t_hbm.at[i_vmem.at[0]], o_vmem)

        pltpu.emit_pipeline(
            body, grid=(n_out // SC_WINDOW,),
            in_specs=[pl.BlockSpec((1, SC_WINDOW), lambda i: (0, i))],
            out_specs=[pl.BlockSpec((SC_WINDOW, LANES), lambda i: (i, 0))],
            core_axis_name=("core", "subcore"), dimension_semantics=(pltpu.PARALLEL,),
        )(i_hbm, o_hbm)

    return gather(table, idx)


def _expert_ffn_kernel(blk_e_ref, n_valid_ref, x_ref, w1_ref, w3_ref, w2_ref, y_ref):
    del blk_e_ref

    @pl.when(pl.program_id(0) < n_valid_ref[0])
    def _():
        lo, hi = _unpack_bf16_pairs(_load_planes(x_ref))
        xc = jnp.concatenate([lo.astype(BF16), hi.astype(BF16)], axis=1)
        h1 = _dot(xc, w1_ref[...])
        h3 = _dot(xc, w3_ref[...])
        act = ((h1 * jax.nn.sigmoid(h1)) * h3).astype(BF16)
        _store_planes(y_ref, _pack_bf16_pairs(_dot(act, w2_ref[...])))


def _expert_ffn(xs, blk_e, n_valid, w1, w3, w2):
    rows = xs.shape[1]
    n_blocks = rows // FFN_BLOCK
    live = lambda b, be, nv: jnp.minimum(b, nv[0] - 1)
    grid_spec = pltpu.PrefetchScalarGridSpec(
        num_scalar_prefetch=2, grid=(n_blocks,),
        in_specs=[pl.BlockSpec((SUB_PER_ROW, FFN_BLOCK, LANES), lambda b, be, nv: (0, live(b, be, nv), 0)),
                  pl.BlockSpec((None, D_MODEL, EXPERT_FF), lambda b, be, nv: (be[b], 0, 0)),
                  pl.BlockSpec((None, D_MODEL, EXPERT_FF), lambda b, be, nv: (be[b], 0, 0)),
                  pl.BlockSpec((None, EXPERT_FF, D_MODEL), lambda b, be, nv: (be[b], 0, 0))],
        out_specs=pl.BlockSpec((SUB_PER_ROW, FFN_BLOCK, LANES), lambda b, be, nv: (0, live(b, be, nv), 0)))
    return pl.pallas_call(
        _expert_ffn_kernel, grid_spec=grid_spec,
        out_shape=jax.ShapeDtypeStruct((SUB_PER_ROW, rows, LANES), jnp.uint32),
        compiler_params=pltpu.CompilerParams(dimension_semantics=("arbitrary",), vmem_limit_bytes=VMEM_LIMIT),
        name="moe_ffn",
    )(blk_e, n_valid, xs, w1, w3, w2)


def _combine_kernel(h_ref, ya_ref, yb_ref, wt_ref, gfin_ref, o_ref):
    a_lo, a_hi = _unpack_bf16_pairs(_load_planes(ya_ref))
    b_lo, b_hi = _unpack_bf16_pairs(_load_planes(yb_ref))
    wa = wt_ref[:, 0:1]
    wb = wt_ref[:, 1:2]
    o_ref[:, :HALF] = h_ref[:, :HALF] + (wa * a_lo + wb * b_lo)
    o_ref[:, HALF:] = h_ref[:, HALF:] + (wa * a_hi + wb * b_hi)
    o_ref[...] = _rms(o_ref[...], gfin_ref[...])


def _combine(h, ya, yb, wt, gfin, *, tile):
    tokens = h.shape[0]
    return pl.pallas_call(
        _combine_kernel,
        grid=(tokens // tile,),
        in_specs=[pl.BlockSpec((tile, D_MODEL), lambda i: (i, 0)),
                  pl.BlockSpec((SUB_PER_ROW, tile, LANES), lambda i: (0, i, 0)),
                  pl.BlockSpec((SUB_PER_ROW, tile, LANES), lambda i: (0, i, 0)), pl.BlockSpec((tile, 2), lambda i: (i, 0)),
                  pl.BlockSpec((1, D_MODEL), lambda i: (0, 0))],
        out_specs=pl.BlockSpec((tile, D_MODEL), lambda i: (i, 0)),
        out_shape=jax.ShapeDtypeStruct((tokens, D_MODEL), F32),
        compiler_params=pltpu.CompilerParams(dimension_semantics=("arbitrary",), vmem_limit_bytes=VMEM_LIMIT),
        name="moe_combine",
    )(h, ya, yb, wt, gfin)


def _moe_dispatch(h, gffn, wrt, rbias):
    tokens = h.shape[0]
    xp, e12, w12, r12, cnt = _route(h, gffn, wrt, rbias, tile=MOE_TILE)
    count = cnt[:, 0]
    padded = (count + FFN_BLOCK - 1) // FFN_BLOCK * FFN_BLOCK
    end = jnp.cumsum(padded)
    start = end - padded
    n_rows = 2 * tokens + N_EXPERTS * FFN_BLOCK
    n_blocks = n_rows // FFN_BLOCK
    n_valid = (end[-1:] // FFN_BLOCK).astype(jnp.int32)
    first_row = jnp.minimum(jnp.arange(n_blocks, dtype=jnp.int32), n_valid - 1) * FFN_BLOCK
    blk_e = jnp.sum((end[None, :] <= first_row[:, None]).astype(jnp.int32), axis=1)
    experts = jnp.arange(N_EXPERTS, dtype=jnp.int32)
    start_of = jnp.sum(jnp.where(e12[:, :, None] == experts, start.astype(jnp.int32), 0), axis=-1)
    rows_ab = start_of + r12
    idx_a, idx_b = _sub_row_index(rows_ab[0], n_rows), _sub_row_index(rows_ab[1], n_rows)
    xs = _sc_scatter_two(xp.reshape(-1, LANES), idx_a, idx_b, n_rows * SUB_PER_ROW)
    return xs.reshape(SUB_PER_ROW, n_rows, LANES), (blk_e, n_valid, idx_a, idx_b, w12)


def _moe_finish(h, xs, plan, w1, w3, w2, gfin):
    blk_e, n_valid, idx_a, idx_b, w12 = plan
    ys = _expert_ffn(xs, blk_e, n_valid, w1, w3, w2).reshape(-1, LANES)
    ya = _sc_gather(ys, idx_a).reshape(SUB_PER_ROW, -1, LANES)
    yb = _sc_gather(ys, idx_b).reshape(SUB_PER_ROW, -1, LANES)
    return _combine(h, ya, yb, w12.T, gfin, tile=MOE_TILE)


def kernel(x_prompt, x_sample, mem_prompt, cache_conv, cache_mem_k, cache_mem_v, norm_mix_g, w_in, gm_ln_g, gm_ln_b, gm_ws, gm_bs, conv_w, conv_b, cv_ln_g, cv_ln_b, w_out, norm_mem_g, norm_xa_g, xa_wq, xa_wk, xa_wv, xa_wo, norm_ffn_g, router_g, router_g_b, router_e, router_e_b, exp_w1, exp_w3, exp_w2, final_norm_g):
    depth = w_in.shape[0]
    assert depth == 1, "single-layer trunk"
    nb, seq, _ = x_prompt.shape
    ns = x_sample.shape[0]
    row = lambda a: a.reshape(1, -1)

    l = 0
    gmix, gxa, gffn, gmem = row(norm_mix_g[l]), row(norm_xa_g[l]), row(norm_ffn_g[l]), row(norm_mem_g[l])
    gfin = row(final_norm_g)
    win, wout = w_in[l].astype(BF16), w_out[l].astype(BF16)
    wq, wk, wv, wo = (w[l].astype(BF16) for w in (xa_wq, xa_wk, xa_wv, xa_wo))
    glng, glnb = row(gm_ln_g[l]), row(gm_ln_b[l])
    cw, cb, clng, clnb = conv_w[l], row(conv_b[l]), row(cv_ln_g[l]), row(cv_ln_b[l])
    ws, bst = gm_ws[l], gm_bs[l].T
    ws0 = jnp.repeat(gm_ws[l][:, 0, 0], GM_HEAD_DIM).reshape(1, GM_WIDTH)
    bs0 = jnp.repeat(gm_bs[l][:, 0], GM_HEAD_DIM).reshape(1, GM_WIDTH)

    n_exp = N_GROUPS * EXPERTS_PER_GROUP
    pad_g = SUBLANES - N_GROUPS
    pad_t = ROUTER_ROWS - SUBLANES - n_exp
    wrt = jnp.concatenate([router_g[l].T, jnp.zeros((pad_g, D_MODEL), F32),
                           router_e[l].reshape(D_MODEL, n_exp).T, jnp.zeros((pad_t, D_MODEL), F32)], axis=0).astype(BF16)
    rbias = jnp.concatenate([router_g_b[l], jnp.zeros((pad_g,), F32), router_e_b[l].reshape(n_exp),
                             jnp.zeros((pad_t,), F32)]).reshape(ROUTER_ROWS, 1)

    mk, mv, kb, vb = _memkv(mem_prompt.reshape(nb * N_MEM, D_MODEL), gmem, wk, wv)
    kb, vb = kb.reshape(nb, N_MEM, D_MODEL), vb.reshape(nb, N_MEM, D_MODEL)
    hp, conv_tail, gmv_p, w1, w3, w2 = _prompt_mix_attn(
        x_prompt, gmix, win, glng, glnb, ws, bst, cw, cb, clng, clnb, wout, gxa, wq, kb, vb, wo,
        exp_w1[l].reshape(-1, EXPERT_FF), exp_w3[l].reshape(-1, EXPERT_FF), exp_w2[l].reshape(-1, D_MODEL))
    h2d = hp.reshape(nb * seq, D_MODEL)
    xs, plan = _moe_dispatch(h2d, gffn, wrt, rbias)
    w1e, w3e, w2e = (w1.reshape(N_EXPERTS, D_MODEL, EXPERT_FF), w3.reshape(N_EXPERTS, D_MODEL, EXPERT_FF),
                     w2.reshape(N_EXPERTS, EXPERT_FF, D_MODEL))
    w1 = w1.reshape(N_GROUPS, EXPERTS_PER_GROUP, D_MODEL, EXPERT_FF)
    w3 = w3.reshape(N_GROUPS, EXPERTS_PER_GROUP, D_MODEL, EXPERT_FF)
    w2 = w2.reshape(N_GROUPS, EXPERTS_PER_GROUP * EXPERT_FF, D_MODEL)

    cache_t = jnp.transpose(cache_conv[l], (1, 0, 2))
    ys, glu_s, gmv_s, q_s = _sample_mix(x_sample.reshape(ns, D_MODEL), gmix, win, glng, glnb, ws0, bs0,
                                        cache_t, cw, cb, clng, clnb, wout, gxa, wq)
    qh, kh, vh = (_split_heads(q_s.reshape(ns, XA_HEADS, XA_HEAD_DIM)), _split_heads(cache_mem_k[l]),
                  _split_heads(cache_mem_v[l]))
    half = ns // 2
    o_first = _sample_attn(qh, kh, vh, 0, half)
    xs, o_first = lax.optimization_barrier((xs, o_first))
    y_prompt = _moe_finish(h2d, xs, plan, w1e, w3e, w2e, gfin)
    o_second = _sample_attn(qh, kh, vh, half, ns - half)
    o_s = _merge_heads(jnp.concatenate([o_first, o_second], axis=0))
    hs = _sample_proj(ys, o_s, wo)
    y_sample = _moe(hs, gffn, wrt, rbias, w1, w3, w2, gfin, tile=ns)

    conv_prompt = conv_tail[:, HALO - (CONV_WIDTH - 1):, :][None]
    conv_sample = jnp.transpose(jnp.concatenate([cache_t[1:], glu_s[None]], axis=0), (1, 0, 2))[None]
    return (y_prompt.reshape(nb, seq, D_MODEL), y_sample.reshape(ns, 1, D_MODEL), conv_prompt, conv_sample,
            gmv_p[None], gmv_s.reshape(1, ns, 1, GM_WIDTH),
            mk.reshape(1, nb, N_MEM, XA_HEADS, XA_HEAD_DIM), mv.reshape(1, nb, N_MEM, XA_HEADS, XA_HEAD_DIM))
```

```python
import functools

import jax
import jax.numpy as jnp
from jax import lax
from jax.experimental import pallas as pl
from jax.experimental.pallas import tpu as pltpu
from jax.experimental.pallas import tpu_sc as plsc

F32 = jnp.float32
BF16 = jnp.bfloat16

D_MODEL = 1024
GM_WIDTH = 512
CV_WIDTH = 512
GM_HEADS = 4
GM_HEAD_DIM = 128
CHUNK = 128
CONV_WIDTH = 31
IN_COLS = 2 * GM_WIDTH + 2 * CV_WIDTH
N_MEM = 256
XA_HEADS = 4
XA_HEAD_DIM = 256
N_GROUPS = 4
EXPERTS_PER_GROUP = 8
EXPERT_FF = 256
EPS = 1e-6

LANES = 128
SUBLANES = 8
HALO = 32
PROMPT_TILE = 512
MOE_TILE = 1024
MOE_SUB = 128
FFN_BLOCK = 512
SC_WINDOW = 128
SAMPLE_ATTN_BLOCK = 8
ROUTER_ROWS = 128
VMEM_LIMIT = 56 * 1024 * 1024


def _rms(x, g):
    return x * lax.rsqrt(jnp.mean(x * x, axis=-1, keepdims=True) + EPS) * g


def _ln(x, g, b):
    mu = jnp.mean(x, axis=-1, keepdims=True)
    xc = x - mu
    var = jnp.mean(xc * xc, axis=-1, keepdims=True)
    return xc * lax.rsqrt(var + EPS) * g + b


def _dot(a, b):
    return jnp.dot(a, b, preferred_element_type=F32)


def _dot_nt(a, b):
    return lax.dot_general(a, b, (((1,), (1,)), ((), ())), preferred_element_type=F32)


def _dot_tn(a, b):
    return lax.dot_general(a, b, (((0,), (0,)), ((), ())), preferred_element_type=F32)


def _memkv_kernel(mem_ref, g_ref, wk_ref, wv_ref, k_ref, v_ref, kb_ref, vb_ref):
    mn = _rms(mem_ref[...], g_ref[...]).astype(BF16)
    k = _dot(mn, wk_ref[...])
    v = _dot(mn, wv_ref[...])
    for h in range(XA_HEADS):
        cols = slice(h * XA_HEAD_DIM, (h + 1) * XA_HEAD_DIM)
        k_ref[:, h, :] = k[:, cols]
        v_ref[:, h, :] = v[:, cols]
    kb_ref[...] = k.astype(BF16)
    vb_ref[...] = v.astype(BF16)


def _memkv(mem2d, g, wk, wv):
    rows = mem2d.shape[0]
    tile = 512
    row_spec = pl.BlockSpec((tile, D_MODEL), lambda i: (i, 0))
    head_spec = pl.BlockSpec((tile, XA_HEADS, XA_HEAD_DIM), lambda i: (i, 0, 0))
    full = lambda shape: pl.BlockSpec(shape, lambda i: (0,) * len(shape))
    return pl.pallas_call(
        _memkv_kernel,
        grid=(rows // tile,),
        in_specs=[row_spec, full((1, D_MODEL)), full((D_MODEL, D_MODEL)), full((D_MODEL, D_MODEL))],
        out_specs=[head_spec, head_spec, row_spec, row_spec],
        out_shape=[jax.ShapeDtypeStruct((rows, XA_HEADS, XA_HEAD_DIM), F32),
                   jax.ShapeDtypeStruct((rows, XA_HEADS, XA_HEAD_DIM), F32),
                   jax.ShapeDtypeStruct((rows, D_MODEL), BF16), jax.ShapeDtypeStruct((rows, D_MODEL), BF16)],
        compiler_params=pltpu.CompilerParams(dimension_semantics=("arbitrary",), vmem_limit_bytes=VMEM_LIMIT),
        name="memkv",
    )(mem2d, g, wk, wv)


def _conv_block(g_s, cw_ref, start, lanes):
    assert start % SUBLANES == 0
    rows = CHUNK + HALO
    win = g_s[start:start + rows, lanes]
    off = HALO - (CONV_WIDTH - 1)
    acc = None
    for b in range(SUBLANES):
        shifted = pltpu.roll(win, rows - (off + b), axis=0) if off + b else win
        for k in range(b, CONV_WIDTH, SUBLANES):
            assert k - b + CHUNK + off + b <= rows
            term = shifted[k - b:k - b + CHUNK] * cw_ref[k:k + 1, lanes]
            acc = term if acc is None else acc + term
    return acc


def _cast_copies(src_refs, dst_refs, in_bufs, out_bufs, sem_in, sem_out, chunk, slot):
    copies_in, copies_out = [], []
    for i, (src, dst, ibuf, obuf) in enumerate(zip(src_refs, dst_refs, in_bufs, out_bufs)):
        rows = obuf.shape[0]
        start = pl.multiple_of(chunk * rows, rows)
        copies_in.append(pltpu.make_async_copy(src.at[pl.ds(start, rows), :], ibuf.at[slot], sem_in.at[slot, i]))
        copies_out.append(pltpu.make_async_copy(obuf, dst.at[pl.ds(start, rows), :], sem_out.at[i]))
    return copies_in, copies_out


def _prompt_kernel(*refs, tiles_per_seq, n_cast_chunks, n_cast_arrays):
    n = n_cast_arrays
    (x_ref, gmix_ref, win_ref, glng_ref, glnb_ref, ws_ref, bst_ref, cw_ref, cb_ref,
     clng_ref, clnb_ref, wout_ref, gxa_ref, wq_ref, kb_ref, vb_ref, wo_ref) = refs[:17]
    srcs = refs[17:17 + n]
    h_ref, conv_ref, gmv_ref = refs[17 + n:20 + n]
    dsts = refs[20 + n:20 + 2 * n]
    z_s, g_s, c_s, ab_s, o_s, y_prev, y_cur = refs[20 + 2 * n:27 + 2 * n]
    ins = refs[27 + 2 * n:27 + 3 * n]
    sts = refs[27 + 3 * n:27 + 4 * n]
    sem_in, sem_out = refs[27 + 4 * n:]
    s = pl.program_id(0)
    n_steps = pl.num_programs(0)
    tb = x_ref.shape[0]
    n_chunks = tb // CHUNK

    slot = s % 2
    chunk_of = lambda step: jnp.minimum(step, n_cast_chunks - 1)

    @pl.when(s == 0)
    def _():
        for c in _cast_copies(srcs, dsts, ins, sts, sem_in, sem_out, chunk_of(s), slot)[0]:
            c.start()

    @pl.when(s + 1 < n_steps)
    def _():
        for c in _cast_copies(srcs, dsts, ins, sts, sem_in, sem_out, chunk_of(s + 1), 1 - slot)[0]:
            c.start()

    cast_in, cast_out = _cast_copies(srcs, dsts, ins, sts, sem_in, sem_out, chunk_of(s), slot)
    for c in cast_in:
        c.wait()
    for ibuf, obuf in zip(ins, sts):
        obuf[...] = ibuf[slot].astype(BF16)
    for c in cast_out:
        c.start()

    @pl.when(s == 0)
    def _():
        y_prev[...] = jnp.zeros(y_prev.shape, F32)

    @pl.when(s % tiles_per_seq == 0)
    def _():
        g_s[0:HALO, :] = jnp.zeros((HALO, CV_WIDTH), F32)

    z_s[...] = _dot(_rms(x_ref[...], gmix_ref[...]).astype(BF16), win_ref[...])

    y = y_prev[...]
    qn = _rms(y, gxa_ref[...]).astype(BF16)
    q = (_dot(qn, wq_ref[...]) * (XA_HEAD_DIM ** -0.5)).astype(BF16)
    for h in range(XA_HEADS):
        cols = slice(h * XA_HEAD_DIM, (h + 1) * XA_HEAD_DIM)
        sc = _dot_nt(q[:, cols], kb_ref[:, cols])
        e = jnp.exp(sc - jnp.max(sc, axis=-1, keepdims=True))
        p = (e / jnp.sum(e, axis=-1, keepdims=True)).astype(BF16)
        o_s[:, cols] = _dot(p, vb_ref[:, cols]).astype(BF16)
    h_ref[...] = y + _dot(o_s[...], wo_ref[...])

    tri = (lax.broadcasted_iota(jnp.int32, (CHUNK, CHUNK), 0)
           >= lax.broadcasted_iota(jnp.int32, (CHUNK, CHUNK), 1))
    wm = [jnp.where(tri, ws_ref[h], 0.0).astype(BF16) for h in range(GM_HEADS)]

    for c in range(n_chunks):
        rows = slice(c * CHUNK, (c + 1) * CHUNK)
        for h in range(GM_HEADS):
            cu = slice(h * GM_HEAD_DIM, (h + 1) * GM_HEAD_DIM)
            cv = slice(GM_WIDTH + h * GM_HEAD_DIM, GM_WIDTH + (h + 1) * GM_HEAD_DIM)
            v = _ln(jax.nn.gelu(z_s[rows, cv]), glng_ref[:, cu], glnb_ref[:, cu])
            if c == n_chunks - 1:
                gmv_ref[:, cu] = v
            mixed = _dot(wm[h], v.astype(BF16)) + bst_ref[:, h:h + 1]
            ab_s[rows, cu] = (jax.nn.gelu(z_s[rows, cu]) * mixed).astype(BF16)
        ca = slice(2 * GM_WIDTH, 2 * GM_WIDTH + CV_WIDTH)
        cg = slice(2 * GM_WIDTH + CV_WIDTH, IN_COLS)
        g_s[HALO + c * CHUNK:HALO + (c + 1) * CHUNK, :] = z_s[rows, ca] * jax.nn.sigmoid(z_s[rows, cg])

    for c in range(n_chunks):
        rows = slice(c * CHUNK, (c + 1) * CHUNK)
        for cb in range(CV_WIDTH // LANES):
            lanes = slice(cb * LANES, (cb + 1) * LANES)
            c_s[rows, lanes] = _conv_block(g_s, cw_ref, c * CHUNK, lanes) + cb_ref[:, lanes]
        b = _ln(c_s[rows, :], clng_ref[...], clnb_ref[...])
        ab_s[rows, GM_WIDTH:] = (b * jax.nn.sigmoid(b)).astype(BF16)

    conv_ref[...] = g_s[tb:tb + HALO, :]
    g_s[0:HALO, :] = g_s[tb:tb + HALO, :]
    y_cur[...] = x_ref[...] + _dot(ab_s[...], wout_ref[...])
    y_prev[...] = y_cur[...]
    for c in cast_out:
        c.wait()


def _prompt_mix_attn(x, first_seq, n_seq, gmix, win, glng, glnb, ws, bst, cw, cb, clng, clnb, wout, gxa, wq, kb, vb,
                     wo, cast_arrays):
    _, seq, _ = x.shape
    tb = PROMPT_TILE
    tps = seq // tb
    last = n_seq * tps - 1
    n_cast = n_seq * tps
    chunk_rows = [a.shape[0] // n_cast for a in cast_arrays]
    for a, r in zip(cast_arrays, chunk_rows):
        assert a.shape[0] == r * n_cast and r % 16 == 0
    hbm = pl.BlockSpec(memory_space=pl.ANY)
    tile = lambda s, lag: jnp.clip(s - lag, 0, last)
    full = lambda shape: pl.BlockSpec(shape, lambda s: (0,) * len(shape))
    rows_in = pl.BlockSpec((None, tb, D_MODEL), lambda s: (first_seq + tile(s, 0) // tps, tile(s, 0) % tps, 0))
    rows_out = pl.BlockSpec((None, tb, D_MODEL), lambda s: (tile(s, 1) // tps, tile(s, 1) % tps, 0))
    kv_spec = pl.BlockSpec((None, N_MEM, D_MODEL), lambda s: (first_seq + tile(s, 1) // tps, 0, 0))
    seq_out = lambda r, c: pl.BlockSpec((None, r, c), lambda s: (tile(s, 0) // tps, 0, 0))
    n_arr = len(cast_arrays)
    return pl.pallas_call(
        functools.partial(_prompt_kernel, tiles_per_seq=tps, n_cast_chunks=n_cast, n_cast_arrays=n_arr),
        grid=(n_seq * tps + 1,),
        in_specs=[rows_in, full((1, D_MODEL)), full((D_MODEL, IN_COLS)), full((1, GM_WIDTH)), full((1, GM_WIDTH)),
                  full((GM_HEADS, CHUNK, CHUNK)), full((CHUNK, GM_HEADS)), full((CONV_WIDTH, CV_WIDTH)),
                  full((1, CV_WIDTH)), full((1, CV_WIDTH)), full((1, CV_WIDTH)), full((D_MODEL, D_MODEL)),
                  full((1, D_MODEL)), full((D_MODEL, D_MODEL)), kv_spec, kv_spec, full((D_MODEL, D_MODEL))]
                 + [hbm] * n_arr,
        out_specs=[rows_out, seq_out(HALO, CV_WIDTH), seq_out(CHUNK, GM_WIDTH)] + [hbm] * n_arr,
        out_shape=[jax.ShapeDtypeStruct((n_seq, seq, D_MODEL), F32),
                   jax.ShapeDtypeStruct((n_seq, HALO, CV_WIDTH), F32),
                   jax.ShapeDtypeStruct((n_seq, CHUNK, GM_WIDTH), F32)]
                  + [jax.ShapeDtypeStruct(a.shape, BF16) for a in cast_arrays],
        scratch_shapes=[pltpu.VMEM((tb, IN_COLS), F32), pltpu.VMEM((HALO + tb, CV_WIDTH), F32),
                        pltpu.VMEM((tb, CV_WIDTH), F32), pltpu.VMEM((tb, D_MODEL), BF16),
                        pltpu.VMEM((tb, D_MODEL), BF16), pltpu.VMEM((tb, D_MODEL), F32),
                        pltpu.VMEM((tb, D_MODEL), F32)]
                       + [pltpu.VMEM((2, r, a.shape[1]), F32) for a, r in zip(cast_arrays, chunk_rows)]
                       + [pltpu.VMEM((r, a.shape[1]), BF16) for a, r in zip(cast_arrays, chunk_rows)]
                       + [pltpu.SemaphoreType.DMA((2, n_arr)), pltpu.SemaphoreType.DMA((n_arr,))],
        compiler_params=pltpu.CompilerParams(dimension_semantics=("arbitrary",), vmem_limit_bytes=VMEM_LIMIT),
        name="prompt_mix_attn",
    )(x, gmix, win, glng, glnb, ws, bst, cw, cb, clng, clnb, wout, gxa, wq, kb, vb, wo, *cast_arrays)


def _sample_mix_kernel(x_ref, gmix_ref, win_ref, glng_ref, glnb_ref, ws0_ref, bs0_ref, cache_ref, cw_ref, cb_ref,
                       clng_ref, clnb_ref, wout_ref, gxa_ref, wq_ref,
                       y_ref, glu_ref, v_ref, q_ref, ab_s):
    x = x_ref[...]
    z = _dot(_rms(x, gmix_ref[...]).astype(BF16), win_ref[...])
    for h in range(GM_HEADS):
        cu = slice(h * GM_HEAD_DIM, (h + 1) * GM_HEAD_DIM)
        cv = slice(GM_WIDTH + h * GM_HEAD_DIM, GM_WIDTH + (h + 1) * GM_HEAD_DIM)
        v = _ln(jax.nn.gelu(z[:, cv]), glng_ref[:, cu], glnb_ref[:, cu])
        v_ref[:, cu] = v
        ab_s[:, cu] = (jax.nn.gelu(z[:, cu]) * (v * ws0_ref[:, cu] + bs0_ref[:, cu])).astype(BF16)
    glu = z[:, 2 * GM_WIDTH:2 * GM_WIDTH + CV_WIDTH] * jax.nn.sigmoid(z[:, 2 * GM_WIDTH + CV_WIDTH:])
    glu_ref[...] = glu
    conv = glu * cw_ref[CONV_WIDTH - 1:CONV_WIDTH, :] + cb_ref[...]
    for k in range(CONV_WIDTH - 1):
        conv = conv + cache_ref[k] * cw_ref[k:k + 1, :]
    b = _ln(conv, clng_ref[...], clnb_ref[...])
    ab_s[:, GM_WIDTH:] = (b * jax.nn.sigmoid(b)).astype(BF16)
    y = x + _dot(ab_s[...], wout_ref[...])
    y_ref[...] = y
    q_ref[...] = _dot(_rms(y, gxa_ref[...]).astype(BF16), wq_ref[...]) * (XA_HEAD_DIM ** -0.5)


def _sample_mix(x, gmix, win, glng, glnb, ws0, bs0, cache_t, cw, cb, clng, clnb, wout, gxa, wq):
    ns = x.shape[0]
    return pl.pallas_call(
        _sample_mix_kernel,
        out_shape=[jax.ShapeDtypeStruct((ns, D_MODEL), F32), jax.ShapeDtypeStruct((ns, CV_WIDTH), F32),
                   jax.ShapeDtypeStruct((ns, GM_WIDTH), F32), jax.ShapeDtypeStruct((ns, D_MODEL), F32)],
        scratch_shapes=[pltpu.VMEM((ns, D_MODEL), BF16)],
        compiler_params=pltpu.CompilerParams(vmem_limit_bytes=VMEM_LIMIT),
        name="sample_mix",
    )(x, gmix, win, glng, glnb, ws0, bs0, cache_t, cw, cb, clng, clnb, wout, gxa, wq)


def _sample_attn_kernel(q_ref, k_ref, v_ref, o_ref):
    ones = jnp.ones((LANES, LANES), BF16)
    rows = N_MEM * SUBLANES
    for i in range(q_ref.shape[0]):
        prod = (k_ref[i] * q_ref[i][None]).reshape(rows, LANES).astype(BF16)
        part = _dot(prod, ones).reshape(N_MEM, SUBLANES, LANES)
        s = part + pltpu.roll(part, XA_HEADS, axis=1)
        e = jnp.exp(s - jnp.max(s, axis=0, keepdims=True))
        p = e / jnp.sum(e, axis=0, keepdims=True)
        o_ref[i] = jnp.sum(p * v_ref[i], axis=0)


def _split_heads(a):
    halves = XA_HEAD_DIM // LANES
    assert halves * XA_HEADS == SUBLANES
    lead = a.shape[:-2]
    a = a.reshape(*lead, XA_HEADS, halves, LANES)
    return jnp.swapaxes(a, -3, -2).reshape(*lead, SUBLANES, LANES)


def _merge_heads(o):
    ns = o.shape[0]
    o = jnp.swapaxes(o.reshape(ns, XA_HEAD_DIM // LANES, XA_HEADS, LANES), 1, 2)
    return o.reshape(ns, XA_HEADS * XA_HEAD_DIM)


def _sample_attn(q, k, v, first, count):
    nb = SAMPLE_ATTN_BLOCK
    assert first % nb == 0 and count % nb == 0
    off = first // nb
    return pl.pallas_call(
        _sample_attn_kernel,
        grid=(count // nb,),
        in_specs=[pl.BlockSpec((nb, SUBLANES, LANES), lambda i: (i + off, 0, 0)),
                  pl.BlockSpec((nb, N_MEM, SUBLANES, LANES), lambda i: (i + off, 0, 0, 0)),
                  pl.BlockSpec((nb, N_MEM, SUBLANES, LANES), lambda i: (i + off, 0, 0, 0))],
        out_specs=pl.BlockSpec((nb, SUBLANES, LANES), lambda i: (i, 0, 0)),
        out_shape=jax.ShapeDtypeStruct((count, SUBLANES, LANES), F32),
        compiler_params=pltpu.CompilerParams(dimension_semantics=("arbitrary",), vmem_limit_bytes=VMEM_LIMIT),
        name="sample_attn",
    )(q, k, v)


def _sample_proj_kernel(y_ref, o_ref, wo_ref, h_ref):
    h_ref[...] = y_ref[...] + _dot(o_ref[...].astype(BF16), wo_ref[...])


def _sample_proj(y, o, wo):
    return pl.pallas_call(
        _sample_proj_kernel,
        out_shape=jax.ShapeDtypeStruct(y.shape, F32),
        compiler_params=pltpu.CompilerParams(vmem_limit_bytes=VMEM_LIMIT),
        name="sample_proj",
    )(y, o, wo)


def _moe_kernel(h_ref, gffn_ref, wrt_ref, rb_ref, w1_ref, w3_ref, w2_ref, gfin_ref,
                o_ref, xt_s, u_s, gidx_s, slot_s, comb_s, act_s, p2_s, y2_s, cnt_s):
    i = pl.program_id(0)
    g = pl.program_id(1)
    rt = h_ref.shape[0]
    sb = act_s.shape[0]

    @pl.when(jnp.logical_and(i == 0, g == 0))
    def _():
        u_s[...] = (lax.broadcasted_iota(jnp.int32, (rt, rt), 0)
                    < lax.broadcasted_iota(jnp.int32, (rt, rt), 1)).astype(F32).astype(BF16)

    @pl.when(g == 0)
    def _route():
        h = h_ref[...]
        o_ref[...] = h
        xt = _rms(h, gffn_ref[...]).astype(BF16)
        xt_s[...] = xt
        lt = _dot_nt(wrt_ref[...], xt) + rb_ref[...]
        gl = [lt[k:k + 1, :] for k in range(N_GROUPS)]
        gmax = jnp.maximum(jnp.maximum(gl[0], gl[1]), jnp.maximum(gl[2], gl[3]))
        gidx = jnp.where(gl[0] == gmax, 0, jnp.where(gl[1] == gmax, 1, jnp.where(gl[2] == gmax, 2, 3)))
        gidx = gidx.astype(jnp.int32)
        sumexp = (jnp.exp(gl[0] - gmax) + jnp.exp(gl[1] - gmax)) + (jnp.exp(gl[2] - gmax) + jnp.exp(gl[3] - gmax))
        p_g = 1.0 / sumexp
        esel = lt[SUBLANES + 3 * EXPERTS_PER_GROUP:SUBLANES + 4 * EXPERTS_PER_GROUP, :]
        for k in (2, 1, 0):
            esel = jnp.where(gidx == k, lt[SUBLANES + k * EXPERTS_PER_GROUP:SUBLANES + (k + 1) * EXPERTS_PER_GROUP, :],
                             esel)
        eidx = lax.broadcasted_iota(jnp.int32, (EXPERTS_PER_GROUP, rt), 0)
        m1 = jnp.max(esel, axis=0, keepdims=True)
        i1 = jnp.min(jnp.where(esel == m1, eidx, EXPERTS_PER_GROUP), axis=0, keepdims=True)
        rest = jnp.where(eidx == i1, -jnp.inf, esel)
        m2 = jnp.max(rest, axis=0, keepdims=True)
        i2 = jnp.min(jnp.where(rest == m2, eidx, EXPERTS_PER_GROUP), axis=0, keepdims=True)
        t2 = jnp.exp(m2 - m1)
        den = 1.0 + t2
        w_top1 = (1.0 / den) * p_g
        w_top2 = (t2 / den) * p_g
        within = jnp.where(eidx == i1, w_top1, 0.0) + jnp.where(eidx == i2, w_top2, 0.0)
        c_hi = within.astype(BF16).astype(F32)
        r1 = within - c_hi
        c_mid = r1.astype(BF16).astype(F32)
        c_lo = (r1 - c_mid).astype(BF16).astype(F32)
        comb_s[0:8, :] = c_hi
        comb_s[8:16, :] = c_mid
        comb_s[16:24, :] = c_lo
        comb_s[24:32, :] = jnp.zeros((8, rt), F32)
        onehot = (eidx == gidx).astype(F32)
        rank = _dot(onehot.astype(BF16), u_s[...])
        slot_s[...] = jnp.sum(onehot * rank, axis=0, keepdims=True).astype(jnp.int32)
        gidx_s[...] = gidx
        for k in range(N_GROUPS):
            cnt_s[k] = jnp.sum(onehot[k:k + 1, :]).astype(jnp.int32)

    n_blk = (cnt_s[g] + sb - 1) // sb

    def sub_block(j, half):
        half_rows = slice(half * sb, (half + 1) * sb)
        rows = lax.broadcasted_iota(jnp.int32, (sb, rt), 0) + j * sb
        hit = jnp.logical_and(rows == slot_s[...], gidx_s[...] == g)
        p = jnp.where(hit, 1.0, 0.0).astype(BF16)
        p2_s[half_rows, :] = p
        xc = _dot(p, xt_s[...]).astype(BF16)
        cexp = _dot_nt(p, comb_s[...].astype(BF16))
        cw = (cexp[:, 0:8] + cexp[:, 8:16]) + cexp[:, 16:24]
        for e in range(EXPERTS_PER_GROUP):
            h1 = _dot(xc, w1_ref[e])
            h3 = _dot(xc, w3_ref[e])
            a = (h1 * jax.nn.sigmoid(h1)) * h3 * cw[:, e:e + 1]
            act_s[:, e * EXPERT_FF:(e + 1) * EXPERT_FF] = a.astype(BF16)
        y2_s[half_rows, :] = _dot(act_s[...], w2_ref[...]).astype(BF16)

    def body(jj, carry):
        sub_block(2 * jj, 0)

        @pl.when(2 * jj + 1 < n_blk)
        def _():
            sub_block(2 * jj + 1, 1)

        @pl.when(2 * jj + 1 >= n_blk)
        def _():
            p2_s[sb:, :] = jnp.zeros((sb, rt), BF16)
            y2_s[sb:, :] = jnp.zeros((sb, D_MODEL), BF16)

        o_ref[...] += _dot_tn(p2_s[...], y2_s[...])
        return carry

    lax.fori_loop(0, (n_blk + 1) // 2, body, 0)

    @pl.when(g == N_GROUPS - 1)
    def _():
        o_ref[...] = _rms(o_ref[...], gfin_ref[...])


def _moe(h, gffn, wrt, rbias, w1, w3, w2, gfin, *, tile):
    tokens = h.shape[0]
    sb = min(MOE_SUB, tile)
    ff = EXPERTS_PER_GROUP * EXPERT_FF
    full = lambda shape: pl.BlockSpec(shape, lambda i, g: (0,) * len(shape))
    row_spec = pl.BlockSpec((tile, D_MODEL), lambda i, g: (i, 0))
    return pl.pallas_call(
        _moe_kernel,
        grid=(tokens // tile, N_GROUPS),
        in_specs=[row_spec, full((1, D_MODEL)), full((ROUTER_ROWS, D_MODEL)), full((ROUTER_ROWS, 1)),
                  pl.BlockSpec((None, EXPERTS_PER_GROUP, D_MODEL, EXPERT_FF), lambda i, g: (g, 0, 0, 0)),
                  pl.BlockSpec((None, EXPERTS_PER_GROUP, D_MODEL, EXPERT_FF), lambda i, g: (g, 0, 0, 0)),
                  pl.BlockSpec((None, ff, D_MODEL), lambda i, g: (g, 0, 0)),
                  full((1, D_MODEL))],
        out_specs=row_spec,
        out_shape=jax.ShapeDtypeStruct((tokens, D_MODEL), F32),
        scratch_shapes=[pltpu.VMEM((tile, D_MODEL), BF16), pltpu.VMEM((tile, tile), BF16),
                        pltpu.VMEM((1, tile), jnp.int32), pltpu.VMEM((1, tile), jnp.int32),
                        pltpu.VMEM((4 * SUBLANES, tile), F32), pltpu.VMEM((sb, ff), BF16),
                        pltpu.VMEM((2 * sb, tile), BF16), pltpu.VMEM((2 * sb, D_MODEL), BF16),
                        pltpu.SMEM((N_GROUPS,), jnp.int32)],
        compiler_params=pltpu.CompilerParams(dimension_semantics=("arbitrary", "arbitrary"),
                                             vmem_limit_bytes=VMEM_LIMIT),
        name="moe",
    )(h, gffn, wrt, rbias, w1, w3, w2, gfin)


HALF = D_MODEL // 2
SUB_PER_ROW = HALF // LANES
N_EXPERTS = N_GROUPS * EXPERTS_PER_GROUP


def _pack_bf16_pairs(x):
    bits = pltpu.bitcast(x.astype(BF16).astype(F32), jnp.uint32)
    return (bits[:, HALF:] & jnp.uint32(0xFFFF0000)) | (bits[:, :HALF] >> 16)


def _unpack_bf16_pairs(w):
    lo = pltpu.bitcast(w << 16, F32)
    hi = pltpu.bitcast(w & jnp.uint32(0xFFFF0000), F32)
    return lo, hi


def _route_kernel(h_ref, gffn_ref, wrt_ref, rb_ref, xp_ref, e_ref, w_ref, r_ref, cnt_ref, u_s, run_s):
    i = pl.program_id(0)
    rt = h_ref.shape[0]

    @pl.when(i == 0)
    def _():
        u_s[...] = (lax.broadcasted_iota(jnp.int32, (rt, rt), 0)
                    < lax.broadcasted_iota(jnp.int32, (rt, rt), 1)).astype(F32).astype(BF16)
        run_s[...] = jnp.zeros(run_s.shape, F32)

    xt = _rms(h_ref[...], gffn_ref[...])
    _store_planes(xp_ref, _pack_bf16_pairs(xt))
    lt = _dot_nt(wrt_ref[...], xt.astype(BF16)) + rb_ref[...]
    gl = [lt[k:k + 1, :] for k in range(N_GROUPS)]
    gmax = jnp.maximum(jnp.maximum(gl[0], gl[1]), jnp.maximum(gl[2], gl[3]))
    gidx = jnp.where(gl[0] == gmax, 0, jnp.where(gl[1] == gmax, 1, jnp.where(gl[2] == gmax, 2, 3)))
    gidx = gidx.astype(jnp.int32)
    sumexp = (jnp.exp(gl[0] - gmax) + jnp.exp(gl[1] - gmax)) + (jnp.exp(gl[2] - gmax) + jnp.exp(gl[3] - gmax))
    p_g = 1.0 / sumexp
    esel = lt[SUBLANES + 3 * EXPERTS_PER_GROUP:SUBLANES + 4 * EXPERTS_PER_GROUP, :]
    for k in (2, 1, 0):
        esel = jnp.where(gidx == k, lt[SUBLANES + k * EXPERTS_PER_GROUP:SUBLANES + (k + 1) * EXPERTS_PER_GROUP, :],
                         esel)
    eidx = lax.broadcasted_iota(jnp.int32, (EXPERTS_PER_GROUP, rt), 0)
    m1 = jnp.max(esel, axis=0, keepdims=True)
    i1 = jnp.min(jnp.where(esel == m1, eidx, EXPERTS_PER_GROUP), axis=0, keepdims=True)
    rest = jnp.where(eidx == i1, -jnp.inf, esel)
    m2 = jnp.max(rest, axis=0, keepdims=True)
    i2 = jnp.min(jnp.where(rest == m2, eidx, EXPERTS_PER_GROUP), axis=0, keepdims=True)
    t2 = jnp.exp(m2 - m1)
    den = 1.0 + t2
    w_ref[0:1, :] = (1.0 / den) * p_g
    w_ref[1:2, :] = (t2 / den) * p_g
    e1 = gidx * EXPERTS_PER_GROUP + i1
    e2 = gidx * EXPERTS_PER_GROUP + i2
    e_ref[0:1, :] = e1
    e_ref[1:2, :] = e2
    xid = lax.broadcasted_iota(jnp.int32, (N_EXPERTS, rt), 0)
    oh1 = (xid == e1).astype(F32)
    oh2 = (xid == e2).astype(F32)
    both = oh1 + oh2
    before = _dot(both.astype(BF16), u_s[...]) + run_s[:, 0:1]
    r_ref[0:1, :] = jnp.sum(oh1 * before, axis=0, keepdims=True).astype(jnp.int32)
    r_ref[1:2, :] = jnp.sum(oh2 * before, axis=0, keepdims=True).astype(jnp.int32)
    run_s[...] = run_s[...] + jnp.sum(both, axis=1, keepdims=True)
    cnt_ref[...] = run_s[...].astype(jnp.int32)


def _route(h, gffn, wrt, rbias, *, tile):
    tokens = h.shape[0]
    full = lambda shape: pl.BlockSpec(shape, lambda i: (0,) * len(shape))
    lanes = lambda rows: pl.BlockSpec((rows, tile), lambda i: (0, i))
    return pl.pallas_call(
        _route_kernel,
        grid=(tokens // tile,),
        in_specs=[pl.BlockSpec((tile, D_MODEL), lambda i: (i, 0)), full((1, D_MODEL)),
                  full((ROUTER_ROWS, D_MODEL)), full((ROUTER_ROWS, 1))],
        out_specs=[pl.BlockSpec((SUB_PER_ROW, tile, LANES), lambda i: (0, i, 0)), lanes(2), lanes(2), lanes(2),
                   full((N_EXPERTS, LANES))],
        out_shape=[jax.ShapeDtypeStruct((SUB_PER_ROW, tokens, LANES), jnp.uint32),
                   jax.ShapeDtypeStruct((2, tokens), jnp.int32),
                   jax.ShapeDtypeStruct((2, tokens), F32), jax.ShapeDtypeStruct((2, tokens), jnp.int32),
                   jax.ShapeDtypeStruct((N_EXPERTS, LANES), jnp.int32)],
        scratch_shapes=[pltpu.VMEM((tile, tile), BF16), pltpu.VMEM((N_EXPERTS, LANES), F32)],
        compiler_params=pltpu.CompilerParams(dimension_semantics=("arbitrary",), vmem_limit_bytes=VMEM_LIMIT),
        name="moe_route",
    )(h, gffn, wrt, rbias)


def _store_planes(ref, words):
    for j in range(SUB_PER_ROW):
        ref[j] = words[:, j * LANES:(j + 1) * LANES]


def _load_planes(ref):
    return jnp.concatenate([ref[j] for j in range(SUB_PER_ROW)], axis=1)


def _sub_row_index(row_of_token, n_rows):
    plane = jnp.arange(SUB_PER_ROW, dtype=jnp.int32)[:, None] * n_rows
    return (row_of_token[None, :] + plane).reshape(1, -1)


def _sc_mesh():
    return plsc.VectorSubcoreMesh(core_axis_name="core", subcore_axis_name="subcore")


def _sc_scatter_two(x_sub, idx_a, idx_b, n_out):
    n_in = x_sub.shape[0]

    @pl.kernel(out_type=jax.ShapeDtypeStruct((n_out, LANES), x_sub.dtype), mesh=_sc_mesh(), scratch_types=[])
    def scatter(x_hbm, a_hbm, b_hbm, o_hbm):
        def body(x_vmem, a_vmem, b_vmem):
            pltpu.sync_copy(x_vmem, o_hbm.at[a_vmem.at[0]])
            pltpu.sync_copy(x_vmem, o_hbm.at[b_vmem.at[0]])

        pltpu.emit_pipeline(
            body, grid=(n_in // SC_WINDOW,),
            in_specs=[pl.BlockSpec((SC_WINDOW, LANES), lambda i: (i, 0)),
                      pl.BlockSpec((1, SC_WINDOW), lambda i: (0, i)),
                      pl.BlockSpec((1, SC_WINDOW), lambda i: (0, i))],
            out_specs=[],
            core_axis_name=("core", "subcore"), dimension_semantics=(pltpu.PARALLEL,),
        )(x_hbm, a_hbm, b_hbm)

    return scatter(x_sub, idx_a, idx_b)


def _sc_gather(table, idx):
    n_out = idx.shape[1]

    @pl.kernel(out_type=jax.ShapeDtypeStruct((n_out, LANES), table.dtype), mesh=_sc_mesh())
    def gather(t_hbm, i_hbm, o_hbm):
        def body(i_vmem, o_vmem):
            pltpu.sync_copy(t_hbm.at[i_vmem.at[0]], o_vmem)

        pltpu.emit_pipeline(
            body, grid=(n_out // SC_WINDOW,),
            in_specs=[pl.BlockSpec((1, SC_WINDOW), lambda i: (0, i))],
            out_specs=[pl.BlockSpec((SC_WINDOW, LANES), lambda i: (i, 0))],
            core_axis_name=("core", "subcore"), dimension_semantics=(pltpu.PARALLEL,),
        )(i_hbm, o_hbm)

    return gather(table, idx)


def _expert_ffn_kernel(blk_e_ref, n_valid_ref, x_ref, w1_ref, w3_ref, w2_ref, y_ref):
    del blk_e_ref

    @pl.when(pl.program_id(0) < n_valid_ref[0])
    def _():
        lo, hi = _unpack_bf16_pairs(_load_planes(x_ref))
        xc = jnp.concatenate([lo.astype(BF16), hi.astype(BF16)], axis=1)
        h1 = _dot(xc, w1_ref[...])
        h3 = _dot(xc, w3_ref[...])
        act = ((h1 * jax.nn.sigmoid(h1)) * h3).astype(BF16)
        _store_planes(y_ref, _pack_bf16_pairs(_dot(act, w2_ref[...])))


def _expert_ffn(xs, blk_e, n_valid, w1, w3, w2):
    rows = xs.shape[1]
    n_blocks = rows // FFN_BLOCK
    live = lambda b, be, nv: jnp.minimum(b, nv[0] - 1)
    grid_spec = pltpu.PrefetchScalarGridSpec(
        num_scalar_prefetch=2, grid=(n_blocks,),
        in_specs=[pl.BlockSpec((SUB_PER_ROW, FFN_BLOCK, LANES), lambda b, be, nv: (0, live(b, be, nv), 0)),
                  pl.BlockSpec((None, D_MODEL, EXPERT_FF), lambda b, be, nv: (be[b], 0, 0)),
                  pl.BlockSpec((None, D_MODEL, EXPERT_FF), lambda b, be, nv: (be[b], 0, 0)),
                  pl.BlockSpec((None, EXPERT_FF, D_MODEL), lambda b, be, nv: (be[b], 0, 0))],
        out_specs=pl.BlockSpec((SUB_PER_ROW, FFN_BLOCK, LANES), lambda b, be, nv: (0, live(b, be, nv), 0)))
    return pl.pallas_call(
        _expert_ffn_kernel, grid_spec=grid_spec,
        out_shape=jax.ShapeDtypeStruct((SUB_PER_ROW, rows, LANES), jnp.uint32),
        compiler_params=pltpu.CompilerParams(dimension_semantics=("arbitrary",), vmem_limit_bytes=VMEM_LIMIT),
        name="moe_ffn",
    )(blk_e, n_valid, xs, w1, w3, w2)


def _combine_kernel(h_ref, ya_ref, yb_ref, wt_ref, gfin_ref, *rest):
    o_ref = rest[-1]
    a_lo, a_hi = _unpack_bf16_pairs(_load_planes(ya_ref))
    b_lo, b_hi = _unpack_bf16_pairs(_load_planes(yb_ref))
    wa = wt_ref[:, 0:1]
    wb = wt_ref[:, 1:2]
    o_ref[:, :HALF] = h_ref[:, :HALF] + (wa * a_lo + wb * b_lo)
    o_ref[:, HALF:] = h_ref[:, HALF:] + (wa * a_hi + wb * b_hi)
    o_ref[...] = _rms(o_ref[...], gfin_ref[...])


def _combine(h, ya, yb, wt, gfin, *, tile, total_tokens, first_token, earlier=None):
    tokens = h.shape[0]
    first_tile = first_token // tile
    in_specs = [pl.BlockSpec((tile, D_MODEL), lambda i: (i, 0)),
                pl.BlockSpec((SUB_PER_ROW, tile, LANES), lambda i: (0, i, 0)),
                pl.BlockSpec((SUB_PER_ROW, tile, LANES), lambda i: (0, i, 0)), pl.BlockSpec((tile, 2), lambda i: (i, 0)),
                pl.BlockSpec((1, D_MODEL), lambda i: (0, 0))]
    args = [h, ya, yb, wt, gfin]
    aliases = {}
    if earlier is not None:
        in_specs.append(pl.BlockSpec(memory_space=pl.ANY))
        args.append(earlier)
        aliases = {len(args) - 1: 0}
    return pl.pallas_call(
        _combine_kernel,
        grid=(tokens // tile,),
        in_specs=in_specs,
        out_specs=pl.BlockSpec((tile, D_MODEL), lambda i: (i + first_tile, 0)),
        out_shape=jax.ShapeDtypeStruct((total_tokens, D_MODEL), F32),
        input_output_aliases=aliases,
        compiler_params=pltpu.CompilerParams(dimension_semantics=("arbitrary",), vmem_limit_bytes=VMEM_LIMIT),
        name="moe_combine",
    )(*args)


def _moe_dispatch(h, gffn, wrt, rbias):
    tokens = h.shape[0]
    xp, e12, w12, r12, cnt = _route(h, gffn, wrt, rbias, tile=MOE_TILE)
    count = cnt[:, 0]
    padded = (count + FFN_BLOCK - 1) // FFN_BLOCK * FFN_BLOCK
    end = jnp.cumsum(padded)
    start = end - padded
    n_rows = 2 * tokens + N_EXPERTS * FFN_BLOCK
    n_blocks = n_rows // FFN_BLOCK
    n_valid = (end[-1:] // FFN_BLOCK).astype(jnp.int32)
    first_row = jnp.minimum(jnp.arange(n_blocks, dtype=jnp.int32), n_valid - 1) * FFN_BLOCK
    blk_e = jnp.sum((end[None, :] <= first_row[:, None]).astype(jnp.int32), axis=1)
    experts = jnp.arange(N_EXPERTS, dtype=jnp.int32)
    start_of = jnp.sum(jnp.where(e12[:, :, None] == experts, start.astype(jnp.int32), 0), axis=-1)
    rows_ab = start_of + r12
    idx_a, idx_b = _sub_row_index(rows_ab[0], n_rows), _sub_row_index(rows_ab[1], n_rows)
    xs = _sc_scatter_two(xp.reshape(-1, LANES), idx_a, idx_b, n_rows * SUB_PER_ROW)
    return xs.reshape(SUB_PER_ROW, n_rows, LANES), (blk_e, n_valid, idx_a, idx_b, w12)


def _moe_finish(h, xs, plan, w1, w3, w2, gfin, *, total_tokens, first_token, earlier=None):
    blk_e, n_valid, idx_a, idx_b, w12 = plan
    ys = _expert_ffn(xs, blk_e, n_valid, w1, w3, w2).reshape(-1, LANES)
    ya = _sc_gather(ys, idx_a).reshape(SUB_PER_ROW, -1, LANES)
    yb = _sc_gather(ys, idx_b).reshape(SUB_PER_ROW, -1, LANES)
    return _combine(h, ya, yb, w12.T, gfin, tile=MOE_TILE, total_tokens=total_tokens, first_token=first_token,
                    earlier=earlier)


def kernel(x_prompt, x_sample, mem_prompt, cache_conv, cache_mem_k, cache_mem_v, norm_mix_g, w_in, gm_ln_g, gm_ln_b, gm_ws, gm_bs, conv_w, conv_b, cv_ln_g, cv_ln_b, w_out, norm_mem_g, norm_xa_g, xa_wq, xa_wk, xa_wv, xa_wo, norm_ffn_g, router_g, router_g_b, router_e, router_e_b, exp_w1, exp_w3, exp_w2, final_norm_g):
    depth = w_in.shape[0]
    assert depth == 1, "single-layer trunk"
    nb, seq, _ = x_prompt.shape
    ns = x_sample.shape[0]
    row = lambda a: a.reshape(1, -1)

    l = 0
    gmix, gxa, gffn, gmem = row(norm_mix_g[l]), row(norm_xa_g[l]), row(norm_ffn_g[l]), row(norm_mem_g[l])
    gfin = row(final_norm_g)
    win, wout = w_in[l].astype(BF16), w_out[l].astype(BF16)
    wq, wk, wv, wo = (w[l].astype(BF16) for w in (xa_wq, xa_wk, xa_wv, xa_wo))
    glng, glnb = row(gm_ln_g[l]), row(gm_ln_b[l])
    cw, cb, clng, clnb = conv_w[l], row(conv_b[l]), row(cv_ln_g[l]), row(cv_ln_b[l])
    ws, bst = gm_ws[l], gm_bs[l].T
    ws0 = jnp.repeat(gm_ws[l][:, 0, 0], GM_HEAD_DIM).reshape(1, GM_WIDTH)
    bs0 = jnp.repeat(gm_bs[l][:, 0], GM_HEAD_DIM).reshape(1, GM_WIDTH)

    n_exp = N_GROUPS * EXPERTS_PER_GROUP
    pad_g = SUBLANES - N_GROUPS
    pad_t = ROUTER_ROWS - SUBLANES - n_exp
    wrt = jnp.concatenate([router_g[l].T, jnp.zeros((pad_g, D_MODEL), F32),
                           router_e[l].reshape(D_MODEL, n_exp).T, jnp.zeros((pad_t, D_MODEL), F32)], axis=0).astype(BF16)
    rbias = jnp.concatenate([router_g_b[l], jnp.zeros((pad_g,), F32), router_e_b[l].reshape(n_exp),
                             jnp.zeros((pad_t,), F32)]).reshape(ROUTER_ROWS, 1)

    mk, mv, kb, vb = _memkv(mem_prompt.reshape(nb * N_MEM, D_MODEL), gmem, wk, wv)
    kb, vb = kb.reshape(nb, N_MEM, D_MODEL), vb.reshape(nb, N_MEM, D_MODEL)
    half_seq = nb // 2
    half_tok = half_seq * seq
    mix_args = (gmix, win, glng, glnb, ws, bst, cw, cb, clng, clnb, wout, gxa, wq, kb, vb, wo)
    hp_a, tail_a, gmv_a, w1, w3 = _prompt_mix_attn(
        x_prompt, 0, half_seq, *mix_args, [exp_w1[l].reshape(-1, EXPERT_FF), exp_w3[l].reshape(-1, EXPERT_FF)])
    xs_a, plan_a = _moe_dispatch(hp_a.reshape(half_tok, D_MODEL), gffn, wrt, rbias)
    hp_b, tail_b, gmv_b, w2 = _prompt_mix_attn(
        x_prompt, half_seq, nb - half_seq, *mix_args, [exp_w2[l].reshape(-1, D_MODEL)])
    xs_b, plan_b = _moe_dispatch(hp_b.reshape(-1, D_MODEL), gffn, wrt, rbias)
    w1e, w3e, w2e = (w1.reshape(N_EXPERTS, D_MODEL, EXPERT_FF), w3.reshape(N_EXPERTS, D_MODEL, EXPERT_FF),
                     w2.reshape(N_EXPERTS, EXPERT_FF, D_MODEL))
    y_prompt = _moe_finish(hp_a.reshape(half_tok, D_MODEL), xs_a, plan_a, w1e, w3e, w2e, gfin,
                           total_tokens=nb * seq, first_token=0)
    y_prompt = _moe_finish(hp_b.reshape(-1, D_MODEL), xs_b, plan_b, w1e, w3e, w2e, gfin,
                           total_tokens=nb * seq, first_token=half_tok, earlier=y_prompt)
    conv_tail = jnp.concatenate([tail_a, tail_b], axis=0)
    gmv_p = jnp.concatenate([gmv_a, gmv_b], axis=0)
    w1 = w1.reshape(N_GROUPS, EXPERTS_PER_GROUP, D_MODEL, EXPERT_FF)
    w3 = w3.reshape(N_GROUPS, EXPERTS_PER_GROUP, D_MODEL, EXPERT_FF)
    w2 = w2.reshape(N_GROUPS, EXPERTS_PER_GROUP * EXPERT_FF, D_MODEL)

    cache_t = jnp.transpose(cache_conv[l], (1, 0, 2))
    ys, glu_s, gmv_s, q_s = _sample_mix(x_sample.reshape(ns, D_MODEL), gmix, win, glng, glnb, ws0, bs0,
                                        cache_t, cw, cb, clng, clnb, wout, gxa, wq)
    qh, kh, vh = (_split_heads(q_s.reshape(ns, XA_HEADS, XA_HEAD_DIM)), _split_heads(cache_mem_k[l]),
                  _split_heads(cache_mem_v[l]))
    o_s = _merge_heads(_sample_attn(qh, kh, vh, 0, ns))
    hs = _sample_proj(ys, o_s, wo)
    y_sample = _moe(hs, gffn, wrt, rbias, w1, w3, w2, gfin, tile=ns)

    conv_prompt = conv_tail[:, HALO - (CONV_WIDTH - 1):, :][None]
    conv_sample = jnp.transpose(jnp.concatenate([cache_t[1:], glu_s[None]], axis=0), (1, 0, 2))[None]
    return (y_prompt.reshape(nb, seq, D_MODEL), y_sample.reshape(ns, 1, D_MODEL), conv_prompt, conv_sample,
            gmv_p[None], gmv_s.reshape(1, ns, 1, GM_WIDTH),
            mk.reshape(1, nb, N_MEM, XA_HEADS, XA_HEAD_DIM), mv.reshape(1, nb, N_MEM, XA_HEADS, XA_HEAD_DIM))
```

```python
import functools

import jax
import jax.numpy as jnp
from jax import lax
from jax.experimental import pallas as pl
from jax.experimental.pallas import tpu as pltpu
from jax.experimental.pallas import tpu_sc as plsc

F32 = jnp.float32
BF16 = jnp.bfloat16

D_MODEL = 1024
GM_WIDTH = 512
CV_WIDTH = 512
GM_HEADS = 4
GM_HEAD_DIM = 128
CHUNK = 128
CONV_WIDTH = 31
IN_COLS = 2 * GM_WIDTH + 2 * CV_WIDTH
N_MEM = 256
XA_HEADS = 4
XA_HEAD_DIM = 256
N_GROUPS = 4
EXPERTS_PER_GROUP = 8
EXPERT_FF = 256
EPS = 1e-6

LANES = 128
SUBLANES = 8
HALO = 32
PROMPT_TILE = 512
MOE_TILE = 1024
MOE_SUB = 128
FFN_BLOCK = 512
SC_WINDOW = 128
SAMPLE_ATTN_BLOCK = 8
ROUTER_ROWS = 128
VMEM_LIMIT = 56 * 1024 * 1024


def _rms(x, g):
    return x * lax.rsqrt(jnp.mean(x * x, axis=-1, keepdims=True) + EPS) * g


def _ln(x, g, b):
    mu = jnp.mean(x, axis=-1, keepdims=True)
    xc = x - mu
    var = jnp.mean(xc * xc, axis=-1, keepdims=True)
    return xc * lax.rsqrt(var + EPS) * g + b


def _dot(a, b):
    return jnp.dot(a, b, preferred_element_type=F32)


def _dot_nt(a, b):
    return lax.dot_general(a, b, (((1,), (1,)), ((), ())), preferred_element_type=F32)


def _dot_tn(a, b):
    return lax.dot_general(a, b, (((0,), (0,)), ((), ())), preferred_element_type=F32)


def _memkv_kernel(mem_ref, g_ref, wk_ref, wv_ref, k_ref, v_ref, kb_ref, vb_ref):
    mn = _rms(mem_ref[...], g_ref[...]).astype(BF16)
    k = _dot(mn, wk_ref[...])
    v = _dot(mn, wv_ref[...])
    for h in range(XA_HEADS):
        cols = slice(h * XA_HEAD_DIM, (h + 1) * XA_HEAD_DIM)
        k_ref[:, h, :] = k[:, cols]
        v_ref[:, h, :] = v[:, cols]
    kb_ref[...] = k.astype(BF16)
    vb_ref[...] = v.astype(BF16)


def _memkv(mem2d, g, wk, wv):
    rows = mem2d.shape[0]
    tile = 512
    row_spec = pl.BlockSpec((tile, D_MODEL), lambda i: (i, 0))
    head_spec = pl.BlockSpec((tile, XA_HEADS, XA_HEAD_DIM), lambda i: (i, 0, 0))
    full = lambda shape: pl.BlockSpec(shape, lambda i: (0,) * len(shape))
    return pl.pallas_call(
        _memkv_kernel,
        grid=(rows // tile,),
        in_specs=[row_spec, full((1, D_MODEL)), full((D_MODEL, D_MODEL)), full((D_MODEL, D_MODEL))],
        out_specs=[head_spec, head_spec, row_spec, row_spec],
        out_shape=[jax.ShapeDtypeStruct((rows, XA_HEADS, XA_HEAD_DIM), F32),
                   jax.ShapeDtypeStruct((rows, XA_HEADS, XA_HEAD_DIM), F32),
                   jax.ShapeDtypeStruct((rows, D_MODEL), BF16), jax.ShapeDtypeStruct((rows, D_MODEL), BF16)],
        compiler_params=pltpu.CompilerParams(dimension_semantics=("arbitrary",), vmem_limit_bytes=VMEM_LIMIT),
        name="memkv",
    )(mem2d, g, wk, wv)


def _conv_block(g_s, cw_ref, start, lanes):
    assert start % SUBLANES == 0
    rows = CHUNK + HALO
    win = g_s[start:start + rows, lanes]
    off = HALO - (CONV_WIDTH - 1)
    acc = None
    for b in range(SUBLANES):
        shifted = pltpu.roll(win, rows - (off + b), axis=0) if off + b else win
        for k in range(b, CONV_WIDTH, SUBLANES):
            assert k - b + CHUNK + off + b <= rows
            term = shifted[k - b:k - b + CHUNK] * cw_ref[k:k + 1, lanes]
            acc = term if acc is None else acc + term
    return acc


def _cast_copies(src_refs, dst_refs, in_bufs, out_bufs, sem_in, sem_out, chunk, slot):
    copies_in, copies_out = [], []
    for i, (src, dst, ibuf, obuf) in enumerate(zip(src_refs, dst_refs, in_bufs, out_bufs)):
        rows = obuf.shape[0]
        start = pl.multiple_of(chunk * rows, rows)
        copies_in.append(pltpu.make_async_copy(src.at[pl.ds(start, rows), :], ibuf.at[slot], sem_in.at[slot, i]))
        copies_out.append(pltpu.make_async_copy(obuf, dst.at[pl.ds(start, rows), :], sem_out.at[i]))
    return copies_in, copies_out


def _prompt_kernel(x_ref, gmix_ref, win_ref, glng_ref, glnb_ref, ws_ref, bst_ref, cw_ref, cb_ref,
                   clng_ref, clnb_ref, wout_ref, gxa_ref, wq_ref, kb_ref, vb_ref, wo_ref,
                   w1f_ref, w3f_ref, w2f_ref, gffn_ref, wrt_ref, rb_ref,
                   h_ref, conv_ref, gmv_ref, w1b_ref, w3b_ref, w2b_ref, xp_ref, e_ref, wgt_ref, r_ref, cnt_ref,
                   z_s, g_s, c_s, ab_s, o_s, y_prev, y_cur, in1, in3, in2, st1, st3, st2, u_s, run_s, sem_in, sem_out,
                   *, tiles_per_seq, n_cast_chunks):
    s = pl.program_id(0)
    n_steps = pl.num_programs(0)
    tb = x_ref.shape[0]
    n_chunks = tb // CHUNK

    srcs, dsts = (w1f_ref, w3f_ref, w2f_ref), (w1b_ref, w3b_ref, w2b_ref)
    ins, sts = (in1, in3, in2), (st1, st3, st2)
    slot = s % 2
    chunk_of = lambda step: jnp.minimum(step, n_cast_chunks - 1)

    @pl.when(s == 0)
    def _():
        for c in _cast_copies(srcs, dsts, ins, sts, sem_in, sem_out, chunk_of(s), slot)[0]:
            c.start()

    @pl.when(s + 1 < n_steps)
    def _():
        for c in _cast_copies(srcs, dsts, ins, sts, sem_in, sem_out, chunk_of(s + 1), 1 - slot)[0]:
            c.start()

    cast_in, cast_out = _cast_copies(srcs, dsts, ins, sts, sem_in, sem_out, chunk_of(s), slot)
    for c in cast_in:
        c.wait()
    for ibuf, obuf in zip(ins, sts):
        obuf[...] = ibuf[slot].astype(BF16)
    for c in cast_out:
        c.start()

    @pl.when(s == 0)
    def _():
        y_prev[...] = jnp.zeros(y_prev.shape, F32)
        u_s[...] = (lax.broadcasted_iota(jnp.int32, (tb, tb), 0)
                    < lax.broadcasted_iota(jnp.int32, (tb, tb), 1)).astype(F32).astype(BF16)
        run_s[...] = jnp.zeros(run_s.shape, F32)

    @pl.when(s % tiles_per_seq == 0)
    def _():
        g_s[0:HALO, :] = jnp.zeros((HALO, CV_WIDTH), F32)

    z_s[...] = _dot(_rms(x_ref[...], gmix_ref[...]).astype(BF16), win_ref[...])

    y = y_prev[...]
    qn = _rms(y, gxa_ref[...]).astype(BF16)
    q = (_dot(qn, wq_ref[...]) * (XA_HEAD_DIM ** -0.5)).astype(BF16)
    for h in range(XA_HEADS):
        cols = slice(h * XA_HEAD_DIM, (h + 1) * XA_HEAD_DIM)
        sc = _dot_nt(q[:, cols], kb_ref[:, cols])
        e = jnp.exp(sc - jnp.max(sc, axis=-1, keepdims=True))
        p = (e / jnp.sum(e, axis=-1, keepdims=True)).astype(BF16)
        o_s[:, cols] = _dot(p, vb_ref[:, cols]).astype(BF16)
    h_ref[...] = y + _dot(o_s[...], wo_ref[...])
    _route_tile(h_ref[...], gffn_ref, wrt_ref, rb_ref, xp_ref, e_ref, wgt_ref, r_ref, cnt_ref, u_s, run_s, s > 0)

    tri = (lax.broadcasted_iota(jnp.int32, (CHUNK, CHUNK), 0)
           >= lax.broadcasted_iota(jnp.int32, (CHUNK, CHUNK), 1))
    wm = [jnp.where(tri, ws_ref[h], 0.0).astype(BF16) for h in range(GM_HEADS)]

    for c in range(n_chunks):
        rows = slice(c * CHUNK, (c + 1) * CHUNK)
        for h in range(GM_HEADS):
            cu = slice(h * GM_HEAD_DIM, (h + 1) * GM_HEAD_DIM)
            cv = slice(GM_WIDTH + h * GM_HEAD_DIM, GM_WIDTH + (h + 1) * GM_HEAD_DIM)
            v = _ln(jax.nn.gelu(z_s[rows, cv]), glng_ref[:, cu], glnb_ref[:, cu])
            if c == n_chunks - 1:
                gmv_ref[:, cu] = v
            mixed = _dot(wm[h], v.astype(BF16)) + bst_ref[:, h:h + 1]
            ab_s[rows, cu] = (jax.nn.gelu(z_s[rows, cu]) * mixed).astype(BF16)
        ca = slice(2 * GM_WIDTH, 2 * GM_WIDTH + CV_WIDTH)
        cg = slice(2 * GM_WIDTH + CV_WIDTH, IN_COLS)
        g_s[HALO + c * CHUNK:HALO + (c + 1) * CHUNK, :] = z_s[rows, ca] * jax.nn.sigmoid(z_s[rows, cg])

    for c in range(n_chunks):
        rows = slice(c * CHUNK, (c + 1) * CHUNK)
        for cb in range(CV_WIDTH // LANES):
            lanes = slice(cb * LANES, (cb + 1) * LANES)
            c_s[rows, lanes] = _conv_block(g_s, cw_ref, c * CHUNK, lanes) + cb_ref[:, lanes]
        b = _ln(c_s[rows, :], clng_ref[...], clnb_ref[...])
        ab_s[rows, GM_WIDTH:] = (b * jax.nn.sigmoid(b)).astype(BF16)

    conv_ref[...] = g_s[tb:tb + HALO, :]
    g_s[0:HALO, :] = g_s[tb:tb + HALO, :]
    y_cur[...] = x_ref[...] + _dot(ab_s[...], wout_ref[...])
    y_prev[...] = y_cur[...]
    for c in cast_out:
        c.wait()


def _prompt_mix_attn(x, gmix, win, glng, glnb, ws, bst, cw, cb, clng, clnb, wout, gxa, wq, kb, vb, wo,
                     w1f, w3f, w2f, gffn, wrt, rbias):
    nb, seq, _ = x.shape
    tb = PROMPT_TILE
    tps = seq // tb
    last = nb * tps - 1
    n_cast = nb * tps
    r13, r2 = w1f.shape[0] // n_cast, w2f.shape[0] // n_cast
    assert w1f.shape == w3f.shape and w1f.shape[0] % n_cast == 0 and w2f.shape[0] % n_cast == 0
    assert r13 % 16 == 0 and r2 % 16 == 0
    hbm = pl.BlockSpec(memory_space=pl.ANY)
    tile = lambda s, lag: jnp.clip(s - lag, 0, last)
    full = lambda shape: pl.BlockSpec(shape, lambda s: (0,) * len(shape))
    rows = lambda lag: pl.BlockSpec((None, tb, D_MODEL), lambda s: (tile(s, lag) // tps, tile(s, lag) % tps, 0))
    kv_spec = pl.BlockSpec((None, N_MEM, D_MODEL), lambda s: (tile(s, 1) // tps, 0, 0))
    seq_out = lambda r, c: pl.BlockSpec((None, r, c), lambda s: (tile(s, 0) // tps, 0, 0))
    lane_out = pl.BlockSpec((2, tb), lambda s: (0, tile(s, 1)))
    tokens = nb * seq
    return pl.pallas_call(
        functools.partial(_prompt_kernel, tiles_per_seq=tps, n_cast_chunks=n_cast),
        grid=(nb * tps + 1,),
        in_specs=[rows(0), full((1, D_MODEL)), full((D_MODEL, IN_COLS)), full((1, GM_WIDTH)), full((1, GM_WIDTH)),
                  full((GM_HEADS, CHUNK, CHUNK)), full((CHUNK, GM_HEADS)), full((CONV_WIDTH, CV_WIDTH)),
                  full((1, CV_WIDTH)), full((1, CV_WIDTH)), full((1, CV_WIDTH)), full((D_MODEL, D_MODEL)),
                  full((1, D_MODEL)), full((D_MODEL, D_MODEL)), kv_spec, kv_spec, full((D_MODEL, D_MODEL)),
                  hbm, hbm, hbm, full((1, D_MODEL)), full((ROUTER_ROWS, D_MODEL)), full((ROUTER_ROWS, 1))],
        out_specs=[rows(1), seq_out(HALO, CV_WIDTH), seq_out(CHUNK, GM_WIDTH), hbm, hbm, hbm,
                   pl.BlockSpec((SUB_PER_ROW, tb, LANES), lambda s: (0, tile(s, 1), 0)),
                   lane_out, lane_out, lane_out, full((N_EXPERTS, LANES))],
        out_shape=[jax.ShapeDtypeStruct((nb, seq, D_MODEL), F32),
                   jax.ShapeDtypeStruct((nb, HALO, CV_WIDTH), F32),
                   jax.ShapeDtypeStruct((nb, CHUNK, GM_WIDTH), F32),
                   jax.ShapeDtypeStruct(w1f.shape, BF16), jax.ShapeDtypeStruct(w3f.shape, BF16),
                   jax.ShapeDtypeStruct(w2f.shape, BF16),
                   jax.ShapeDtypeStruct((SUB_PER_ROW, tokens, LANES), jnp.uint32),
                   jax.ShapeDtypeStruct((2, tokens), jnp.int32), jax.ShapeDtypeStruct((2, tokens), F32),
                   jax.ShapeDtypeStruct((2, tokens), jnp.int32), jax.ShapeDtypeStruct((N_EXPERTS, LANES), jnp.int32)],
        scratch_shapes=[pltpu.VMEM((tb, IN_COLS), F32), pltpu.VMEM((HALO + tb, CV_WIDTH), F32),
                        pltpu.VMEM((tb, CV_WIDTH), F32), pltpu.VMEM((tb, D_MODEL), BF16),
                        pltpu.VMEM((tb, D_MODEL), BF16), pltpu.VMEM((tb, D_MODEL), F32),
                        pltpu.VMEM((tb, D_MODEL), F32),
                        pltpu.VMEM((2, r13, EXPERT_FF), F32), pltpu.VMEM((2, r13, EXPERT_FF), F32),
                        pltpu.VMEM((2, r2, D_MODEL), F32),
                        pltpu.VMEM((r13, EXPERT_FF), BF16), pltpu.VMEM((r13, EXPERT_FF), BF16),
                        pltpu.VMEM((r2, D_MODEL), BF16),
                        pltpu.VMEM((tb, tb), BF16), pltpu.VMEM((N_EXPERTS, LANES), F32),
                        pltpu.SemaphoreType.DMA((2, 3)), pltpu.SemaphoreType.DMA((3,))],
        compiler_params=pltpu.CompilerParams(dimension_semantics=("arbitrary",), vmem_limit_bytes=VMEM_LIMIT),
        name="prompt_mix_attn",
    )(x, gmix, win, glng, glnb, ws, bst, cw, cb, clng, clnb, wout, gxa, wq, kb, vb, wo, w1f, w3f, w2f, gffn, wrt, rbias)


def _sample_mix_kernel(x_ref, gmix_ref, win_ref, glng_ref, glnb_ref, ws0_ref, bs0_ref, cache_ref, cw_ref, cb_ref,
                       clng_ref, clnb_ref, wout_ref, gxa_ref, wq_ref,
                       y_ref, glu_ref, v_ref, q_ref, ab_s):
    x = x_ref[...]
    z = _dot(_rms(x, gmix_ref[...]).astype(BF16), win_ref[...])
    for h in range(GM_HEADS):
        cu = slice(h * GM_HEAD_DIM, (h + 1) * GM_HEAD_DIM)
        cv = slice(GM_WIDTH + h * GM_HEAD_DIM, GM_WIDTH + (h + 1) * GM_HEAD_DIM)
        v = _ln(jax.nn.gelu(z[:, cv]), glng_ref[:, cu], glnb_ref[:, cu])
        v_ref[:, cu] = v
        ab_s[:, cu] = (jax.nn.gelu(z[:, cu]) * (v * ws0_ref[:, cu] + bs0_ref[:, cu])).astype(BF16)
    glu = z[:, 2 * GM_WIDTH:2 * GM_WIDTH + CV_WIDTH] * jax.nn.sigmoid(z[:, 2 * GM_WIDTH + CV_WIDTH:])
    glu_ref[...] = glu
    conv = glu * cw_ref[CONV_WIDTH - 1:CONV_WIDTH, :] + cb_ref[...]
    for k in range(CONV_WIDTH - 1):
        conv = conv + cache_ref[k] * cw_ref[k:k + 1, :]
    b = _ln(conv, clng_ref[...], clnb_ref[...])
    ab_s[:, GM_WIDTH:] = (b * jax.nn.sigmoid(b)).astype(BF16)
    y = x + _dot(ab_s[...], wout_ref[...])
    y_ref[...] = y
    q_ref[...] = _dot(_rms(y, gxa_ref[...]).astype(BF16), wq_ref[...]) * (XA_HEAD_DIM ** -0.5)


def _sample_mix(x, gmix, win, glng, glnb, ws0, bs0, cache_t, cw, cb, clng, clnb, wout, gxa, wq):
    ns = x.shape[0]
    return pl.pallas_call(
        _sample_mix_kernel,
        out_shape=[jax.ShapeDtypeStruct((ns, D_MODEL), F32), jax.ShapeDtypeStruct((ns, CV_WIDTH), F32),
                   jax.ShapeDtypeStruct((ns, GM_WIDTH), F32), jax.ShapeDtypeStruct((ns, D_MODEL), F32)],
        scratch_shapes=[pltpu.VMEM((ns, D_MODEL), BF16)],
        compiler_params=pltpu.CompilerParams(vmem_limit_bytes=VMEM_LIMIT),
        name="sample_mix",
    )(x, gmix, win, glng, glnb, ws0, bs0, cache_t, cw, cb, clng, clnb, wout, gxa, wq)


def _sample_attn_kernel(q_ref, k_ref, v_ref, o_ref):
    ones = jnp.ones((LANES, LANES), BF16)
    rows = N_MEM * SUBLANES
    for i in range(q_ref.shape[0]):
        prod = (k_ref[i] * q_ref[i][None]).reshape(rows, LANES).astype(BF16)
        part = _dot(prod, ones).reshape(N_MEM, SUBLANES, LANES)
        s = part + pltpu.roll(part, XA_HEADS, axis=1)
        e = jnp.exp(s - jnp.max(s, axis=0, keepdims=True))
        p = e / jnp.sum(e, axis=0, keepdims=True)
        o_ref[i] = jnp.sum(p * v_ref[i], axis=0)


def _split_heads(a):
    halves = XA_HEAD_DIM // LANES
    assert halves * XA_HEADS == SUBLANES
    lead = a.shape[:-2]
    a = a.reshape(*lead, XA_HEADS, halves, LANES)
    return jnp.swapaxes(a, -3, -2).reshape(*lead, SUBLANES, LANES)


def _merge_heads(o):
    ns = o.shape[0]
    o = jnp.swapaxes(o.reshape(ns, XA_HEAD_DIM // LANES, XA_HEADS, LANES), 1, 2)
    return o.reshape(ns, XA_HEADS * XA_HEAD_DIM)


def _sample_attn(q, k, v, first, count):
    nb = SAMPLE_ATTN_BLOCK
    assert first % nb == 0 and count % nb == 0
    off = first // nb
    return pl.pallas_call(
        _sample_attn_kernel,
        grid=(count // nb,),
        in_specs=[pl.BlockSpec((nb, SUBLANES, LANES), lambda i: (i + off, 0, 0)),
                  pl.BlockSpec((nb, N_MEM, SUBLANES, LANES), lambda i: (i + off, 0, 0, 0)),
                  pl.BlockSpec((nb, N_MEM, SUBLANES, LANES), lambda i: (i + off, 0, 0, 0))],
        out_specs=pl.BlockSpec((nb, SUBLANES, LANES), lambda i: (i, 0, 0)),
        out_shape=jax.ShapeDtypeStruct((count, SUBLANES, LANES), F32),
        compiler_params=pltpu.CompilerParams(dimension_semantics=("arbitrary",), vmem_limit_bytes=VMEM_LIMIT),
        name="sample_attn",
    )(q, k, v)


def _sample_proj_kernel(y_ref, o_ref, wo_ref, h_ref):
    h_ref[...] = y_ref[...] + _dot(o_ref[...].astype(BF16), wo_ref[...])


def _sample_proj(y, o, wo):
    return pl.pallas_call(
        _sample_proj_kernel,
        out_shape=jax.ShapeDtypeStruct(y.shape, F32),
        compiler_params=pltpu.CompilerParams(vmem_limit_bytes=VMEM_LIMIT),
        name="sample_proj",
    )(y, o, wo)


def _moe_kernel(h_ref, gffn_ref, wrt_ref, rb_ref, w1_ref, w3_ref, w2_ref, gfin_ref,
                o_ref, xt_s, u_s, gidx_s, slot_s, comb_s, act_s, p2_s, y2_s, cnt_s):
    i = pl.program_id(0)
    g = pl.program_id(1)
    rt = h_ref.shape[0]
    sb = act_s.shape[0]

    @pl.when(jnp.logical_and(i == 0, g == 0))
    def _():
        u_s[...] = (lax.broadcasted_iota(jnp.int32, (rt, rt), 0)
                    < lax.broadcasted_iota(jnp.int32, (rt, rt), 1)).astype(F32).astype(BF16)

    @pl.when(g == 0)
    def _route():
        h = h_ref[...]
        o_ref[...] = h
        xt = _rms(h, gffn_ref[...]).astype(BF16)
        xt_s[...] = xt
        lt = _dot_nt(wrt_ref[...], xt) + rb_ref[...]
        gl = [lt[k:k + 1, :] for k in range(N_GROUPS)]
        gmax = jnp.maximum(jnp.maximum(gl[0], gl[1]), jnp.maximum(gl[2], gl[3]))
        gidx = jnp.where(gl[0] == gmax, 0, jnp.where(gl[1] == gmax, 1, jnp.where(gl[2] == gmax, 2, 3)))
        gidx = gidx.astype(jnp.int32)
        sumexp = (jnp.exp(gl[0] - gmax) + jnp.exp(gl[1] - gmax)) + (jnp.exp(gl[2] - gmax) + jnp.exp(gl[3] - gmax))
        p_g = 1.0 / sumexp
        esel = lt[SUBLANES + 3 * EXPERTS_PER_GROUP:SUBLANES + 4 * EXPERTS_PER_GROUP, :]
        for k in (2, 1, 0):
            esel = jnp.where(gidx == k, lt[SUBLANES + k * EXPERTS_PER_GROUP:SUBLANES + (k + 1) * EXPERTS_PER_GROUP, :],
                             esel)
        eidx = lax.broadcasted_iota(jnp.int32, (EXPERTS_PER_GROUP, rt), 0)
        m1 = jnp.max(esel, axis=0, keepdims=True)
        i1 = jnp.min(jnp.where(esel == m1, eidx, EXPERTS_PER_GROUP), axis=0, keepdims=True)
        rest = jnp.where(eidx == i1, -jnp.inf, esel)
        m2 = jnp.max(rest, axis=0, keepdims=True)
        i2 = jnp.min(jnp.where(rest == m2, eidx, EXPERTS_PER_GROUP), axis=0, keepdims=True)
        t2 = jnp.exp(m2 - m1)
        den = 1.0 + t2
        w_top1 = (1.0 / den) * p_g
        w_top2 = (t2 / den) * p_g
        within = jnp.where(eidx == i1, w_top1, 0.0) + jnp.where(eidx == i2, w_top2, 0.0)
        c_hi = within.astype(BF16).astype(F32)
        r1 = within - c_hi
        c_mid = r1.astype(BF16).astype(F32)
        c_lo = (r1 - c_mid).astype(BF16).astype(F32)
        comb_s[0:8, :] = c_hi
        comb_s[8:16, :] = c_mid
        comb_s[16:24, :] = c_lo
        comb_s[24:32, :] = jnp.zeros((8, rt), F32)
        onehot = (eidx == gidx).astype(F32)
        rank = _dot(onehot.astype(BF16), u_s[...])
        slot_s[...] = jnp.sum(onehot * rank, axis=0, keepdims=True).astype(jnp.int32)
        gidx_s[...] = gidx
        for k in range(N_GROUPS):
            cnt_s[k] = jnp.sum(onehot[k:k + 1, :]).astype(jnp.int32)

    n_blk = (cnt_s[g] + sb - 1) // sb

    def sub_block(j, half):
        half_rows = slice(half * sb, (half + 1) * sb)
        rows = lax.broadcasted_iota(jnp.int32, (sb, rt), 0) + j * sb
        hit = jnp.logical_and(rows == slot_s[...], gidx_s[...] == g)
        p = jnp.where(hit, 1.0, 0.0).astype(BF16)
        p2_s[half_rows, :] = p
        xc = _dot(p, xt_s[...]).astype(BF16)
        cexp = _dot_nt(p, comb_s[...].astype(BF16))
        cw = (cexp[:, 0:8] + cexp[:, 8:16]) + cexp[:, 16:24]
        for e in range(EXPERTS_PER_GROUP):
            h1 = _dot(xc, w1_ref[e])
            h3 = _dot(xc, w3_ref[e])
            a = (h1 * jax.nn.sigmoid(h1)) * h3 * cw[:, e:e + 1]
            act_s[:, e * EXPERT_FF:(e + 1) * EXPERT_FF] = a.astype(BF16)
        y2_s[half_rows, :] = _dot(act_s[...], w2_ref[...]).astype(BF16)

    def body(jj, carry):
        sub_block(2 * jj, 0)

        @pl.when(2 * jj + 1 < n_blk)
        def _():
            sub_block(2 * jj + 1, 1)

        @pl.when(2 * jj + 1 >= n_blk)
        def _():
            p2_s[sb:, :] = jnp.zeros((sb, rt), BF16)
            y2_s[sb:, :] = jnp.zeros((sb, D_MODEL), BF16)

        o_ref[...] += _dot_tn(p2_s[...], y2_s[...])
        return carry

    lax.fori_loop(0, (n_blk + 1) // 2, body, 0)

    @pl.when(g == N_GROUPS - 1)
    def _():
        o_ref[...] = _rms(o_ref[...], gfin_ref[...])


def _moe(h, gffn, wrt, rbias, w1, w3, w2, gfin, *, tile):
    tokens = h.shape[0]
    sb = min(MOE_SUB, tile)
    ff = EXPERTS_PER_GROUP * EXPERT_FF
    full = lambda shape: pl.BlockSpec(shape, lambda i, g: (0,) * len(shape))
    row_spec = pl.BlockSpec((tile, D_MODEL), lambda i, g: (i, 0))
    return pl.pallas_call(
        _moe_kernel,
        grid=(tokens // tile, N_GROUPS),
        in_specs=[row_spec, full((1, D_MODEL)), full((ROUTER_ROWS, D_MODEL)), full((ROUTER_ROWS, 1)),
                  pl.BlockSpec((None, EXPERTS_PER_GROUP, D_MODEL, EXPERT_FF), lambda i, g: (g, 0, 0, 0)),
                  pl.BlockSpec((None, EXPERTS_PER_GROUP, D_MODEL, EXPERT_FF), lambda i, g: (g, 0, 0, 0)),
                  pl.BlockSpec((None, ff, D_MODEL), lambda i, g: (g, 0, 0)),
                  full((1, D_MODEL))],
        out_specs=row_spec,
        out_shape=jax.ShapeDtypeStruct((tokens, D_MODEL), F32),
        scratch_shapes=[pltpu.VMEM((tile, D_MODEL), BF16), pltpu.VMEM((tile, tile), BF16),
                        pltpu.VMEM((1, tile), jnp.int32), pltpu.VMEM((1, tile), jnp.int32),
                        pltpu.VMEM((4 * SUBLANES, tile), F32), pltpu.VMEM((sb, ff), BF16),
                        pltpu.VMEM((2 * sb, tile), BF16), pltpu.VMEM((2 * sb, D_MODEL), BF16),
                        pltpu.SMEM((N_GROUPS,), jnp.int32)],
        compiler_params=pltpu.CompilerParams(dimension_semantics=("arbitrary", "arbitrary"),
                                             vmem_limit_bytes=VMEM_LIMIT),
        name="moe",
    )(h, gffn, wrt, rbias, w1, w3, w2, gfin)


HALF = D_MODEL // 2
SUB_PER_ROW = HALF // LANES
N_EXPERTS = N_GROUPS * EXPERTS_PER_GROUP


def _pack_bf16_pairs(x):
    bits = pltpu.bitcast(x.astype(BF16).astype(F32), jnp.uint32)
    return (bits[:, HALF:] & jnp.uint32(0xFFFF0000)) | (bits[:, :HALF] >> 16)


def _unpack_bf16_pairs(w):
    lo = pltpu.bitcast(w << 16, F32)
    hi = pltpu.bitcast(w & jnp.uint32(0xFFFF0000), F32)
    return lo, hi


def _route_tile(h, gffn_ref, wrt_ref, rb_ref, xp_ref, e_ref, w_ref, r_ref, cnt_ref, u_s, run_s, live):
    rt = h.shape[0]
    xt = _rms(h, gffn_ref[...])
    _store_planes(xp_ref, _pack_bf16_pairs(xt))
    lt = _dot_nt(wrt_ref[...], xt.astype(BF16)) + rb_ref[...]
    gl = [lt[k:k + 1, :] for k in range(N_GROUPS)]
    gmax = jnp.maximum(jnp.maximum(gl[0], gl[1]), jnp.maximum(gl[2], gl[3]))
    gidx = jnp.where(gl[0] == gmax, 0, jnp.where(gl[1] == gmax, 1, jnp.where(gl[2] == gmax, 2, 3)))
    gidx = gidx.astype(jnp.int32)
    sumexp = (jnp.exp(gl[0] - gmax) + jnp.exp(gl[1] - gmax)) + (jnp.exp(gl[2] - gmax) + jnp.exp(gl[3] - gmax))
    p_g = 1.0 / sumexp
    esel = lt[SUBLANES + 3 * EXPERTS_PER_GROUP:SUBLANES + 4 * EXPERTS_PER_GROUP, :]
    for k in (2, 1, 0):
        esel = jnp.where(gidx == k, lt[SUBLANES + k * EXPERTS_PER_GROUP:SUBLANES + (k + 1) * EXPERTS_PER_GROUP, :],
                         esel)
    eidx = lax.broadcasted_iota(jnp.int32, (EXPERTS_PER_GROUP, rt), 0)
    m1 = jnp.max(esel, axis=0, keepdims=True)
    i1 = jnp.min(jnp.where(esel == m1, eidx, EXPERTS_PER_GROUP), axis=0, keepdims=True)
    rest = jnp.where(eidx == i1, -jnp.inf, esel)
    m2 = jnp.max(rest, axis=0, keepdims=True)
    i2 = jnp.min(jnp.where(rest == m2, eidx, EXPERTS_PER_GROUP), axis=0, keepdims=True)
    t2 = jnp.exp(m2 - m1)
    den = 1.0 + t2
    w_ref[0:1, :] = (1.0 / den) * p_g
    w_ref[1:2, :] = (t2 / den) * p_g
    e1 = gidx * EXPERTS_PER_GROUP + i1
    e2 = gidx * EXPERTS_PER_GROUP + i2
    e_ref[0:1, :] = e1
    e_ref[1:2, :] = e2
    xid = lax.broadcasted_iota(jnp.int32, (N_EXPERTS, rt), 0)
    oh1 = (xid == e1).astype(F32)
    oh2 = (xid == e2).astype(F32)
    both = oh1 + oh2
    before = _dot(both.astype(BF16), u_s[...]) + run_s[:, 0:1]
    r_ref[0:1, :] = jnp.sum(oh1 * before, axis=0, keepdims=True).astype(jnp.int32)
    r_ref[1:2, :] = jnp.sum(oh2 * before, axis=0, keepdims=True).astype(jnp.int32)
    run_s[...] = run_s[...] + jnp.where(live, jnp.sum(both, axis=1, keepdims=True), 0.0)
    cnt_ref[...] = run_s[...].astype(jnp.int32)


def _store_planes(ref, words):
    for j in range(SUB_PER_ROW):
        ref[j] = words[:, j * LANES:(j + 1) * LANES]


def _load_planes(ref):
    return jnp.concatenate([ref[j] for j in range(SUB_PER_ROW)], axis=1)


def _sub_row_index(row_of_token, n_rows):
    plane = jnp.arange(SUB_PER_ROW, dtype=jnp.int32)[:, None] * n_rows
    return (row_of_token[None, :] + plane).reshape(1, -1)


def _sc_mesh():
    return plsc.VectorSubcoreMesh(core_axis_name="core", subcore_axis_name="subcore")


def _sc_scatter_two(x_sub, idx_a, idx_b, n_out):
    n_in = x_sub.shape[0]

    @pl.kernel(out_type=jax.ShapeDtypeStruct((n_out, LANES), x_sub.dtype), mesh=_sc_mesh(), scratch_types=[])
    def scatter(x_hbm, a_hbm, b_hbm, o_hbm):
        def body(x_vmem, a_vmem, b_vmem):
            pltpu.sync_copy(x_vmem, o_hbm.at[a_vmem.at[0]])
            pltpu.sync_copy(x_vmem, o_hbm.at[b_vmem.at[0]])

        pltpu.emit_pipeline(
            body, grid=(n_in // SC_WINDOW,),
            in_specs=[pl.BlockSpec((SC_WINDOW, LANES), lambda i: (i, 0)),
                      pl.BlockSpec((1, SC_WINDOW), lambda i: (0, i)),
                      pl.BlockSpec((1, SC_WINDOW), lambda i: (0, i))],
            out_specs=[],
            core_axis_name=("core", "subcore"), dimension_semantics=(pltpu.PARALLEL,),
        )(x_hbm, a_hbm, b_hbm)

    return scatter(x_sub, idx_a, idx_b)


def _sc_gather(table, idx):
    n_out = idx.shape[1]

    @pl.kernel(out_type=jax.ShapeDtypeStruct((n_out, LANES), table.dtype), mesh=_sc_mesh())
    def gather(t_hbm, i_hbm, o_hbm):
        def body(i_vmem, o_vmem):
            pltpu.sync_copy(t_hbm.at[i_vmem.at[0]], o_vmem)

        pltpu.emit_pipeline(
            body, grid=(n_out // SC_WINDOW,),
            in_specs=[pl.BlockSpec((1, SC_WINDOW), lambda i: (0, i))],
            out_specs=[pl.BlockSpec((SC_WINDOW, LANES), lambda i: (i, 0))],
            core_axis_name=("core", "subcore"), dimension_semantics=(pltpu.PARALLEL,),
        )(i_hbm, o_hbm)

    return gather(table, idx)


def _expert_ffn_kernel(blk_e_ref, n_valid_ref, x_ref, w1_ref, w3_ref, w2_ref, y_ref):
    del blk_e_ref

    @pl.when(pl.program_id(0) < n_valid_ref[0])
    def _():
        lo, hi = _unpack_bf16_pairs(_load_planes(x_ref))
        xc = jnp.concatenate([lo.astype(BF16), hi.astype(BF16)], axis=1)
        h1 = _dot(xc, w1_ref[...])
        h3 = _dot(xc, w3_ref[...])
        act = ((h1 * jax.nn.sigmoid(h1)) * h3).astype(BF16)
        _store_planes(y_ref, _pack_bf16_pairs(_dot(act, w2_ref[...])))


def _expert_ffn(xs, blk_e, n_valid, w1, w3, w2):
    rows = xs.shape[1]
    n_blocks = rows // FFN_BLOCK
    live = lambda b, be, nv: jnp.minimum(b, nv[0] - 1)
    grid_spec = pltpu.PrefetchScalarGridSpec(
        num_scalar_prefetch=2, grid=(n_blocks,),
        in_specs=[pl.BlockSpec((SUB_PER_ROW, FFN_BLOCK, LANES), lambda b, be, nv: (0, live(b, be, nv), 0)),
                  pl.BlockSpec((None, D_MODEL, EXPERT_FF), lambda b, be, nv: (be[b], 0, 0)),
                  pl.BlockSpec((None, D_MODEL, EXPERT_FF), lambda b, be, nv: (be[b], 0, 0)),
                  pl.BlockSpec((None, EXPERT_FF, D_MODEL), lambda b, be, nv: (be[b], 0, 0))],
        out_specs=pl.BlockSpec((SUB_PER_ROW, FFN_BLOCK, LANES), lambda b, be, nv: (0, live(b, be, nv), 0)))
    return pl.pallas_call(
        _expert_ffn_kernel, grid_spec=grid_spec,
        out_shape=jax.ShapeDtypeStruct((SUB_PER_ROW, rows, LANES), jnp.uint32),
        compiler_params=pltpu.CompilerParams(dimension_semantics=("arbitrary",), vmem_limit_bytes=VMEM_LIMIT),
        name="moe_ffn",
    )(blk_e, n_valid, xs, w1, w3, w2)


def _combine_kernel(h_ref, ya_ref, yb_ref, wt_ref, gfin_ref, o_ref):
    a_lo, a_hi = _unpack_bf16_pairs(_load_planes(ya_ref))
    b_lo, b_hi = _unpack_bf16_pairs(_load_planes(yb_ref))
    wa = wt_ref[:, 0:1]
    wb = wt_ref[:, 1:2]
    o_ref[:, :HALF] = h_ref[:, :HALF] + (wa * a_lo + wb * b_lo)
    o_ref[:, HALF:] = h_ref[:, HALF:] + (wa * a_hi + wb * b_hi)
    o_ref[...] = _rms(o_ref[...], gfin_ref[...])


def _combine(h, ya, yb, wt, gfin, *, tile):
    tokens = h.shape[0]
    return pl.pallas_call(
        _combine_kernel,
        grid=(tokens // tile,),
        in_specs=[pl.BlockSpec((tile, D_MODEL), lambda i: (i, 0)),
                  pl.BlockSpec((SUB_PER_ROW, tile, LANES), lambda i: (0, i, 0)),
                  pl.BlockSpec((SUB_PER_ROW, tile, LANES), lambda i: (0, i, 0)), pl.BlockSpec((tile, 2), lambda i: (i, 0)),
                  pl.BlockSpec((1, D_MODEL), lambda i: (0, 0))],
        out_specs=pl.BlockSpec((tile, D_MODEL), lambda i: (i, 0)),
        out_shape=jax.ShapeDtypeStruct((tokens, D_MODEL), F32),
        compiler_params=pltpu.CompilerParams(dimension_semantics=("arbitrary",), vmem_limit_bytes=VMEM_LIMIT),
        name="moe_combine",
    )(h, ya, yb, wt, gfin)


def _moe_dispatch(xp, e12, w12, r12, cnt):
    tokens = e12.shape[1]
    count = cnt[:, 0]
    padded = (count + FFN_BLOCK - 1) // FFN_BLOCK * FFN_BLOCK
    end = jnp.cumsum(padded)
    start = end - padded
    n_rows = 2 * tokens + N_EXPERTS * FFN_BLOCK
    n_blocks = n_rows // FFN_BLOCK
    n_valid = (end[-1:] // FFN_BLOCK).astype(jnp.int32)
    first_row = jnp.minimum(jnp.arange(n_blocks, dtype=jnp.int32), n_valid - 1) * FFN_BLOCK
    blk_e = jnp.sum((end[None, :] <= first_row[:, None]).astype(jnp.int32), axis=1)
    experts = jnp.arange(N_EXPERTS, dtype=jnp.int32)
    start_of = jnp.sum(jnp.where(e12[:, :, None] == experts, start.astype(jnp.int32), 0), axis=-1)
    rows_ab = start_of + r12
    idx_a, idx_b = _sub_row_index(rows_ab[0], n_rows), _sub_row_index(rows_ab[1], n_rows)
    xs = _sc_scatter_two(xp.reshape(-1, LANES), idx_a, idx_b, n_rows * SUB_PER_ROW)
    return xs.reshape(SUB_PER_ROW, n_rows, LANES), (blk_e, n_valid, idx_a, idx_b, w12)


def _moe_finish(h, xs, plan, w1, w3, w2, gfin):
    blk_e, n_valid, idx_a, idx_b, w12 = plan
    ys = _expert_ffn(xs, blk_e, n_valid, w1, w3, w2).reshape(-1, LANES)
    ya = _sc_gather(ys, idx_a).reshape(SUB_PER_ROW, -1, LANES)
    yb = _sc_gather(ys, idx_b).reshape(SUB_PER_ROW, -1, LANES)
    return _combine(h, ya, yb, w12.T, gfin, tile=MOE_TILE)


def kernel(x_prompt, x_sample, mem_prompt, cache_conv, cache_mem_k, cache_mem_v, norm_mix_g, w_in, gm_ln_g, gm_ln_b, gm_ws, gm_bs, conv_w, conv_b, cv_ln_g, cv_ln_b, w_out, norm_mem_g, norm_xa_g, xa_wq, xa_wk, xa_wv, xa_wo, norm_ffn_g, router_g, router_g_b, router_e, router_e_b, exp_w1, exp_w3, exp_w2, final_norm_g):
    depth = w_in.shape[0]
    assert depth == 1, "single-layer trunk"
    nb, seq, _ = x_prompt.shape
    ns = x_sample.shape[0]
    row = lambda a: a.reshape(1, -1)

    l = 0
    gmix, gxa, gffn, gmem = row(norm_mix_g[l]), row(norm_xa_g[l]), row(norm_ffn_g[l]), row(norm_mem_g[l])
    gfin = row(final_norm_g)
    win, wout = w_in[l].astype(BF16), w_out[l].astype(BF16)
    wq, wk, wv, wo = (w[l].astype(BF16) for w in (xa_wq, xa_wk, xa_wv, xa_wo))
    glng, glnb = row(gm_ln_g[l]), row(gm_ln_b[l])
    cw, cb, clng, clnb = conv_w[l], row(conv_b[l]), row(cv_ln_g[l]), row(cv_ln_b[l])
    ws, bst = gm_ws[l], gm_bs[l].T
    ws0 = jnp.repeat(gm_ws[l][:, 0, 0], GM_HEAD_DIM).reshape(1, GM_WIDTH)
    bs0 = jnp.repeat(gm_bs[l][:, 0], GM_HEAD_DIM).reshape(1, GM_WIDTH)

    n_exp = N_GROUPS * EXPERTS_PER_GROUP
    pad_g = SUBLANES - N_GROUPS
    pad_t = ROUTER_ROWS - SUBLANES - n_exp
    wrt = jnp.concatenate([router_g[l].T, jnp.zeros((pad_g, D_MODEL), F32),
                           router_e[l].reshape(D_MODEL, n_exp).T, jnp.zeros((pad_t, D_MODEL), F32)], axis=0).astype(BF16)
    rbias = jnp.concatenate([router_g_b[l], jnp.zeros((pad_g,), F32), router_e_b[l].reshape(n_exp),
                             jnp.zeros((pad_t,), F32)]).reshape(ROUTER_ROWS, 1)

    mk, mv, kb, vb = _memkv(mem_prompt.reshape(nb * N_MEM, D_MODEL), gmem, wk, wv)
    kb, vb = kb.reshape(nb, N_MEM, D_MODEL), vb.reshape(nb, N_MEM, D_MODEL)
    hp, conv_tail, gmv_p, w1, w3, w2, *routed = _prompt_mix_attn(
        x_prompt, gmix, win, glng, glnb, ws, bst, cw, cb, clng, clnb, wout, gxa, wq, kb, vb, wo,
        exp_w1[l].reshape(-1, EXPERT_FF), exp_w3[l].reshape(-1, EXPERT_FF), exp_w2[l].reshape(-1, D_MODEL),
        gffn, wrt, rbias)
    h2d = hp.reshape(nb * seq, D_MODEL)
    xs, plan = _moe_dispatch(*routed)
    w1e, w3e, w2e = (w1.reshape(N_EXPERTS, D_MODEL, EXPERT_FF), w3.reshape(N_EXPERTS, D_MODEL, EXPERT_FF),
                     w2.reshape(N_EXPERTS, EXPERT_FF, D_MODEL))
    w1 = w1.reshape(N_GROUPS, EXPERTS_PER_GROUP, D_MODEL, EXPERT_FF)
    w3 = w3.reshape(N_GROUPS, EXPERTS_PER_GROUP, D_MODEL, EXPERT_FF)
    w2 = w2.reshape(N_GROUPS, EXPERTS_PER_GROUP * EXPERT_FF, D_MODEL)

    cache_t = jnp.transpose(cache_conv[l], (1, 0, 2))
    ys, glu_s, gmv_s, q_s = _sample_mix(x_sample.reshape(ns, D_MODEL), gmix, win, glng, glnb, ws0, bs0,
                                        cache_t, cw, cb, clng, clnb, wout, gxa, wq)
    qh, kh, vh = (_split_heads(q_s.reshape(ns, XA_HEADS, XA_HEAD_DIM)), _split_heads(cache_mem_k[l]),
                  _split_heads(cache_mem_v[l]))
    half = ns // 2
    o_first = _sample_attn(qh, kh, vh, 0, half)
    xs, o_first = lax.optimization_barrier((xs, o_first))
    y_prompt = _moe_finish(h2d, xs, plan, w1e, w3e, w2e, gfin)
    o_second = _sample_attn(qh, kh, vh, half, ns - half)
    o_s = _merge_heads(jnp.concatenate([o_first, o_second], axis=0))
    hs = _sample_proj(ys, o_s, wo)
    y_sample = _moe(hs, gffn, wrt, rbias, w1, w3, w2, gfin, tile=ns)

    conv_prompt = conv_tail[:, HALO - (CONV_WIDTH - 1):, :][None]
    conv_sample = jnp.transpose(jnp.concatenate([cache_t[1:], glu_s[None]], axis=0), (1, 0, 2))[None]
    return (y_prompt.reshape(nb, seq, D_MODEL), y_sample.reshape(ns, 1, D_MODEL), conv_prompt, conv_sample,
            gmv_p[None], gmv_s.reshape(1, ns, 1, GM_WIDTH),
            mk.reshape(1, nb, N_MEM, XA_HEADS, XA_HEAD_DIM), mv.reshape(1, nb, N_MEM, XA_HEADS, XA_HEAD_DIM))
```

```python
import functools

import jax
import jax.numpy as jnp
from jax import lax
from jax.experimental import pallas as pl
from jax.experimental.pallas import tpu as pltpu
from jax.experimental.pallas import tpu_sc as plsc

F32 = jnp.float32
BF16 = jnp.bfloat16

D_MODEL = 1024
GM_WIDTH = 512
CV_WIDTH = 512
GM_HEADS = 4
GM_HEAD_DIM = 128
CHUNK = 128
CONV_WIDTH = 31
IN_COLS = 2 * GM_WIDTH + 2 * CV_WIDTH
N_MEM = 256
XA_HEADS = 4
XA_HEAD_DIM = 256
N_GROUPS = 4
EXPERTS_PER_GROUP = 8
EXPERT_FF = 256
EPS = 1e-6

LANES = 128
SUBLANES = 8
HALO = 32
PROMPT_TILE = 512
MOE_TILE = 1024
MOE_SUB = 128
FFN_BLOCK = 512
SC_WINDOW = 128
SAMPLE_ATTN_BLOCK = 8
ROUTER_ROWS = 128
VMEM_LIMIT = 56 * 1024 * 1024


def _rms(x, g):
    return x * lax.rsqrt(jnp.mean(x * x, axis=-1, keepdims=True) + EPS) * g


def _ln(x, g, b):
    mu = jnp.mean(x, axis=-1, keepdims=True)
    xc = x - mu
    var = jnp.mean(xc * xc, axis=-1, keepdims=True)
    return xc * lax.rsqrt(var + EPS) * g + b


def _dot(a, b):
    return jnp.dot(a, b, preferred_element_type=F32)


def _dot_nt(a, b):
    return lax.dot_general(a, b, (((1,), (1,)), ((), ())), preferred_element_type=F32)


def _dot_tn(a, b):
    return lax.dot_general(a, b, (((0,), (0,)), ((), ())), preferred_element_type=F32)


def _memkv_kernel(mem_ref, g_ref, wk_ref, wv_ref, k_ref, v_ref, kb_ref, vb_ref):
    mn = _rms(mem_ref[...], g_ref[...]).astype(BF16)
    k = _dot(mn, wk_ref[...].astype(BF16))
    v = _dot(mn, wv_ref[...].astype(BF16))
    for h in range(XA_HEADS):
        cols = slice(h * XA_HEAD_DIM, (h + 1) * XA_HEAD_DIM)
        k_ref[:, h, :] = k[:, cols]
        v_ref[:, h, :] = v[:, cols]
    kb_ref[...] = k.astype(BF16)
    vb_ref[...] = v.astype(BF16)


def _memkv(mem2d, g, wk, wv):
    rows = mem2d.shape[0]
    tile = 512
    row_spec = pl.BlockSpec((tile, D_MODEL), lambda i: (i, 0))
    head_spec = pl.BlockSpec((tile, XA_HEADS, XA_HEAD_DIM), lambda i: (i, 0, 0))
    full = lambda shape: pl.BlockSpec(shape, lambda i: (0,) * len(shape))
    return pl.pallas_call(
        _memkv_kernel,
        grid=(rows // tile,),
        in_specs=[row_spec, full((1, D_MODEL)), full((D_MODEL, D_MODEL)), full((D_MODEL, D_MODEL))],
        out_specs=[head_spec, head_spec, row_spec, row_spec],
        out_shape=[jax.ShapeDtypeStruct((rows, XA_HEADS, XA_HEAD_DIM), F32),
                   jax.ShapeDtypeStruct((rows, XA_HEADS, XA_HEAD_DIM), F32),
                   jax.ShapeDtypeStruct((rows, D_MODEL), BF16), jax.ShapeDtypeStruct((rows, D_MODEL), BF16)],
        compiler_params=pltpu.CompilerParams(dimension_semantics=("arbitrary",), vmem_limit_bytes=VMEM_LIMIT),
        name="memkv",
    )(mem2d, g, wk, wv)


def _conv_block(g_s, cw_ref, start, lanes):
    assert start % SUBLANES == 0
    rows = CHUNK + HALO
    win = g_s[start:start + rows, lanes]
    off = HALO - (CONV_WIDTH - 1)
    acc = None
    for b in range(SUBLANES):
        shifted = pltpu.roll(win, rows - (off + b), axis=0) if off + b else win
        for k in range(b, CONV_WIDTH, SUBLANES):
            assert k - b + CHUNK + off + b <= rows
            term = shifted[k - b:k - b + CHUNK] * cw_ref[k:k + 1, lanes]
            acc = term if acc is None else acc + term
    return acc


def _cast_copies(src_refs, dst_refs, in_bufs, out_bufs, sem_in, sem_out, chunk, slot):
    copies_in, copies_out = [], []
    for i, (src, dst, ibuf, obuf) in enumerate(zip(src_refs, dst_refs, in_bufs, out_bufs)):
        rows = obuf.shape[0]
        start = pl.multiple_of(chunk * rows, rows)
        copies_in.append(pltpu.make_async_copy(src.at[pl.ds(start, rows), :], ibuf.at[slot], sem_in.at[slot, i]))
        copies_out.append(pltpu.make_async_copy(obuf, dst.at[pl.ds(start, rows), :], sem_out.at[i]))
    return copies_in, copies_out


def _prompt_kernel(x_ref, gmix_ref, win_ref, glng_ref, glnb_ref, ws_ref, bst_ref, cw_ref, cb_ref,
                   clng_ref, clnb_ref, wout_ref, gxa_ref, wq_ref, kb_ref, vb_ref, wo_ref,
                   w1f_ref, w3f_ref, w2f_ref,
                   h_ref, conv_ref, gmv_ref, w1b_ref, w3b_ref, w2b_ref,
                   z_s, g_s, c_s, ab_s, o_s, y_prev, y_cur, in1, in3, in2, st1, st3, st2, sem_in, sem_out,
                   *, tiles_per_seq, n_cast_chunks):
    s = pl.program_id(0)
    n_steps = pl.num_programs(0)
    tb = x_ref.shape[0]
    n_chunks = tb // CHUNK

    srcs, dsts = (w1f_ref, w3f_ref, w2f_ref), (w1b_ref, w3b_ref, w2b_ref)
    ins, sts = (in1, in3, in2), (st1, st3, st2)
    slot = s % 2
    chunk_of = lambda step: jnp.minimum(step, n_cast_chunks - 1)

    @pl.when(s == 0)
    def _():
        for c in _cast_copies(srcs, dsts, ins, sts, sem_in, sem_out, chunk_of(s), slot)[0]:
            c.start()

    @pl.when(s + 1 < n_steps)
    def _():
        for c in _cast_copies(srcs, dsts, ins, sts, sem_in, sem_out, chunk_of(s + 1), 1 - slot)[0]:
            c.start()

    cast_in, cast_out = _cast_copies(srcs, dsts, ins, sts, sem_in, sem_out, chunk_of(s), slot)
    for c in cast_in:
        c.wait()
    for ibuf, obuf in zip(ins, sts):
        obuf[...] = ibuf[slot].astype(BF16)
    for c in cast_out:
        c.start()

    @pl.when(s == 0)
    def _():
        y_prev[...] = jnp.zeros(y_prev.shape, F32)

    @pl.when(s % tiles_per_seq == 0)
    def _():
        g_s[0:HALO, :] = jnp.zeros((HALO, CV_WIDTH), F32)

    z_s[...] = _dot(_rms(x_ref[...], gmix_ref[...]).astype(BF16), win_ref[...])

    y = y_prev[...]
    qn = _rms(y, gxa_ref[...]).astype(BF16)
    q = (_dot(qn, wq_ref[...]) * (XA_HEAD_DIM ** -0.5)).astype(BF16)
    for h in range(XA_HEADS):
        cols = slice(h * XA_HEAD_DIM, (h + 1) * XA_HEAD_DIM)
        sc = _dot_nt(q[:, cols], kb_ref[:, cols])
        e = jnp.exp(sc - jnp.max(sc, axis=-1, keepdims=True))
        p = (e / jnp.sum(e, axis=-1, keepdims=True)).astype(BF16)
        o_s[:, cols] = _dot(p, vb_ref[:, cols]).astype(BF16)
    h_ref[...] = y + _dot(o_s[...], wo_ref[...])

    tri = (lax.broadcasted_iota(jnp.int32, (CHUNK, CHUNK), 0)
           >= lax.broadcasted_iota(jnp.int32, (CHUNK, CHUNK), 1))
    wm = [jnp.where(tri, ws_ref[h], 0.0).astype(BF16) for h in range(GM_HEADS)]

    for c in range(n_chunks):
        rows = slice(c * CHUNK, (c + 1) * CHUNK)
        for h in range(GM_HEADS):
            cu = slice(h * GM_HEAD_DIM, (h + 1) * GM_HEAD_DIM)
            cv = slice(GM_WIDTH + h * GM_HEAD_DIM, GM_WIDTH + (h + 1) * GM_HEAD_DIM)
            v = _ln(jax.nn.gelu(z_s[rows, cv]), glng_ref[:, cu], glnb_ref[:, cu])
            if c == n_chunks - 1:
                gmv_ref[:, cu] = v
            mixed = _dot(wm[h], v.astype(BF16)) + bst_ref[:, h:h + 1]
            ab_s[rows, cu] = (jax.nn.gelu(z_s[rows, cu]) * mixed).astype(BF16)
        ca = slice(2 * GM_WIDTH, 2 * GM_WIDTH + CV_WIDTH)
        cg = slice(2 * GM_WIDTH + CV_WIDTH, IN_COLS)
        g_s[HALO + c * CHUNK:HALO + (c + 1) * CHUNK, :] = z_s[rows, ca] * jax.nn.sigmoid(z_s[rows, cg])

    for c in range(n_chunks):
        rows = slice(c * CHUNK, (c + 1) * CHUNK)
        for cb in range(CV_WIDTH // LANES):
            lanes = slice(cb * LANES, (cb + 1) * LANES)
            c_s[rows, lanes] = _conv_block(g_s, cw_ref, c * CHUNK, lanes) + cb_ref[:, lanes]
        b = _ln(c_s[rows, :], clng_ref[...], clnb_ref[...])
        ab_s[rows, GM_WIDTH:] = (b * jax.nn.sigmoid(b)).astype(BF16)

    conv_ref[...] = g_s[tb:tb + HALO, :]
    g_s[0:HALO, :] = g_s[tb:tb + HALO, :]
    y_cur[...] = x_ref[...] + _dot(ab_s[...], wout_ref[...])
    y_prev[...] = y_cur[...]
    for c in cast_out:
        c.wait()


def _prompt_mix_attn(x, gmix, win, glng, glnb, ws, bst, cw, cb, clng, clnb, wout, gxa, wq, kb, vb, wo,
                     w1f, w3f, w2f):
    nb, seq, _ = x.shape
    tb = PROMPT_TILE
    tps = seq // tb
    last = nb * tps - 1
    n_cast = nb * tps
    r13, r2 = w1f.shape[0] // n_cast, w2f.shape[0] // n_cast
    assert w1f.shape == w3f.shape and w1f.shape[0] % n_cast == 0 and w2f.shape[0] % n_cast == 0
    assert r13 % 16 == 0 and r2 % 16 == 0
    hbm = pl.BlockSpec(memory_space=pl.ANY)
    tile = lambda s, lag: jnp.clip(s - lag, 0, last)
    full = lambda shape: pl.BlockSpec(shape, lambda s: (0,) * len(shape))
    rows = lambda lag: pl.BlockSpec((None, tb, D_MODEL), lambda s: (tile(s, lag) // tps, tile(s, lag) % tps, 0))
    kv_spec = pl.BlockSpec((None, N_MEM, D_MODEL), lambda s: (tile(s, 1) // tps, 0, 0))
    seq_out = lambda r, c: pl.BlockSpec((None, r, c), lambda s: (tile(s, 0) // tps, 0, 0))
    return pl.pallas_call(
        functools.partial(_prompt_kernel, tiles_per_seq=tps, n_cast_chunks=n_cast),
        grid=(nb * tps + 1,),
        in_specs=[rows(0), full((1, D_MODEL)), full((D_MODEL, IN_COLS)), full((1, GM_WIDTH)), full((1, GM_WIDTH)),
                  full((GM_HEADS, CHUNK, CHUNK)), full((CHUNK, GM_HEADS)), full((CONV_WIDTH, CV_WIDTH)),
                  full((1, CV_WIDTH)), full((1, CV_WIDTH)), full((1, CV_WIDTH)), full((D_MODEL, D_MODEL)),
                  full((1, D_MODEL)), full((D_MODEL, D_MODEL)), kv_spec, kv_spec, full((D_MODEL, D_MODEL)),
                  hbm, hbm, hbm],
        out_specs=[rows(1), seq_out(HALO, CV_WIDTH), seq_out(CHUNK, GM_WIDTH), hbm, hbm, hbm],
        out_shape=[jax.ShapeDtypeStruct((nb, seq, D_MODEL), F32),
                   jax.ShapeDtypeStruct((nb, HALO, CV_WIDTH), F32),
                   jax.ShapeDtypeStruct((nb, CHUNK, GM_WIDTH), F32),
                   jax.ShapeDtypeStruct(w1f.shape, BF16), jax.ShapeDtypeStruct(w3f.shape, BF16),
                   jax.ShapeDtypeStruct(w2f.shape, BF16)],
        scratch_shapes=[pltpu.VMEM((tb, IN_COLS), F32), pltpu.VMEM((HALO + tb, CV_WIDTH), F32),
                        pltpu.VMEM((tb, CV_WIDTH), F32), pltpu.VMEM((tb, D_MODEL), BF16),
                        pltpu.VMEM((tb, D_MODEL), BF16), pltpu.VMEM((tb, D_MODEL), F32),
                        pltpu.VMEM((tb, D_MODEL), F32),
                        pltpu.VMEM((2, r13, EXPERT_FF), F32), pltpu.VMEM((2, r13, EXPERT_FF), F32),
                        pltpu.VMEM((2, r2, D_MODEL), F32),
                        pltpu.VMEM((r13, EXPERT_FF), BF16), pltpu.VMEM((r13, EXPERT_FF), BF16),
                        pltpu.VMEM((r2, D_MODEL), BF16),
                        pltpu.SemaphoreType.DMA((2, 3)), pltpu.SemaphoreType.DMA((3,))],
        compiler_params=pltpu.CompilerParams(dimension_semantics=("arbitrary",), vmem_limit_bytes=VMEM_LIMIT),
        name="prompt_mix_attn",
    )(x, gmix, win, glng, glnb, ws, bst, cw, cb, clng, clnb, wout, gxa, wq, kb, vb, wo, w1f, w3f, w2f)


def _sample_mix_kernel(x_ref, gmix_ref, win_ref, glng_ref, glnb_ref, ws0_ref, bs0_ref, cache_ref, cw_ref, cb_ref,
                       clng_ref, clnb_ref, wout_ref, gxa_ref, wq_ref,
                       y_ref, conv_ref, v_ref, q_ref, ab_s):
    x = x_ref[...]
    z = _dot(_rms(x, gmix_ref[...]).astype(BF16), win_ref[...])
    for h in range(GM_HEADS):
        cu = slice(h * GM_HEAD_DIM, (h + 1) * GM_HEAD_DIM)
        cv = slice(GM_WIDTH + h * GM_HEAD_DIM, GM_WIDTH + (h + 1) * GM_HEAD_DIM)
        v = _ln(jax.nn.gelu(z[:, cv]), glng_ref[:, cu], glnb_ref[:, cu])
        v_ref[:, cu] = v
        ab_s[:, cu] = (jax.nn.gelu(z[:, cu]) * (v * ws0_ref[:, cu] + bs0_ref[:, cu])).astype(BF16)
    glu = z[:, 2 * GM_WIDTH:2 * GM_WIDTH + CV_WIDTH] * jax.nn.sigmoid(z[:, 2 * GM_WIDTH + CV_WIDTH:])
    conv_ref[0:CONV_WIDTH - 2] = cache_ref[1:CONV_WIDTH - 1]
    conv_ref[CONV_WIDTH - 2] = glu
    conv = glu * cw_ref[CONV_WIDTH - 1:CONV_WIDTH, :] + cb_ref[...]
    for k in range(CONV_WIDTH - 1):
        conv = conv + cache_ref[k] * cw_ref[k:k + 1, :]
    b = _ln(conv, clng_ref[...], clnb_ref[...])
    ab_s[:, GM_WIDTH:] = (b * jax.nn.sigmoid(b)).astype(BF16)
    y = x + _dot(ab_s[...], wout_ref[...])
    y_ref[...] = y
    q_ref[...] = _dot(_rms(y, gxa_ref[...]).astype(BF16), wq_ref[...]) * (XA_HEAD_DIM ** -0.5)


def _sample_mix(x, gmix, win, glng, glnb, ws0, bs0, cache_t, cw, cb, clng, clnb, wout, gxa, wq):
    ns = x.shape[0]
    return pl.pallas_call(
        _sample_mix_kernel,
        out_shape=[jax.ShapeDtypeStruct((ns, D_MODEL), F32), jax.ShapeDtypeStruct(cache_t.shape, F32),
                   jax.ShapeDtypeStruct((ns, GM_WIDTH), F32), jax.ShapeDtypeStruct((ns, D_MODEL), F32)],
        scratch_shapes=[pltpu.VMEM((ns, D_MODEL), BF16)],
        compiler_params=pltpu.CompilerParams(vmem_limit_bytes=VMEM_LIMIT),
        name="sample_mix",
    )(x, gmix, win, glng, glnb, ws0, bs0, cache_t, cw, cb, clng, clnb, wout, gxa, wq)


def _sample_attn_kernel(q_ref, k_ref, v_ref, o_ref):
    ones = jnp.ones((LANES, LANES), BF16)
    rows = N_MEM * SUBLANES
    for i in range(q_ref.shape[0]):
        prod = (k_ref[i] * q_ref[i][None]).reshape(rows, LANES).astype(BF16)
        part = _dot(prod, ones).reshape(N_MEM, SUBLANES, LANES)
        s = part + pltpu.roll(part, XA_HEADS, axis=1)
        e = jnp.exp(s - jnp.max(s, axis=0, keepdims=True))
        p = e / jnp.sum(e, axis=0, keepdims=True)
        o_ref[i] = jnp.sum(p * v_ref[i], axis=0)


def _split_heads(a):
    halves = XA_HEAD_DIM // LANES
    assert halves * XA_HEADS == SUBLANES
    lead = a.shape[:-2]
    a = a.reshape(*lead, XA_HEADS, halves, LANES)
    return jnp.swapaxes(a, -3, -2).reshape(*lead, SUBLANES, LANES)


def _merge_heads(o):
    ns = o.shape[0]
    o = jnp.swapaxes(o.reshape(ns, XA_HEAD_DIM // LANES, XA_HEADS, LANES), 1, 2)
    return o.reshape(ns, XA_HEADS * XA_HEAD_DIM)


def _sample_attn(q, k, v, first, count):
    nb = SAMPLE_ATTN_BLOCK
    assert first % nb == 0 and count % nb == 0
    off = first // nb
    return pl.pallas_call(
        _sample_attn_kernel,
        grid=(count // nb,),
        in_specs=[pl.BlockSpec((nb, SUBLANES, LANES), lambda i: (i + off, 0, 0)),
                  pl.BlockSpec((nb, N_MEM, SUBLANES, LANES), lambda i: (i + off, 0, 0, 0)),
                  pl.BlockSpec((nb, N_MEM, SUBLANES, LANES), lambda i: (i + off, 0, 0, 0))],
        out_specs=pl.BlockSpec((nb, SUBLANES, LANES), lambda i: (i, 0, 0)),
        out_shape=jax.ShapeDtypeStruct((count, SUBLANES, LANES), F32),
        compiler_params=pltpu.CompilerParams(dimension_semantics=("arbitrary",), vmem_limit_bytes=VMEM_LIMIT),
        name="sample_attn",
    )(q, k, v)


def _sample_proj_kernel(y_ref, o_ref, wo_ref, h_ref):
    h_ref[...] = y_ref[...] + _dot(o_ref[...].astype(BF16), wo_ref[...])


def _sample_proj(y, o, wo):
    return pl.pallas_call(
        _sample_proj_kernel,
        out_shape=jax.ShapeDtypeStruct(y.shape, F32),
        compiler_params=pltpu.CompilerParams(vmem_limit_bytes=VMEM_LIMIT),
        name="sample_proj",
    )(y, o, wo)


def _moe_kernel(h_ref, gffn_ref, wrt_ref, rb_ref, w1_ref, w3_ref, w2_ref, gfin_ref,
                o_ref, xt_s, u_s, gidx_s, slot_s, comb_s, act_s, p2_s, y2_s, cnt_s):
    i = pl.program_id(0)
    g = pl.program_id(1)
    rt = h_ref.shape[0]
    sb = act_s.shape[0]

    @pl.when(jnp.logical_and(i == 0, g == 0))
    def _():
        u_s[...] = (lax.broadcasted_iota(jnp.int32, (rt, rt), 0)
                    < lax.broadcasted_iota(jnp.int32, (rt, rt), 1)).astype(F32).astype(BF16)

    @pl.when(g == 0)
    def _route():
        h = h_ref[...]
        o_ref[...] = h
        xt = _rms(h, gffn_ref[...]).astype(BF16)
        xt_s[...] = xt
        lt = _dot_nt(wrt_ref[...], xt) + rb_ref[...]
        gl = [lt[k:k + 1, :] for k in range(N_GROUPS)]
        gmax = jnp.maximum(jnp.maximum(gl[0], gl[1]), jnp.maximum(gl[2], gl[3]))
        gidx = jnp.where(gl[0] == gmax, 0, jnp.where(gl[1] == gmax, 1, jnp.where(gl[2] == gmax, 2, 3)))
        gidx = gidx.astype(jnp.int32)
        sumexp = (jnp.exp(gl[0] - gmax) + jnp.exp(gl[1] - gmax)) + (jnp.exp(gl[2] - gmax) + jnp.exp(gl[3] - gmax))
        p_g = 1.0 / sumexp
        esel = lt[SUBLANES + 3 * EXPERTS_PER_GROUP:SUBLANES + 4 * EXPERTS_PER_GROUP, :]
        for k in (2, 1, 0):
            esel = jnp.where(gidx == k, lt[SUBLANES + k * EXPERTS_PER_GROUP:SUBLANES + (k + 1) * EXPERTS_PER_GROUP, :],
                             esel)
        eidx = lax.broadcasted_iota(jnp.int32, (EXPERTS_PER_GROUP, rt), 0)
        m1 = jnp.max(esel, axis=0, keepdims=True)
        i1 = jnp.min(jnp.where(esel == m1, eidx, EXPERTS_PER_GROUP), axis=0, keepdims=True)
        rest = jnp.where(eidx == i1, -jnp.inf, esel)
        m2 = jnp.max(rest, axis=0, keepdims=True)
        i2 = jnp.min(jnp.where(rest == m2, eidx, EXPERTS_PER_GROUP), axis=0, keepdims=True)
        t2 = jnp.exp(m2 - m1)
        den = 1.0 + t2
        w_top1 = (1.0 / den) * p_g
        w_top2 = (t2 / den) * p_g
        within = jnp.where(eidx == i1, w_top1, 0.0) + jnp.where(eidx == i2, w_top2, 0.0)
        c_hi = within.astype(BF16).astype(F32)
        r1 = within - c_hi
        c_mid = r1.astype(BF16).astype(F32)
        c_lo = (r1 - c_mid).astype(BF16).astype(F32)
        comb_s[0:8, :] = c_hi
        comb_s[8:16, :] = c_mid
        comb_s[16:24, :] = c_lo
        comb_s[24:32, :] = jnp.zeros((8, rt), F32)
        onehot = (eidx == gidx).astype(F32)
        rank = _dot(onehot.astype(BF16), u_s[...])
        slot_s[...] = jnp.sum(onehot * rank, axis=0, keepdims=True).astype(jnp.int32)
        gidx_s[...] = gidx
        for k in range(N_GROUPS):
            cnt_s[k] = jnp.sum(onehot[k:k + 1, :]).astype(jnp.int32)

    n_blk = (cnt_s[g] + sb - 1) // sb

    def sub_block(j, half):
        half_rows = slice(half * sb, (half + 1) * sb)
        rows = lax.broadcasted_iota(jnp.int32, (sb, rt), 0) + j * sb
        hit = jnp.logical_and(rows == slot_s[...], gidx_s[...] == g)
        p = jnp.where(hit, 1.0, 0.0).astype(BF16)
        p2_s[half_rows, :] = p
        xc = _dot(p, xt_s[...]).astype(BF16)
        cexp = _dot_nt(p, comb_s[...].astype(BF16))
        cw = (cexp[:, 0:8] + cexp[:, 8:16]) + cexp[:, 16:24]
        for e in range(EXPERTS_PER_GROUP):
            h1 = _dot(xc, w1_ref[e])
            h3 = _dot(xc, w3_ref[e])
            a = (h1 * jax.nn.sigmoid(h1)) * h3 * cw[:, e:e + 1]
            act_s[:, e * EXPERT_FF:(e + 1) * EXPERT_FF] = a.astype(BF16)
        y2_s[half_rows, :] = _dot(act_s[...], w2_ref[...]).astype(BF16)

    def body(jj, carry):
        sub_block(2 * jj, 0)

        @pl.when(2 * jj + 1 < n_blk)
        def _():
            sub_block(2 * jj + 1, 1)

        @pl.when(2 * jj + 1 >= n_blk)
        def _():
            p2_s[sb:, :] = jnp.zeros((sb, rt), BF16)
            y2_s[sb:, :] = jnp.zeros((sb, D_MODEL), BF16)

        o_ref[...] += _dot_tn(p2_s[...], y2_s[...])
        return carry

    lax.fori_loop(0, (n_blk + 1) // 2, body, 0)

    @pl.when(g == N_GROUPS - 1)
    def _():
        o_ref[...] = _rms(o_ref[...], gfin_ref[...])


def _moe(h, gffn, wrt, rbias, w1, w3, w2, gfin, *, tile):
    tokens = h.shape[0]
    sb = min(MOE_SUB, tile)
    ff = EXPERTS_PER_GROUP * EXPERT_FF
    full = lambda shape: pl.BlockSpec(shape, lambda i, g: (0,) * len(shape))
    row_spec = pl.BlockSpec((tile, D_MODEL), lambda i, g: (i, 0))
    return pl.pallas_call(
        _moe_kernel,
        grid=(tokens // tile, N_GROUPS),
        in_specs=[row_spec, full((1, D_MODEL)), full((ROUTER_ROWS, D_MODEL)), full((ROUTER_ROWS, 1)),
                  pl.BlockSpec((None, EXPERTS_PER_GROUP, D_MODEL, EXPERT_FF), lambda i, g: (g, 0, 0, 0)),
                  pl.BlockSpec((None, EXPERTS_PER_GROUP, D_MODEL, EXPERT_FF), lambda i, g: (g, 0, 0, 0)),
                  pl.BlockSpec((None, ff, D_MODEL), lambda i, g: (g, 0, 0)),
                  full((1, D_MODEL))],
        out_specs=row_spec,
        out_shape=jax.ShapeDtypeStruct((tokens, D_MODEL), F32),
        scratch_shapes=[pltpu.VMEM((tile, D_MODEL), BF16), pltpu.VMEM((tile, tile), BF16),
                        pltpu.VMEM((1, tile), jnp.int32), pltpu.VMEM((1, tile), jnp.int32),
                        pltpu.VMEM((4 * SUBLANES, tile), F32), pltpu.VMEM((sb, ff), BF16),
                        pltpu.VMEM((2 * sb, tile), BF16), pltpu.VMEM((2 * sb, D_MODEL), BF16),
                        pltpu.SMEM((N_GROUPS,), jnp.int32)],
        compiler_params=pltpu.CompilerParams(dimension_semantics=("arbitrary", "arbitrary"),
                                             vmem_limit_bytes=VMEM_LIMIT),
        name="moe",
    )(h, gffn, wrt, rbias, w1, w3, w2, gfin)


HALF = D_MODEL // 2
SUB_PER_ROW = HALF // LANES
N_EXPERTS = N_GROUPS * EXPERTS_PER_GROUP


def _pack_bf16_pairs(x):
    bits = pltpu.bitcast(x.astype(BF16).astype(F32), jnp.uint32)
    return (bits[:, HALF:] & jnp.uint32(0xFFFF0000)) | (bits[:, :HALF] >> 16)


def _unpack_bf16_pairs(w):
    lo = pltpu.bitcast(w << 16, F32)
    hi = pltpu.bitcast(w & jnp.uint32(0xFFFF0000), F32)
    return lo, hi


def _route_kernel(h_ref, gffn_ref, wrt_ref, rb_ref, xp_ref, e_ref, w_ref, r_ref, cnt_ref, u_s, run_s):
    i = pl.program_id(0)
    rt = h_ref.shape[0]

    @pl.when(i == 0)
    def _():
        u_s[...] = (lax.broadcasted_iota(jnp.int32, (rt, rt), 0)
                    < lax.broadcasted_iota(jnp.int32, (rt, rt), 1)).astype(F32).astype(BF16)
        run_s[...] = jnp.zeros(run_s.shape, F32)

    xt = _rms(h_ref[...], gffn_ref[...])
    _store_planes(xp_ref, _pack_bf16_pairs(xt))
    lt = _dot_nt(wrt_ref[...], xt.astype(BF16)) + rb_ref[...]
    gl = [lt[k:k + 1, :] for k in range(N_GROUPS)]
    gmax = jnp.maximum(jnp.maximum(gl[0], gl[1]), jnp.maximum(gl[2], gl[3]))
    gidx = jnp.where(gl[0] == gmax, 0, jnp.where(gl[1] == gmax, 1, jnp.where(gl[2] == gmax, 2, 3)))
    gidx = gidx.astype(jnp.int32)
    sumexp = (jnp.exp(gl[0] - gmax) + jnp.exp(gl[1] - gmax)) + (jnp.exp(gl[2] - gmax) + jnp.exp(gl[3] - gmax))
    p_g = 1.0 / sumexp
    esel = lt[SUBLANES + 3 * EXPERTS_PER_GROUP:SUBLANES + 4 * EXPERTS_PER_GROUP, :]
    for k in (2, 1, 0):
        esel = jnp.where(gidx == k, lt[SUBLANES + k * EXPERTS_PER_GROUP:SUBLANES + (k + 1) * EXPERTS_PER_GROUP, :],
                         esel)
    eidx = lax.broadcasted_iota(jnp.int32, (EXPERTS_PER_GROUP, rt), 0)
    m1 = jnp.max(esel, axis=0, keepdims=True)
    i1 = jnp.min(jnp.where(esel == m1, eidx, EXPERTS_PER_GROUP), axis=0, keepdims=True)
    rest = jnp.where(eidx == i1, -jnp.inf, esel)
    m2 = jnp.max(rest, axis=0, keepdims=True)
    i2 = jnp.min(jnp.where(rest == m2, eidx, EXPERTS_PER_GROUP), axis=0, keepdims=True)
    t2 = jnp.exp(m2 - m1)
    den = 1.0 + t2
    w_ref[0:1, :] = (1.0 / den) * p_g
    w_ref[1:2, :] = (t2 / den) * p_g
    e1 = gidx * EXPERTS_PER_GROUP + i1
    e2 = gidx * EXPERTS_PER_GROUP + i2
    e_ref[0:1, :] = e1
    e_ref[1:2, :] = e2
    xid = lax.broadcasted_iota(jnp.int32, (N_EXPERTS, rt), 0)
    oh1 = (xid == e1).astype(F32)
    oh2 = (xid == e2).astype(F32)
    both = oh1 + oh2
    before = _dot(both.astype(BF16), u_s[...]) + run_s[:, 0:1]
    r_ref[0:1, :] = jnp.sum(oh1 * before, axis=0, keepdims=True).astype(jnp.int32)
    r_ref[1:2, :] = jnp.sum(oh2 * before, axis=0, keepdims=True).astype(jnp.int32)
    run_s[...] = run_s[...] + jnp.sum(both, axis=1, keepdims=True)
    cnt_ref[...] = run_s[...].astype(jnp.int32)


def _route(h, gffn, wrt, rbias, *, tile):
    tokens = h.shape[0]
    full = lambda shape: pl.BlockSpec(shape, lambda i: (0,) * len(shape))
    lanes = lambda rows: pl.BlockSpec((rows, tile), lambda i: (0, i))
    return pl.pallas_call(
        _route_kernel,
        grid=(tokens // tile,),
        in_specs=[pl.BlockSpec((tile, D_MODEL), lambda i: (i, 0)), full((1, D_MODEL)),
                  full((ROUTER_ROWS, D_MODEL)), full((ROUTER_ROWS, 1))],
        out_specs=[pl.BlockSpec((SUB_PER_ROW, tile, LANES), lambda i: (0, i, 0)), lanes(2), lanes(2), lanes(2),
                   full((N_EXPERTS, LANES))],
        out_shape=[jax.ShapeDtypeStruct((SUB_PER_ROW, tokens, LANES), jnp.uint32),
                   jax.ShapeDtypeStruct((2, tokens), jnp.int32),
                   jax.ShapeDtypeStruct((2, tokens), F32), jax.ShapeDtypeStruct((2, tokens), jnp.int32),
                   jax.ShapeDtypeStruct((N_EXPERTS, LANES), jnp.int32)],
        scratch_shapes=[pltpu.VMEM((tile, tile), BF16), pltpu.VMEM((N_EXPERTS, LANES), F32)],
        compiler_params=pltpu.CompilerParams(dimension_semantics=("arbitrary",), vmem_limit_bytes=VMEM_LIMIT),
        name="moe_route",
    )(h, gffn, wrt, rbias)


def _store_planes(ref, words):
    for j in range(SUB_PER_ROW):
        ref[j] = words[:, j * LANES:(j + 1) * LANES]


def _load_planes(ref):
    return jnp.concatenate([ref[j] for j in range(SUB_PER_ROW)], axis=1)


def _sub_row_index(row_of_token, n_rows):
    plane = jnp.arange(SUB_PER_ROW, dtype=jnp.int32)[:, None] * n_rows
    return (row_of_token[None, :] + plane).reshape(1, -1)


def _sc_mesh():
    return plsc.VectorSubcoreMesh(core_axis_name="core", subcore_axis_name="subcore")


def _sc_scatter_two(x_sub, idx_a, idx_b, n_out):
    n_in = x_sub.shape[0]

    @pl.kernel(out_type=jax.ShapeDtypeStruct((n_out, LANES), x_sub.dtype), mesh=_sc_mesh(), scratch_types=[])
    def scatter(x_hbm, a_hbm, b_hbm, o_hbm):
        def body(x_vmem, a_vmem, b_vmem):
            pltpu.sync_copy(x_vmem, o_hbm.at[a_vmem.at[0]])
            pltpu.sync_copy(x_vmem, o_hbm.at[b_vmem.at[0]])

        pltpu.emit_pipeline(
            body, grid=(n_in // SC_WINDOW,),
            in_specs=[pl.BlockSpec((SC_WINDOW, LANES), lambda i: (i, 0)),
                      pl.BlockSpec((1, SC_WINDOW), lambda i: (0, i)),
                      pl.BlockSpec((1, SC_WINDOW), lambda i: (0, i))],
            out_specs=[],
            core_axis_name=("core", "subcore"), dimension_semantics=(pltpu.PARALLEL,),
        )(x_hbm, a_hbm, b_hbm)

    return scatter(x_sub, idx_a, idx_b)


def _sc_gather(table, idx):
    n_out = idx.shape[1]

    @pl.kernel(out_type=jax.ShapeDtypeStruct((n_out, LANES), table.dtype), mesh=_sc_mesh())
    def gather(t_hbm, i_hbm, o_hbm):
        def body(i_vmem, o_vmem):
            pltpu.sync_copy(t_hbm.at[i_vmem.at[0]], o_vmem)

        pltpu.emit_pipeline(
            body, grid=(n_out // SC_WINDOW,),
            in_specs=[pl.BlockSpec((1, SC_WINDOW), lambda i: (0, i))],
            out_specs=[pl.BlockSpec((SC_WINDOW, LANES), lambda i: (i, 0))],
            core_axis_name=("core", "subcore"), dimension_semantics=(pltpu.PARALLEL,),
        )(i_hbm, o_hbm)

    return gather(table, idx)


def _expert_ffn_kernel(blk_e_ref, n_valid_ref, x_ref, w1_ref, w3_ref, w2_ref, y_ref):
    del blk_e_ref

    @pl.when(pl.program_id(0) < n_valid_ref[0])
    def _():
        lo, hi = _unpack_bf16_pairs(_load_planes(x_ref))
        xc = jnp.concatenate([lo.astype(BF16), hi.astype(BF16)], axis=1)
        h1 = _dot(xc, w1_ref[...])
        h3 = _dot(xc, w3_ref[...])
        act = ((h1 * jax.nn.sigmoid(h1)) * h3).astype(BF16)
        _store_planes(y_ref, _pack_bf16_pairs(_dot(act, w2_ref[...])))


def _expert_ffn(xs, blk_e, n_valid, w1, w3, w2):
    rows = xs.shape[1]
    n_blocks = rows // FFN_BLOCK
    live = lambda b, be, nv: jnp.minimum(b, nv[0] - 1)
    grid_spec = pltpu.PrefetchScalarGridSpec(
        num_scalar_prefetch=2, grid=(n_blocks,),
        in_specs=[pl.BlockSpec((SUB_PER_ROW, FFN_BLOCK, LANES), lambda b, be, nv: (0, live(b, be, nv), 0)),
                  pl.BlockSpec((None, D_MODEL, EXPERT_FF), lambda b, be, nv: (be[b], 0, 0)),
                  pl.BlockSpec((None, D_MODEL, EXPERT_FF), lambda b, be, nv: (be[b], 0, 0)),
                  pl.BlockSpec((None, EXPERT_FF, D_MODEL), lambda b, be, nv: (be[b], 0, 0))],
        out_specs=pl.BlockSpec((SUB_PER_ROW, FFN_BLOCK, LANES), lambda b, be, nv: (0, live(b, be, nv), 0)))
    return pl.pallas_call(
        _expert_ffn_kernel, grid_spec=grid_spec,
        out_shape=jax.ShapeDtypeStruct((SUB_PER_ROW, rows, LANES), jnp.uint32),
        compiler_params=pltpu.CompilerParams(dimension_semantics=("arbitrary",), vmem_limit_bytes=VMEM_LIMIT),
        name="moe_ffn",
    )(blk_e, n_valid, xs, w1, w3, w2)


def _combine_kernel(h_ref, ya_ref, yb_ref, wt_ref, gfin_ref, o_ref):
    a_lo, a_hi = _unpack_bf16_pairs(_load_planes(ya_ref))
    b_lo, b_hi = _unpack_bf16_pairs(_load_planes(yb_ref))
    wa = wt_ref[:, 0:1]
    wb = wt_ref[:, 1:2]
    o_ref[:, :HALF] = h_ref[:, :HALF] + (wa * a_lo + wb * b_lo)
    o_ref[:, HALF:] = h_ref[:, HALF:] + (wa * a_hi + wb * b_hi)
    o_ref[...] = _rms(o_ref[...], gfin_ref[...])


def _combine(h, ya, yb, wt, gfin, *, tile):
    tokens = h.shape[0]
    return pl.pallas_call(
        _combine_kernel,
        grid=(tokens // tile,),
        in_specs=[pl.BlockSpec((tile, D_MODEL), lambda i: (i, 0)),
                  pl.BlockSpec((SUB_PER_ROW, tile, LANES), lambda i: (0, i, 0)),
                  pl.BlockSpec((SUB_PER_ROW, tile, LANES), lambda i: (0, i, 0)), pl.BlockSpec((tile, 2), lambda i: (i, 0)),
                  pl.BlockSpec((1, D_MODEL), lambda i: (0, 0))],
        out_specs=pl.BlockSpec((tile, D_MODEL), lambda i: (i, 0)),
        out_shape=jax.ShapeDtypeStruct((tokens, D_MODEL), F32),
        compiler_params=pltpu.CompilerParams(dimension_semantics=("arbitrary",), vmem_limit_bytes=VMEM_LIMIT),
        name="moe_combine",
    )(h, ya, yb, wt, gfin)


def _moe_dispatch(h, gffn, wrt, rbias):
    tokens = h.shape[0]
    xp, e12, w12, r12, cnt = _route(h, gffn, wrt, rbias, tile=MOE_TILE)
    count = cnt[:, 0]
    padded = (count + FFN_BLOCK - 1) // FFN_BLOCK * FFN_BLOCK
    end = jnp.cumsum(padded)
    start = end - padded
    n_rows = 2 * tokens + N_EXPERTS * FFN_BLOCK
    n_blocks = n_rows // FFN_BLOCK
    n_valid = (end[-1:] // FFN_BLOCK).astype(jnp.int32)
    first_row = jnp.minimum(jnp.arange(n_blocks, dtype=jnp.int32), n_valid - 1) * FFN_BLOCK
    blk_e = jnp.sum((end[None, :] <= first_row[:, None]).astype(jnp.int32), axis=1)
    experts = jnp.arange(N_EXPERTS, dtype=jnp.int32)
    start_of = jnp.sum(jnp.where(e12[:, :, None] == experts, start.astype(jnp.int32), 0), axis=-1)
    rows_ab = start_of + r12
    idx_a, idx_b = _sub_row_index(rows_ab[0], n_rows), _sub_row_index(rows_ab[1], n_rows)
    xs = _sc_scatter_two(xp.reshape(-1, LANES), idx_a, idx_b, n_rows * SUB_PER_ROW)
    return xs.reshape(SUB_PER_ROW, n_rows, LANES), (blk_e, n_valid, idx_a, idx_b, w12)


def _moe_finish(h, xs, plan, w1, w3, w2, gfin):
    blk_e, n_valid, idx_a, idx_b, w12 = plan
    ys = _expert_ffn(xs, blk_e, n_valid, w1, w3, w2).reshape(-1, LANES)
    ya = _sc_gather(ys, idx_a).reshape(SUB_PER_ROW, -1, LANES)
    yb = _sc_gather(ys, idx_b).reshape(SUB_PER_ROW, -1, LANES)
    return _combine(h, ya, yb, w12.T, gfin, tile=MOE_TILE)


def kernel(x_prompt, x_sample, mem_prompt, cache_conv, cache_mem_k, cache_mem_v, norm_mix_g, w_in, gm_ln_g, gm_ln_b, gm_ws, gm_bs, conv_w, conv_b, cv_ln_g, cv_ln_b, w_out, norm_mem_g, norm_xa_g, xa_wq, xa_wk, xa_wv, xa_wo, norm_ffn_g, router_g, router_g_b, router_e, router_e_b, exp_w1, exp_w3, exp_w2, final_norm_g):
    depth = w_in.shape[0]
    assert depth == 1, "single-layer trunk"
    nb, seq, _ = x_prompt.shape
    ns = x_sample.shape[0]
    row = lambda a: a.reshape(1, -1)

    l = 0
    gmix, gxa, gffn, gmem = row(norm_mix_g[l]), row(norm_xa_g[l]), row(norm_ffn_g[l]), row(norm_mem_g[l])
    gfin = row(final_norm_g)
    win, wout = w_in[l].astype(BF16), w_out[l].astype(BF16)
    wq, wo = xa_wq[l].astype(BF16), xa_wo[l].astype(BF16)
    wk, wv = xa_wk[l], xa_wv[l]
    glng, glnb = row(gm_ln_g[l]), row(gm_ln_b[l])
    cw, cb, clng, clnb = conv_w[l], row(conv_b[l]), row(cv_ln_g[l]), row(cv_ln_b[l])
    ws, bst = gm_ws[l], gm_bs[l].T
    ws0 = jnp.repeat(gm_ws[l][:, 0, 0], GM_HEAD_DIM).reshape(1, GM_WIDTH)
    bs0 = jnp.repeat(gm_bs[l][:, 0], GM_HEAD_DIM).reshape(1, GM_WIDTH)

    n_exp = N_GROUPS * EXPERTS_PER_GROUP
    pad_g = SUBLANES - N_GROUPS
    pad_t = ROUTER_ROWS - SUBLANES - n_exp
    wrt = jnp.concatenate([router_g[l].T, jnp.zeros((pad_g, D_MODEL), F32),
                           router_e[l].reshape(D_MODEL, n_exp).T, jnp.zeros((pad_t, D_MODEL), F32)], axis=0).astype(BF16)
    rbias = jnp.concatenate([router_g_b[l], jnp.zeros((pad_g,), F32), router_e_b[l].reshape(n_exp),
                             jnp.zeros((pad_t,), F32)]).reshape(ROUTER_ROWS, 1)

    mk, mv, kb, vb = _memkv(mem_prompt.reshape(nb * N_MEM, D_MODEL), gmem, wk, wv)
    kb, vb = kb.reshape(nb, N_MEM, D_MODEL), vb.reshape(nb, N_MEM, D_MODEL)
    hp, conv_tail, gmv_p, w1, w3, w2 = _prompt_mix_attn(
        x_prompt, gmix, win, glng, glnb, ws, bst, cw, cb, clng, clnb, wout, gxa, wq, kb, vb, wo,
        exp_w1[l].reshape(-1, EXPERT_FF), exp_w3[l].reshape(-1, EXPERT_FF), exp_w2[l].reshape(-1, D_MODEL))
    h2d = hp.reshape(nb * seq, D_MODEL)
    xs, plan = _moe_dispatch(h2d, gffn, wrt, rbias)
    w1e, w3e, w2e = (w1.reshape(N_EXPERTS, D_MODEL, EXPERT_FF), w3.reshape(N_EXPERTS, D_MODEL, EXPERT_FF),
                     w2.reshape(N_EXPERTS, EXPERT_FF, D_MODEL))
    w1 = w1.reshape(N_GROUPS, EXPERTS_PER_GROUP, D_MODEL, EXPERT_FF)
    w3 = w3.reshape(N_GROUPS, EXPERTS_PER_GROUP, D_MODEL, EXPERT_FF)
    w2 = w2.reshape(N_GROUPS, EXPERTS_PER_GROUP * EXPERT_FF, D_MODEL)

    cache_t = jnp.transpose(cache_conv[l], (1, 0, 2))
    ys, conv_t, gmv_s, q_s = _sample_mix(x_sample.reshape(ns, D_MODEL), gmix, win, glng, glnb, ws0, bs0,
                                        cache_t, cw, cb, clng, clnb, wout, gxa, wq)
    qh, kh, vh = (_split_heads(q_s.reshape(ns, XA_HEADS, XA_HEAD_DIM)), _split_heads(cache_mem_k[l]),
                  _split_heads(cache_mem_v[l]))
    half = ns // 2
    o_first = _sample_attn(qh, kh, vh, 0, half)
    xs, o_first = lax.optimization_barrier((xs, o_first))
    y_prompt = _moe_finish(h2d, xs, plan, w1e, w3e, w2e, gfin)
    o_second = _sample_attn(qh, kh, vh, half, ns - half)
    o_s = _merge_heads(jnp.concatenate([o_first, o_second], axis=0))
    hs = _sample_proj(ys, o_s, wo)
    y_sample = _moe(hs, gffn, wrt, rbias, w1, w3, w2, gfin, tile=ns)

    conv_prompt = conv_tail[:, HALO - (CONV_WIDTH - 1):, :][None]
    conv_sample = jnp.transpose(conv_t, (1, 0, 2))[None]
    return (y_prompt.reshape(nb, seq, D_MODEL), y_sample.reshape(ns, 1, D_MODEL), conv_prompt, conv_sample,
            gmv_p[None], gmv_s.reshape(1, ns, 1, GM_WIDTH),
            mk.reshape(1, nb, N_MEM, XA_HEADS, XA_HEAD_DIM), mv.reshape(1, nb, N_MEM, XA_HEADS, XA_HEAD_DIM))
```

```python
import functools

import jax
import jax.numpy as jnp
from jax import lax
from jax.experimental import pallas as pl
from jax.experimental.pallas import tpu as pltpu
from jax.experimental.pallas import tpu_sc as plsc

F32 = jnp.float32
BF16 = jnp.bfloat16

D_MODEL = 1024
GM_WIDTH = 512
CV_WIDTH = 512
GM_HEADS = 4
GM_HEAD_DIM = 128
CHUNK = 128
CONV_WIDTH = 31
IN_COLS = 2 * GM_WIDTH + 2 * CV_WIDTH
N_MEM = 256
XA_HEADS = 4
XA_HEAD_DIM = 256
N_GROUPS = 4
EXPERTS_PER_GROUP = 8
EXPERT_FF = 256
EPS = 1e-6

LANES = 128
SUBLANES = 8
HALO = 32
PROMPT_TILE = 512
MOE_TILE = 1024
MOE_SUB = 128
FFN_BLOCK = 512
FFN_PER_STEP = 2
SC_WINDOW = 128
SAMPLE_ATTN_BLOCK = 8
ROUTER_ROWS = 128
VMEM_LIMIT = 56 * 1024 * 1024


def _rms(x, g):
    return x * lax.rsqrt(jnp.mean(x * x, axis=-1, keepdims=True) + EPS) * g


def _ln(x, g, b):
    mu = jnp.mean(x, axis=-1, keepdims=True)
    xc = x - mu
    var = jnp.mean(xc * xc, axis=-1, keepdims=True)
    return xc * lax.rsqrt(var + EPS) * g + b


def _dot(a, b):
    return jnp.dot(a, b, preferred_element_type=F32)


def _dot_nt(a, b):
    return lax.dot_general(a, b, (((1,), (1,)), ((), ())), preferred_element_type=F32)


def _dot_tn(a, b):
    return lax.dot_general(a, b, (((0,), (0,)), ((), ())), preferred_element_type=F32)


def _memkv_kernel(mem_ref, g_ref, wk_ref, wv_ref, k_ref, v_ref, kb_ref, vb_ref):
    mn = _rms(mem_ref[...], g_ref[...]).astype(BF16)
    k = _dot(mn, wk_ref[...].astype(BF16))
    v = _dot(mn, wv_ref[...].astype(BF16))
    for h in range(XA_HEADS):
        cols = slice(h * XA_HEAD_DIM, (h + 1) * XA_HEAD_DIM)
        k_ref[:, h, :] = k[:, cols]
        v_ref[:, h, :] = v[:, cols]
    kb_ref[...] = k.astype(BF16)
    vb_ref[...] = v.astype(BF16)


def _memkv(mem2d, g, wk, wv):
    rows = mem2d.shape[0]
    tile = 512
    row_spec = pl.BlockSpec((tile, D_MODEL), lambda i: (i, 0))
    head_spec = pl.BlockSpec((tile, XA_HEADS, XA_HEAD_DIM), lambda i: (i, 0, 0))
    full = lambda shape: pl.BlockSpec(shape, lambda i: (0,) * len(shape))
    return pl.pallas_call(
        _memkv_kernel,
        grid=(rows // tile,),
        in_specs=[row_spec, full((1, D_MODEL)), full((D_MODEL, D_MODEL)), full((D_MODEL, D_MODEL))],
        out_specs=[head_spec, head_spec, row_spec, row_spec],
        out_shape=[jax.ShapeDtypeStruct((rows, XA_HEADS, XA_HEAD_DIM), F32),
                   jax.ShapeDtypeStruct((rows, XA_HEADS, XA_HEAD_DIM), F32),
                   jax.ShapeDtypeStruct((rows, D_MODEL), BF16), jax.ShapeDtypeStruct((rows, D_MODEL), BF16)],
        compiler_params=pltpu.CompilerParams(dimension_semantics=("arbitrary",), vmem_limit_bytes=VMEM_LIMIT),
        name="memkv",
    )(mem2d, g, wk, wv)


def _conv_block(g_s, cw_ref, start, lanes):
    assert start % SUBLANES == 0
    rows = CHUNK + HALO
    win = g_s[start:start + rows, lanes]
    off = HALO - (CONV_WIDTH - 1)
    acc = None
    for b in range(SUBLANES):
        shifted = pltpu.roll(win, rows - (off + b), axis=0) if off + b else win
        for k in range(b, CONV_WIDTH, SUBLANES):
            assert k - b + CHUNK + off + b <= rows
            term = shifted[k - b:k - b + CHUNK] * cw_ref[k:k + 1, lanes]
            acc = term if acc is None else acc + term
    return acc


def _cast_copies(src_refs, dst_refs, in_bufs, out_bufs, sem_in, sem_out, chunk, slot):
    copies_in, copies_out = [], []
    for i, (src, dst, ibuf, obuf) in enumerate(zip(src_refs, dst_refs, in_bufs, out_bufs)):
        rows = obuf.shape[0]
        start = pl.multiple_of(chunk * rows, rows)
        copies_in.append(pltpu.make_async_copy(src.at[pl.ds(start, rows), :], ibuf.at[slot], sem_in.at[slot, i]))
        copies_out.append(pltpu.make_async_copy(obuf, dst.at[pl.ds(start, rows), :], sem_out.at[i]))
    return copies_in, copies_out


def _prompt_kernel(x_ref, gmix_ref, win_ref, glng_ref, glnb_ref, ws_ref, bst_ref, cw_ref, cb_ref,
                   clng_ref, clnb_ref, wout_ref, gxa_ref, wq_ref, kb_ref, vb_ref, wo_ref,
                   w1f_ref, w3f_ref, w2f_ref,
                   h_ref, conv_ref, gmv_ref, w1b_ref, w3b_ref, w2b_ref,
                   z_s, g_s, c_s, ab_s, o_s, y_prev, y_cur, in1, in3, in2, st1, st3, st2, sem_in, sem_out,
                   *, tiles_per_seq, n_cast_chunks):
    s = pl.program_id(0)
    n_steps = pl.num_programs(0)
    tb = x_ref.shape[0]
    n_chunks = tb // CHUNK

    srcs, dsts = (w1f_ref, w3f_ref, w2f_ref), (w1b_ref, w3b_ref, w2b_ref)
    ins, sts = (in1, in3, in2), (st1, st3, st2)
    slot = s % 2
    chunk_of = lambda step: jnp.minimum(step, n_cast_chunks - 1)

    @pl.when(s == 0)
    def _():
        for c in _cast_copies(srcs, dsts, ins, sts, sem_in, sem_out, chunk_of(s), slot)[0]:
            c.start()

    @pl.when(s + 1 < n_steps)
    def _():
        for c in _cast_copies(srcs, dsts, ins, sts, sem_in, sem_out, chunk_of(s + 1), 1 - slot)[0]:
            c.start()

    cast_in, cast_out = _cast_copies(srcs, dsts, ins, sts, sem_in, sem_out, chunk_of(s), slot)
    for c in cast_in:
        c.wait()
    for ibuf, obuf in zip(ins, sts):
        obuf[...] = ibuf[slot].astype(BF16)
    for c in cast_out:
        c.start()

    @pl.when(s == 0)
    def _():
        y_prev[...] = jnp.zeros(y_prev.shape, F32)

    @pl.when(s % tiles_per_seq == 0)
    def _():
        g_s[0:HALO, :] = jnp.zeros((HALO, CV_WIDTH), F32)

    z_s[...] = _dot(_rms(x_ref[...], gmix_ref[...]).astype(BF16), win_ref[...])

    y = y_prev[...]
    qn = _rms(y, gxa_ref[...]).astype(BF16)
    q = (_dot(qn, wq_ref[...]) * (XA_HEAD_DIM ** -0.5)).astype(BF16)
    for h in range(XA_HEADS):
        cols = slice(h * XA_HEAD_DIM, (h + 1) * XA_HEAD_DIM)
        sc = _dot_nt(q[:, cols], kb_ref[:, cols])
        e = jnp.exp(sc - jnp.max(sc, axis=-1, keepdims=True))
        p = (e / jnp.sum(e, axis=-1, keepdims=True)).astype(BF16)
        o_s[:, cols] = _dot(p, vb_ref[:, cols]).astype(BF16)
    h_ref[...] = y + _dot(o_s[...], wo_ref[...])

    tri = (lax.broadcasted_iota(jnp.int32, (CHUNK, CHUNK), 0)
           >= lax.broadcasted_iota(jnp.int32, (CHUNK, CHUNK), 1))
    wm = [jnp.where(tri, ws_ref[h], 0.0).astype(BF16) for h in range(GM_HEADS)]

    for c in range(n_chunks):
        rows = slice(c * CHUNK, (c + 1) * CHUNK)
        for h in range(GM_HEADS):
            cu = slice(h * GM_HEAD_DIM, (h + 1) * GM_HEAD_DIM)
            cv = slice(GM_WIDTH + h * GM_HEAD_DIM, GM_WIDTH + (h + 1) * GM_HEAD_DIM)
            v = _ln(jax.nn.gelu(z_s[rows, cv]), glng_ref[:, cu], glnb_ref[:, cu])
            if c == n_chunks - 1:
                gmv_ref[:, cu] = v
            mixed = _dot(wm[h], v.astype(BF16)) + bst_ref[:, h:h + 1]
            ab_s[rows, cu] = (jax.nn.gelu(z_s[rows, cu]) * mixed).astype(BF16)
        ca = slice(2 * GM_WIDTH, 2 * GM_WIDTH + CV_WIDTH)
        cg = slice(2 * GM_WIDTH + CV_WIDTH, IN_COLS)
        g_s[HALO + c * CHUNK:HALO + (c + 1) * CHUNK, :] = z_s[rows, ca] * jax.nn.sigmoid(z_s[rows, cg])

    for c in range(n_chunks):
        rows = slice(c * CHUNK, (c + 1) * CHUNK)
        for cb in range(CV_WIDTH // LANES):
            lanes = slice(cb * LANES, (cb + 1) * LANES)
            c_s[rows, lanes] = _conv_block(g_s, cw_ref, c * CHUNK, lanes) + cb_ref[:, lanes]
        b = _ln(c_s[rows, :], clng_ref[...], clnb_ref[...])
        ab_s[rows, GM_WIDTH:] = (b * jax.nn.sigmoid(b)).astype(BF16)

    conv_ref[...] = g_s[tb:tb + HALO, :]
    g_s[0:HALO, :] = g_s[tb:tb + HALO, :]
    y_cur[...] = x_ref[...] + _dot(ab_s[...], wout_ref[...])
    y_prev[...] = y_cur[...]
    for c in cast_out:
        c.wait()


def _prompt_mix_attn(x, gmix, win, glng, glnb, ws, bst, cw, cb, clng, clnb, wout, gxa, wq, kb, vb, wo,
                     w1f, w3f, w2f):
    nb, seq, _ = x.shape
    tb = PROMPT_TILE
    tps = seq // tb
    last = nb * tps - 1
    n_cast = nb * tps
    r13, r2 = w1f.shape[0] // n_cast, w2f.shape[0] // n_cast
    assert w1f.shape == w3f.shape and w1f.shape[0] % n_cast == 0 and w2f.shape[0] % n_cast == 0
    assert r13 % 16 == 0 and r2 % 16 == 0
    hbm = pl.BlockSpec(memory_space=pl.ANY)
    tile = lambda s, lag: jnp.clip(s - lag, 0, last)
    full = lambda shape: pl.BlockSpec(shape, lambda s: (0,) * len(shape))
    rows = lambda lag: pl.BlockSpec((None, tb, D_MODEL), lambda s: (tile(s, lag) // tps, tile(s, lag) % tps, 0))
    kv_spec = pl.BlockSpec((None, N_MEM, D_MODEL), lambda s: (tile(s, 1) // tps, 0, 0))
    seq_out = lambda r, c: pl.BlockSpec((None, r, c), lambda s: (tile(s, 0) // tps, 0, 0))
    return pl.pallas_call(
        functools.partial(_prompt_kernel, tiles_per_seq=tps, n_cast_chunks=n_cast),
        grid=(nb * tps + 1,),
        in_specs=[rows(0), full((1, D_MODEL)), full((D_MODEL, IN_COLS)), full((1, GM_WIDTH)), full((1, GM_WIDTH)),
                  full((GM_HEADS, CHUNK, CHUNK)), full((CHUNK, GM_HEADS)), full((CONV_WIDTH, CV_WIDTH)),
                  full((1, CV_WIDTH)), full((1, CV_WIDTH)), full((1, CV_WIDTH)), full((D_MODEL, D_MODEL)),
                  full((1, D_MODEL)), full((D_MODEL, D_MODEL)), kv_spec, kv_spec, full((D_MODEL, D_MODEL)),
                  hbm, hbm, hbm],
        out_specs=[rows(1), seq_out(HALO, CV_WIDTH), seq_out(CHUNK, GM_WIDTH), hbm, hbm, hbm],
        out_shape=[jax.ShapeDtypeStruct((nb, seq, D_MODEL), F32),
                   jax.ShapeDtypeStruct((nb, HALO, CV_WIDTH), F32),
                   jax.ShapeDtypeStruct((nb, CHUNK, GM_WIDTH), F32),
                   jax.ShapeDtypeStruct(w1f.shape, BF16), jax.ShapeDtypeStruct(w3f.shape, BF16),
                   jax.ShapeDtypeStruct(w2f.shape, BF16)],
        scratch_shapes=[pltpu.VMEM((tb, IN_COLS), F32), pltpu.VMEM((HALO + tb, CV_WIDTH), F32),
                        pltpu.VMEM((tb, CV_WIDTH), F32), pltpu.VMEM((tb, D_MODEL), BF16),
                        pltpu.VMEM((tb, D_MODEL), BF16), pltpu.VMEM((tb, D_MODEL), F32),
                        pltpu.VMEM((tb, D_MODEL), F32),
                        pltpu.VMEM((2, r13, EXPERT_FF), F32), pltpu.VMEM((2, r13, EXPERT_FF), F32),
                        pltpu.VMEM((2, r2, D_MODEL), F32),
                        pltpu.VMEM((r13, EXPERT_FF), BF16), pltpu.VMEM((r13, EXPERT_FF), BF16),
                        pltpu.VMEM((r2, D_MODEL), BF16),
                        pltpu.SemaphoreType.DMA((2, 3)), pltpu.SemaphoreType.DMA((3,))],
        compiler_params=pltpu.CompilerParams(dimension_semantics=("arbitrary",), vmem_limit_bytes=VMEM_LIMIT),
        name="prompt_mix_attn",
    )(x, gmix, win, glng, glnb, ws, bst, cw, cb, clng, clnb, wout, gxa, wq, kb, vb, wo, w1f, w3f, w2f)


def _sample_mix_kernel(x_ref, gmix_ref, win_ref, glng_ref, glnb_ref, ws0_ref, bs0_ref, cache_ref, cw_ref, cb_ref,
                       clng_ref, clnb_ref, wout_ref, gxa_ref, wq_ref,
                       y_ref, conv_ref, v_ref, q_ref, ab_s):
    x = x_ref[...]
    z = _dot(_rms(x, gmix_ref[...]).astype(BF16), win_ref[...])
    for h in range(GM_HEADS):
        cu = slice(h * GM_HEAD_DIM, (h + 1) * GM_HEAD_DIM)
        cv = slice(GM_WIDTH + h * GM_HEAD_DIM, GM_WIDTH + (h + 1) * GM_HEAD_DIM)
        v = _ln(jax.nn.gelu(z[:, cv]), glng_ref[:, cu], glnb_ref[:, cu])
        v_ref[:, cu] = v
        ab_s[:, cu] = (jax.nn.gelu(z[:, cu]) * (v * ws0_ref[:, cu] + bs0_ref[:, cu])).astype(BF16)
    glu = z[:, 2 * GM_WIDTH:2 * GM_WIDTH + CV_WIDTH] * jax.nn.sigmoid(z[:, 2 * GM_WIDTH + CV_WIDTH:])
    conv_ref[0:CONV_WIDTH - 2] = cache_ref[1:CONV_WIDTH - 1]
    conv_ref[CONV_WIDTH - 2] = glu
    conv = glu * cw_ref[CONV_WIDTH - 1:CONV_WIDTH, :] + cb_ref[...]
    for k in range(CONV_WIDTH - 1):
        conv = conv + cache_ref[k] * cw_ref[k:k + 1, :]
    b = _ln(conv, clng_ref[...], clnb_ref[...])
    ab_s[:, GM_WIDTH:] = (b * jax.nn.sigmoid(b)).astype(BF16)
    y = x + _dot(ab_s[...], wout_ref[...])
    y_ref[...] = y
    q_ref[...] = _dot(_rms(y, gxa_ref[...]).astype(BF16), wq_ref[...]) * (XA_HEAD_DIM ** -0.5)


def _sample_mix(x, gmix, win, glng, glnb, ws0, bs0, cache_t, cw, cb, clng, clnb, wout, gxa, wq):
    ns = x.shape[0]
    return pl.pallas_call(
        _sample_mix_kernel,
        out_shape=[jax.ShapeDtypeStruct((ns, D_MODEL), F32), jax.ShapeDtypeStruct(cache_t.shape, F32),
                   jax.ShapeDtypeStruct((ns, GM_WIDTH), F32), jax.ShapeDtypeStruct((ns, D_MODEL), F32)],
        scratch_shapes=[pltpu.VMEM((ns, D_MODEL), BF16)],
        compiler_params=pltpu.CompilerParams(vmem_limit_bytes=VMEM_LIMIT),
        name="sample_mix",
    )(x, gmix, win, glng, glnb, ws0, bs0, cache_t, cw, cb, clng, clnb, wout, gxa, wq)


def _sample_attn_kernel(q_ref, k_ref, v_ref, o_ref):
    ones = jnp.ones((LANES, LANES), BF16)
    rows = N_MEM * SUBLANES
    for i in range(q_ref.shape[0]):
        prod = (k_ref[i] * q_ref[i][None]).reshape(rows, LANES).astype(BF16)
        part = _dot(prod, ones).reshape(N_MEM, SUBLANES, LANES)
        s = part + pltpu.roll(part, XA_HEADS, axis=1)
        e = jnp.exp(s - jnp.max(s, axis=0, keepdims=True))
        p = e / jnp.sum(e, axis=0, keepdims=True)
        o_ref[i] = jnp.sum(p * v_ref[i], axis=0)


def _split_heads(a):
    halves = XA_HEAD_DIM // LANES
    assert halves * XA_HEADS == SUBLANES
    lead = a.shape[:-2]
    a = a.reshape(*lead, XA_HEADS, halves, LANES)
    return jnp.swapaxes(a, -3, -2).reshape(*lead, SUBLANES, LANES)


def _merge_heads(o):
    ns = o.shape[0]
    o = jnp.swapaxes(o.reshape(ns, XA_HEAD_DIM // LANES, XA_HEADS, LANES), 1, 2)
    return o.reshape(ns, XA_HEADS * XA_HEAD_DIM)


def _sample_attn(q, k, v, first, count):
    nb = SAMPLE_ATTN_BLOCK
    assert first % nb == 0 and count % nb == 0
    off = first // nb
    return pl.pallas_call(
        _sample_attn_kernel,
        grid=(count // nb,),
        in_specs=[pl.BlockSpec((nb, SUBLANES, LANES), lambda i: (i + off, 0, 0)),
                  pl.BlockSpec((nb, N_MEM, SUBLANES, LANES), lambda i: (i + off, 0, 0, 0)),
                  pl.BlockSpec((nb, N_MEM, SUBLANES, LANES), lambda i: (i + off, 0, 0, 0))],
        out_specs=pl.BlockSpec((nb, SUBLANES, LANES), lambda i: (i, 0, 0)),
        out_shape=jax.ShapeDtypeStruct((count, SUBLANES, LANES), F32),
        compiler_params=pltpu.CompilerParams(dimension_semantics=("arbitrary",), vmem_limit_bytes=VMEM_LIMIT),
        name="sample_attn",
    )(q, k, v)


def _sample_proj_kernel(y_ref, o_ref, wo_ref, h_ref):
    h_ref[...] = y_ref[...] + _dot(o_ref[...].astype(BF16), wo_ref[...])


def _sample_proj(y, o, wo):
    return pl.pallas_call(
        _sample_proj_kernel,
        out_shape=jax.ShapeDtypeStruct(y.shape, F32),
        compiler_params=pltpu.CompilerParams(vmem_limit_bytes=VMEM_LIMIT),
        name="sample_proj",
    )(y, o, wo)


def _moe_kernel(h_ref, gffn_ref, wrt_ref, rb_ref, w1_ref, w3_ref, w2_ref, gfin_ref,
                o_ref, xt_s, u_s, gidx_s, slot_s, comb_s, act_s, p2_s, y2_s, cnt_s):
    i = pl.program_id(0)
    g = pl.program_id(1)
    rt = h_ref.shape[0]
    sb = act_s.shape[0]

    @pl.when(jnp.logical_and(i == 0, g == 0))
    def _():
        u_s[...] = (lax.broadcasted_iota(jnp.int32, (rt, rt), 0)
                    < lax.broadcasted_iota(jnp.int32, (rt, rt), 1)).astype(F32).astype(BF16)

    @pl.when(g == 0)
    def _route():
        h = h_ref[...]
        o_ref[...] = h
        xt = _rms(h, gffn_ref[...]).astype(BF16)
        xt_s[...] = xt
        lt = _dot_nt(wrt_ref[...], xt) + rb_ref[...]
        gl = [lt[k:k + 1, :] for k in range(N_GROUPS)]
        gmax = jnp.maximum(jnp.maximum(gl[0], gl[1]), jnp.maximum(gl[2], gl[3]))
        gidx = jnp.where(gl[0] == gmax, 0, jnp.where(gl[1] == gmax, 1, jnp.where(gl[2] == gmax, 2, 3)))
        gidx = gidx.astype(jnp.int32)
        sumexp = (jnp.exp(gl[0] - gmax) + jnp.exp(gl[1] - gmax)) + (jnp.exp(gl[2] - gmax) + jnp.exp(gl[3] - gmax))
        p_g = 1.0 / sumexp
        esel = lt[SUBLANES + 3 * EXPERTS_PER_GROUP:SUBLANES + 4 * EXPERTS_PER_GROUP, :]
        for k in (2, 1, 0):
            esel = jnp.where(gidx == k, lt[SUBLANES + k * EXPERTS_PER_GROUP:SUBLANES + (k + 1) * EXPERTS_PER_GROUP, :],
                             esel)
        eidx = lax.broadcasted_iota(jnp.int32, (EXPERTS_PER_GROUP, rt), 0)
        m1 = jnp.max(esel, axis=0, keepdims=True)
        i1 = jnp.min(jnp.where(esel == m1, eidx, EXPERTS_PER_GROUP), axis=0, keepdims=True)
        rest = jnp.where(eidx == i1, -jnp.inf, esel)
        m2 = jnp.max(rest, axis=0, keepdims=True)
        i2 = jnp.min(jnp.where(rest == m2, eidx, EXPERTS_PER_GROUP), axis=0, keepdims=True)
        t2 = jnp.exp(m2 - m1)
        den = 1.0 + t2
        w_top1 = (1.0 / den) * p_g
        w_top2 = (t2 / den) * p_g
        within = jnp.where(eidx == i1, w_top1, 0.0) + jnp.where(eidx == i2, w_top2, 0.0)
        c_hi = within.astype(BF16).astype(F32)
        r1 = within - c_hi
        c_mid = r1.astype(BF16).astype(F32)
        c_lo = (r1 - c_mid).astype(BF16).astype(F32)
        comb_s[0:8, :] = c_hi
        comb_s[8:16, :] = c_mid
        comb_s[16:24, :] = c_lo
        comb_s[24:32, :] = jnp.zeros((8, rt), F32)
        onehot = (eidx == gidx).astype(F32)
        rank = _dot(onehot.astype(BF16), u_s[...])
        slot_s[...] = jnp.sum(onehot * rank, axis=0, keepdims=True).astype(jnp.int32)
        gidx_s[...] = gidx
        for k in range(N_GROUPS):
            cnt_s[k] = jnp.sum(onehot[k:k + 1, :]).astype(jnp.int32)

    n_blk = (cnt_s[g] + sb - 1) // sb

    def sub_block(j, half):
        half_rows = slice(half * sb, (half + 1) * sb)
        rows = lax.broadcasted_iota(jnp.int32, (sb, rt), 0) + j * sb
        hit = jnp.logical_and(rows == slot_s[...], gidx_s[...] == g)
        p = jnp.where(hit, 1.0, 0.0).astype(BF16)
        p2_s[half_rows, :] = p
        xc = _dot(p, xt_s[...]).astype(BF16)
        cexp = _dot_nt(p, comb_s[...].astype(BF16))
        cw = (cexp[:, 0:8] + cexp[:, 8:16]) + cexp[:, 16:24]
        for e in range(EXPERTS_PER_GROUP):
            h1 = _dot(xc, w1_ref[e])
            h3 = _dot(xc, w3_ref[e])
            a = (h1 * jax.nn.sigmoid(h1)) * h3 * cw[:, e:e + 1]
            act_s[:, e * EXPERT_FF:(e + 1) * EXPERT_FF] = a.astype(BF16)
        y2_s[half_rows, :] = _dot(act_s[...], w2_ref[...]).astype(BF16)

    def body(jj, carry):
        sub_block(2 * jj, 0)

        @pl.when(2 * jj + 1 < n_blk)
        def _():
            sub_block(2 * jj + 1, 1)

        @pl.when(2 * jj + 1 >= n_blk)
        def _():
            p2_s[sb:, :] = jnp.zeros((sb, rt), BF16)
            y2_s[sb:, :] = jnp.zeros((sb, D_MODEL), BF16)

        o_ref[...] += _dot_tn(p2_s[...], y2_s[...])
        return carry

    lax.fori_loop(0, (n_blk + 1) // 2, body, 0)

    @pl.when(g == N_GROUPS - 1)
    def _():
        o_ref[...] = _rms(o_ref[...], gfin_ref[...])


def _moe(h, gffn, wrt, rbias, w1, w3, w2, gfin, *, tile):
    tokens = h.shape[0]
    sb = min(MOE_SUB, tile)
    ff = EXPERTS_PER_GROUP * EXPERT_FF
    full = lambda shape: pl.BlockSpec(shape, lambda i, g: (0,) * len(shape))
    row_spec = pl.BlockSpec((tile, D_MODEL), lambda i, g: (i, 0))
    return pl.pallas_call(
        _moe_kernel,
        grid=(tokens // tile, N_GROUPS),
        in_specs=[row_spec, full((1, D_MODEL)), full((ROUTER_ROWS, D_MODEL)), full((ROUTER_ROWS, 1)),
                  pl.BlockSpec((None, EXPERTS_PER_GROUP, D_MODEL, EXPERT_FF), lambda i, g: (g, 0, 0, 0)),
                  pl.BlockSpec((None, EXPERTS_PER_GROUP, D_MODEL, EXPERT_FF), lambda i, g: (g, 0, 0, 0)),
                  pl.BlockSpec((None, ff, D_MODEL), lambda i, g: (g, 0, 0)),
                  full((1, D_MODEL))],
        out_specs=row_spec,
        out_shape=jax.ShapeDtypeStruct((tokens, D_MODEL), F32),
        scratch_shapes=[pltpu.VMEM((tile, D_MODEL), BF16), pltpu.VMEM((tile, tile), BF16),
                        pltpu.VMEM((1, tile), jnp.int32), pltpu.VMEM((1, tile), jnp.int32),
                        pltpu.VMEM((4 * SUBLANES, tile), F32), pltpu.VMEM((sb, ff), BF16),
                        pltpu.VMEM((2 * sb, tile), BF16), pltpu.VMEM((2 * sb, D_MODEL), BF16),
                        pltpu.SMEM((N_GROUPS,), jnp.int32)],
        compiler_params=pltpu.CompilerParams(dimension_semantics=("arbitrary", "arbitrary"),
                                             vmem_limit_bytes=VMEM_LIMIT),
        name="moe",
    )(h, gffn, wrt, rbias, w1, w3, w2, gfin)


HALF = D_MODEL // 2
SUB_PER_ROW = HALF // LANES
N_EXPERTS = N_GROUPS * EXPERTS_PER_GROUP


def _pack_bf16_pairs(x):
    bits = pltpu.bitcast(x.astype(BF16).astype(F32), jnp.uint32)
    return (bits[:, HALF:] & jnp.uint32(0xFFFF0000)) | (bits[:, :HALF] >> 16)


def _unpack_bf16_pairs(w):
    lo = pltpu.bitcast(w << 16, F32)
    hi = pltpu.bitcast(w & jnp.uint32(0xFFFF0000), F32)
    return lo, hi


def _route_kernel(h_ref, gffn_ref, wrt_ref, rb_ref, xp_ref, e_ref, w_ref, r_ref, cnt_ref, u_s, run_s):
    i = pl.program_id(0)
    rt = h_ref.shape[0]

    @pl.when(i == 0)
    def _():
        u_s[...] = (lax.broadcasted_iota(jnp.int32, (rt, rt), 0)
                    < lax.broadcasted_iota(jnp.int32, (rt, rt), 1)).astype(F32).astype(BF16)
        run_s[...] = jnp.zeros(run_s.shape, F32)

    xt = _rms(h_ref[...], gffn_ref[...])
    _store_planes(xp_ref, _pack_bf16_pairs(xt))
    lt = _dot_nt(wrt_ref[...], xt.astype(BF16)) + rb_ref[...]
    gl = [lt[k:k + 1, :] for k in range(N_GROUPS)]
    gmax = jnp.maximum(jnp.maximum(gl[0], gl[1]), jnp.maximum(gl[2], gl[3]))
    gidx = jnp.where(gl[0] == gmax, 0, jnp.where(gl[1] == gmax, 1, jnp.where(gl[2] == gmax, 2, 3)))
    gidx = gidx.astype(jnp.int32)
    sumexp = (jnp.exp(gl[0] - gmax) + jnp.exp(gl[1] - gmax)) + (jnp.exp(gl[2] - gmax) + jnp.exp(gl[3] - gmax))
    p_g = 1.0 / sumexp
    esel = lt[SUBLANES + 3 * EXPERTS_PER_GROUP:SUBLANES + 4 * EXPERTS_PER_GROUP, :]
    for k in (2, 1, 0):
        esel = jnp.where(gidx == k, lt[SUBLANES + k * EXPERTS_PER_GROUP:SUBLANES + (k + 1) * EXPERTS_PER_GROUP, :],
                         esel)
    eidx = lax.broadcasted_iota(jnp.int32, (EXPERTS_PER_GROUP, rt), 0)
    m1 = jnp.max(esel, axis=0, keepdims=True)
    i1 = jnp.min(jnp.where(esel == m1, eidx, EXPERTS_PER_GROUP), axis=0, keepdims=True)
    rest = jnp.where(eidx == i1, -jnp.inf, esel)
    m2 = jnp.max(rest, axis=0, keepdims=True)
    i2 = jnp.min(jnp.where(rest == m2, eidx, EXPERTS_PER_GROUP), axis=0, keepdims=True)
    t2 = jnp.exp(m2 - m1)
    den = 1.0 + t2
    w_ref[...] = jnp.zeros(w_ref.shape, F32)
    w_ref[0:1, :] = (1.0 / den) * p_g
    w_ref[1:2, :] = (t2 / den) * p_g
    e1 = gidx * EXPERTS_PER_GROUP + i1
    e2 = gidx * EXPERTS_PER_GROUP + i2
    e_ref[0:1, :] = e1
    e_ref[1:2, :] = e2
    xid = lax.broadcasted_iota(jnp.int32, (N_EXPERTS, rt), 0)
    oh1 = (xid == e1).astype(F32)
    oh2 = (xid == e2).astype(F32)
    both = oh1 + oh2
    before = _dot(both.astype(BF16), u_s[...]) + run_s[:, 0:1]
    r_ref[0:1, :] = jnp.sum(oh1 * before, axis=0, keepdims=True).astype(jnp.int32)
    r_ref[1:2, :] = jnp.sum(oh2 * before, axis=0, keepdims=True).astype(jnp.int32)
    run_s[...] = run_s[...] + jnp.sum(both, axis=1, keepdims=True)
    cnt_ref[...] = run_s[...].astype(jnp.int32)


def _route(h, gffn, wrt, rbias, *, tile):
    tokens = h.shape[0]
    full = lambda shape: pl.BlockSpec(shape, lambda i: (0,) * len(shape))
    lanes = lambda rows: pl.BlockSpec((rows, tile), lambda i: (0, i))
    return pl.pallas_call(
        _route_kernel,
        grid=(tokens // tile,),
        in_specs=[pl.BlockSpec((tile, D_MODEL), lambda i: (i, 0)), full((1, D_MODEL)),
                  full((ROUTER_ROWS, D_MODEL)), full((ROUTER_ROWS, 1))],
        out_specs=[pl.BlockSpec((SUB_PER_ROW, tile, LANES), lambda i: (0, i, 0)), lanes(2), lanes(SUBLANES), lanes(2),
                   full((N_EXPERTS, LANES))],
        out_shape=[jax.ShapeDtypeStruct((SUB_PER_ROW, tokens, LANES), jnp.uint32),
                   jax.ShapeDtypeStruct((2, tokens), jnp.int32),
                   jax.ShapeDtypeStruct((SUBLANES, tokens), F32), jax.ShapeDtypeStruct((2, tokens), jnp.int32),
                   jax.ShapeDtypeStruct((N_EXPERTS, LANES), jnp.int32)],
        scratch_shapes=[pltpu.VMEM((tile, tile), BF16), pltpu.VMEM((N_EXPERTS, LANES), F32)],
        compiler_params=pltpu.CompilerParams(dimension_semantics=("arbitrary",), vmem_limit_bytes=VMEM_LIMIT),
        name="moe_route",
    )(h, gffn, wrt, rbias)


def _store_planes(ref, words):
    for j in range(SUB_PER_ROW):
        ref[j] = words[:, j * LANES:(j + 1) * LANES]


def _load_planes(ref):
    return jnp.concatenate([ref[j] for j in range(SUB_PER_ROW)], axis=1)


def _sub_row_index(row_of_token, n_rows):
    plane = jnp.arange(SUB_PER_ROW, dtype=jnp.int32)[:, None] * n_rows
    return (row_of_token[None, :] + plane).reshape(1, -1)


def _sc_mesh():
    return plsc.VectorSubcoreMesh(core_axis_name="core", subcore_axis_name="subcore")


def _sc_scatter_two(x_sub, idx_a, idx_b, n_out):
    n_in = x_sub.shape[0]

    @pl.kernel(out_type=jax.ShapeDtypeStruct((n_out, LANES), x_sub.dtype), mesh=_sc_mesh(), scratch_types=[])
    def scatter(x_hbm, a_hbm, b_hbm, o_hbm):
        def body(x_vmem, a_vmem, b_vmem):
            pltpu.sync_copy(x_vmem, o_hbm.at[a_vmem.at[0]])
            pltpu.sync_copy(x_vmem, o_hbm.at[b_vmem.at[0]])

        pltpu.emit_pipeline(
            body, grid=(n_in // SC_WINDOW,),
            in_specs=[pl.BlockSpec((SC_WINDOW, LANES), lambda i: (i, 0)),
                      pl.BlockSpec((1, SC_WINDOW), lambda i: (0, i)),
                      pl.BlockSpec((1, SC_WINDOW), lambda i: (0, i))],
            out_specs=[],
            core_axis_name=("core", "subcore"), dimension_semantics=(pltpu.PARALLEL,),
        )(x_hbm, a_hbm, b_hbm)

    return scatter(x_sub, idx_a, idx_b)


def _sc_gather(table, idx):
    n_out = idx.shape[1]

    @pl.kernel(out_type=jax.ShapeDtypeStruct((n_out, LANES), table.dtype), mesh=_sc_mesh())
    def gather(t_hbm, i_hbm, o_hbm):
        def body(i_vmem, o_vmem):
            pltpu.sync_copy(t_hbm.at[i_vmem.at[0]], o_vmem)

        pltpu.emit_pipeline(
            body, grid=(n_out // SC_WINDOW,),
            in_specs=[pl.BlockSpec((1, SC_WINDOW), lambda i: (0, i))],
            out_specs=[pl.BlockSpec((SC_WINDOW, LANES), lambda i: (i, 0))],
            core_axis_name=("core", "subcore"), dimension_semantics=(pltpu.PARALLEL,),
        )(i_hbm, o_hbm)

    return gather(table, idx)


def _expert_ffn_kernel(blk_e_ref, n_valid_ref, x_ref, *refs):
    del blk_e_ref
    y_ref = refs[-1]
    for i in range(FFN_PER_STEP):
        w1_ref, w3_ref, w2_ref = refs[3 * i:3 * i + 3]
        rows = slice(i * FFN_BLOCK, (i + 1) * FFN_BLOCK)

        @pl.when(pl.program_id(0) * FFN_PER_STEP + i < n_valid_ref[0])
        def _(w1_ref=w1_ref, w3_ref=w3_ref, w2_ref=w2_ref, rows=rows):
            words = jnp.concatenate([x_ref[j, rows, :] for j in range(SUB_PER_ROW)], axis=1)
            lo, hi = _unpack_bf16_pairs(words)
            xc = jnp.concatenate([lo.astype(BF16), hi.astype(BF16)], axis=1)
            h1 = _dot(xc, w1_ref[...])
            h3 = _dot(xc, w3_ref[...])
            act = ((h1 * jax.nn.sigmoid(h1)) * h3).astype(BF16)
            packed = _pack_bf16_pairs(_dot(act, w2_ref[...]))
            for j in range(SUB_PER_ROW):
                y_ref[j, rows, :] = packed[:, j * LANES:(j + 1) * LANES]


def _expert_ffn(xs, blk_e, n_valid, w1, w3, w2):
    rows = xs.shape[1]
    step_rows = FFN_BLOCK * FFN_PER_STEP
    assert rows % step_rows == 0
    live = lambda s, be, nv: jnp.minimum(s, (nv[0] + FFN_PER_STEP - 1) // FFN_PER_STEP - 1)
    x_spec = pl.BlockSpec((SUB_PER_ROW, step_rows, LANES), lambda s, be, nv: (0, live(s, be, nv), 0))
    w_specs = []
    for i in range(FFN_PER_STEP):
        expert = lambda s, be, nv, i=i: (be[s * FFN_PER_STEP + i], 0, 0)
        w_specs += [pl.BlockSpec((None, D_MODEL, EXPERT_FF), expert), pl.BlockSpec((None, D_MODEL, EXPERT_FF), expert),
                    pl.BlockSpec((None, EXPERT_FF, D_MODEL), expert)]
    grid_spec = pltpu.PrefetchScalarGridSpec(
        num_scalar_prefetch=2, grid=(rows // step_rows,), in_specs=[x_spec] + w_specs, out_specs=x_spec)
    return pl.pallas_call(
        _expert_ffn_kernel, grid_spec=grid_spec,
        out_shape=jax.ShapeDtypeStruct((SUB_PER_ROW, rows, LANES), jnp.uint32),
        compiler_params=pltpu.CompilerParams(dimension_semantics=("arbitrary",), vmem_limit_bytes=VMEM_LIMIT),
        name="moe_ffn",
    )(blk_e, n_valid, xs, *([w1, w3, w2] * FFN_PER_STEP))


def _combine_kernel(h_ref, ya_ref, yb_ref, wt_ref, gfin_ref, o_ref):
    a_lo, a_hi = _unpack_bf16_pairs(_load_planes(ya_ref))
    b_lo, b_hi = _unpack_bf16_pairs(_load_planes(yb_ref))
    wt = jnp.transpose(wt_ref[...])
    wa = wt[:, 0:1]
    wb = wt[:, 1:2]
    o_ref[:, :HALF] = h_ref[:, :HALF] + (wa * a_lo + wb * b_lo)
    o_ref[:, HALF:] = h_ref[:, HALF:] + (wa * a_hi + wb * b_hi)
    o_ref[...] = _rms(o_ref[...], gfin_ref[...])


def _combine(h, ya, yb, wt, gfin, *, tile):
    tokens = h.shape[0]
    return pl.pallas_call(
        _combine_kernel,
        grid=(tokens // tile,),
        in_specs=[pl.BlockSpec((tile, D_MODEL), lambda i: (i, 0)),
                  pl.BlockSpec((SUB_PER_ROW, tile, LANES), lambda i: (0, i, 0)),
                  pl.BlockSpec((SUB_PER_ROW, tile, LANES), lambda i: (0, i, 0)), pl.BlockSpec((SUBLANES, tile), lambda i: (0, i)),
                  pl.BlockSpec((1, D_MODEL), lambda i: (0, 0))],
        out_specs=pl.BlockSpec((tile, D_MODEL), lambda i: (i, 0)),
        out_shape=jax.ShapeDtypeStruct((tokens, D_MODEL), F32),
        compiler_params=pltpu.CompilerParams(dimension_semantics=("arbitrary",), vmem_limit_bytes=VMEM_LIMIT),
        name="moe_combine",
    )(h, ya, yb, wt, gfin)


def _moe_dispatch(h, gffn, wrt, rbias):
    tokens = h.shape[0]
    xp, e12, w12, r12, cnt = _route(h, gffn, wrt, rbias, tile=MOE_TILE)
    count = cnt[:, 0]
    padded = (count + FFN_BLOCK - 1) // FFN_BLOCK * FFN_BLOCK
    end = jnp.cumsum(padded)
    start = end - padded
    n_rows = 2 * tokens + N_EXPERTS * FFN_BLOCK
    n_blocks = n_rows // FFN_BLOCK
    n_valid = (end[-1:] // FFN_BLOCK).astype(jnp.int32)
    first_row = jnp.minimum(jnp.arange(n_blocks, dtype=jnp.int32), n_valid - 1) * FFN_BLOCK
    blk_e = jnp.sum((end[None, :] <= first_row[:, None]).astype(jnp.int32), axis=1)
    experts = jnp.arange(N_EXPERTS, dtype=jnp.int32)
    start_of = jnp.sum(jnp.where(e12[:, :, None] == experts, start.astype(jnp.int32), 0), axis=-1)
    rows_ab = start_of + r12
    idx_a, idx_b = _sub_row_index(rows_ab[0], n_rows), _sub_row_index(rows_ab[1], n_rows)
    xs = _sc_scatter_two(xp.reshape(-1, LANES), idx_a, idx_b, n_rows * SUB_PER_ROW)
    return xs.reshape(SUB_PER_ROW, n_rows, LANES), (blk_e, n_valid, idx_a, idx_b, w12)


def _moe_finish(h, xs, plan, w1, w3, w2, gfin):
    blk_e, n_valid, idx_a, idx_b, w12 = plan
    ys = _expert_ffn(xs, blk_e, n_valid, w1, w3, w2).reshape(-1, LANES)
    ya = _sc_gather(ys, idx_a).reshape(SUB_PER_ROW, -1, LANES)
    yb = _sc_gather(ys, idx_b).reshape(SUB_PER_ROW, -1, LANES)
    return _combine(h, ya, yb, w12, gfin, tile=MOE_TILE)


def kernel(x_prompt, x_sample, mem_prompt, cache_conv, cache_mem_k, cache_mem_v, norm_mix_g, w_in, gm_ln_g, gm_ln_b, gm_ws, gm_bs, conv_w, conv_b, cv_ln_g, cv_ln_b, w_out, norm_mem_g, norm_xa_g, xa_wq, xa_wk, xa_wv, xa_wo, norm_ffn_g, router_g, router_g_b, router_e, router_e_b, exp_w1, exp_w3, exp_w2, final_norm_g):
    depth = w_in.shape[0]
    assert depth == 1, "single-layer trunk"
    nb, seq, _ = x_prompt.shape
    ns = x_sample.shape[0]
    row = lambda a: a.reshape(1, -1)

    l = 0
    gmix, gxa, gffn, gmem = row(norm_mix_g[l]), row(norm_xa_g[l]), row(norm_ffn_g[l]), row(norm_mem_g[l])
    gfin = row(final_norm_g)
    win, wout = w_in[l].astype(BF16), w_out[l].astype(BF16)
    wq, wo = xa_wq[l].astype(BF16), xa_wo[l].astype(BF16)
    wk, wv = xa_wk[l], xa_wv[l]
    glng, glnb = row(gm_ln_g[l]), row(gm_ln_b[l])
    cw, cb, clng, clnb = conv_w[l], row(conv_b[l]), row(cv_ln_g[l]), row(cv_ln_b[l])
    ws, bst = gm_ws[l], gm_bs[l].T
    ws0 = jnp.repeat(gm_ws[l][:, 0, 0], GM_HEAD_DIM).reshape(1, GM_WIDTH)
    bs0 = jnp.repeat(gm_bs[l][:, 0], GM_HEAD_DIM).reshape(1, GM_WIDTH)

    n_exp = N_GROUPS * EXPERTS_PER_GROUP
    pad_g = SUBLANES - N_GROUPS
    pad_t = ROUTER_ROWS - SUBLANES - n_exp
    wrt = jnp.concatenate([router_g[l].T, jnp.zeros((pad_g, D_MODEL), F32),
                           router_e[l].reshape(D_MODEL, n_exp).T, jnp.zeros((pad_t, D_MODEL), F32)], axis=0).astype(BF16)
    rbias = jnp.concatenate([router_g_b[l], jnp.zeros((pad_g,), F32), router_e_b[l].reshape(n_exp),
                             jnp.zeros((pad_t,), F32)]).reshape(ROUTER_ROWS, 1)

    mk, mv, kb, vb = _memkv(mem_prompt.reshape(nb * N_MEM, D_MODEL), gmem, wk, wv)
    kb, vb = kb.reshape(nb, N_MEM, D_MODEL), vb.reshape(nb, N_MEM, D_MODEL)
    hp, conv_tail, gmv_p, w1, w3, w2 = _prompt_mix_attn(
        x_prompt, gmix, win, glng, glnb, ws, bst, cw, cb, clng, clnb, wout, gxa, wq, kb, vb, wo,
        exp_w1[l].reshape(-1, EXPERT_FF), exp_w3[l].reshape(-1, EXPERT_FF), exp_w2[l].reshape(-1, D_MODEL))
    h2d = hp.reshape(nb * seq, D_MODEL)
    xs, plan = _moe_dispatch(h2d, gffn, wrt, rbias)
    w1e, w3e, w2e = (w1.reshape(N_EXPERTS, D_MODEL, EXPERT_FF), w3.reshape(N_EXPERTS, D_MODEL, EXPERT_FF),
                     w2.reshape(N_EXPERTS, EXPERT_FF, D_MODEL))
    w1 = w1.reshape(N_GROUPS, EXPERTS_PER_GROUP, D_MODEL, EXPERT_FF)
    w3 = w3.reshape(N_GROUPS, EXPERTS_PER_GROUP, D_MODEL, EXPERT_FF)
    w2 = w2.reshape(N_GROUPS, EXPERTS_PER_GROUP * EXPERT_FF, D_MODEL)

    cache_t = jnp.transpose(cache_conv[l], (1, 0, 2))
    ys, conv_t, gmv_s, q_s = _sample_mix(x_sample.reshape(ns, D_MODEL), gmix, win, glng, glnb, ws0, bs0,
                                        cache_t, cw, cb, clng, clnb, wout, gxa, wq)
    qh, kh, vh = (_split_heads(q_s.reshape(ns, XA_HEADS, XA_HEAD_DIM)), _split_heads(cache_mem_k[l]),
                  _split_heads(cache_mem_v[l]))
    half = ns // 2
    o_first = _sample_attn(qh, kh, vh, 0, half)
    xs, o_first = lax.optimization_barrier((xs, o_first))
    y_prompt = _moe_finish(h2d, xs, plan, w1e, w3e, w2e, gfin)
    o_second = _sample_attn(qh, kh, vh, half, ns - half)
    o_s = _merge_heads(jnp.concatenate([o_first, o_second], axis=0))
    hs = _sample_proj(ys, o_s, wo)
    y_sample = _moe(hs, gffn, wrt, rbias, w1, w3, w2, gfin, tile=ns)

    conv_prompt = conv_tail[:, HALO - (CONV_WIDTH - 1):, :][None]
    conv_sample = jnp.transpose(conv_t, (1, 0, 2))[None]
    return (y_prompt.reshape(nb, seq, D_MODEL), y_sample.reshape(ns, 1, D_MODEL), conv_prompt, conv_sample,
            gmv_p[None], gmv_s.reshape(1, ns, 1, GM_WIDTH),
            mk.reshape(1, nb, N_MEM, XA_HEADS, XA_HEAD_DIM), mv.reshape(1, nb, N_MEM, XA_HEADS, XA_HEAD_DIM))
```

```python
import functools

import jax
import jax.numpy as jnp
from jax import lax
from jax.experimental import pallas as pl
from jax.experimental.pallas import tpu as pltpu
from jax.experimental.pallas import tpu_sc as plsc

F32 = jnp.float32
BF16 = jnp.bfloat16

D_MODEL = 1024
GM_WIDTH = 512
CV_WIDTH = 512
GM_HEADS = 4
GM_HEAD_DIM = 128
CHUNK = 128
CONV_WIDTH = 31
IN_COLS = 2 * GM_WIDTH + 2 * CV_WIDTH
N_MEM = 256
XA_HEADS = 4
XA_HEAD_DIM = 256
N_GROUPS = 4
EXPERTS_PER_GROUP = 8
EXPERT_FF = 256
EPS = 1e-6

LANES = 128
SUBLANES = 8
HALO = 32
PROMPT_TILE = 512
MOE_TILE = 1024
MOE_SUB = 128
FFN_BLOCK = 512
FFN_PER_STEP = 4
SC_WINDOW = 128
SAMPLE_ATTN_BLOCK = 8
ROUTER_ROWS = 128
VMEM_LIMIT = 56 * 1024 * 1024


def _rms(x, g):
    return x * lax.rsqrt(jnp.mean(x * x, axis=-1, keepdims=True) + EPS) * g


def _ln(x, g, b):
    mu = jnp.mean(x, axis=-1, keepdims=True)
    xc = x - mu
    var = jnp.mean(xc * xc, axis=-1, keepdims=True)
    return xc * lax.rsqrt(var + EPS) * g + b


def _dot(a, b):
    return jnp.dot(a, b, preferred_element_type=F32)


def _dot_nt(a, b):
    return lax.dot_general(a, b, (((1,), (1,)), ((), ())), preferred_element_type=F32)


def _dot_tn(a, b):
    return lax.dot_general(a, b, (((0,), (0,)), ((), ())), preferred_element_type=F32)


def _memkv_kernel(mem_ref, g_ref, wk_ref, wv_ref, k_ref, v_ref, kb_ref, vb_ref):
    mn = _rms(mem_ref[...], g_ref[...]).astype(BF16)
    k = _dot(mn, wk_ref[...].astype(BF16))
    v = _dot(mn, wv_ref[...].astype(BF16))
    for h in range(XA_HEADS):
        cols = slice(h * XA_HEAD_DIM, (h + 1) * XA_HEAD_DIM)
        k_ref[:, h, :] = k[:, cols]
        v_ref[:, h, :] = v[:, cols]
    kb_ref[...] = k.astype(BF16)
    vb_ref[...] = v.astype(BF16)


def _memkv(mem2d, g, wk, wv):
    rows = mem2d.shape[0]
    tile = 512
    row_spec = pl.BlockSpec((tile, D_MODEL), lambda i: (i, 0))
    head_spec = pl.BlockSpec((tile, XA_HEADS, XA_HEAD_DIM), lambda i: (i, 0, 0))
    full = lambda shape: pl.BlockSpec(shape, lambda i: (0,) * len(shape))
    return pl.pallas_call(
        _memkv_kernel,
        grid=(rows // tile,),
        in_specs=[row_spec, full((1, D_MODEL)), full((D_MODEL, D_MODEL)), full((D_MODEL, D_MODEL))],
        out_specs=[head_spec, head_spec, row_spec, row_spec],
        out_shape=[jax.ShapeDtypeStruct((rows, XA_HEADS, XA_HEAD_DIM), F32),
                   jax.ShapeDtypeStruct((rows, XA_HEADS, XA_HEAD_DIM), F32),
                   jax.ShapeDtypeStruct((rows, D_MODEL), BF16), jax.ShapeDtypeStruct((rows, D_MODEL), BF16)],
        compiler_params=pltpu.CompilerParams(dimension_semantics=("arbitrary",), vmem_limit_bytes=VMEM_LIMIT),
        name="memkv",
    )(mem2d, g, wk, wv)


def _conv_block(g_s, cw_ref, start, lanes):
    assert start % SUBLANES == 0
    rows = CHUNK + HALO
    win = g_s[start:start + rows, lanes]
    off = HALO - (CONV_WIDTH - 1)
    acc = None
    for b in range(SUBLANES):
        shifted = pltpu.roll(win, rows - (off + b), axis=0) if off + b else win
        for k in range(b, CONV_WIDTH, SUBLANES):
            assert k - b + CHUNK + off + b <= rows
            term = shifted[k - b:k - b + CHUNK] * cw_ref[k:k + 1, lanes]
            acc = term if acc is None else acc + term
    return acc


def _cast_copies(src_refs, dst_refs, in_bufs, out_bufs, sem_in, sem_out, chunk, slot):
    copies_in, copies_out = [], []
    for i, (src, dst, ibuf, obuf) in enumerate(zip(src_refs, dst_refs, in_bufs, out_bufs)):
        rows = obuf.shape[0]
        start = pl.multiple_of(chunk * rows, rows)
        copies_in.append(pltpu.make_async_copy(src.at[pl.ds(start, rows), :], ibuf.at[slot], sem_in.at[slot, i]))
        copies_out.append(pltpu.make_async_copy(obuf, dst.at[pl.ds(start, rows), :], sem_out.at[i]))
    return copies_in, copies_out


def _prompt_kernel(x_ref, gmix_ref, win_ref, glng_ref, glnb_ref, ws_ref, bst_ref, cw_ref, cb_ref,
                   clng_ref, clnb_ref, wout_ref, gxa_ref, wq_ref, kb_ref, vb_ref, wo_ref,
                   w1f_ref, w3f_ref, w2f_ref,
                   h_ref, conv_ref, gmv_ref, w1b_ref, w3b_ref, w2b_ref,
                   z_s, g_s, c_s, ab_s, o_s, y_prev, y_cur, in1, in3, in2, st1, st3, st2, sem_in, sem_out,
                   *, tiles_per_seq, n_cast_chunks):
    s = pl.program_id(0)
    n_steps = pl.num_programs(0)
    tb = x_ref.shape[0]
    n_chunks = tb // CHUNK

    srcs, dsts = (w1f_ref, w3f_ref, w2f_ref), (w1b_ref, w3b_ref, w2b_ref)
    ins, sts = (in1, in3, in2), (st1, st3, st2)
    slot = s % 2
    chunk_of = lambda step: jnp.minimum(step, n_cast_chunks - 1)

    @pl.when(s == 0)
    def _():
        for c in _cast_copies(srcs, dsts, ins, sts, sem_in, sem_out, chunk_of(s), slot)[0]:
            c.start()

    @pl.when(s + 1 < n_steps)
    def _():
        for c in _cast_copies(srcs, dsts, ins, sts, sem_in, sem_out, chunk_of(s + 1), 1 - slot)[0]:
            c.start()

    cast_in, cast_out = _cast_copies(srcs, dsts, ins, sts, sem_in, sem_out, chunk_of(s), slot)
    for c in cast_in:
        c.wait()
    for ibuf, obuf in zip(ins, sts):
        obuf[...] = ibuf[slot].astype(BF16)
    for c in cast_out:
        c.start()

    @pl.when(s == 0)
    def _():
        y_prev[...] = jnp.zeros(y_prev.shape, F32)

    @pl.when(s % tiles_per_seq == 0)
    def _():
        g_s[0:HALO, :] = jnp.zeros((HALO, CV_WIDTH), F32)

    z_s[...] = _dot(_rms(x_ref[...], gmix_ref[...]).astype(BF16), win_ref[...])

    y = y_prev[...]
    qn = _rms(y, gxa_ref[...]).astype(BF16)
    q = (_dot(qn, wq_ref[...]) * (XA_HEAD_DIM ** -0.5)).astype(BF16)
    for h in range(XA_HEADS):
        cols = slice(h * XA_HEAD_DIM, (h + 1) * XA_HEAD_DIM)
        sc = _dot_nt(q[:, cols], kb_ref[:, cols])
        e = jnp.exp(sc - jnp.max(sc, axis=-1, keepdims=True))
        p = (e / jnp.sum(e, axis=-1, keepdims=True)).astype(BF16)
        o_s[:, cols] = _dot(p, vb_ref[:, cols]).astype(BF16)
    h_ref[...] = y + _dot(o_s[...], wo_ref[...])

    tri = (lax.broadcasted_iota(jnp.int32, (CHUNK, CHUNK), 0)
           >= lax.broadcasted_iota(jnp.int32, (CHUNK, CHUNK), 1))
    wm = [jnp.where(tri, ws_ref[h], 0.0).astype(BF16) for h in range(GM_HEADS)]

    for c in range(n_chunks):
        rows = slice(c * CHUNK, (c + 1) * CHUNK)
        for h in range(GM_HEADS):
            cu = slice(h * GM_HEAD_DIM, (h + 1) * GM_HEAD_DIM)
            cv = slice(GM_WIDTH + h * GM_HEAD_DIM, GM_WIDTH + (h + 1) * GM_HEAD_DIM)
            v = _ln(jax.nn.gelu(z_s[rows, cv]), glng_ref[:, cu], glnb_ref[:, cu])
            if c == n_chunks - 1:
                gmv_ref[:, cu] = v
            mixed = _dot(wm[h], v.astype(BF16)) + bst_ref[:, h:h + 1]
            ab_s[rows, cu] = (jax.nn.gelu(z_s[rows, cu]) * mixed).astype(BF16)
        ca = slice(2 * GM_WIDTH, 2 * GM_WIDTH + CV_WIDTH)
        cg = slice(2 * GM_WIDTH + CV_WIDTH, IN_COLS)
        g_s[HALO + c * CHUNK:HALO + (c + 1) * CHUNK, :] = z_s[rows, ca] * jax.nn.sigmoid(z_s[rows, cg])

    for c in range(n_chunks):
        rows = slice(c * CHUNK, (c + 1) * CHUNK)
        for cb in range(CV_WIDTH // LANES):
            lanes = slice(cb * LANES, (cb + 1) * LANES)
            c_s[rows, lanes] = _conv_block(g_s, cw_ref, c * CHUNK, lanes) + cb_ref[:, lanes]
        b = _ln(c_s[rows, :], clng_ref[...], clnb_ref[...])
        ab_s[rows, GM_WIDTH:] = (b * jax.nn.sigmoid(b)).astype(BF16)

    conv_ref[...] = g_s[tb:tb + HALO, :]
    g_s[0:HALO, :] = g_s[tb:tb + HALO, :]
    y_cur[...] = x_ref[...] + _dot(ab_s[...], wout_ref[...])
    y_prev[...] = y_cur[...]
    for c in cast_out:
        c.wait()


def _prompt_mix_attn(x, gmix, win, glng, glnb, ws, bst, cw, cb, clng, clnb, wout, gxa, wq, kb, vb, wo,
                     w1f, w3f, w2f):
    nb, seq, _ = x.shape
    tb = PROMPT_TILE
    tps = seq // tb
    last = nb * tps - 1
    n_cast = nb * tps
    r13, r2 = w1f.shape[0] // n_cast, w2f.shape[0] // n_cast
    assert w1f.shape == w3f.shape and w1f.shape[0] % n_cast == 0 and w2f.shape[0] % n_cast == 0
    assert r13 % 16 == 0 and r2 % 16 == 0
    hbm = pl.BlockSpec(memory_space=pl.ANY)
    tile = lambda s, lag: jnp.clip(s - lag, 0, last)
    full = lambda shape: pl.BlockSpec(shape, lambda s: (0,) * len(shape))
    rows = lambda lag: pl.BlockSpec((None, tb, D_MODEL), lambda s: (tile(s, lag) // tps, tile(s, lag) % tps, 0))
    kv_spec = pl.BlockSpec((None, N_MEM, D_MODEL), lambda s: (tile(s, 1) // tps, 0, 0))
    seq_out = lambda r, c: pl.BlockSpec((None, r, c), lambda s: (tile(s, 0) // tps, 0, 0))
    return pl.pallas_call(
        functools.partial(_prompt_kernel, tiles_per_seq=tps, n_cast_chunks=n_cast),
        grid=(nb * tps + 1,),
        in_specs=[rows(0), full((1, D_MODEL)), full((D_MODEL, IN_COLS)), full((1, GM_WIDTH)), full((1, GM_WIDTH)),
                  full((GM_HEADS, CHUNK, CHUNK)), full((CHUNK, GM_HEADS)), full((CONV_WIDTH, CV_WIDTH)),
                  full((1, CV_WIDTH)), full((1, CV_WIDTH)), full((1, CV_WIDTH)), full((D_MODEL, D_MODEL)),
                  full((1, D_MODEL)), full((D_MODEL, D_MODEL)), kv_spec, kv_spec, full((D_MODEL, D_MODEL)),
                  hbm, hbm, hbm],
        out_specs=[rows(1), seq_out(HALO, CV_WIDTH), seq_out(CHUNK, GM_WIDTH), hbm, hbm, hbm],
        out_shape=[jax.ShapeDtypeStruct((nb, seq, D_MODEL), F32),
                   jax.ShapeDtypeStruct((nb, HALO, CV_WIDTH), F32),
                   jax.ShapeDtypeStruct((nb, CHUNK, GM_WIDTH), F32),
                   jax.ShapeDtypeStruct(w1f.shape, BF16), jax.ShapeDtypeStruct(w3f.shape, BF16),
                   jax.ShapeDtypeStruct(w2f.shape, BF16)],
        scratch_shapes=[pltpu.VMEM((tb, IN_COLS), F32), pltpu.VMEM((HALO + tb, CV_WIDTH), F32),
                        pltpu.VMEM((tb, CV_WIDTH), F32), pltpu.VMEM((tb, D_MODEL), BF16),
                        pltpu.VMEM((tb, D_MODEL), BF16), pltpu.VMEM((tb, D_MODEL), F32),
                        pltpu.VMEM((tb, D_MODEL), F32),
                        pltpu.VMEM((2, r13, EXPERT_FF), F32), pltpu.VMEM((2, r13, EXPERT_FF), F32),
                        pltpu.VMEM((2, r2, D_MODEL), F32),
                        pltpu.VMEM((r13, EXPERT_FF), BF16), pltpu.VMEM((r13, EXPERT_FF), BF16),
                        pltpu.VMEM((r2, D_MODEL), BF16),
                        pltpu.SemaphoreType.DMA((2, 3)), pltpu.SemaphoreType.DMA((3,))],
        compiler_params=pltpu.CompilerParams(dimension_semantics=("arbitrary",), vmem_limit_bytes=VMEM_LIMIT),
        name="prompt_mix_attn",
    )(x, gmix, win, glng, glnb, ws, bst, cw, cb, clng, clnb, wout, gxa, wq, kb, vb, wo, w1f, w3f, w2f)


def _sample_mix_kernel(x_ref, gmix_ref, win_ref, glng_ref, glnb_ref, ws0_ref, bs0_ref, cache_ref, cw_ref, cb_ref,
                       clng_ref, clnb_ref, wout_ref, gxa_ref, wq_ref,
                       y_ref, conv_ref, v_ref, q_ref, ab_s):
    x = x_ref[...]
    z = _dot(_rms(x, gmix_ref[...]).astype(BF16), win_ref[...])
    for h in range(GM_HEADS):
        cu = slice(h * GM_HEAD_DIM, (h + 1) * GM_HEAD_DIM)
        cv = slice(GM_WIDTH + h * GM_HEAD_DIM, GM_WIDTH + (h + 1) * GM_HEAD_DIM)
        v = _ln(jax.nn.gelu(z[:, cv]), glng_ref[:, cu], glnb_ref[:, cu])
        v_ref[:, cu] = v
        ab_s[:, cu] = (jax.nn.gelu(z[:, cu]) * (v * ws0_ref[:, cu] + bs0_ref[:, cu])).astype(BF16)
    glu = z[:, 2 * GM_WIDTH:2 * GM_WIDTH + CV_WIDTH] * jax.nn.sigmoid(z[:, 2 * GM_WIDTH + CV_WIDTH:])
    conv_ref[0:CONV_WIDTH - 2] = cache_ref[1:CONV_WIDTH - 1]
    conv_ref[CONV_WIDTH - 2] = glu
    conv = glu * cw_ref[CONV_WIDTH - 1:CONV_WIDTH, :] + cb_ref[...]
    for k in range(CONV_WIDTH - 1):
        conv = conv + cache_ref[k] * cw_ref[k:k + 1, :]
    b = _ln(conv, clng_ref[...], clnb_ref[...])
    ab_s[:, GM_WIDTH:] = (b * jax.nn.sigmoid(b)).astype(BF16)
    y = x + _dot(ab_s[...], wout_ref[...])
    y_ref[...] = y
    q_ref[...] = _dot(_rms(y, gxa_ref[...]).astype(BF16), wq_ref[...]) * (XA_HEAD_DIM ** -0.5)


def _sample_mix(x, gmix, win, glng, glnb, ws0, bs0, cache_t, cw, cb, clng, clnb, wout, gxa, wq):
    ns = x.shape[0]
    return pl.pallas_call(
        _sample_mix_kernel,
        out_shape=[jax.ShapeDtypeStruct((ns, D_MODEL), F32), jax.ShapeDtypeStruct(cache_t.shape, F32),
                   jax.ShapeDtypeStruct((ns, GM_WIDTH), F32), jax.ShapeDtypeStruct((ns, D_MODEL), F32)],
        scratch_shapes=[pltpu.VMEM((ns, D_MODEL), BF16)],
        compiler_params=pltpu.CompilerParams(vmem_limit_bytes=VMEM_LIMIT),
        name="sample_mix",
    )(x, gmix, win, glng, glnb, ws0, bs0, cache_t, cw, cb, clng, clnb, wout, gxa, wq)


def _sample_attn_kernel(q_ref, k_ref, v_ref, o_ref):
    ones = jnp.ones((LANES, LANES), BF16)
    rows = N_MEM * SUBLANES
    for i in range(q_ref.shape[0]):
        prod = (k_ref[i] * q_ref[i][None]).reshape(rows, LANES).astype(BF16)
        part = _dot(prod, ones).reshape(N_MEM, SUBLANES, LANES)
        s = part + pltpu.roll(part, XA_HEADS, axis=1)
        e = jnp.exp(s - jnp.max(s, axis=0, keepdims=True))
        p = e / jnp.sum(e, axis=0, keepdims=True)
        o_ref[i] = jnp.sum(p * v_ref[i], axis=0)


def _split_heads(a):
    halves = XA_HEAD_DIM // LANES
    assert halves * XA_HEADS == SUBLANES
    lead = a.shape[:-2]
    a = a.reshape(*lead, XA_HEADS, halves, LANES)
    return jnp.swapaxes(a, -3, -2).reshape(*lead, SUBLANES, LANES)


def _merge_heads(o):
    ns = o.shape[0]
    o = jnp.swapaxes(o.reshape(ns, XA_HEAD_DIM // LANES, XA_HEADS, LANES), 1, 2)
    return o.reshape(ns, XA_HEADS * XA_HEAD_DIM)


def _sample_attn(q, k, v, first, count):
    nb = SAMPLE_ATTN_BLOCK
    assert first % nb == 0 and count % nb == 0
    off = first // nb
    return pl.pallas_call(
        _sample_attn_kernel,
        grid=(count // nb,),
        in_specs=[pl.BlockSpec((nb, SUBLANES, LANES), lambda i: (i + off, 0, 0)),
                  pl.BlockSpec((nb, N_MEM, SUBLANES, LANES), lambda i: (i + off, 0, 0, 0)),
                  pl.BlockSpec((nb, N_MEM, SUBLANES, LANES), lambda i: (i + off, 0, 0, 0))],
        out_specs=pl.BlockSpec((nb, SUBLANES, LANES), lambda i: (i, 0, 0)),
        out_shape=jax.ShapeDtypeStruct((count, SUBLANES, LANES), F32),
        compiler_params=pltpu.CompilerParams(dimension_semantics=("arbitrary",), vmem_limit_bytes=VMEM_LIMIT),
        name="sample_attn",
    )(q, k, v)


def _sample_proj_kernel(y_ref, o_ref, wo_ref, h_ref):
    h_ref[...] = y_ref[...] + _dot(o_ref[...].astype(BF16), wo_ref[...])


def _sample_proj(y, o, wo):
    return pl.pallas_call(
        _sample_proj_kernel,
        out_shape=jax.ShapeDtypeStruct(y.shape, F32),
        compiler_params=pltpu.CompilerParams(vmem_limit_bytes=VMEM_LIMIT),
        name="sample_proj",
    )(y, o, wo)


def _moe_kernel(h_ref, gffn_ref, wrt_ref, rb_ref, w1_ref, w3_ref, w2_ref, gfin_ref,
                o_ref, xt_s, u_s, gidx_s, slot_s, comb_s, act_s, p2_s, y2_s, cnt_s):
    i = pl.program_id(0)
    g = pl.program_id(1)
    rt = h_ref.shape[0]
    sb = act_s.shape[0]

    @pl.when(jnp.logical_and(i == 0, g == 0))
    def _():
        u_s[...] = (lax.broadcasted_iota(jnp.int32, (rt, rt), 0)
                    < lax.broadcasted_iota(jnp.int32, (rt, rt), 1)).astype(F32).astype(BF16)

    @pl.when(g == 0)
    def _route():
        h = h_ref[...]
        o_ref[...] = h
        xt = _rms(h, gffn_ref[...]).astype(BF16)
        xt_s[...] = xt
        lt = _dot_nt(wrt_ref[...], xt) + rb_ref[...]
        gl = [lt[k:k + 1, :] for k in range(N_GROUPS)]
        gmax = jnp.maximum(jnp.maximum(gl[0], gl[1]), jnp.maximum(gl[2], gl[3]))
        gidx = jnp.where(gl[0] == gmax, 0, jnp.where(gl[1] == gmax, 1, jnp.where(gl[2] == gmax, 2, 3)))
        gidx = gidx.astype(jnp.int32)
        sumexp = (jnp.exp(gl[0] - gmax) + jnp.exp(gl[1] - gmax)) + (jnp.exp(gl[2] - gmax) + jnp.exp(gl[3] - gmax))
        p_g = 1.0 / sumexp
        esel = lt[SUBLANES + 3 * EXPERTS_PER_GROUP:SUBLANES + 4 * EXPERTS_PER_GROUP, :]
        for k in (2, 1, 0):
            esel = jnp.where(gidx == k, lt[SUBLANES + k * EXPERTS_PER_GROUP:SUBLANES + (k + 1) * EXPERTS_PER_GROUP, :],
                             esel)
        eidx = lax.broadcasted_iota(jnp.int32, (EXPERTS_PER_GROUP, rt), 0)
        m1 = jnp.max(esel, axis=0, keepdims=True)
        i1 = jnp.min(jnp.where(esel == m1, eidx, EXPERTS_PER_GROUP), axis=0, keepdims=True)
        rest = jnp.where(eidx == i1, -jnp.inf, esel)
        m2 = jnp.max(rest, axis=0, keepdims=True)
        i2 = jnp.min(jnp.where(rest == m2, eidx, EXPERTS_PER_GROUP), axis=0, keepdims=True)
        t2 = jnp.exp(m2 - m1)
        den = 1.0 + t2
        w_top1 = (1.0 / den) * p_g
        w_top2 = (t2 / den) * p_g
        within = jnp.where(eidx == i1, w_top1, 0.0) + jnp.where(eidx == i2, w_top2, 0.0)
        c_hi = within.astype(BF16).astype(F32)
        r1 = within - c_hi
        c_mid = r1.astype(BF16).astype(F32)
        c_lo = (r1 - c_mid).astype(BF16).astype(F32)
        comb_s[0:8, :] = c_hi
        comb_s[8:16, :] = c_mid
        comb_s[16:24, :] = c_lo
        comb_s[24:32, :] = jnp.zeros((8, rt), F32)
        onehot = (eidx == gidx).astype(F32)
        rank = _dot(onehot.astype(BF16), u_s[...])
        slot_s[...] = jnp.sum(onehot * rank, axis=0, keepdims=True).astype(jnp.int32)
        gidx_s[...] = gidx
        for k in range(N_GROUPS):
            cnt_s[k] = jnp.sum(onehot[k:k + 1, :]).astype(jnp.int32)

    n_blk = (cnt_s[g] + sb - 1) // sb

    def sub_block(j, half):
        half_rows = slice(half * sb, (half + 1) * sb)
        rows = lax.broadcasted_iota(jnp.int32, (sb, rt), 0) + j * sb
        hit = jnp.logical_and(rows == slot_s[...], gidx_s[...] == g)
        p = jnp.where(hit, 1.0, 0.0).astype(BF16)
        p2_s[half_rows, :] = p
        xc = _dot(p, xt_s[...]).astype(BF16)
        cexp = _dot_nt(p, comb_s[...].astype(BF16))
        cw = (cexp[:, 0:8] + cexp[:, 8:16]) + cexp[:, 16:24]
        for e in range(EXPERTS_PER_GROUP):
            h1 = _dot(xc, w1_ref[e])
            h3 = _dot(xc, w3_ref[e])
            a = (h1 * jax.nn.sigmoid(h1)) * h3 * cw[:, e:e + 1]
            act_s[:, e * EXPERT_FF:(e + 1) * EXPERT_FF] = a.astype(BF16)
        y2_s[half_rows, :] = _dot(act_s[...], w2_ref[...]).astype(BF16)

    def body(jj, carry):
        sub_block(2 * jj, 0)

        @pl.when(2 * jj + 1 < n_blk)
        def _():
            sub_block(2 * jj + 1, 1)

        @pl.when(2 * jj + 1 >= n_blk)
        def _():
            p2_s[sb:, :] = jnp.zeros((sb, rt), BF16)
            y2_s[sb:, :] = jnp.zeros((sb, D_MODEL), BF16)

        o_ref[...] += _dot_tn(p2_s[...], y2_s[...])
        return carry

    lax.fori_loop(0, (n_blk + 1) // 2, body, 0)

    @pl.when(g == N_GROUPS - 1)
    def _():
        o_ref[...] = _rms(o_ref[...], gfin_ref[...])


def _moe(h, gffn, wrt, rbias, w1, w3, w2, gfin, *, tile):
    tokens = h.shape[0]
    sb = min(MOE_SUB, tile)
    ff = EXPERTS_PER_GROUP * EXPERT_FF
    full = lambda shape: pl.BlockSpec(shape, lambda i, g: (0,) * len(shape))
    row_spec = pl.BlockSpec((tile, D_MODEL), lambda i, g: (i, 0))
    return pl.pallas_call(
        _moe_kernel,
        grid=(tokens // tile, N_GROUPS),
        in_specs=[row_spec, full((1, D_MODEL)), full((ROUTER_ROWS, D_MODEL)), full((ROUTER_ROWS, 1)),
                  pl.BlockSpec((None, EXPERTS_PER_GROUP, D_MODEL, EXPERT_FF), lambda i, g: (g, 0, 0, 0)),
                  pl.BlockSpec((None, EXPERTS_PER_GROUP, D_MODEL, EXPERT_FF), lambda i, g: (g, 0, 0, 0)),
                  pl.BlockSpec((None, ff, D_MODEL), lambda i, g: (g, 0, 0)),
                  full((1, D_MODEL))],
        out_specs=row_spec,
        out_shape=jax.ShapeDtypeStruct((tokens, D_MODEL), F32),
        scratch_shapes=[pltpu.VMEM((tile, D_MODEL), BF16), pltpu.VMEM((tile, tile), BF16),
                        pltpu.VMEM((1, tile), jnp.int32), pltpu.VMEM((1, tile), jnp.int32),
                        pltpu.VMEM((4 * SUBLANES, tile), F32), pltpu.VMEM((sb, ff), BF16),
                        pltpu.VMEM((2 * sb, tile), BF16), pltpu.VMEM((2 * sb, D_MODEL), BF16),
                        pltpu.SMEM((N_GROUPS,), jnp.int32)],
        compiler_params=pltpu.CompilerParams(dimension_semantics=("arbitrary", "arbitrary"),
                                             vmem_limit_bytes=VMEM_LIMIT),
        name="moe",
    )(h, gffn, wrt, rbias, w1, w3, w2, gfin)


HALF = D_MODEL // 2
SUB_PER_ROW = HALF // LANES
N_EXPERTS = N_GROUPS * EXPERTS_PER_GROUP


def _pack_bf16_pairs(x):
    bits = pltpu.bitcast(x.astype(BF16).astype(F32), jnp.uint32)
    return (bits[:, HALF:] & jnp.uint32(0xFFFF0000)) | (bits[:, :HALF] >> 16)


def _unpack_bf16_pairs(w):
    lo = pltpu.bitcast(w << 16, F32)
    hi = pltpu.bitcast(w & jnp.uint32(0xFFFF0000), F32)
    return lo, hi


def _route_kernel(h_ref, gffn_ref, wrt_ref, rb_ref, xp_ref, e_ref, w_ref, r_ref, cnt_ref, u_s, run_s):
    i = pl.program_id(0)
    rt = h_ref.shape[0]

    @pl.when(i == 0)
    def _():
        u_s[...] = (lax.broadcasted_iota(jnp.int32, (rt, rt), 0)
                    < lax.broadcasted_iota(jnp.int32, (rt, rt), 1)).astype(F32).astype(BF16)
        run_s[...] = jnp.zeros(run_s.shape, F32)

    xt = _rms(h_ref[...], gffn_ref[...])
    _store_planes(xp_ref, _pack_bf16_pairs(xt))
    lt = _dot_nt(wrt_ref[...], xt.astype(BF16)) + rb_ref[...]
    gl = [lt[k:k + 1, :] for k in range(N_GROUPS)]
    gmax = jnp.maximum(jnp.maximum(gl[0], gl[1]), jnp.maximum(gl[2], gl[3]))
    gidx = jnp.where(gl[0] == gmax, 0, jnp.where(gl[1] == gmax, 1, jnp.where(gl[2] == gmax, 2, 3)))
    gidx = gidx.astype(jnp.int32)
    sumexp = (jnp.exp(gl[0] - gmax) + jnp.exp(gl[1] - gmax)) + (jnp.exp(gl[2] - gmax) + jnp.exp(gl[3] - gmax))
    p_g = 1.0 / sumexp
    esel = lt[SUBLANES + 3 * EXPERTS_PER_GROUP:SUBLANES + 4 * EXPERTS_PER_GROUP, :]
    for k in (2, 1, 0):
        esel = jnp.where(gidx == k, lt[SUBLANES + k * EXPERTS_PER_GROUP:SUBLANES + (k + 1) * EXPERTS_PER_GROUP, :],
                         esel)
    eidx = lax.broadcasted_iota(jnp.int32, (EXPERTS_PER_GROUP, rt), 0)
    m1 = jnp.max(esel, axis=0, keepdims=True)
    i1 = jnp.min(jnp.where(esel == m1, eidx, EXPERTS_PER_GROUP), axis=0, keepdims=True)
    rest = jnp.where(eidx == i1, -jnp.inf, esel)
    m2 = jnp.max(rest, axis=0, keepdims=True)
    i2 = jnp.min(jnp.where(rest == m2, eidx, EXPERTS_PER_GROUP), axis=0, keepdims=True)
    t2 = jnp.exp(m2 - m1)
    den = 1.0 + t2
    w_ref[...] = jnp.zeros(w_ref.shape, F32)
    w_ref[0:1, :] = (1.0 / den) * p_g
    w_ref[1:2, :] = (t2 / den) * p_g
    e1 = gidx * EXPERTS_PER_GROUP + i1
    e2 = gidx * EXPERTS_PER_GROUP + i2
    e_ref[0:1, :] = e1
    e_ref[1:2, :] = e2
    xid = lax.broadcasted_iota(jnp.int32, (N_EXPERTS, rt), 0)
    oh1 = (xid == e1).astype(F32)
    oh2 = (xid == e2).astype(F32)
    both = oh1 + oh2
    before = _dot(both.astype(BF16), u_s[...]) + run_s[:, 0:1]
    r_ref[0:1, :] = jnp.sum(oh1 * before, axis=0, keepdims=True).astype(jnp.int32)
    r_ref[1:2, :] = jnp.sum(oh2 * before, axis=0, keepdims=True).astype(jnp.int32)
    run_s[...] = run_s[...] + jnp.sum(both, axis=1, keepdims=True)
    cnt_ref[...] = run_s[...].astype(jnp.int32)


def _route(h, gffn, wrt, rbias, *, tile):
    tokens = h.shape[0]
    full = lambda shape: pl.BlockSpec(shape, lambda i: (0,) * len(shape))
    lanes = lambda rows: pl.BlockSpec((rows, tile), lambda i: (0, i))
    return pl.pallas_call(
        _route_kernel,
        grid=(tokens // tile,),
        in_specs=[pl.BlockSpec((tile, D_MODEL), lambda i: (i, 0)), full((1, D_MODEL)),
                  full((ROUTER_ROWS, D_MODEL)), full((ROUTER_ROWS, 1))],
        out_specs=[pl.BlockSpec((SUB_PER_ROW, tile, LANES), lambda i: (0, i, 0)), lanes(2), lanes(SUBLANES), lanes(2),
                   full((N_EXPERTS, LANES))],
        out_shape=[jax.ShapeDtypeStruct((SUB_PER_ROW, tokens, LANES), jnp.uint32),
                   jax.ShapeDtypeStruct((2, tokens), jnp.int32),
                   jax.ShapeDtypeStruct((SUBLANES, tokens), F32), jax.ShapeDtypeStruct((2, tokens), jnp.int32),
                   jax.ShapeDtypeStruct((N_EXPERTS, LANES), jnp.int32)],
        scratch_shapes=[pltpu.VMEM((tile, tile), BF16), pltpu.VMEM((N_EXPERTS, LANES), F32)],
        compiler_params=pltpu.CompilerParams(dimension_semantics=("arbitrary",), vmem_limit_bytes=VMEM_LIMIT),
        name="moe_route",
    )(h, gffn, wrt, rbias)


def _store_planes(ref, words):
    for j in range(SUB_PER_ROW):
        ref[j] = words[:, j * LANES:(j + 1) * LANES]


def _load_planes(ref):
    return jnp.concatenate([ref[j] for j in range(SUB_PER_ROW)], axis=1)


def _sub_row_index(row_of_token, n_rows):
    plane = jnp.arange(SUB_PER_ROW, dtype=jnp.int32)[:, None] * n_rows
    return (row_of_token[None, :] + plane).reshape(1, -1)


def _sc_mesh():
    return plsc.VectorSubcoreMesh(core_axis_name="core", subcore_axis_name="subcore")


def _sc_scatter_two(x_sub, idx_a, idx_b, n_out):
    n_in = x_sub.shape[0]

    @pl.kernel(out_type=jax.ShapeDtypeStruct((n_out, LANES), x_sub.dtype), mesh=_sc_mesh(), scratch_types=[])
    def scatter(x_hbm, a_hbm, b_hbm, o_hbm):
        def body(x_vmem, a_vmem, b_vmem):
            pltpu.sync_copy(x_vmem, o_hbm.at[a_vmem.at[0]])
            pltpu.sync_copy(x_vmem, o_hbm.at[b_vmem.at[0]])

        pltpu.emit_pipeline(
            body, grid=(n_in // SC_WINDOW,),
            in_specs=[pl.BlockSpec((SC_WINDOW, LANES), lambda i: (i, 0)),
                      pl.BlockSpec((1, SC_WINDOW), lambda i: (0, i)),
                      pl.BlockSpec((1, SC_WINDOW), lambda i: (0, i))],
            out_specs=[],
            core_axis_name=("core", "subcore"), dimension_semantics=(pltpu.PARALLEL,),
        )(x_hbm, a_hbm, b_hbm)

    return scatter(x_sub, idx_a, idx_b)


def _sc_gather(table, idx):
    n_out = idx.shape[1]

    @pl.kernel(out_type=jax.ShapeDtypeStruct((n_out, LANES), table.dtype), mesh=_sc_mesh())
    def gather(t_hbm, i_hbm, o_hbm):
        def body(i_vmem, o_vmem):
            pltpu.sync_copy(t_hbm.at[i_vmem.at[0]], o_vmem)

        pltpu.emit_pipeline(
            body, grid=(n_out // SC_WINDOW,),
            in_specs=[pl.BlockSpec((1, SC_WINDOW), lambda i: (0, i))],
            out_specs=[pl.BlockSpec((SC_WINDOW, LANES), lambda i: (i, 0))],
            core_axis_name=("core", "subcore"), dimension_semantics=(pltpu.PARALLEL,),
        )(i_hbm, o_hbm)

    return gather(table, idx)


def _expert_ffn_kernel(blk_e_ref, n_valid_ref, x_ref, *refs):
    del blk_e_ref
    y_ref = refs[-1]
    for i in range(FFN_PER_STEP):
        w1_ref, w3_ref, w2_ref = refs[3 * i:3 * i + 3]
        rows = slice(i * FFN_BLOCK, (i + 1) * FFN_BLOCK)

        @pl.when(pl.program_id(0) * FFN_PER_STEP + i < n_valid_ref[0])
        def _(w1_ref=w1_ref, w3_ref=w3_ref, w2_ref=w2_ref, rows=rows):
            words = jnp.concatenate([x_ref[j, rows, :] for j in range(SUB_PER_ROW)], axis=1)
            lo, hi = _unpack_bf16_pairs(words)
            xc = jnp.concatenate([lo.astype(BF16), hi.astype(BF16)], axis=1)
            h1 = _dot(xc, w1_ref[...])
            h3 = _dot(xc, w3_ref[...])
            act = ((h1 * jax.nn.sigmoid(h1)) * h3).astype(BF16)
            packed = _pack_bf16_pairs(_dot(act, w2_ref[...]))
            for j in range(SUB_PER_ROW):
                y_ref[j, rows, :] = packed[:, j * LANES:(j + 1) * LANES]


def _expert_ffn(xs, blk_e, n_valid, w1, w3, w2):
    rows = xs.shape[1]
    step_rows = FFN_BLOCK * FFN_PER_STEP
    assert rows % step_rows == 0
    live = lambda s, be, nv: jnp.minimum(s, (nv[0] + FFN_PER_STEP - 1) // FFN_PER_STEP - 1)
    x_spec = pl.BlockSpec((SUB_PER_ROW, step_rows, LANES), lambda s, be, nv: (0, live(s, be, nv), 0))
    w_specs = []
    for i in range(FFN_PER_STEP):
        expert = lambda s, be, nv, i=i: (be[s * FFN_PER_STEP + i], 0, 0)
        w_specs += [pl.BlockSpec((None, D_MODEL, EXPERT_FF), expert), pl.BlockSpec((None, D_MODEL, EXPERT_FF), expert),
                    pl.BlockSpec((None, EXPERT_FF, D_MODEL), expert)]
    grid_spec = pltpu.PrefetchScalarGridSpec(
        num_scalar_prefetch=2, grid=(rows // step_rows,), in_specs=[x_spec] + w_specs, out_specs=x_spec)
    return pl.pallas_call(
        _expert_ffn_kernel, grid_spec=grid_spec,
        out_shape=jax.ShapeDtypeStruct((SUB_PER_ROW, rows, LANES), jnp.uint32),
        compiler_params=pltpu.CompilerParams(dimension_semantics=("arbitrary",), vmem_limit_bytes=VMEM_LIMIT),
        name="moe_ffn",
    )(blk_e, n_valid, xs, *([w1, w3, w2] * FFN_PER_STEP))


def _combine_kernel(h_ref, ya_ref, yb_ref, wt_ref, gfin_ref, o_ref):
    a_lo, a_hi = _unpack_bf16_pairs(_load_planes(ya_ref))
    b_lo, b_hi = _unpack_bf16_pairs(_load_planes(yb_ref))
    wt = jnp.transpose(wt_ref[...])
    wa = wt[:, 0:1]
    wb = wt[:, 1:2]
    o_ref[:, :HALF] = h_ref[:, :HALF] + (wa * a_lo + wb * b_lo)
    o_ref[:, HALF:] = h_ref[:, HALF:] + (wa * a_hi + wb * b_hi)
    o_ref[...] = _rms(o_ref[...], gfin_ref[...])


def _combine(h, ya, yb, wt, gfin, *, tile):
    tokens = h.shape[0]
    return pl.pallas_call(
        _combine_kernel,
        grid=(tokens // tile,),
        in_specs=[pl.BlockSpec((tile, D_MODEL), lambda i: (i, 0)),
                  pl.BlockSpec((SUB_PER_ROW, tile, LANES), lambda i: (0, i, 0)),
                  pl.BlockSpec((SUB_PER_ROW, tile, LANES), lambda i: (0, i, 0)), pl.BlockSpec((SUBLANES, tile), lambda i: (0, i)),
                  pl.BlockSpec((1, D_MODEL), lambda i: (0, 0))],
        out_specs=pl.BlockSpec((tile, D_MODEL), lambda i: (i, 0)),
        out_shape=jax.ShapeDtypeStruct((tokens, D_MODEL), F32),
        compiler_params=pltpu.CompilerParams(dimension_semantics=("arbitrary",), vmem_limit_bytes=VMEM_LIMIT),
        name="moe_combine",
    )(h, ya, yb, wt, gfin)


def _moe_dispatch(h, gffn, wrt, rbias):
    tokens = h.shape[0]
    xp, e12, w12, r12, cnt = _route(h, gffn, wrt, rbias, tile=MOE_TILE)
    count = cnt[:, 0]
    padded = (count + FFN_BLOCK - 1) // FFN_BLOCK * FFN_BLOCK
    end = jnp.cumsum(padded)
    start = end - padded
    n_rows = 2 * tokens + N_EXPERTS * FFN_BLOCK
    n_blocks = n_rows // FFN_BLOCK
    n_valid = (end[-1:] // FFN_BLOCK).astype(jnp.int32)
    first_row = jnp.minimum(jnp.arange(n_blocks, dtype=jnp.int32), n_valid - 1) * FFN_BLOCK
    blk_e = jnp.sum((end[None, :] <= first_row[:, None]).astype(jnp.int32), axis=1)
    experts = jnp.arange(N_EXPERTS, dtype=jnp.int32)
    start_of = jnp.sum(jnp.where(e12[:, :, None] == experts, start.astype(jnp.int32), 0), axis=-1)
    rows_ab = start_of + r12
    idx_a, idx_b = _sub_row_index(rows_ab[0], n_rows), _sub_row_index(rows_ab[1], n_rows)
    xs = _sc_scatter_two(xp.reshape(-1, LANES), idx_a, idx_b, n_rows * SUB_PER_ROW)
    return xs.reshape(SUB_PER_ROW, n_rows, LANES), (blk_e, n_valid, idx_a, idx_b, w12)


def _moe_finish(h, xs, plan, w1, w3, w2, gfin):
    blk_e, n_valid, idx_a, idx_b, w12 = plan
    ys = _expert_ffn(xs, blk_e, n_valid, w1, w3, w2).reshape(-1, LANES)
    ya = _sc_gather(ys, idx_a).reshape(SUB_PER_ROW, -1, LANES)
    yb = _sc_gather(ys, idx_b).reshape(SUB_PER_ROW, -1, LANES)
    return _combine(h, ya, yb, w12, gfin, tile=MOE_TILE)


def kernel(x_prompt, x_sample, mem_prompt, cache_conv, cache_mem_k, cache_mem_v, norm_mix_g, w_in, gm_ln_g, gm_ln_b, gm_ws, gm_bs, conv_w, conv_b, cv_ln_g, cv_ln_b, w_out, norm_mem_g, norm_xa_g, xa_wq, xa_wk, xa_wv, xa_wo, norm_ffn_g, router_g, router_g_b, router_e, router_e_b, exp_w1, exp_w3, exp_w2, final_norm_g):
    depth = w_in.shape[0]
    assert depth == 1, "single-layer trunk"
    nb, seq, _ = x_prompt.shape
    ns = x_sample.shape[0]
    row = lambda a: a.reshape(1, -1)

    l = 0
    gmix, gxa, gffn, gmem = row(norm_mix_g[l]), row(norm_xa_g[l]), row(norm_ffn_g[l]), row(norm_mem_g[l])
    gfin = row(final_norm_g)
    win, wout = w_in[l].astype(BF16), w_out[l].astype(BF16)
    wq, wo = xa_wq[l].astype(BF16), xa_wo[l].astype(BF16)
    wk, wv = xa_wk[l], xa_wv[l]
    glng, glnb = row(gm_ln_g[l]), row(gm_ln_b[l])
    cw, cb, clng, clnb = conv_w[l], row(conv_b[l]), row(cv_ln_g[l]), row(cv_ln_b[l])
    ws, bst = gm_ws[l], gm_bs[l].T
    ws0 = jnp.repeat(gm_ws[l][:, 0, 0], GM_HEAD_DIM).reshape(1, GM_WIDTH)
    bs0 = jnp.repeat(gm_bs[l][:, 0], GM_HEAD_DIM).reshape(1, GM_WIDTH)

    n_exp = N_GROUPS * EXPERTS_PER_GROUP
    pad_g = SUBLANES - N_GROUPS
    pad_t = ROUTER_ROWS - SUBLANES - n_exp
    wrt = jnp.concatenate([router_g[l].T, jnp.zeros((pad_g, D_MODEL), F32),
                           router_e[l].reshape(D_MODEL, n_exp).T, jnp.zeros((pad_t, D_MODEL), F32)], axis=0).astype(BF16)
    rbias = jnp.concatenate([router_g_b[l], jnp.zeros((pad_g,), F32), router_e_b[l].reshape(n_exp),
                             jnp.zeros((pad_t,), F32)]).reshape(ROUTER_ROWS, 1)

    mk, mv, kb, vb = _memkv(mem_prompt.reshape(nb * N_MEM, D_MODEL), gmem, wk, wv)
    kb, vb = kb.reshape(nb, N_MEM, D_MODEL), vb.reshape(nb, N_MEM, D_MODEL)
    hp, conv_tail, gmv_p, w1, w3, w2 = _prompt_mix_attn(
        x_prompt, gmix, win, glng, glnb, ws, bst, cw, cb, clng, clnb, wout, gxa, wq, kb, vb, wo,
        exp_w1[l].reshape(-1, EXPERT_FF), exp_w3[l].reshape(-1, EXPERT_FF), exp_w2[l].reshape(-1, D_MODEL))
    h2d = hp.reshape(nb * seq, D_MODEL)
    xs, plan = _moe_dispatch(h2d, gffn, wrt, rbias)
    w1e, w3e, w2e = (w1.reshape(N_EXPERTS, D_MODEL, EXPERT_FF), w3.reshape(N_EXPERTS, D_MODEL, EXPERT_FF),
                     w2.reshape(N_EXPERTS, EXPERT_FF, D_MODEL))
    w1 = w1.reshape(N_GROUPS, EXPERTS_PER_GROUP, D_MODEL, EXPERT_FF)
    w3 = w3.reshape(N_GROUPS, EXPERTS_PER_GROUP, D_MODEL, EXPERT_FF)
    w2 = w2.reshape(N_GROUPS, EXPERTS_PER_GROUP * EXPERT_FF, D_MODEL)

    cache_t = jnp.transpose(cache_conv[l], (1, 0, 2))
    ys, conv_t, gmv_s, q_s = _sample_mix(x_sample.reshape(ns, D_MODEL), gmix, win, glng, glnb, ws0, bs0,
                                        cache_t, cw, cb, clng, clnb, wout, gxa, wq)
    qh, kh, vh = (_split_heads(q_s.reshape(ns, XA_HEADS, XA_HEAD_DIM)), _split_heads(cache_mem_k[l]),
                  _split_heads(cache_mem_v[l]))
    half = ns // 2
    o_first = _sample_attn(qh, kh, vh, 0, half)
    xs, o_first = lax.optimization_barrier((xs, o_first))
    y_prompt = _moe_finish(h2d, xs, plan, w1e, w3e, w2e, gfin)
    o_second = _sample_attn(qh, kh, vh, half, ns - half)
    o_s = _merge_heads(jnp.concatenate([o_first, o_second], axis=0))
    hs = _sample_proj(ys, o_s, wo)
    y_sample = _moe(hs, gffn, wrt, rbias, w1, w3, w2, gfin, tile=ns)

    conv_prompt = conv_tail[:, HALO - (CONV_WIDTH - 1):, :][None]
    conv_sample = jnp.transpose(conv_t, (1, 0, 2))[None]
    return (y_prompt.reshape(nb, seq, D_MODEL), y_sample.reshape(ns, 1, D_MODEL), conv_prompt, conv_sample,
            gmv_p[None], gmv_s.reshape(1, ns, 1, GM_WIDTH),
            mk.reshape(1, nb, N_MEM, XA_HEADS, XA_HEAD_DIM), mv.reshape(1, nb, N_MEM, XA_HEADS, XA_HEAD_DIM))
```

```python
import functools

import jax
import jax.numpy as jnp
from jax import lax
from jax.experimental import pallas as pl
from jax.experimental.pallas import tpu as pltpu
from jax.experimental.pallas import tpu_sc as plsc

F32 = jnp.float32
BF16 = jnp.bfloat16

D_MODEL = 1024
GM_WIDTH = 512
CV_WIDTH = 512
GM_HEADS = 4
GM_HEAD_DIM = 128
CHUNK = 128
CONV_WIDTH = 31
IN_COLS = 2 * GM_WIDTH + 2 * CV_WIDTH
N_MEM = 256
XA_HEADS = 4
XA_HEAD_DIM = 256
N_GROUPS = 4
EXPERTS_PER_GROUP = 8
EXPERT_FF = 256
EPS = 1e-6

LANES = 128
SUBLANES = 8
HALO = 32
PROMPT_TILE = 512
MOE_TILE = 1024
MOE_SUB = 128
FFN_BLOCK = 512
FFN_PER_STEP = 2
SC_WINDOW = 128
SAMPLE_ATTN_BLOCK = 8
ROUTER_ROWS = 128
VMEM_LIMIT = 56 * 1024 * 1024


def _rms(x, g):
    return x * lax.rsqrt(jnp.mean(x * x, axis=-1, keepdims=True) + EPS) * g


def _ln(x, g, b):
    mu = jnp.mean(x, axis=-1, keepdims=True)
    xc = x - mu
    var = jnp.mean(xc * xc, axis=-1, keepdims=True)
    return xc * lax.rsqrt(var + EPS) * g + b


def _dot(a, b):
    return jnp.dot(a, b, preferred_element_type=F32)


def _dot_nt(a, b):
    return lax.dot_general(a, b, (((1,), (1,)), ((), ())), preferred_element_type=F32)


def _dot_tn(a, b):
    return lax.dot_general(a, b, (((0,), (0,)), ((), ())), preferred_element_type=F32)


def _memkv_kernel(mem_ref, g_ref, wk_ref, wv_ref, k_ref, v_ref, kb_ref, vb_ref):
    mn = _rms(mem_ref[...], g_ref[...]).astype(BF16)
    k = _dot(mn, wk_ref[...].astype(BF16))
    v = _dot(mn, wv_ref[...].astype(BF16))
    for h in range(XA_HEADS):
        cols = slice(h * XA_HEAD_DIM, (h + 1) * XA_HEAD_DIM)
        k_ref[:, h, :] = k[:, cols]
        v_ref[:, h, :] = v[:, cols]
    kb_ref[...] = k.astype(BF16)
    vb_ref[...] = v.astype(BF16)


def _memkv(mem2d, g, wk, wv):
    rows = mem2d.shape[0]
    tile = 512
    row_spec = pl.BlockSpec((tile, D_MODEL), lambda i: (i, 0))
    head_spec = pl.BlockSpec((tile, XA_HEADS, XA_HEAD_DIM), lambda i: (i, 0, 0))
    full = lambda shape: pl.BlockSpec(shape, lambda i: (0,) * len(shape))
    return pl.pallas_call(
        _memkv_kernel,
        grid=(rows // tile,),
        in_specs=[row_spec, full((1, D_MODEL)), full((D_MODEL, D_MODEL)), full((D_MODEL, D_MODEL))],
        out_specs=[head_spec, head_spec, row_spec, row_spec],
        out_shape=[jax.ShapeDtypeStruct((rows, XA_HEADS, XA_HEAD_DIM), F32),
                   jax.ShapeDtypeStruct((rows, XA_HEADS, XA_HEAD_DIM), F32),
                   jax.ShapeDtypeStruct((rows, D_MODEL), BF16), jax.ShapeDtypeStruct((rows, D_MODEL), BF16)],
        compiler_params=pltpu.CompilerParams(dimension_semantics=("arbitrary",), vmem_limit_bytes=VMEM_LIMIT),
        name="memkv",
    )(mem2d, g, wk, wv)


def _conv_block(g_s, cw_ref, start, lanes):
    assert start % SUBLANES == 0
    rows = CHUNK + HALO
    win = g_s[start:start + rows, lanes]
    off = HALO - (CONV_WIDTH - 1)
    acc = None
    for b in range(SUBLANES):
        shifted = pltpu.roll(win, rows - (off + b), axis=0) if off + b else win
        for k in range(b, CONV_WIDTH, SUBLANES):
            assert k - b + CHUNK + off + b <= rows
            term = shifted[k - b:k - b + CHUNK] * cw_ref[k:k + 1, lanes]
            acc = term if acc is None else acc + term
    return acc


def _cast_in_copies(src_refs, in_bufs, sem_in, chunk, slot):
    copies = []
    for i, (src, ibuf) in enumerate(zip(src_refs, in_bufs)):
        rows = ibuf.shape[1]
        start = pl.multiple_of(chunk * rows, rows)
        copies.append(pltpu.make_async_copy(src.at[pl.ds(start, rows), :], ibuf.at[slot], sem_in.at[slot, i]))
    return copies


def _cast_out_copies(dst_refs, out_bufs, sem_out, chunk):
    copies = []
    for i, (dst, obuf) in enumerate(zip(dst_refs, out_bufs)):
        rows = obuf.shape[0]
        start = pl.multiple_of(chunk * rows, rows)
        copies.append(pltpu.make_async_copy(obuf, dst.at[pl.ds(start, rows), :], sem_out.at[i]))
    return copies


def _prompt_kernel(x_ref, gmix_ref, win_ref, glng_ref, glnb_ref, ws_ref, bst_ref, cw_ref, cb_ref,
                   clng_ref, clnb_ref, wout_ref, gxa_ref, wq_ref, kb_ref, vb_ref, wo_ref,
                   w1f_ref, w3f_ref, w2f_ref,
                   h_ref, conv_ref, gmv_ref, w1b_ref, w3b_ref, w2b_ref,
                   z_s, g_s, c_s, ab_s, o_s, y_prev, y_cur, in1, in3, in2, st1, st3, st2, sem_in, sem_out,
                   *, tiles_per_seq, n_cast_chunks):
    s = pl.program_id(0)
    n_steps = pl.num_programs(0)
    tb = x_ref.shape[0]
    n_chunks = tb // CHUNK

    srcs, dsts = (w1f_ref, w3f_ref, w2f_ref), (w1b_ref, w3b_ref, w2b_ref)
    ins, sts = (in1, in3, in2), (st1, st3, st2)
    slot = s % 2
    chunk_of = lambda step: jnp.minimum(step, n_cast_chunks - 1)

    @pl.when(s == 0)
    def _():
        for c in _cast_in_copies(srcs, ins, sem_in, chunk_of(s), slot):
            c.start()

    @pl.when(s + 1 < n_steps)
    def _():
        for c in _cast_in_copies(srcs, ins, sem_in, chunk_of(s + 1), 1 - slot):
            c.start()

    cast_in = _cast_in_copies(srcs, ins, sem_in, chunk_of(s), slot)
    cast_out = _cast_out_copies(dsts, sts, sem_out, chunk_of(s))
    for c in cast_in:
        c.wait()
    for ibuf, obuf in zip(ins, sts):
        obuf[...] = ibuf[slot].astype(BF16)
    for c in cast_out:
        c.start()

    @pl.when(s == 0)
    def _():
        y_prev[...] = jnp.zeros(y_prev.shape, F32)

    @pl.when(s % tiles_per_seq == 0)
    def _():
        g_s[0:HALO, :] = jnp.zeros((HALO, CV_WIDTH), F32)

    z_s[...] = _dot(_rms(x_ref[...], gmix_ref[...]).astype(BF16), win_ref[...])

    y = y_prev[...]
    qn = _rms(y, gxa_ref[...]).astype(BF16)
    q = (_dot(qn, wq_ref[...]) * (XA_HEAD_DIM ** -0.5)).astype(BF16)
    for h in range(XA_HEADS):
        cols = slice(h * XA_HEAD_DIM, (h + 1) * XA_HEAD_DIM)
        sc = _dot_nt(q[:, cols], kb_ref[:, cols])
        e = jnp.exp(sc - jnp.max(sc, axis=-1, keepdims=True))
        p = (e / jnp.sum(e, axis=-1, keepdims=True)).astype(BF16)
        o_s[:, cols] = _dot(p, vb_ref[:, cols]).astype(BF16)
    h_ref[...] = y + _dot(o_s[...], wo_ref[...])

    tri = (lax.broadcasted_iota(jnp.int32, (CHUNK, CHUNK), 0)
           >= lax.broadcasted_iota(jnp.int32, (CHUNK, CHUNK), 1))
    wm = [jnp.where(tri, ws_ref[h], 0.0).astype(BF16) for h in range(GM_HEADS)]

    for c in range(n_chunks):
        rows = slice(c * CHUNK, (c + 1) * CHUNK)
        for h in range(GM_HEADS):
            cu = slice(h * GM_HEAD_DIM, (h + 1) * GM_HEAD_DIM)
            cv = slice(GM_WIDTH + h * GM_HEAD_DIM, GM_WIDTH + (h + 1) * GM_HEAD_DIM)
            v = _ln(jax.nn.gelu(z_s[rows, cv]), glng_ref[:, cu], glnb_ref[:, cu])
            if c == n_chunks - 1:
                gmv_ref[:, cu] = v
            mixed = _dot(wm[h], v.astype(BF16)) + bst_ref[:, h:h + 1]
            ab_s[rows, cu] = (jax.nn.gelu(z_s[rows, cu]) * mixed).astype(BF16)
        ca = slice(2 * GM_WIDTH, 2 * GM_WIDTH + CV_WIDTH)
        cg = slice(2 * GM_WIDTH + CV_WIDTH, IN_COLS)
        g_s[HALO + c * CHUNK:HALO + (c + 1) * CHUNK, :] = z_s[rows, ca] * jax.nn.sigmoid(z_s[rows, cg])

    for c in range(n_chunks):
        rows = slice(c * CHUNK, (c + 1) * CHUNK)
        for cb in range(CV_WIDTH // LANES):
            lanes = slice(cb * LANES, (cb + 1) * LANES)
            c_s[rows, lanes] = _conv_block(g_s, cw_ref, c * CHUNK, lanes) + cb_ref[:, lanes]
        b = _ln(c_s[rows, :], clng_ref[...], clnb_ref[...])
        ab_s[rows, GM_WIDTH:] = (b * jax.nn.sigmoid(b)).astype(BF16)

    conv_ref[...] = g_s[tb:tb + HALO, :]
    g_s[0:HALO, :] = g_s[tb:tb + HALO, :]
    y_cur[...] = x_ref[...] + _dot(ab_s[...], wout_ref[...])
    y_prev[...] = y_cur[...]
    for c in cast_out:
        c.wait()


def _prompt_mix_attn(x, gmix, win, glng, glnb, ws, bst, cw, cb, clng, clnb, wout, gxa, wq, kb, vb, wo,
                     w1f, w3f, w2f):
    nb, seq, _ = x.shape
    tb = PROMPT_TILE
    tps = seq // tb
    last = nb * tps - 1
    n_cast = nb * tps
    r13, r2 = w1f.shape[0] // n_cast, w2f.shape[0] // n_cast
    assert w1f.shape == w3f.shape and w1f.shape[0] % n_cast == 0 and w2f.shape[0] % n_cast == 0
    assert r13 % 16 == 0 and r2 % 16 == 0
    hbm = pl.BlockSpec(memory_space=pl.ANY)
    tile = lambda s, lag: jnp.clip(s - lag, 0, last)
    full = lambda shape: pl.BlockSpec(shape, lambda s: (0,) * len(shape))
    rows = lambda lag: pl.BlockSpec((None, tb, D_MODEL), lambda s: (tile(s, lag) // tps, tile(s, lag) % tps, 0))
    kv_spec = pl.BlockSpec((None, N_MEM, D_MODEL), lambda s: (tile(s, 1) // tps, 0, 0))
    seq_out = lambda r, c: pl.BlockSpec((None, r, c), lambda s: (tile(s, 0) // tps, 0, 0))
    return pl.pallas_call(
        functools.partial(_prompt_kernel, tiles_per_seq=tps, n_cast_chunks=n_cast),
        grid=(nb * tps + 1,),
        in_specs=[rows(0), full((1, D_MODEL)), full((D_MODEL, IN_COLS)), full((1, GM_WIDTH)), full((1, GM_WIDTH)),
                  full((GM_HEADS, CHUNK, CHUNK)), full((CHUNK, GM_HEADS)), full((CONV_WIDTH, CV_WIDTH)),
                  full((1, CV_WIDTH)), full((1, CV_WIDTH)), full((1, CV_WIDTH)), full((D_MODEL, D_MODEL)),
                  full((1, D_MODEL)), full((D_MODEL, D_MODEL)), kv_spec, kv_spec, full((D_MODEL, D_MODEL)),
                  hbm, hbm, hbm],
        out_specs=[rows(1), seq_out(HALO, CV_WIDTH), seq_out(CHUNK, GM_WIDTH), hbm, hbm, hbm],
        out_shape=[jax.ShapeDtypeStruct((nb, seq, D_MODEL), F32),
                   jax.ShapeDtypeStruct((nb, HALO, CV_WIDTH), F32),
                   jax.ShapeDtypeStruct((nb, CHUNK, GM_WIDTH), F32),
                   jax.ShapeDtypeStruct(w1f.shape, BF16), jax.ShapeDtypeStruct(w3f.shape, BF16),
                   jax.ShapeDtypeStruct(w2f.shape, BF16)],
        scratch_shapes=[pltpu.VMEM((tb, IN_COLS), F32), pltpu.VMEM((HALO + tb, CV_WIDTH), F32),
                        pltpu.VMEM((tb, CV_WIDTH), F32), pltpu.VMEM((tb, D_MODEL), BF16),
                        pltpu.VMEM((tb, D_MODEL), BF16), pltpu.VMEM((tb, D_MODEL), F32),
                        pltpu.VMEM((tb, D_MODEL), F32),
                        pltpu.VMEM((2, r13, EXPERT_FF), F32), pltpu.VMEM((2, r13, EXPERT_FF), F32),
                        pltpu.VMEM((2, r2, D_MODEL), F32),
                        pltpu.VMEM((r13, EXPERT_FF), BF16), pltpu.VMEM((r13, EXPERT_FF), BF16),
                        pltpu.VMEM((r2, D_MODEL), BF16),
                        pltpu.SemaphoreType.DMA((2, 3)), pltpu.SemaphoreType.DMA((3,))],
        compiler_params=pltpu.CompilerParams(dimension_semantics=("arbitrary",), vmem_limit_bytes=VMEM_LIMIT),
        name="prompt_mix_attn",
    )(x, gmix, win, glng, glnb, ws, bst, cw, cb, clng, clnb, wout, gxa, wq, kb, vb, wo, w1f, w3f, w2f)


def _sample_mix_kernel(x_ref, gmix_ref, win_ref, glng_ref, glnb_ref, ws0_ref, bs0_ref, cache_ref, cw_ref, cb_ref,
                       clng_ref, clnb_ref, wout_ref, gxa_ref, wq_ref,
                       y_ref, conv_ref, v_ref, q_ref, ab_s):
    x = x_ref[...]
    z = _dot(_rms(x, gmix_ref[...]).astype(BF16), win_ref[...])
    for h in range(GM_HEADS):
        cu = slice(h * GM_HEAD_DIM, (h + 1) * GM_HEAD_DIM)
        cv = slice(GM_WIDTH + h * GM_HEAD_DIM, GM_WIDTH + (h + 1) * GM_HEAD_DIM)
        v = _ln(jax.nn.gelu(z[:, cv]), glng_ref[:, cu], glnb_ref[:, cu])
        v_ref[:, cu] = v
        ab_s[:, cu] = (jax.nn.gelu(z[:, cu]) * (v * ws0_ref[:, cu] + bs0_ref[:, cu])).astype(BF16)
    glu = z[:, 2 * GM_WIDTH:2 * GM_WIDTH + CV_WIDTH] * jax.nn.sigmoid(z[:, 2 * GM_WIDTH + CV_WIDTH:])
    conv_ref[0:CONV_WIDTH - 2] = cache_ref[1:CONV_WIDTH - 1]
    conv_ref[CONV_WIDTH - 2] = glu
    conv = glu * cw_ref[CONV_WIDTH - 1:CONV_WIDTH, :] + cb_ref[...]
    for k in range(CONV_WIDTH - 1):
        conv = conv + cache_ref[k] * cw_ref[k:k + 1, :]
    b = _ln(conv, clng_ref[...], clnb_ref[...])
    ab_s[:, GM_WIDTH:] = (b * jax.nn.sigmoid(b)).astype(BF16)
    y = x + _dot(ab_s[...], wout_ref[...])
    y_ref[...] = y
    q_ref[...] = _dot(_rms(y, gxa_ref[...]).astype(BF16), wq_ref[...]) * (XA_HEAD_DIM ** -0.5)


def _sample_mix(x, gmix, win, glng, glnb, ws0, bs0, cache_t, cw, cb, clng, clnb, wout, gxa, wq):
    ns = x.shape[0]
    return pl.pallas_call(
        _sample_mix_kernel,
        out_shape=[jax.ShapeDtypeStruct((ns, D_MODEL), F32), jax.ShapeDtypeStruct(cache_t.shape, F32),
                   jax.ShapeDtypeStruct((ns, GM_WIDTH), F32), jax.ShapeDtypeStruct((ns, D_MODEL), F32)],
        scratch_shapes=[pltpu.VMEM((ns, D_MODEL), BF16)],
        compiler_params=pltpu.CompilerParams(vmem_limit_bytes=VMEM_LIMIT),
        name="sample_mix",
    )(x, gmix, win, glng, glnb, ws0, bs0, cache_t, cw, cb, clng, clnb, wout, gxa, wq)


def _sample_attn_kernel(q_ref, k_ref, v_ref, o_ref):
    ones = jnp.ones((LANES, LANES), BF16)
    rows = N_MEM * SUBLANES
    for i in range(q_ref.shape[0]):
        prod = (k_ref[i] * q_ref[i][None]).reshape(rows, LANES).astype(BF16)
        part = _dot(prod, ones).reshape(N_MEM, SUBLANES, LANES)
        s = part + pltpu.roll(part, XA_HEADS, axis=1)
        e = jnp.exp(s - jnp.max(s, axis=0, keepdims=True))
        p = e / jnp.sum(e, axis=0, keepdims=True)
        o_ref[i] = jnp.sum(p * v_ref[i], axis=0)


def _split_heads(a):
    halves = XA_HEAD_DIM // LANES
    assert halves * XA_HEADS == SUBLANES
    lead = a.shape[:-2]
    a = a.reshape(*lead, XA_HEADS, halves, LANES)
    return jnp.swapaxes(a, -3, -2).reshape(*lead, SUBLANES, LANES)


def _merge_heads(o):
    ns = o.shape[0]
    o = jnp.swapaxes(o.reshape(ns, XA_HEAD_DIM // LANES, XA_HEADS, LANES), 1, 2)
    return o.reshape(ns, XA_HEADS * XA_HEAD_DIM)


def _sample_attn(q, k, v, first, count):
    nb = SAMPLE_ATTN_BLOCK
    assert first % nb == 0 and count % nb == 0
    off = first // nb
    return pl.pallas_call(
        _sample_attn_kernel,
        grid=(count // nb,),
        in_specs=[pl.BlockSpec((nb, SUBLANES, LANES), lambda i: (i + off, 0, 0)),
                  pl.BlockSpec((nb, N_MEM, SUBLANES, LANES), lambda i: (i + off, 0, 0, 0)),
                  pl.BlockSpec((nb, N_MEM, SUBLANES, LANES), lambda i: (i + off, 0, 0, 0))],
        out_specs=pl.BlockSpec((nb, SUBLANES, LANES), lambda i: (i, 0, 0)),
        out_shape=jax.ShapeDtypeStruct((count, SUBLANES, LANES), F32),
        compiler_params=pltpu.CompilerParams(dimension_semantics=("arbitrary",), vmem_limit_bytes=VMEM_LIMIT),
        name="sample_attn",
    )(q, k, v)


def _sample_proj_kernel(y_ref, o_ref, wo_ref, h_ref):
    h_ref[...] = y_ref[...] + _dot(o_ref[...].astype(BF16), wo_ref[...])


def _sample_proj(y, o, wo):
    return pl.pallas_call(
        _sample_proj_kernel,
        out_shape=jax.ShapeDtypeStruct(y.shape, F32),
        compiler_params=pltpu.CompilerParams(vmem_limit_bytes=VMEM_LIMIT),
        name="sample_proj",
    )(y, o, wo)


def _moe_kernel(h_ref, gffn_ref, wrt_ref, rb_ref, w1_ref, w3_ref, w2_ref, gfin_ref,
                o_ref, xt_s, u_s, gidx_s, slot_s, comb_s, act_s, p2_s, y2_s, cnt_s):
    i = pl.program_id(0)
    g = pl.program_id(1)
    rt = h_ref.shape[0]
    sb = act_s.shape[0]

    @pl.when(jnp.logical_and(i == 0, g == 0))
    def _():
        u_s[...] = (lax.broadcasted_iota(jnp.int32, (rt, rt), 0)
                    < lax.broadcasted_iota(jnp.int32, (rt, rt), 1)).astype(F32).astype(BF16)

    @pl.when(g == 0)
    def _route():
        h = h_ref[...]
        o_ref[...] = h
        xt = _rms(h, gffn_ref[...]).astype(BF16)
        xt_s[...] = xt
        lt = _dot_nt(wrt_ref[...], xt) + rb_ref[...]
        gl = [lt[k:k + 1, :] for k in range(N_GROUPS)]
        gmax = jnp.maximum(jnp.maximum(gl[0], gl[1]), jnp.maximum(gl[2], gl[3]))
        gidx = jnp.where(gl[0] == gmax, 0, jnp.where(gl[1] == gmax, 1, jnp.where(gl[2] == gmax, 2, 3)))
        gidx = gidx.astype(jnp.int32)
        sumexp = (jnp.exp(gl[0] - gmax) + jnp.exp(gl[1] - gmax)) + (jnp.exp(gl[2] - gmax) + jnp.exp(gl[3] - gmax))
        p_g = 1.0 / sumexp
        esel = lt[SUBLANES + 3 * EXPERTS_PER_GROUP:SUBLANES + 4 * EXPERTS_PER_GROUP, :]
        for k in (2, 1, 0):
            esel = jnp.where(gidx == k, lt[SUBLANES + k * EXPERTS_PER_GROUP:SUBLANES + (k + 1) * EXPERTS_PER_GROUP, :],
                             esel)
        eidx = lax.broadcasted_iota(jnp.int32, (EXPERTS_PER_GROUP, rt), 0)
        m1 = jnp.max(esel, axis=0, keepdims=True)
        i1 = jnp.min(jnp.where(esel == m1, eidx, EXPERTS_PER_GROUP), axis=0, keepdims=True)
        rest = jnp.where(eidx == i1, -jnp.inf, esel)
        m2 = jnp.max(rest, axis=0, keepdims=True)
        i2 = jnp.min(jnp.where(rest == m2, eidx, EXPERTS_PER_GROUP), axis=0, keepdims=True)
        t2 = jnp.exp(m2 - m1)
        den = 1.0 + t2
        w_top1 = (1.0 / den) * p_g
        w_top2 = (t2 / den) * p_g
        within = jnp.where(eidx == i1, w_top1, 0.0) + jnp.where(eidx == i2, w_top2, 0.0)
        c_hi = within.astype(BF16).astype(F32)
        r1 = within - c_hi
        c_mid = r1.astype(BF16).astype(F32)
        c_lo = (r1 - c_mid).astype(BF16).astype(F32)
        comb_s[0:8, :] = c_hi
        comb_s[8:16, :] = c_mid
        comb_s[16:24, :] = c_lo
        comb_s[24:32, :] = jnp.zeros((8, rt), F32)
        onehot = (eidx == gidx).astype(F32)
        rank = _dot(onehot.astype(BF16), u_s[...])
        slot_s[...] = jnp.sum(onehot * rank, axis=0, keepdims=True).astype(jnp.int32)
        gidx_s[...] = gidx
        for k in range(N_GROUPS):
            cnt_s[k] = jnp.sum(onehot[k:k + 1, :]).astype(jnp.int32)

    n_blk = (cnt_s[g] + sb - 1) // sb

    def sub_block(j, half):
        half_rows = slice(half * sb, (half + 1) * sb)
        rows = lax.broadcasted_iota(jnp.int32, (sb, rt), 0) + j * sb
        hit = jnp.logical_and(rows == slot_s[...], gidx_s[...] == g)
        p = jnp.where(hit, 1.0, 0.0).astype(BF16)
        p2_s[half_rows, :] = p
        xc = _dot(p, xt_s[...]).astype(BF16)
        cexp = _dot_nt(p, comb_s[...].astype(BF16))
        cw = (cexp[:, 0:8] + cexp[:, 8:16]) + cexp[:, 16:24]
        for e in range(EXPERTS_PER_GROUP):
            h1 = _dot(xc, w1_ref[e])
            h3 = _dot(xc, w3_ref[e])
            a = (h1 * jax.nn.sigmoid(h1)) * h3 * cw[:, e:e + 1]
            act_s[:, e * EXPERT_FF:(e + 1) * EXPERT_FF] = a.astype(BF16)
        y2_s[half_rows, :] = _dot(act_s[...], w2_ref[...]).astype(BF16)

    def body(jj, carry):
        sub_block(2 * jj, 0)

        @pl.when(2 * jj + 1 < n_blk)
        def _():
            sub_block(2 * jj + 1, 1)

        @pl.when(2 * jj + 1 >= n_blk)
        def _():
            p2_s[sb:, :] = jnp.zeros((sb, rt), BF16)
            y2_s[sb:, :] = jnp.zeros((sb, D_MODEL), BF16)

        o_ref[...] += _dot_tn(p2_s[...], y2_s[...])
        return carry

    lax.fori_loop(0, (n_blk + 1) // 2, body, 0)

    @pl.when(g == N_GROUPS - 1)
    def _():
        o_ref[...] = _rms(o_ref[...], gfin_ref[...])


def _moe(h, gffn, wrt, rbias, w1, w3, w2, gfin, *, tile):
    tokens = h.shape[0]
    sb = min(MOE_SUB, tile)
    ff = EXPERTS_PER_GROUP * EXPERT_FF
    full = lambda shape: pl.BlockSpec(shape, lambda i, g: (0,) * len(shape))
    row_spec = pl.BlockSpec((tile, D_MODEL), lambda i, g: (i, 0))
    return pl.pallas_call(
        _moe_kernel,
        grid=(tokens // tile, N_GROUPS),
        in_specs=[row_spec, full((1, D_MODEL)), full((ROUTER_ROWS, D_MODEL)), full((ROUTER_ROWS, 1)),
                  pl.BlockSpec((None, EXPERTS_PER_GROUP, D_MODEL, EXPERT_FF), lambda i, g: (g, 0, 0, 0)),
                  pl.BlockSpec((None, EXPERTS_PER_GROUP, D_MODEL, EXPERT_FF), lambda i, g: (g, 0, 0, 0)),
                  pl.BlockSpec((None, ff, D_MODEL), lambda i, g: (g, 0, 0)),
                  full((1, D_MODEL))],
        out_specs=row_spec,
        out_shape=jax.ShapeDtypeStruct((tokens, D_MODEL), F32),
        scratch_shapes=[pltpu.VMEM((tile, D_MODEL), BF16), pltpu.VMEM((tile, tile), BF16),
                        pltpu.VMEM((1, tile), jnp.int32), pltpu.VMEM((1, tile), jnp.int32),
                        pltpu.VMEM((4 * SUBLANES, tile), F32), pltpu.VMEM((sb, ff), BF16),
                        pltpu.VMEM((2 * sb, tile), BF16), pltpu.VMEM((2 * sb, D_MODEL), BF16),
                        pltpu.SMEM((N_GROUPS,), jnp.int32)],
        compiler_params=pltpu.CompilerParams(dimension_semantics=("arbitrary", "arbitrary"),
                                             vmem_limit_bytes=VMEM_LIMIT),
        name="moe",
    )(h, gffn, wrt, rbias, w1, w3, w2, gfin)


HALF = D_MODEL // 2
SUB_PER_ROW = HALF // LANES
N_EXPERTS = N_GROUPS * EXPERTS_PER_GROUP


def _pack_bf16_pairs(x):
    bits = pltpu.bitcast(x.astype(BF16).astype(F32), jnp.uint32)
    return (bits[:, HALF:] & jnp.uint32(0xFFFF0000)) | (bits[:, :HALF] >> 16)


def _unpack_bf16_pairs(w):
    lo = pltpu.bitcast(w << 16, F32)
    hi = pltpu.bitcast(w & jnp.uint32(0xFFFF0000), F32)
    return lo, hi


def _route_kernel(h_ref, gffn_ref, wrt_ref, rb_ref, xp_ref, e_ref, w_ref, r_ref, cnt_ref, u_s, run_s):
    i = pl.program_id(0)
    rt = h_ref.shape[0]

    @pl.when(i == 0)
    def _():
        u_s[...] = (lax.broadcasted_iota(jnp.int32, (rt, rt), 0)
                    < lax.broadcasted_iota(jnp.int32, (rt, rt), 1)).astype(F32).astype(BF16)
        run_s[...] = jnp.zeros(run_s.shape, F32)

    xt = _rms(h_ref[...], gffn_ref[...])
    _store_planes(xp_ref, _pack_bf16_pairs(xt))
    lt = _dot_nt(wrt_ref[...], xt.astype(BF16)) + rb_ref[...]
    gl = [lt[k:k + 1, :] for k in range(N_GROUPS)]
    gmax = jnp.maximum(jnp.maximum(gl[0], gl[1]), jnp.maximum(gl[2], gl[3]))
    gidx = jnp.where(gl[0] == gmax, 0, jnp.where(gl[1] == gmax, 1, jnp.where(gl[2] == gmax, 2, 3)))
    gidx = gidx.astype(jnp.int32)
    sumexp = (jnp.exp(gl[0] - gmax) + jnp.exp(gl[1] - gmax)) + (jnp.exp(gl[2] - gmax) + jnp.exp(gl[3] - gmax))
    p_g = 1.0 / sumexp
    esel = lt[SUBLANES + 3 * EXPERTS_PER_GROUP:SUBLANES + 4 * EXPERTS_PER_GROUP, :]
    for k in (2, 1, 0):
        esel = jnp.where(gidx == k, lt[SUBLANES + k * EXPERTS_PER_GROUP:SUBLANES + (k + 1) * EXPERTS_PER_GROUP, :],
                         esel)
    eidx = lax.broadcasted_iota(jnp.int32, (EXPERTS_PER_GROUP, rt), 0)
    m1 = jnp.max(esel, axis=0, keepdims=True)
    i1 = jnp.min(jnp.where(esel == m1, eidx, EXPERTS_PER_GROUP), axis=0, keepdims=True)
    rest = jnp.where(eidx == i1, -jnp.inf, esel)
    m2 = jnp.max(rest, axis=0, keepdims=True)
    i2 = jnp.min(jnp.where(rest == m2, eidx, EXPERTS_PER_GROUP), axis=0, keepdims=True)
    t2 = jnp.exp(m2 - m1)
    den = 1.0 + t2
    w_ref[...] = jnp.zeros(w_ref.shape, F32)
    w_ref[0:1, :] = (1.0 / den) * p_g
    w_ref[1:2, :] = (t2 / den) * p_g
    e1 = gidx * EXPERTS_PER_GROUP + i1
    e2 = gidx * EXPERTS_PER_GROUP + i2
    e_ref[0:1, :] = e1
    e_ref[1:2, :] = e2
    xid = lax.broadcasted_iota(jnp.int32, (N_EXPERTS, rt), 0)
    oh1 = (xid == e1).astype(F32)
    oh2 = (xid == e2).astype(F32)
    both = oh1 + oh2
    before = _dot(both.astype(BF16), u_s[...]) + run_s[:, 0:1]
    r_ref[0:1, :] = jnp.sum(oh1 * before, axis=0, keepdims=True).astype(jnp.int32)
    r_ref[1:2, :] = jnp.sum(oh2 * before, axis=0, keepdims=True).astype(jnp.int32)
    run_s[...] = run_s[...] + jnp.sum(both, axis=1, keepdims=True)
    cnt_ref[...] = run_s[...].astype(jnp.int32)


def _route(h, gffn, wrt, rbias, *, tile):
    tokens = h.shape[0]
    full = lambda shape: pl.BlockSpec(shape, lambda i: (0,) * len(shape))
    lanes = lambda rows: pl.BlockSpec((rows, tile), lambda i: (0, i))
    return pl.pallas_call(
        _route_kernel,
        grid=(tokens // tile,),
        in_specs=[pl.BlockSpec((tile, D_MODEL), lambda i: (i, 0)), full((1, D_MODEL)),
                  full((ROUTER_ROWS, D_MODEL)), full((ROUTER_ROWS, 1))],
        out_specs=[pl.BlockSpec((SUB_PER_ROW, tile, LANES), lambda i: (0, i, 0)), lanes(2), lanes(SUBLANES), lanes(2),
                   full((N_EXPERTS, LANES))],
        out_shape=[jax.ShapeDtypeStruct((SUB_PER_ROW, tokens, LANES), jnp.uint32),
                   jax.ShapeDtypeStruct((2, tokens), jnp.int32),
                   jax.ShapeDtypeStruct((SUBLANES, tokens), F32), jax.ShapeDtypeStruct((2, tokens), jnp.int32),
                   jax.ShapeDtypeStruct((N_EXPERTS, LANES), jnp.int32)],
        scratch_shapes=[pltpu.VMEM((tile, tile), BF16), pltpu.VMEM((N_EXPERTS, LANES), F32)],
        compiler_params=pltpu.CompilerParams(dimension_semantics=("arbitrary",), vmem_limit_bytes=VMEM_LIMIT),
        name="moe_route",
    )(h, gffn, wrt, rbias)


def _store_planes(ref, words):
    for j in range(SUB_PER_ROW):
        ref[j] = words[:, j * LANES:(j + 1) * LANES]


def _load_planes(ref):
    return jnp.concatenate([ref[j] for j in range(SUB_PER_ROW)], axis=1)


def _sub_row_index(row_of_token, n_rows):
    plane = jnp.arange(SUB_PER_ROW, dtype=jnp.int32)[:, None] * n_rows
    return (row_of_token[None, :] + plane).reshape(1, -1)


def _sc_mesh():
    return plsc.VectorSubcoreMesh(core_axis_name="core", subcore_axis_name="subcore")


def _sc_scatter_two(x_sub, idx_a, idx_b, n_out):
    n_in = x_sub.shape[0]

    @pl.kernel(out_type=jax.ShapeDtypeStruct((n_out, LANES), x_sub.dtype), mesh=_sc_mesh(), scratch_types=[])
    def scatter(x_hbm, a_hbm, b_hbm, o_hbm):
        def body(x_vmem, a_vmem, b_vmem):
            pltpu.sync_copy(x_vmem, o_hbm.at[a_vmem.at[0]])
            pltpu.sync_copy(x_vmem, o_hbm.at[b_vmem.at[0]])

        pltpu.emit_pipeline(
            body, grid=(n_in // SC_WINDOW,),
            in_specs=[pl.BlockSpec((SC_WINDOW, LANES), lambda i: (i, 0)),
                      pl.BlockSpec((1, SC_WINDOW), lambda i: (0, i)),
                      pl.BlockSpec((1, SC_WINDOW), lambda i: (0, i))],
            out_specs=[],
            core_axis_name=("core", "subcore"), dimension_semantics=(pltpu.PARALLEL,),
        )(x_hbm, a_hbm, b_hbm)

    return scatter(x_sub, idx_a, idx_b)


def _sc_gather(table, idx):
    n_out = idx.shape[1]

    @pl.kernel(out_type=jax.ShapeDtypeStruct((n_out, LANES), table.dtype), mesh=_sc_mesh())
    def gather(t_hbm, i_hbm, o_hbm):
        def body(i_vmem, o_vmem):
            pltpu.sync_copy(t_hbm.at[i_vmem.at[0]], o_vmem)

        pltpu.emit_pipeline(
            body, grid=(n_out // SC_WINDOW,),
            in_specs=[pl.BlockSpec((1, SC_WINDOW), lambda i: (0, i))],
            out_specs=[pl.BlockSpec((SC_WINDOW, LANES), lambda i: (i, 0))],
            core_axis_name=("core", "subcore"), dimension_semantics=(pltpu.PARALLEL,),
        )(i_hbm, o_hbm)

    return gather(table, idx)


def _expert_ffn_kernel(blk_e_ref, n_valid_ref, x_ref, *refs):
    del blk_e_ref
    y_ref = refs[-1]
    for i in range(FFN_PER_STEP):
        w1_ref, w3_ref, w2_ref = refs[3 * i:3 * i + 3]
        rows = slice(i * FFN_BLOCK, (i + 1) * FFN_BLOCK)

        @pl.when(pl.program_id(0) * FFN_PER_STEP + i < n_valid_ref[0])
        def _(w1_ref=w1_ref, w3_ref=w3_ref, w2_ref=w2_ref, rows=rows):
            words = jnp.concatenate([x_ref[j, rows, :] for j in range(SUB_PER_ROW)], axis=1)
            lo, hi = _unpack_bf16_pairs(words)
            xc = jnp.concatenate([lo.astype(BF16), hi.astype(BF16)], axis=1)
            h1 = _dot(xc, w1_ref[...])
            h3 = _dot(xc, w3_ref[...])
            act = ((h1 * jax.nn.sigmoid(h1)) * h3).astype(BF16)
            packed = _pack_bf16_pairs(_dot(act, w2_ref[...]))
            for j in range(SUB_PER_ROW):
                y_ref[j, rows, :] = packed[:, j * LANES:(j + 1) * LANES]


def _expert_ffn(xs, blk_e, n_valid, w1, w3, w2):
    rows = xs.shape[1]
    step_rows = FFN_BLOCK * FFN_PER_STEP
    assert rows % step_rows == 0
    live = lambda s, be, nv: jnp.minimum(s, (nv[0] + FFN_PER_STEP - 1) // FFN_PER_STEP - 1)
    x_spec = pl.BlockSpec((SUB_PER_ROW, step_rows, LANES), lambda s, be, nv: (0, live(s, be, nv), 0))
    w_specs = []
    for i in range(FFN_PER_STEP):
        expert = lambda s, be, nv, i=i: (be[s * FFN_PER_STEP + i], 0, 0)
        w_specs += [pl.BlockSpec((None, D_MODEL, EXPERT_FF), expert), pl.BlockSpec((None, D_MODEL, EXPERT_FF), expert),
                    pl.BlockSpec((None, EXPERT_FF, D_MODEL), expert)]
    grid_spec = pltpu.PrefetchScalarGridSpec(
        num_scalar_prefetch=2, grid=(rows // step_rows,), in_specs=[x_spec] + w_specs, out_specs=x_spec)
    return pl.pallas_call(
        _expert_ffn_kernel, grid_spec=grid_spec,
        out_shape=jax.ShapeDtypeStruct((SUB_PER_ROW, rows, LANES), jnp.uint32),
        compiler_params=pltpu.CompilerParams(dimension_semantics=("arbitrary",), vmem_limit_bytes=VMEM_LIMIT),
        name="moe_ffn",
    )(blk_e, n_valid, xs, *([w1, w3, w2] * FFN_PER_STEP))


def _combine_kernel(h_ref, ya_ref, yb_ref, wt_ref, gfin_ref, o_ref):
    a_lo, a_hi = _unpack_bf16_pairs(_load_planes(ya_ref))
    b_lo, b_hi = _unpack_bf16_pairs(_load_planes(yb_ref))
    wt = jnp.transpose(wt_ref[...])
    wa = wt[:, 0:1]
    wb = wt[:, 1:2]
    o_ref[:, :HALF] = h_ref[:, :HALF] + (wa * a_lo + wb * b_lo)
    o_ref[:, HALF:] = h_ref[:, HALF:] + (wa * a_hi + wb * b_hi)
    o_ref[...] = _rms(o_ref[...], gfin_ref[...])


def _combine(h, ya, yb, wt, gfin, *, tile):
    tokens = h.shape[0]
    return pl.pallas_call(
        _combine_kernel,
        grid=(tokens // tile,),
        in_specs=[pl.BlockSpec((tile, D_MODEL), lambda i: (i, 0)),
                  pl.BlockSpec((SUB_PER_ROW, tile, LANES), lambda i: (0, i, 0)),
                  pl.BlockSpec((SUB_PER_ROW, tile, LANES), lambda i: (0, i, 0)), pl.BlockSpec((SUBLANES, tile), lambda i: (0, i)),
                  pl.BlockSpec((1, D_MODEL), lambda i: (0, 0))],
        out_specs=pl.BlockSpec((tile, D_MODEL), lambda i: (i, 0)),
        out_shape=jax.ShapeDtypeStruct((tokens, D_MODEL), F32),
        compiler_params=pltpu.CompilerParams(dimension_semantics=("arbitrary",), vmem_limit_bytes=VMEM_LIMIT),
        name="moe_combine",
    )(h, ya, yb, wt, gfin)


def _moe_dispatch(h, gffn, wrt, rbias):
    tokens = h.shape[0]
    xp, e12, w12, r12, cnt = _route(h, gffn, wrt, rbias, tile=MOE_TILE)
    count = cnt[:, 0]
    padded = (count + FFN_BLOCK - 1) // FFN_BLOCK * FFN_BLOCK
    end = jnp.cumsum(padded)
    start = end - padded
    n_rows = 2 * tokens + N_EXPERTS * FFN_BLOCK
    n_blocks = n_rows // FFN_BLOCK
    n_valid = (end[-1:] // FFN_BLOCK).astype(jnp.int32)
    first_row = jnp.minimum(jnp.arange(n_blocks, dtype=jnp.int32), n_valid - 1) * FFN_BLOCK
    blk_e = jnp.sum((end[None, :] <= first_row[:, None]).astype(jnp.int32), axis=1)
    experts = jnp.arange(N_EXPERTS, dtype=jnp.int32)
    start_of = jnp.sum(jnp.where(e12[:, :, None] == experts, start.astype(jnp.int32), 0), axis=-1)
    rows_ab = start_of + r12
    idx_a, idx_b = _sub_row_index(rows_ab[0], n_rows), _sub_row_index(rows_ab[1], n_rows)
    xs = _sc_scatter_two(xp.reshape(-1, LANES), idx_a, idx_b, n_rows * SUB_PER_ROW)
    return xs.reshape(SUB_PER_ROW, n_rows, LANES), (blk_e, n_valid, idx_a, idx_b, w12)


def _moe_finish(h, xs, plan, w1, w3, w2, gfin):
    blk_e, n_valid, idx_a, idx_b, w12 = plan
    ys = _expert_ffn(xs, blk_e, n_valid, w1, w3, w2).reshape(-1, LANES)
    ya = _sc_gather(ys, idx_a).reshape(SUB_PER_ROW, -1, LANES)
    yb = _sc_gather(ys, idx_b).reshape(SUB_PER_ROW, -1, LANES)
    return _combine(h, ya, yb, w12, gfin, tile=MOE_TILE)


def kernel(x_prompt, x_sample, mem_prompt, cache_conv, cache_mem_k, cache_mem_v, norm_mix_g, w_in, gm_ln_g, gm_ln_b, gm_ws, gm_bs, conv_w, conv_b, cv_ln_g, cv_ln_b, w_out, norm_mem_g, norm_xa_g, xa_wq, xa_wk, xa_wv, xa_wo, norm_ffn_g, router_g, router_g_b, router_e, router_e_b, exp_w1, exp_w3, exp_w2, final_norm_g):
    depth = w_in.shape[0]
    assert depth == 1, "single-layer trunk"
    nb, seq, _ = x_prompt.shape
    ns = x_sample.shape[0]
    row = lambda a: a.reshape(1, -1)

    l = 0
    gmix, gxa, gffn, gmem = row(norm_mix_g[l]), row(norm_xa_g[l]), row(norm_ffn_g[l]), row(norm_mem_g[l])
    gfin = row(final_norm_g)
    win, wout = w_in[l].astype(BF16), w_out[l].astype(BF16)
    wq, wo = xa_wq[l].astype(BF16), xa_wo[l].astype(BF16)
    wk, wv = xa_wk[l], xa_wv[l]
    glng, glnb = row(gm_ln_g[l]), row(gm_ln_b[l])
    cw, cb, clng, clnb = conv_w[l], row(conv_b[l]), row(cv_ln_g[l]), row(cv_ln_b[l])
    ws, bst = gm_ws[l], gm_bs[l].T
    ws0 = jnp.repeat(gm_ws[l][:, 0, 0], GM_HEAD_DIM).reshape(1, GM_WIDTH)
    bs0 = jnp.repeat(gm_bs[l][:, 0], GM_HEAD_DIM).reshape(1, GM_WIDTH)

    n_exp = N_GROUPS * EXPERTS_PER_GROUP
    pad_g = SUBLANES - N_GROUPS
    pad_t = ROUTER_ROWS - SUBLANES - n_exp
    wrt = jnp.concatenate([router_g[l].T, jnp.zeros((pad_g, D_MODEL), F32),
                           router_e[l].reshape(D_MODEL, n_exp).T, jnp.zeros((pad_t, D_MODEL), F32)], axis=0).astype(BF16)
    rbias = jnp.concatenate([router_g_b[l], jnp.zeros((pad_g,), F32), router_e_b[l].reshape(n_exp),
                             jnp.zeros((pad_t,), F32)]).reshape(ROUTER_ROWS, 1)

    mk, mv, kb, vb = _memkv(mem_prompt.reshape(nb * N_MEM, D_MODEL), gmem, wk, wv)
    kb, vb = kb.reshape(nb, N_MEM, D_MODEL), vb.reshape(nb, N_MEM, D_MODEL)
    hp, conv_tail, gmv_p, w1, w3, w2 = _prompt_mix_attn(
        x_prompt, gmix, win, glng, glnb, ws, bst, cw, cb, clng, clnb, wout, gxa, wq, kb, vb, wo,
        exp_w1[l].reshape(-1, EXPERT_FF), exp_w3[l].reshape(-1, EXPERT_FF), exp_w2[l].reshape(-1, D_MODEL))
    h2d = hp.reshape(nb * seq, D_MODEL)
    xs, plan = _moe_dispatch(h2d, gffn, wrt, rbias)
    w1e, w3e, w2e = (w1.reshape(N_EXPERTS, D_MODEL, EXPERT_FF), w3.reshape(N_EXPERTS, D_MODEL, EXPERT_FF),
                     w2.reshape(N_EXPERTS, EXPERT_FF, D_MODEL))
    w1 = w1.reshape(N_GROUPS, EXPERTS_PER_GROUP, D_MODEL, EXPERT_FF)
    w3 = w3.reshape(N_GROUPS, EXPERTS_PER_GROUP, D_MODEL, EXPERT_FF)
    w2 = w2.reshape(N_GROUPS, EXPERTS_PER_GROUP * EXPERT_FF, D_MODEL)

    cache_t = jnp.transpose(cache_conv[l], (1, 0, 2))
    ys, conv_t, gmv_s, q_s = _sample_mix(x_sample.reshape(ns, D_MODEL), gmix, win, glng, glnb, ws0, bs0,
                                        cache_t, cw, cb, clng, clnb, wout, gxa, wq)
    qh, kh, vh = (_split_heads(q_s.reshape(ns, XA_HEADS, XA_HEAD_DIM)), _split_heads(cache_mem_k[l]),
                  _split_heads(cache_mem_v[l]))
    half = ns // 2
    o_first = _sample_attn(qh, kh, vh, 0, half)
    xs, o_first = lax.optimization_barrier((xs, o_first))
    y_prompt = _moe_finish(h2d, xs, plan, w1e, w3e, w2e, gfin)
    o_second = _sample_attn(qh, kh, vh, half, ns - half)
    o_s = _merge_heads(jnp.concatenate([o_first, o_second], axis=0))
    hs = _sample_proj(ys, o_s, wo)
    y_sample = _moe(hs, gffn, wrt, rbias, w1, w3, w2, gfin, tile=ns)

    conv_prompt = conv_tail[:, HALO - (CONV_WIDTH - 1):, :][None]
    conv_sample = jnp.transpose(conv_t, (1, 0, 2))[None]
    return (y_prompt.reshape(nb, seq, D_MODEL), y_sample.reshape(ns, 1, D_MODEL), conv_prompt, conv_sample,
            gmv_p[None], gmv_s.reshape(1, ns, 1, GM_WIDTH),
            mk.reshape(1, nb, N_MEM, XA_HEADS, XA_HEAD_DIM), mv.reshape(1, nb, N_MEM, XA_HEADS, XA_HEAD_DIM))
```

```python
import functools

import jax
import jax.numpy as jnp
from jax import lax
from jax.experimental import pallas as pl
from jax.experimental.pallas import tpu as pltpu
from jax.experimental.pallas import tpu_sc as plsc

F32 = jnp.float32
BF16 = jnp.bfloat16

D_MODEL = 1024
GM_WIDTH = 512
CV_WIDTH = 512
GM_HEADS = 4
GM_HEAD_DIM = 128
CHUNK = 128
CONV_WIDTH = 31
IN_COLS = 2 * GM_WIDTH + 2 * CV_WIDTH
N_MEM = 256
XA_HEADS = 4
XA_HEAD_DIM = 256
N_GROUPS = 4
EXPERTS_PER_GROUP = 8
EXPERT_FF = 256
EPS = 1e-6

LANES = 128
SUBLANES = 8
HALO = 32
PROMPT_TILE = 512
MOE_TILE = 1024
MOE_SUB = 128
ROUTE_TILES_PER_STEP = 2
FFN_BLOCK = 512
FFN_PER_STEP = 2
SC_WINDOW = 128
SAMPLE_ATTN_BLOCK = 8
ROUTER_ROWS = 128
VMEM_LIMIT = 56 * 1024 * 1024


def _rms(x, g):
    return x * lax.rsqrt(jnp.mean(x * x, axis=-1, keepdims=True) + EPS) * g


def _ln(x, g, b):
    mu = jnp.mean(x, axis=-1, keepdims=True)
    xc = x - mu
    var = jnp.mean(xc * xc, axis=-1, keepdims=True)
    return xc * lax.rsqrt(var + EPS) * g + b


def _dot(a, b):
    return jnp.dot(a, b, preferred_element_type=F32)


def _dot_nt(a, b):
    return lax.dot_general(a, b, (((1,), (1,)), ((), ())), preferred_element_type=F32)


def _dot_tn(a, b):
    return lax.dot_general(a, b, (((0,), (0,)), ((), ())), preferred_element_type=F32)


def _memkv_kernel(mem_ref, g_ref, wk_ref, wv_ref, k_ref, v_ref, kb_ref, vb_ref):
    mn = _rms(mem_ref[...], g_ref[...]).astype(BF16)
    k = _dot(mn, wk_ref[...].astype(BF16))
    v = _dot(mn, wv_ref[...].astype(BF16))
    for h in range(XA_HEADS):
        cols = slice(h * XA_HEAD_DIM, (h + 1) * XA_HEAD_DIM)
        k_ref[:, h, :] = k[:, cols]
        v_ref[:, h, :] = v[:, cols]
    kb_ref[...] = k.astype(BF16)
    vb_ref[...] = v.astype(BF16)


def _memkv(mem2d, g, wk, wv):
    rows = mem2d.shape[0]
    tile = 512
    row_spec = pl.BlockSpec((tile, D_MODEL), lambda i: (i, 0))
    head_spec = pl.BlockSpec((tile, XA_HEADS, XA_HEAD_DIM), lambda i: (i, 0, 0))
    full = lambda shape: pl.BlockSpec(shape, lambda i: (0,) * len(shape))
    return pl.pallas_call(
        _memkv_kernel,
        grid=(rows // tile,),
        in_specs=[row_spec, full((1, D_MODEL)), full((D_MODEL, D_MODEL)), full((D_MODEL, D_MODEL))],
        out_specs=[head_spec, head_spec, row_spec, row_spec],
        out_shape=[jax.ShapeDtypeStruct((rows, XA_HEADS, XA_HEAD_DIM), F32),
                   jax.ShapeDtypeStruct((rows, XA_HEADS, XA_HEAD_DIM), F32),
                   jax.ShapeDtypeStruct((rows, D_MODEL), BF16), jax.ShapeDtypeStruct((rows, D_MODEL), BF16)],
        compiler_params=pltpu.CompilerParams(dimension_semantics=("arbitrary",), vmem_limit_bytes=VMEM_LIMIT),
        name="memkv",
    )(mem2d, g, wk, wv)


def _conv_block(g_s, cw_ref, start, lanes):
    assert start % SUBLANES == 0
    rows = CHUNK + HALO
    win = g_s[start:start + rows, lanes]
    off = HALO - (CONV_WIDTH - 1)
    acc = None
    for b in range(SUBLANES):
        shifted = pltpu.roll(win, rows - (off + b), axis=0) if off + b else win
        for k in range(b, CONV_WIDTH, SUBLANES):
            assert k - b + CHUNK + off + b <= rows
            term = shifted[k - b:k - b + CHUNK] * cw_ref[k:k + 1, lanes]
            acc = term if acc is None else acc + term
    return acc


def _cast_in_copies(src_refs, in_bufs, sem_in, chunk, slot):
    copies = []
    for i, (src, ibuf) in enumerate(zip(src_refs, in_bufs)):
        rows = ibuf.shape[1]
        start = pl.multiple_of(chunk * rows, rows)
        copies.append(pltpu.make_async_copy(src.at[pl.ds(start, rows), :], ibuf.at[slot], sem_in.at[slot, i]))
    return copies


def _cast_out_copies(dst_refs, out_bufs, sem_out, chunk):
    copies = []
    for i, (dst, obuf) in enumerate(zip(dst_refs, out_bufs)):
        rows = obuf.shape[0]
        start = pl.multiple_of(chunk * rows, rows)
        copies.append(pltpu.make_async_copy(obuf, dst.at[pl.ds(start, rows), :], sem_out.at[i]))
    return copies


def _prompt_kernel(x_ref, gmix_ref, win_ref, glng_ref, glnb_ref, ws_ref, bst_ref, cw_ref, cb_ref,
                   clng_ref, clnb_ref, wout_ref, gxa_ref, wq_ref, kb_ref, vb_ref, wo_ref,
                   w1f_ref, w3f_ref, w2f_ref,
                   h_ref, conv_ref, gmv_ref, w1b_ref, w3b_ref, w2b_ref,
                   z_s, g_s, c_s, ab_s, o_s, y_prev, y_cur, in1, in3, in2, st1, st3, st2, sem_in, sem_out,
                   *, tiles_per_seq, n_cast_chunks):
    s = pl.program_id(0)
    n_steps = pl.num_programs(0)
    tb = x_ref.shape[0]
    n_chunks = tb // CHUNK

    srcs, dsts = (w1f_ref, w3f_ref, w2f_ref), (w1b_ref, w3b_ref, w2b_ref)
    ins, sts = (in1, in3, in2), (st1, st3, st2)
    slot = s % 2
    chunk_of = lambda step: jnp.minimum(step, n_cast_chunks - 1)

    @pl.when(s == 0)
    def _():
        for c in _cast_in_copies(srcs, ins, sem_in, chunk_of(s), slot):
            c.start()

    @pl.when(s + 1 < n_steps)
    def _():
        for c in _cast_in_copies(srcs, ins, sem_in, chunk_of(s + 1), 1 - slot):
            c.start()

    cast_in = _cast_in_copies(srcs, ins, sem_in, chunk_of(s), slot)
    cast_out = _cast_out_copies(dsts, sts, sem_out, chunk_of(s))
    for c in cast_in:
        c.wait()
    for ibuf, obuf in zip(ins, sts):
        obuf[...] = ibuf[slot].astype(BF16)
    for c in cast_out:
        c.start()

    @pl.when(s == 0)
    def _():
        y_prev[...] = jnp.zeros(y_prev.shape, F32)

    @pl.when(s % tiles_per_seq == 0)
    def _():
        g_s[0:HALO, :] = jnp.zeros((HALO, CV_WIDTH), F32)

    z_s[...] = _dot(_rms(x_ref[...], gmix_ref[...]).astype(BF16), win_ref[...])

    y = y_prev[...]
    qn = _rms(y, gxa_ref[...]).astype(BF16)
    q = (_dot(qn, wq_ref[...]) * (XA_HEAD_DIM ** -0.5)).astype(BF16)
    for h in range(XA_HEADS):
        cols = slice(h * XA_HEAD_DIM, (h + 1) * XA_HEAD_DIM)
        sc = _dot_nt(q[:, cols], kb_ref[:, cols])
        e = jnp.exp(sc - jnp.max(sc, axis=-1, keepdims=True))
        p = (e / jnp.sum(e, axis=-1, keepdims=True)).astype(BF16)
        o_s[:, cols] = _dot(p, vb_ref[:, cols]).astype(BF16)
    h_ref[...] = y + _dot(o_s[...], wo_ref[...])

    tri = (lax.broadcasted_iota(jnp.int32, (CHUNK, CHUNK), 0)
           >= lax.broadcasted_iota(jnp.int32, (CHUNK, CHUNK), 1))
    wm = [jnp.where(tri, ws_ref[h], 0.0).astype(BF16) for h in range(GM_HEADS)]

    for c in range(n_chunks):
        rows = slice(c * CHUNK, (c + 1) * CHUNK)
        for h in range(GM_HEADS):
            cu = slice(h * GM_HEAD_DIM, (h + 1) * GM_HEAD_DIM)
            cv = slice(GM_WIDTH + h * GM_HEAD_DIM, GM_WIDTH + (h + 1) * GM_HEAD_DIM)
            v = _ln(jax.nn.gelu(z_s[rows, cv]), glng_ref[:, cu], glnb_ref[:, cu])
            if c == n_chunks - 1:
                gmv_ref[:, cu] = v
            mixed = _dot(wm[h], v.astype(BF16)) + bst_ref[:, h:h + 1]
            ab_s[rows, cu] = (jax.nn.gelu(z_s[rows, cu]) * mixed).astype(BF16)
        ca = slice(2 * GM_WIDTH, 2 * GM_WIDTH + CV_WIDTH)
        cg = slice(2 * GM_WIDTH + CV_WIDTH, IN_COLS)
        g_s[HALO + c * CHUNK:HALO + (c + 1) * CHUNK, :] = z_s[rows, ca] * jax.nn.sigmoid(z_s[rows, cg])

    for c in range(n_chunks):
        rows = slice(c * CHUNK, (c + 1) * CHUNK)
        for cb in range(CV_WIDTH // LANES):
            lanes = slice(cb * LANES, (cb + 1) * LANES)
            c_s[rows, lanes] = _conv_block(g_s, cw_ref, c * CHUNK, lanes) + cb_ref[:, lanes]
        b = _ln(c_s[rows, :], clng_ref[...], clnb_ref[...])
        ab_s[rows, GM_WIDTH:] = (b * jax.nn.sigmoid(b)).astype(BF16)

    conv_ref[...] = g_s[tb:tb + HALO, :]
    g_s[0:HALO, :] = g_s[tb:tb + HALO, :]
    y_cur[...] = x_ref[...] + _dot(ab_s[...], wout_ref[...])
    y_prev[...] = y_cur[...]
    for c in cast_out:
        c.wait()


def _prompt_mix_attn(x, gmix, win, glng, glnb, ws, bst, cw, cb, clng, clnb, wout, gxa, wq, kb, vb, wo,
                     w1f, w3f, w2f):
    nb, seq, _ = x.shape
    tb = PROMPT_TILE
    tps = seq // tb
    last = nb * tps - 1
    n_cast = nb * tps
    r13, r2 = w1f.shape[0] // n_cast, w2f.shape[0] // n_cast
    assert w1f.shape == w3f.shape and w1f.shape[0] % n_cast == 0 and w2f.shape[0] % n_cast == 0
    assert r13 % 16 == 0 and r2 % 16 == 0
    hbm = pl.BlockSpec(memory_space=pl.ANY)
    tile = lambda s, lag: jnp.clip(s - lag, 0, last)
    full = lambda shape: pl.BlockSpec(shape, lambda s: (0,) * len(shape))
    rows = lambda lag: pl.BlockSpec((None, tb, D_MODEL), lambda s: (tile(s, lag) // tps, tile(s, lag) % tps, 0))
    kv_spec = pl.BlockSpec((None, N_MEM, D_MODEL), lambda s: (tile(s, 1) // tps, 0, 0))
    seq_out = lambda r, c: pl.BlockSpec((None, r, c), lambda s: (tile(s, 0) // tps, 0, 0))
    return pl.pallas_call(
        functools.partial(_prompt_kernel, tiles_per_seq=tps, n_cast_chunks=n_cast),
        grid=(nb * tps + 1,),
        in_specs=[rows(0), full((1, D_MODEL)), full((D_MODEL, IN_COLS)), full((1, GM_WIDTH)), full((1, GM_WIDTH)),
                  full((GM_HEADS, CHUNK, CHUNK)), full((CHUNK, GM_HEADS)), full((CONV_WIDTH, CV_WIDTH)),
                  full((1, CV_WIDTH)), full((1, CV_WIDTH)), full((1, CV_WIDTH)), full((D_MODEL, D_MODEL)),
                  full((1, D_MODEL)), full((D_MODEL, D_MODEL)), kv_spec, kv_spec, full((D_MODEL, D_MODEL)),
                  hbm, hbm, hbm],
        out_specs=[rows(1), seq_out(HALO, CV_WIDTH), seq_out(CHUNK, GM_WIDTH), hbm, hbm, hbm],
        out_shape=[jax.ShapeDtypeStruct((nb, seq, D_MODEL), F32),
                   jax.ShapeDtypeStruct((nb, HALO, CV_WIDTH), F32),
                   jax.ShapeDtypeStruct((nb, CHUNK, GM_WIDTH), F32),
                   jax.ShapeDtypeStruct(w1f.shape, BF16), jax.ShapeDtypeStruct(w3f.shape, BF16),
                   jax.ShapeDtypeStruct(w2f.shape, BF16)],
        scratch_shapes=[pltpu.VMEM((tb, IN_COLS), F32), pltpu.VMEM((HALO + tb, CV_WIDTH), F32),
                        pltpu.VMEM((tb, CV_WIDTH), F32), pltpu.VMEM((tb, D_MODEL), BF16),
                        pltpu.VMEM((tb, D_MODEL), BF16), pltpu.VMEM((tb, D_MODEL), F32),
                        pltpu.VMEM((tb, D_MODEL), F32),
                        pltpu.VMEM((2, r13, EXPERT_FF), F32), pltpu.VMEM((2, r13, EXPERT_FF), F32),
                        pltpu.VMEM((2, r2, D_MODEL), F32),
                        pltpu.VMEM((r13, EXPERT_FF), BF16), pltpu.VMEM((r13, EXPERT_FF), BF16),
                        pltpu.VMEM((r2, D_MODEL), BF16),
                        pltpu.SemaphoreType.DMA((2, 3)), pltpu.SemaphoreType.DMA((3,))],
        compiler_params=pltpu.CompilerParams(dimension_semantics=("arbitrary",), vmem_limit_bytes=VMEM_LIMIT),
        name="prompt_mix_attn",
    )(x, gmix, win, glng, glnb, ws, bst, cw, cb, clng, clnb, wout, gxa, wq, kb, vb, wo, w1f, w3f, w2f)


def _sample_mix_kernel(x_ref, gmix_ref, win_ref, glng_ref, glnb_ref, ws0_ref, bs0_ref, cache_ref, cw_ref, cb_ref,
                       clng_ref, clnb_ref, wout_ref, gxa_ref, wq_ref,
                       y_ref, conv_ref, v_ref, q_ref, ab_s):
    x = x_ref[...]
    z = _dot(_rms(x, gmix_ref[...]).astype(BF16), win_ref[...])
    for h in range(GM_HEADS):
        cu = slice(h * GM_HEAD_DIM, (h + 1) * GM_HEAD_DIM)
        cv = slice(GM_WIDTH + h * GM_HEAD_DIM, GM_WIDTH + (h + 1) * GM_HEAD_DIM)
        v = _ln(jax.nn.gelu(z[:, cv]), glng_ref[:, cu], glnb_ref[:, cu])
        v_ref[:, cu] = v
        ab_s[:, cu] = (jax.nn.gelu(z[:, cu]) * (v * ws0_ref[:, cu] + bs0_ref[:, cu])).astype(BF16)
    glu = z[:, 2 * GM_WIDTH:2 * GM_WIDTH + CV_WIDTH] * jax.nn.sigmoid(z[:, 2 * GM_WIDTH + CV_WIDTH:])
    conv_ref[0:CONV_WIDTH - 2] = cache_ref[1:CONV_WIDTH - 1]
    conv_ref[CONV_WIDTH - 2] = glu
    conv = glu * cw_ref[CONV_WIDTH - 1:CONV_WIDTH, :] + cb_ref[...]
    for k in range(CONV_WIDTH - 1):
        conv = conv + cache_ref[k] * cw_ref[k:k + 1, :]
    b = _ln(conv, clng_ref[...], clnb_ref[...])
    ab_s[:, GM_WIDTH:] = (b * jax.nn.sigmoid(b)).astype(BF16)
    y = x + _dot(ab_s[...], wout_ref[...])
    y_ref[...] = y
    q_ref[...] = _dot(_rms(y, gxa_ref[...]).astype(BF16), wq_ref[...]) * (XA_HEAD_DIM ** -0.5)


def _sample_mix(x, gmix, win, glng, glnb, ws0, bs0, cache_t, cw, cb, clng, clnb, wout, gxa, wq):
    ns = x.shape[0]
    return pl.pallas_call(
        _sample_mix_kernel,
        out_shape=[jax.ShapeDtypeStruct((ns, D_MODEL), F32), jax.ShapeDtypeStruct(cache_t.shape, F32),
                   jax.ShapeDtypeStruct((ns, GM_WIDTH), F32), jax.ShapeDtypeStruct((ns, D_MODEL), F32)],
        scratch_shapes=[pltpu.VMEM((ns, D_MODEL), BF16)],
        compiler_params=pltpu.CompilerParams(vmem_limit_bytes=VMEM_LIMIT),
        name="sample_mix",
    )(x, gmix, win, glng, glnb, ws0, bs0, cache_t, cw, cb, clng, clnb, wout, gxa, wq)


def _sample_attn_kernel(q_ref, k_ref, v_ref, o_ref):
    ones = jnp.ones((LANES, LANES), BF16)
    rows = N_MEM * SUBLANES
    for i in range(q_ref.shape[0]):
        prod = (k_ref[i] * q_ref[i][None]).reshape(rows, LANES).astype(BF16)
        part = _dot(prod, ones).reshape(N_MEM, SUBLANES, LANES)
        s = part + pltpu.roll(part, XA_HEADS, axis=1)
        e = jnp.exp(s - jnp.max(s, axis=0, keepdims=True))
        p = e / jnp.sum(e, axis=0, keepdims=True)
        o_ref[i] = jnp.sum(p * v_ref[i], axis=0)


def _split_heads(a):
    halves = XA_HEAD_DIM // LANES
    assert halves * XA_HEADS == SUBLANES
    lead = a.shape[:-2]
    a = a.reshape(*lead, XA_HEADS, halves, LANES)
    return jnp.swapaxes(a, -3, -2).reshape(*lead, SUBLANES, LANES)


def _merge_heads(o):
    ns = o.shape[0]
    o = jnp.swapaxes(o.reshape(ns, XA_HEAD_DIM // LANES, XA_HEADS, LANES), 1, 2)
    return o.reshape(ns, XA_HEADS * XA_HEAD_DIM)


def _sample_attn(q, k, v, first, count):
    nb = SAMPLE_ATTN_BLOCK
    assert first % nb == 0 and count % nb == 0
    off = first // nb
    return pl.pallas_call(
        _sample_attn_kernel,
        grid=(count // nb,),
        in_specs=[pl.BlockSpec((nb, SUBLANES, LANES), lambda i: (i + off, 0, 0)),
                  pl.BlockSpec((nb, N_MEM, SUBLANES, LANES), lambda i: (i + off, 0, 0, 0)),
                  pl.BlockSpec((nb, N_MEM, SUBLANES, LANES), lambda i: (i + off, 0, 0, 0))],
        out_specs=pl.BlockSpec((nb, SUBLANES, LANES), lambda i: (i, 0, 0)),
        out_shape=jax.ShapeDtypeStruct((count, SUBLANES, LANES), F32),
        compiler_params=pltpu.CompilerParams(dimension_semantics=("arbitrary",), vmem_limit_bytes=VMEM_LIMIT),
        name="sample_attn",
    )(q, k, v)


def _sample_proj_kernel(y_ref, o_ref, wo_ref, h_ref):
    h_ref[...] = y_ref[...] + _dot(o_ref[...].astype(BF16), wo_ref[...])


def _sample_proj(y, o, wo):
    return pl.pallas_call(
        _sample_proj_kernel,
        out_shape=jax.ShapeDtypeStruct(y.shape, F32),
        compiler_params=pltpu.CompilerParams(vmem_limit_bytes=VMEM_LIMIT),
        name="sample_proj",
    )(y, o, wo)


def _moe_kernel(h_ref, gffn_ref, wrt_ref, rb_ref, w1_ref, w3_ref, w2_ref, gfin_ref,
                o_ref, xt_s, u_s, gidx_s, slot_s, comb_s, act_s, p2_s, y2_s, cnt_s):
    i = pl.program_id(0)
    g = pl.program_id(1)
    rt = h_ref.shape[0]
    sb = act_s.shape[0]

    @pl.when(jnp.logical_and(i == 0, g == 0))
    def _():
        u_s[...] = (lax.broadcasted_iota(jnp.int32, (rt, rt), 0)
                    < lax.broadcasted_iota(jnp.int32, (rt, rt), 1)).astype(F32).astype(BF16)

    @pl.when(g == 0)
    def _route():
        h = h_ref[...]
        o_ref[...] = h
        xt = _rms(h, gffn_ref[...]).astype(BF16)
        xt_s[...] = xt
        lt = _dot_nt(wrt_ref[...], xt) + rb_ref[...]
        gl = [lt[k:k + 1, :] for k in range(N_GROUPS)]
        gmax = jnp.maximum(jnp.maximum(gl[0], gl[1]), jnp.maximum(gl[2], gl[3]))
        gidx = jnp.where(gl[0] == gmax, 0, jnp.where(gl[1] == gmax, 1, jnp.where(gl[2] == gmax, 2, 3)))
        gidx = gidx.astype(jnp.int32)
        sumexp = (jnp.exp(gl[0] - gmax) + jnp.exp(gl[1] - gmax)) + (jnp.exp(gl[2] - gmax) + jnp.exp(gl[3] - gmax))
        p_g = 1.0 / sumexp
        esel = lt[SUBLANES + 3 * EXPERTS_PER_GROUP:SUBLANES + 4 * EXPERTS_PER_GROUP, :]
        for k in (2, 1, 0):
            esel = jnp.where(gidx == k, lt[SUBLANES + k * EXPERTS_PER_GROUP:SUBLANES + (k + 1) * EXPERTS_PER_GROUP, :],
                             esel)
        eidx = lax.broadcasted_iota(jnp.int32, (EXPERTS_PER_GROUP, rt), 0)
        m1 = jnp.max(esel, axis=0, keepdims=True)
        i1 = jnp.min(jnp.where(esel == m1, eidx, EXPERTS_PER_GROUP), axis=0, keepdims=True)
        rest = jnp.where(eidx == i1, -jnp.inf, esel)
        m2 = jnp.max(rest, axis=0, keepdims=True)
        i2 = jnp.min(jnp.where(rest == m2, eidx, EXPERTS_PER_GROUP), axis=0, keepdims=True)
        t2 = jnp.exp(m2 - m1)
        den = 1.0 + t2
        w_top1 = (1.0 / den) * p_g
        w_top2 = (t2 / den) * p_g
        within = jnp.where(eidx == i1, w_top1, 0.0) + jnp.where(eidx == i2, w_top2, 0.0)
        c_hi = within.astype(BF16).astype(F32)
        r1 = within - c_hi
        c_mid = r1.astype(BF16).astype(F32)
        c_lo = (r1 - c_mid).astype(BF16).astype(F32)
        comb_s[0:8, :] = c_hi
        comb_s[8:16, :] = c_mid
        comb_s[16:24, :] = c_lo
        comb_s[24:32, :] = jnp.zeros((8, rt), F32)
        onehot = (eidx == gidx).astype(F32)
        rank = _dot(onehot.astype(BF16), u_s[...])
        slot_s[...] = jnp.sum(onehot * rank, axis=0, keepdims=True).astype(jnp.int32)
        gidx_s[...] = gidx
        for k in range(N_GROUPS):
            cnt_s[k] = jnp.sum(onehot[k:k + 1, :]).astype(jnp.int32)

    n_blk = (cnt_s[g] + sb - 1) // sb

    def sub_block(j, half):
        half_rows = slice(half * sb, (half + 1) * sb)
        rows = lax.broadcasted_iota(jnp.int32, (sb, rt), 0) + j * sb
        hit = jnp.logical_and(rows == slot_s[...], gidx_s[...] == g)
        p = jnp.where(hit, 1.0, 0.0).astype(BF16)
        p2_s[half_rows, :] = p
        xc = _dot(p, xt_s[...]).astype(BF16)
        cexp = _dot_nt(p, comb_s[...].astype(BF16))
        cw = (cexp[:, 0:8] + cexp[:, 8:16]) + cexp[:, 16:24]
        for e in range(EXPERTS_PER_GROUP):
            h1 = _dot(xc, w1_ref[e])
            h3 = _dot(xc, w3_ref[e])
            a = (h1 * jax.nn.sigmoid(h1)) * h3 * cw[:, e:e + 1]
            act_s[:, e * EXPERT_FF:(e + 1) * EXPERT_FF] = a.astype(BF16)
        y2_s[half_rows, :] = _dot(act_s[...], w2_ref[...]).astype(BF16)

    def body(jj, carry):
        sub_block(2 * jj, 0)

        @pl.when(2 * jj + 1 < n_blk)
        def _():
            sub_block(2 * jj + 1, 1)

        @pl.when(2 * jj + 1 >= n_blk)
        def _():
            p2_s[sb:, :] = jnp.zeros((sb, rt), BF16)
            y2_s[sb:, :] = jnp.zeros((sb, D_MODEL), BF16)

        o_ref[...] += _dot_tn(p2_s[...], y2_s[...])
        return carry

    lax.fori_loop(0, (n_blk + 1) // 2, body, 0)

    @pl.when(g == N_GROUPS - 1)
    def _():
        o_ref[...] = _rms(o_ref[...], gfin_ref[...])


def _moe(h, gffn, wrt, rbias, w1, w3, w2, gfin, *, tile):
    tokens = h.shape[0]
    sb = min(MOE_SUB, tile)
    ff = EXPERTS_PER_GROUP * EXPERT_FF
    full = lambda shape: pl.BlockSpec(shape, lambda i, g: (0,) * len(shape))
    row_spec = pl.BlockSpec((tile, D_MODEL), lambda i, g: (i, 0))
    return pl.pallas_call(
        _moe_kernel,
        grid=(tokens // tile, N_GROUPS),
        in_specs=[row_spec, full((1, D_MODEL)), full((ROUTER_ROWS, D_MODEL)), full((ROUTER_ROWS, 1)),
                  pl.BlockSpec((None, EXPERTS_PER_GROUP, D_MODEL, EXPERT_FF), lambda i, g: (g, 0, 0, 0)),
                  pl.BlockSpec((None, EXPERTS_PER_GROUP, D_MODEL, EXPERT_FF), lambda i, g: (g, 0, 0, 0)),
                  pl.BlockSpec((None, ff, D_MODEL), lambda i, g: (g, 0, 0)),
                  full((1, D_MODEL))],
        out_specs=row_spec,
        out_shape=jax.ShapeDtypeStruct((tokens, D_MODEL), F32),
        scratch_shapes=[pltpu.VMEM((tile, D_MODEL), BF16), pltpu.VMEM((tile, tile), BF16),
                        pltpu.VMEM((1, tile), jnp.int32), pltpu.VMEM((1, tile), jnp.int32),
                        pltpu.VMEM((4 * SUBLANES, tile), F32), pltpu.VMEM((sb, ff), BF16),
                        pltpu.VMEM((2 * sb, tile), BF16), pltpu.VMEM((2 * sb, D_MODEL), BF16),
                        pltpu.SMEM((N_GROUPS,), jnp.int32)],
        compiler_params=pltpu.CompilerParams(dimension_semantics=("arbitrary", "arbitrary"),
                                             vmem_limit_bytes=VMEM_LIMIT),
        name="moe",
    )(h, gffn, wrt, rbias, w1, w3, w2, gfin)


HALF = D_MODEL // 2
SUB_PER_ROW = HALF // LANES
N_EXPERTS = N_GROUPS * EXPERTS_PER_GROUP


def _pack_bf16_pairs(x):
    bits = pltpu.bitcast(x.astype(BF16).astype(F32), jnp.uint32)
    return (bits[:, HALF:] & jnp.uint32(0xFFFF0000)) | (bits[:, :HALF] >> 16)


def _unpack_bf16_pairs(w):
    lo = pltpu.bitcast(w << 16, F32)
    hi = pltpu.bitcast(w & jnp.uint32(0xFFFF0000), F32)
    return lo, hi


def _route_kernel(h_ref, gffn_ref, wrt_ref, rb_ref, xp_ref, e_ref, w_ref, r_ref, cnt_ref, u_s, run_s):
    i = pl.program_id(0)
    rt = u_s.shape[0]

    @pl.when(i == 0)
    def _():
        u_s[...] = (lax.broadcasted_iota(jnp.int32, (rt, rt), 0)
                    < lax.broadcasted_iota(jnp.int32, (rt, rt), 1)).astype(F32).astype(BF16)
        run_s[...] = jnp.zeros(run_s.shape, F32)

    for t in range(h_ref.shape[0] // rt):
        tok = pl.ds(t * rt, rt)
        _route_tile(h_ref.at[tok, :], gffn_ref, wrt_ref, rb_ref, xp_ref.at[:, tok, :], e_ref.at[:, tok],
                    w_ref.at[:, tok], r_ref.at[:, tok], u_s, run_s)
    cnt_ref[...] = run_s[...].astype(jnp.int32)


def _route_tile(h_ref, gffn_ref, wrt_ref, rb_ref, xp_ref, e_ref, w_ref, r_ref, u_s, run_s):
    rt = h_ref.shape[0]
    xt = _rms(h_ref[...], gffn_ref[...])
    _store_planes(xp_ref, _pack_bf16_pairs(xt))
    lt = _dot_nt(wrt_ref[...], xt.astype(BF16)) + rb_ref[...]
    gl = [lt[k:k + 1, :] for k in range(N_GROUPS)]
    gmax = jnp.maximum(jnp.maximum(gl[0], gl[1]), jnp.maximum(gl[2], gl[3]))
    gidx = jnp.where(gl[0] == gmax, 0, jnp.where(gl[1] == gmax, 1, jnp.where(gl[2] == gmax, 2, 3)))
    gidx = gidx.astype(jnp.int32)
    sumexp = (jnp.exp(gl[0] - gmax) + jnp.exp(gl[1] - gmax)) + (jnp.exp(gl[2] - gmax) + jnp.exp(gl[3] - gmax))
    p_g = 1.0 / sumexp
    esel = lt[SUBLANES + 3 * EXPERTS_PER_GROUP:SUBLANES + 4 * EXPERTS_PER_GROUP, :]
    for k in (2, 1, 0):
        esel = jnp.where(gidx == k, lt[SUBLANES + k * EXPERTS_PER_GROUP:SUBLANES + (k + 1) * EXPERTS_PER_GROUP, :],
                         esel)
    eidx = lax.broadcasted_iota(jnp.int32, (EXPERTS_PER_GROUP, rt), 0)
    m1 = jnp.max(esel, axis=0, keepdims=True)
    i1 = jnp.min(jnp.where(esel == m1, eidx, EXPERTS_PER_GROUP), axis=0, keepdims=True)
    rest = jnp.where(eidx == i1, -jnp.inf, esel)
    m2 = jnp.max(rest, axis=0, keepdims=True)
    i2 = jnp.min(jnp.where(rest == m2, eidx, EXPERTS_PER_GROUP), axis=0, keepdims=True)
    t2 = jnp.exp(m2 - m1)
    den = 1.0 + t2
    w_ref[...] = jnp.zeros(w_ref.shape, F32)
    w_ref[0:1, :] = (1.0 / den) * p_g
    w_ref[1:2, :] = (t2 / den) * p_g
    e1 = gidx * EXPERTS_PER_GROUP + i1
    e2 = gidx * EXPERTS_PER_GROUP + i2
    e_ref[0:1, :] = e1
    e_ref[1:2, :] = e2
    xid = lax.broadcasted_iota(jnp.int32, (N_EXPERTS, rt), 0)
    oh1 = (xid == e1).astype(F32)
    oh2 = (xid == e2).astype(F32)
    both = oh1 + oh2
    before = _dot(both.astype(BF16), u_s[...]) + run_s[:, 0:1]
    r_ref[0:1, :] = jnp.sum(oh1 * before, axis=0, keepdims=True).astype(jnp.int32)
    r_ref[1:2, :] = jnp.sum(oh2 * before, axis=0, keepdims=True).astype(jnp.int32)
    run_s[...] = run_s[...] + jnp.sum(both, axis=1, keepdims=True)


def _route(h, gffn, wrt, rbias, *, tile):
    tokens = h.shape[0]
    step = tile * ROUTE_TILES_PER_STEP
    assert tokens % step == 0
    full = lambda shape: pl.BlockSpec(shape, lambda i: (0,) * len(shape))
    lanes = lambda rows: pl.BlockSpec((rows, step), lambda i: (0, i))
    return pl.pallas_call(
        _route_kernel,
        grid=(tokens // step,),
        in_specs=[pl.BlockSpec((step, D_MODEL), lambda i: (i, 0)), full((1, D_MODEL)),
                  full((ROUTER_ROWS, D_MODEL)), full((ROUTER_ROWS, 1))],
        out_specs=[pl.BlockSpec((SUB_PER_ROW, step, LANES), lambda i: (0, i, 0)), lanes(2), lanes(SUBLANES), lanes(2),
                   full((N_EXPERTS, LANES))],
        out_shape=[jax.ShapeDtypeStruct((SUB_PER_ROW, tokens, LANES), jnp.uint32),
                   jax.ShapeDtypeStruct((2, tokens), jnp.int32),
                   jax.ShapeDtypeStruct((SUBLANES, tokens), F32), jax.ShapeDtypeStruct((2, tokens), jnp.int32),
                   jax.ShapeDtypeStruct((N_EXPERTS, LANES), jnp.int32)],
        scratch_shapes=[pltpu.VMEM((tile, tile), BF16), pltpu.VMEM((N_EXPERTS, LANES), F32)],
        compiler_params=pltpu.CompilerParams(dimension_semantics=("arbitrary",), vmem_limit_bytes=VMEM_LIMIT),
        name="moe_route",
    )(h, gffn, wrt, rbias)


def _store_planes(ref, words):
    for j in range(SUB_PER_ROW):
        ref[j] = words[:, j * LANES:(j + 1) * LANES]


def _load_planes(ref):
    return jnp.concatenate([ref[j] for j in range(SUB_PER_ROW)], axis=1)


def _sub_row_index(row_of_token, n_rows):
    plane = jnp.arange(SUB_PER_ROW, dtype=jnp.int32)[:, None] * n_rows
    return (row_of_token[None, :] + plane).reshape(1, -1)


def _sc_mesh():
    return plsc.VectorSubcoreMesh(core_axis_name="core", subcore_axis_name="subcore")


def _sc_scatter_two(x_sub, idx_a, idx_b, n_out):
    n_in = x_sub.shape[0]

    @pl.kernel(out_type=jax.ShapeDtypeStruct((n_out, LANES), x_sub.dtype), mesh=_sc_mesh(), scratch_types=[])
    def scatter(x_hbm, a_hbm, b_hbm, o_hbm):
        def body(x_vmem, a_vmem, b_vmem):
            pltpu.sync_copy(x_vmem, o_hbm.at[a_vmem.at[0]])
            pltpu.sync_copy(x_vmem, o_hbm.at[b_vmem.at[0]])

        pltpu.emit_pipeline(
            body, grid=(n_in // SC_WINDOW,),
            in_specs=[pl.BlockSpec((SC_WINDOW, LANES), lambda i: (i, 0)),
                      pl.BlockSpec((1, SC_WINDOW), lambda i: (0, i)),
                      pl.BlockSpec((1, SC_WINDOW), lambda i: (0, i))],
            out_specs=[],
            core_axis_name=("core", "subcore"), dimension_semantics=(pltpu.PARALLEL,),
        )(x_hbm, a_hbm, b_hbm)

    return scatter(x_sub, idx_a, idx_b)


def _sc_gather(table, idx):
    n_out = idx.shape[1]

    @pl.kernel(out_type=jax.ShapeDtypeStruct((n_out, LANES), table.dtype), mesh=_sc_mesh())
    def gather(t_hbm, i_hbm, o_hbm):
        def body(i_vmem, o_vmem):
            pltpu.sync_copy(t_hbm.at[i_vmem.at[0]], o_vmem)

        pltpu.emit_pipeline(
            body, grid=(n_out // SC_WINDOW,),
            in_specs=[pl.BlockSpec((1, SC_WINDOW), lambda i: (0, i))],
            out_specs=[pl.BlockSpec((SC_WINDOW, LANES), lambda i: (i, 0))],
            core_axis_name=("core", "subcore"), dimension_semantics=(pltpu.PARALLEL,),
        )(i_hbm, o_hbm)

    return gather(table, idx)


def _expert_ffn_kernel(blk_e_ref, n_valid_ref, x_ref, *refs):
    del blk_e_ref
    y_ref = refs[-1]
    for i in range(FFN_PER_STEP):
        w1_ref, w3_ref, w2_ref = refs[3 * i:3 * i + 3]
        rows = slice(i * FFN_BLOCK, (i + 1) * FFN_BLOCK)

        @pl.when(pl.program_id(0) * FFN_PER_STEP + i < n_valid_ref[0])
        def _(w1_ref=w1_ref, w3_ref=w3_ref, w2_ref=w2_ref, rows=rows):
            words = jnp.concatenate([x_ref[j, rows, :] for j in range(SUB_PER_ROW)], axis=1)
            lo, hi = _unpack_bf16_pairs(words)
            xc = jnp.concatenate([lo.astype(BF16), hi.astype(BF16)], axis=1)
            h1 = _dot(xc, w1_ref[...])
            h3 = _dot(xc, w3_ref[...])
            act = ((h1 * jax.nn.sigmoid(h1)) * h3).astype(BF16)
            packed = _pack_bf16_pairs(_dot(act, w2_ref[...]))
            for j in range(SUB_PER_ROW):
                y_ref[j, rows, :] = packed[:, j * LANES:(j + 1) * LANES]


def _expert_ffn(xs, blk_e, n_valid, w1, w3, w2):
    rows = xs.shape[1]
    step_rows = FFN_BLOCK * FFN_PER_STEP
    assert rows % step_rows == 0
    live = lambda s, be, nv: jnp.minimum(s, (nv[0] + FFN_PER_STEP - 1) // FFN_PER_STEP - 1)
    x_spec = pl.BlockSpec((SUB_PER_ROW, step_rows, LANES), lambda s, be, nv: (0, live(s, be, nv), 0))
    w_specs = []
    for i in range(FFN_PER_STEP):
        expert = lambda s, be, nv, i=i: (be[s * FFN_PER_STEP + i], 0, 0)
        w_specs += [pl.BlockSpec((None, D_MODEL, EXPERT_FF), expert), pl.BlockSpec((None, D_MODEL, EXPERT_FF), expert),
                    pl.BlockSpec((None, EXPERT_FF, D_MODEL), expert)]
    grid_spec = pltpu.PrefetchScalarGridSpec(
        num_scalar_prefetch=2, grid=(rows // step_rows,), in_specs=[x_spec] + w_specs, out_specs=x_spec)
    return pl.pallas_call(
        _expert_ffn_kernel, grid_spec=grid_spec,
        out_shape=jax.ShapeDtypeStruct((SUB_PER_ROW, rows, LANES), jnp.uint32),
        compiler_params=pltpu.CompilerParams(dimension_semantics=("arbitrary",), vmem_limit_bytes=VMEM_LIMIT),
        name="moe_ffn",
    )(blk_e, n_valid, xs, *([w1, w3, w2] * FFN_PER_STEP))


def _combine_kernel(h_ref, ya_ref, yb_ref, wt_ref, gfin_ref, o_ref):
    a_lo, a_hi = _unpack_bf16_pairs(_load_planes(ya_ref))
    b_lo, b_hi = _unpack_bf16_pairs(_load_planes(yb_ref))
    wt = jnp.transpose(wt_ref[...])
    wa = wt[:, 0:1]
    wb = wt[:, 1:2]
    o_ref[:, :HALF] = h_ref[:, :HALF] + (wa * a_lo + wb * b_lo)
    o_ref[:, HALF:] = h_ref[:, HALF:] + (wa * a_hi + wb * b_hi)
    o_ref[...] = _rms(o_ref[...], gfin_ref[...])


def _combine(h, ya, yb, wt, gfin, *, tile):
    tokens = h.shape[0]
    return pl.pallas_call(
        _combine_kernel,
        grid=(tokens // tile,),
        in_specs=[pl.BlockSpec((tile, D_MODEL), lambda i: (i, 0)),
                  pl.BlockSpec((SUB_PER_ROW, tile, LANES), lambda i: (0, i, 0)),
                  pl.BlockSpec((SUB_PER_ROW, tile, LANES), lambda i: (0, i, 0)), pl.BlockSpec((SUBLANES, tile), lambda i: (0, i)),
                  pl.BlockSpec((1, D_MODEL), lambda i: (0, 0))],
        out_specs=pl.BlockSpec((tile, D_MODEL), lambda i: (i, 0)),
        out_shape=jax.ShapeDtypeStruct((tokens, D_MODEL), F32),
        compiler_params=pltpu.CompilerParams(dimension_semantics=("arbitrary",), vmem_limit_bytes=VMEM_LIMIT),
        name="moe_combine",
    )(h, ya, yb, wt, gfin)


def _moe_dispatch(h, gffn, wrt, rbias):
    tokens = h.shape[0]
    xp, e12, w12, r12, cnt = _route(h, gffn, wrt, rbias, tile=MOE_TILE)
    count = cnt[:, 0]
    padded = (count + FFN_BLOCK - 1) // FFN_BLOCK * FFN_BLOCK
    end = jnp.cumsum(padded)
    start = end - padded
    n_rows = 2 * tokens + N_EXPERTS * FFN_BLOCK
    n_blocks = n_rows // FFN_BLOCK
    n_valid = (end[-1:] // FFN_BLOCK).astype(jnp.int32)
    first_row = jnp.minimum(jnp.arange(n_blocks, dtype=jnp.int32), n_valid - 1) * FFN_BLOCK
    blk_e = jnp.sum((end[None, :] <= first_row[:, None]).astype(jnp.int32), axis=1)
    experts = jnp.arange(N_EXPERTS, dtype=jnp.int32)
    start_of = jnp.sum(jnp.where(e12[:, :, None] == experts, start.astype(jnp.int32), 0), axis=-1)
    rows_ab = start_of + r12
    idx_a, idx_b = _sub_row_index(rows_ab[0], n_rows), _sub_row_index(rows_ab[1], n_rows)
    xs = _sc_scatter_two(xp.reshape(-1, LANES), idx_a, idx_b, n_rows * SUB_PER_ROW)
    return xs.reshape(SUB_PER_ROW, n_rows, LANES), (blk_e, n_valid, idx_a, idx_b, w12)


def _moe_finish(h, xs, plan, w1, w3, w2, gfin):
    blk_e, n_valid, idx_a, idx_b, w12 = plan
    ys = _expert_ffn(xs, blk_e, n_valid, w1, w3, w2).reshape(-1, LANES)
    ya = _sc_gather(ys, idx_a).reshape(SUB_PER_ROW, -1, LANES)
    yb = _sc_gather(ys, idx_b).reshape(SUB_PER_ROW, -1, LANES)
    return _combine(h, ya, yb, w12, gfin, tile=MOE_TILE)


def kernel(x_prompt, x_sample, mem_prompt, cache_conv, cache_mem_k, cache_mem_v, norm_mix_g, w_in, gm_ln_g, gm_ln_b, gm_ws, gm_bs, conv_w, conv_b, cv_ln_g, cv_ln_b, w_out, norm_mem_g, norm_xa_g, xa_wq, xa_wk, xa_wv, xa_wo, norm_ffn_g, router_g, router_g_b, router_e, router_e_b, exp_w1, exp_w3, exp_w2, final_norm_g):
    depth = w_in.shape[0]
    assert depth == 1, "single-layer trunk"
    nb, seq, _ = x_prompt.shape
    ns = x_sample.shape[0]
    row = lambda a: a.reshape(1, -1)

    l = 0
    gmix, gxa, gffn, gmem = row(norm_mix_g[l]), row(norm_xa_g[l]), row(norm_ffn_g[l]), row(norm_mem_g[l])
    gfin = row(final_norm_g)
    win, wout = w_in[l].astype(BF16), w_out[l].astype(BF16)
    wq, wo = xa_wq[l].astype(BF16), xa_wo[l].astype(BF16)
    wk, wv = xa_wk[l], xa_wv[l]
    glng, glnb = row(gm_ln_g[l]), row(gm_ln_b[l])
    cw, cb, clng, clnb = conv_w[l], row(conv_b[l]), row(cv_ln_g[l]), row(cv_ln_b[l])
    ws, bst = gm_ws[l], gm_bs[l].T
    ws0 = jnp.repeat(gm_ws[l][:, 0, 0], GM_HEAD_DIM).reshape(1, GM_WIDTH)
    bs0 = jnp.repeat(gm_bs[l][:, 0], GM_HEAD_DIM).reshape(1, GM_WIDTH)

    n_exp = N_GROUPS * EXPERTS_PER_GROUP
    pad_g = SUBLANES - N_GROUPS
    pad_t = ROUTER_ROWS - SUBLANES - n_exp
    wrt = jnp.concatenate([router_g[l].T, jnp.zeros((pad_g, D_MODEL), F32),
                           router_e[l].reshape(D_MODEL, n_exp).T, jnp.zeros((pad_t, D_MODEL), F32)], axis=0).astype(BF16)
    rbias = jnp.concatenate([router_g_b[l], jnp.zeros((pad_g,), F32), router_e_b[l].reshape(n_exp),
                             jnp.zeros((pad_t,), F32)]).reshape(ROUTER_ROWS, 1)

    mk, mv, kb, vb = _memkv(mem_prompt.reshape(nb * N_MEM, D_MODEL), gmem, wk, wv)
    kb, vb = kb.reshape(nb, N_MEM, D_MODEL), vb.reshape(nb, N_MEM, D_MODEL)
    cache_t = jnp.transpose(cache_conv[l], (1, 0, 2))
    ys, conv_t, gmv_s, q_s = _sample_mix(x_sample.reshape(ns, D_MODEL), gmix, win, glng, glnb, ws0, bs0,
                                        cache_t, cw, cb, clng, clnb, wout, gxa, wq)
    kb, q_s = lax.optimization_barrier((kb, q_s))
    hp, conv_tail, gmv_p, w1, w3, w2 = _prompt_mix_attn(
        x_prompt, gmix, win, glng, glnb, ws, bst, cw, cb, clng, clnb, wout, gxa, wq, kb, vb, wo,
        exp_w1[l].reshape(-1, EXPERT_FF), exp_w3[l].reshape(-1, EXPERT_FF), exp_w2[l].reshape(-1, D_MODEL))
    h2d = hp.reshape(nb * seq, D_MODEL)
    xs, plan = _moe_dispatch(h2d, gffn, wrt, rbias)
    w1e, w3e, w2e = (w1.reshape(N_EXPERTS, D_MODEL, EXPERT_FF), w3.reshape(N_EXPERTS, D_MODEL, EXPERT_FF),
                     w2.reshape(N_EXPERTS, EXPERT_FF, D_MODEL))
    w1 = w1.reshape(N_GROUPS, EXPERTS_PER_GROUP, D_MODEL, EXPERT_FF)
    w3 = w3.reshape(N_GROUPS, EXPERTS_PER_GROUP, D_MODEL, EXPERT_FF)
    w2 = w2.reshape(N_GROUPS, EXPERTS_PER_GROUP * EXPERT_FF, D_MODEL)

    qh, kh, vh =(_split_heads(q_s.reshape(ns, XA_HEADS, XA_HEAD_DIM)), _split_heads(cache_mem_k[l]),
                  _split_heads(cache_mem_v[l]))
    half = ns // 2
    o_first = _sample_attn(qh, kh, vh, 0, half)
    xs, o_first = lax.optimization_barrier((xs, o_first))
    y_prompt = _moe_finish(h2d, xs, plan, w1e, w3e, w2e, gfin)
    o_second = _sample_attn(qh, kh, vh, half, ns - half)
    o_s = _merge_heads(jnp.concatenate([o_first, o_second], axis=0))
    hs = _sample_proj(ys, o_s, wo)
    y_sample = _moe(hs, gffn, wrt, rbias, w1, w3, w2, gfin, tile=ns)

    conv_prompt = conv_tail[:, HALO - (CONV_WIDTH - 1):, :][None]
    conv_sample = jnp.transpose(conv_t, (1, 0, 2))[None]
    return (y_prompt.reshape(nb, seq, D_MODEL), y_sample.reshape(ns, 1, D_MODEL), conv_prompt, conv_sample,
            gmv_p[None], gmv_s.reshape(1, ns, 1, GM_WIDTH),
            mk.reshape(1, nb, N_MEM, XA_HEADS, XA_HEAD_DIM), mv.reshape(1, nb, N_MEM, XA_HEADS, XA_HEAD_DIM))
```

```python
import functools

import jax
import jax.numpy as jnp
from jax import lax
from jax.experimental import pallas as pl
from jax.experimental.pallas import tpu as pltpu
from jax.experimental.pallas import tpu_sc as plsc

F32 = jnp.float32
BF16 = jnp.bfloat16

D_MODEL = 1024
GM_WIDTH = 512
CV_WIDTH = 512
GM_HEADS = 4
GM_HEAD_DIM = 128
CHUNK = 128
CONV_WIDTH = 31
IN_COLS = 2 * GM_WIDTH + 2 * CV_WIDTH
N_MEM = 256
XA_HEADS = 4
XA_HEAD_DIM = 256
N_GROUPS = 4
EXPERTS_PER_GROUP = 8
EXPERT_FF = 256
EPS = 1e-6

LANES = 128
SUBLANES = 8
HALO = 32
PROMPT_TILE = 512
MOE_TILE = 1024
MOE_SUB = 128
ROUTE_TILES_PER_STEP = 2
FFN_BLOCK = 512
FFN_PER_STEP = 2
SC_WINDOW = 128
SAMPLE_ATTN_BLOCK = 4
KV_RING = 3
ROUTER_ROWS = 128
VMEM_LIMIT = 56 * 1024 * 1024


def _rms(x, g):
    return x * lax.rsqrt(jnp.mean(x * x, axis=-1, keepdims=True) + EPS) * g


def _ln(x, g, b):
    mu = jnp.mean(x, axis=-1, keepdims=True)
    xc = x - mu
    var = jnp.mean(xc * xc, axis=-1, keepdims=True)
    return xc * lax.rsqrt(var + EPS) * g + b


def _dot(a, b):
    return jnp.dot(a, b, preferred_element_type=F32)


def _dot_nt(a, b):
    return lax.dot_general(a, b, (((1,), (1,)), ((), ())), preferred_element_type=F32)


def _dot_tn(a, b):
    return lax.dot_general(a, b, (((0,), (0,)), ((), ())), preferred_element_type=F32)


def _memkv_kernel(mem_ref, g_ref, wk_ref, wv_ref, k_ref, v_ref, kb_ref, vb_ref):
    mn = _rms(mem_ref[...], g_ref[...]).astype(BF16)
    k = _dot(mn, wk_ref[...].astype(BF16))
    v = _dot(mn, wv_ref[...].astype(BF16))
    for h in range(XA_HEADS):
        cols = slice(h * XA_HEAD_DIM, (h + 1) * XA_HEAD_DIM)
        k_ref[:, h, :] = k[:, cols]
        v_ref[:, h, :] = v[:, cols]
    kb_ref[...] = k.astype(BF16)
    vb_ref[...] = v.astype(BF16)


def _memkv(mem2d, g, wk, wv):
    rows = mem2d.shape[0]
    tile = 512
    row_spec = pl.BlockSpec((tile, D_MODEL), lambda i: (i, 0))
    head_spec = pl.BlockSpec((tile, XA_HEADS, XA_HEAD_DIM), lambda i: (i, 0, 0))
    full = lambda shape: pl.BlockSpec(shape, lambda i: (0,) * len(shape))
    return pl.pallas_call(
        _memkv_kernel,
        grid=(rows // tile,),
        in_specs=[row_spec, full((1, D_MODEL)), full((D_MODEL, D_MODEL)), full((D_MODEL, D_MODEL))],
        out_specs=[head_spec, head_spec, row_spec, row_spec],
        out_shape=[jax.ShapeDtypeStruct((rows, XA_HEADS, XA_HEAD_DIM), F32),
                   jax.ShapeDtypeStruct((rows, XA_HEADS, XA_HEAD_DIM), F32),
                   jax.ShapeDtypeStruct((rows, D_MODEL), BF16), jax.ShapeDtypeStruct((rows, D_MODEL), BF16)],
        compiler_params=pltpu.CompilerParams(dimension_semantics=("arbitrary",), vmem_limit_bytes=VMEM_LIMIT),
        name="memkv",
    )(mem2d, g, wk, wv)


def _conv_block(g_s, cw_ref, start, lanes):
    assert start % SUBLANES == 0
    rows = CHUNK + HALO
    win = g_s[start:start + rows, lanes]
    off = HALO - (CONV_WIDTH - 1)
    acc = None
    for b in range(SUBLANES):
        shifted = pltpu.roll(win, rows - (off + b), axis=0) if off + b else win
        for k in range(b, CONV_WIDTH, SUBLANES):
            assert k - b + CHUNK + off + b <= rows
            term = shifted[k - b:k - b + CHUNK] * cw_ref[k:k + 1, lanes]
            acc = term if acc is None else acc + term
    return acc


def _cast_in_copies(src_refs, in_bufs, sem_in, chunk, slot):
    copies = []
    for i, (src, ibuf) in enumerate(zip(src_refs, in_bufs)):
        rows = ibuf.shape[1]
        start = pl.multiple_of(chunk * rows, rows)
        copies.append(pltpu.make_async_copy(src.at[pl.ds(start, rows), :], ibuf.at[slot], sem_in.at[slot, i]))
    return copies


def _cast_out_copies(dst_refs, out_bufs, sem_out, chunk):
    copies = []
    for i, (dst, obuf) in enumerate(zip(dst_refs, out_bufs)):
        rows = obuf.shape[0]
        start = pl.multiple_of(chunk * rows, rows)
        copies.append(pltpu.make_async_copy(obuf, dst.at[pl.ds(start, rows), :], sem_out.at[i]))
    return copies


def _prompt_kernel(x_ref, gmix_ref, win_ref, glng_ref, glnb_ref, ws_ref, bst_ref, cw_ref, cb_ref,
                   clng_ref, clnb_ref, wout_ref, gxa_ref, wq_ref, kb_ref, vb_ref, wo_ref,
                   w1f_ref, w3f_ref, w2f_ref,
                   h_ref, conv_ref, gmv_ref, w1b_ref, w3b_ref, w2b_ref,
                   z_s, g_s, c_s, ab_s, o_s, y_prev, y_cur, in1, in3, in2, st1, st3, st2, sem_in, sem_out,
                   *, tiles_per_seq, n_cast_chunks):
    s = pl.program_id(0)
    n_steps = pl.num_programs(0)
    tb = x_ref.shape[0]
    n_chunks = tb // CHUNK

    srcs, dsts = (w1f_ref, w3f_ref, w2f_ref), (w1b_ref, w3b_ref, w2b_ref)
    ins, sts = (in1, in3, in2), (st1, st3, st2)
    slot = s % 2
    chunk_of = lambda step: jnp.minimum(step, n_cast_chunks - 1)

    @pl.when(s == 0)
    def _():
        for c in _cast_in_copies(srcs, ins, sem_in, chunk_of(s), slot):
            c.start()

    @pl.when(s + 1 < n_steps)
    def _():
        for c in _cast_in_copies(srcs, ins, sem_in, chunk_of(s + 1), 1 - slot):
            c.start()

    cast_in = _cast_in_copies(srcs, ins, sem_in, chunk_of(s), slot)
    cast_out = _cast_out_copies(dsts, sts, sem_out, chunk_of(s))
    for c in cast_in:
        c.wait()
    for ibuf, obuf in zip(ins, sts):
        obuf[...] = ibuf[slot].astype(BF16)
    for c in cast_out:
        c.start()

    @pl.when(s == 0)
    def _():
        y_prev[...] = jnp.zeros(y_prev.shape, F32)

    @pl.when(s % tiles_per_seq == 0)
    def _():
        g_s[0:HALO, :] = jnp.zeros((HALO, CV_WIDTH), F32)

    z_s[...] = _dot(_rms(x_ref[...], gmix_ref[...]).astype(BF16), win_ref[...])

    y = y_prev[...]
    qn = _rms(y, gxa_ref[...]).astype(BF16)
    q = (_dot(qn, wq_ref[...]) * (XA_HEAD_DIM ** -0.5)).astype(BF16)
    for h in range(XA_HEADS):
        cols = slice(h * XA_HEAD_DIM, (h + 1) * XA_HEAD_DIM)
        sc = _dot_nt(q[:, cols], kb_ref[:, cols])
        e = jnp.exp(sc - jnp.max(sc, axis=-1, keepdims=True))
        p = (e / jnp.sum(e, axis=-1, keepdims=True)).astype(BF16)
        o_s[:, cols] = _dot(p, vb_ref[:, cols]).astype(BF16)
    h_ref[...] = y + _dot(o_s[...], wo_ref[...])

    tri = (lax.broadcasted_iota(jnp.int32, (CHUNK, CHUNK), 0)
           >= lax.broadcasted_iota(jnp.int32, (CHUNK, CHUNK), 1))
    wm = [jnp.where(tri, ws_ref[h], 0.0).astype(BF16) for h in range(GM_HEADS)]

    for c in range(n_chunks):
        rows = slice(c * CHUNK, (c + 1) * CHUNK)
        for h in range(GM_HEADS):
            cu = slice(h * GM_HEAD_DIM, (h + 1) * GM_HEAD_DIM)
            cv = slice(GM_WIDTH + h * GM_HEAD_DIM, GM_WIDTH + (h + 1) * GM_HEAD_DIM)
            v = _ln(jax.nn.gelu(z_s[rows, cv]), glng_ref[:, cu], glnb_ref[:, cu])
            if c == n_chunks - 1:
                gmv_ref[:, cu] = v
            mixed = _dot(wm[h], v.astype(BF16)) + bst_ref[:, h:h + 1]
            ab_s[rows, cu] = (jax.nn.gelu(z_s[rows, cu]) * mixed).astype(BF16)
        ca = slice(2 * GM_WIDTH, 2 * GM_WIDTH + CV_WIDTH)
        cg = slice(2 * GM_WIDTH + CV_WIDTH, IN_COLS)
        g_s[HALO + c * CHUNK:HALO + (c + 1) * CHUNK, :] = z_s[rows, ca] * jax.nn.sigmoid(z_s[rows, cg])

    for c in range(n_chunks):
        rows = slice(c * CHUNK, (c + 1) * CHUNK)
        for cb in range(CV_WIDTH // LANES):
            lanes = slice(cb * LANES, (cb + 1) * LANES)
            c_s[rows, lanes] = _conv_block(g_s, cw_ref, c * CHUNK, lanes) + cb_ref[:, lanes]
        b = _ln(c_s[rows, :], clng_ref[...], clnb_ref[...])
        ab_s[rows, GM_WIDTH:] = (b * jax.nn.sigmoid(b)).astype(BF16)

    conv_ref[...] = g_s[tb:tb + HALO, :]
    g_s[0:HALO, :] = g_s[tb:tb + HALO, :]
    y_cur[...] = x_ref[...] + _dot(ab_s[...], wout_ref[...])
    y_prev[...] = y_cur[...]
    for c in cast_out:
        c.wait()


def _prompt_mix_attn(x, gmix, win, glng, glnb, ws, bst, cw, cb, clng, clnb, wout, gxa, wq, kb, vb, wo,
                     w1f, w3f, w2f):
    nb, seq, _ = x.shape
    tb = PROMPT_TILE
    tps = seq // tb
    last = nb * tps - 1
    n_cast = nb * tps
    r13, r2 = w1f.shape[0] // n_cast, w2f.shape[0] // n_cast
    assert w1f.shape == w3f.shape and w1f.shape[0] % n_cast == 0 and w2f.shape[0] % n_cast == 0
    assert r13 % 16 == 0 and r2 % 16 == 0
    hbm = pl.BlockSpec(memory_space=pl.ANY)
    tile = lambda s, lag: jnp.clip(s - lag, 0, last)
    full = lambda shape: pl.BlockSpec(shape, lambda s: (0,) * len(shape))
    rows = lambda lag: pl.BlockSpec((None, tb, D_MODEL), lambda s: (tile(s, lag) // tps, tile(s, lag) % tps, 0))
    kv_spec = pl.BlockSpec((None, N_MEM, D_MODEL), lambda s: (tile(s, 1) // tps, 0, 0))
    seq_out = lambda r, c: pl.BlockSpec((None, r, c), lambda s: (tile(s, 0) // tps, 0, 0))
    return pl.pallas_call(
        functools.partial(_prompt_kernel, tiles_per_seq=tps, n_cast_chunks=n_cast),
        grid=(nb * tps + 1,),
        in_specs=[rows(0), full((1, D_MODEL)), full((D_MODEL, IN_COLS)), full((1, GM_WIDTH)), full((1, GM_WIDTH)),
                  full((GM_HEADS, CHUNK, CHUNK)), full((CHUNK, GM_HEADS)), full((CONV_WIDTH, CV_WIDTH)),
                  full((1, CV_WIDTH)), full((1, CV_WIDTH)), full((1, CV_WIDTH)), full((D_MODEL, D_MODEL)),
                  full((1, D_MODEL)), full((D_MODEL, D_MODEL)), kv_spec, kv_spec, full((D_MODEL, D_MODEL)),
                  hbm, hbm, hbm],
        out_specs=[rows(1), seq_out(HALO, CV_WIDTH), seq_out(CHUNK, GM_WIDTH), hbm, hbm, hbm],
        out_shape=[jax.ShapeDtypeStruct((nb, seq, D_MODEL), F32),
                   jax.ShapeDtypeStruct((nb, HALO, CV_WIDTH), F32),
                   jax.ShapeDtypeStruct((nb, CHUNK, GM_WIDTH), F32),
                   jax.ShapeDtypeStruct(w1f.shape, BF16), jax.ShapeDtypeStruct(w3f.shape, BF16),
                   jax.ShapeDtypeStruct(w2f.shape, BF16)],
        scratch_shapes=[pltpu.VMEM((tb, IN_COLS), F32), pltpu.VMEM((HALO + tb, CV_WIDTH), F32),
                        pltpu.VMEM((tb, CV_WIDTH), F32), pltpu.VMEM((tb, D_MODEL), BF16),
                        pltpu.VMEM((tb, D_MODEL), BF16), pltpu.VMEM((tb, D_MODEL), F32),
                        pltpu.VMEM((tb, D_MODEL), F32),
                        pltpu.VMEM((2, r13, EXPERT_FF), F32), pltpu.VMEM((2, r13, EXPERT_FF), F32),
                        pltpu.VMEM((2, r2, D_MODEL), F32),
                        pltpu.VMEM((r13, EXPERT_FF), BF16), pltpu.VMEM((r13, EXPERT_FF), BF16),
                        pltpu.VMEM((r2, D_MODEL), BF16),
                        pltpu.SemaphoreType.DMA((2, 3)), pltpu.SemaphoreType.DMA((3,))],
        compiler_params=pltpu.CompilerParams(dimension_semantics=("arbitrary",), vmem_limit_bytes=VMEM_LIMIT),
        name="prompt_mix_attn",
    )(x, gmix, win, glng, glnb, ws, bst, cw, cb, clng, clnb, wout, gxa, wq, kb, vb, wo, w1f, w3f, w2f)


def _sample_mix_kernel(x_ref, gmix_ref, win_ref, glng_ref, glnb_ref, ws0_ref, bs0_ref, cache_ref, cw_ref, cb_ref,
                       clng_ref, clnb_ref, wout_ref, gxa_ref, wq_ref,
                       y_ref, conv_ref, v_ref, q_ref, ab_s):
    x = x_ref[...]
    z = _dot(_rms(x, gmix_ref[...]).astype(BF16), win_ref[...])
    for h in range(GM_HEADS):
        cu = slice(h * GM_HEAD_DIM, (h + 1) * GM_HEAD_DIM)
        cv = slice(GM_WIDTH + h * GM_HEAD_DIM, GM_WIDTH + (h + 1) * GM_HEAD_DIM)
        v = _ln(jax.nn.gelu(z[:, cv]), glng_ref[:, cu], glnb_ref[:, cu])
        v_ref[:, cu] = v
        ab_s[:, cu] = (jax.nn.gelu(z[:, cu]) * (v * ws0_ref[:, cu] + bs0_ref[:, cu])).astype(BF16)
    glu = z[:, 2 * GM_WIDTH:2 * GM_WIDTH + CV_WIDTH] * jax.nn.sigmoid(z[:, 2 * GM_WIDTH + CV_WIDTH:])
    conv_ref[0:CONV_WIDTH - 2] = cache_ref[1:CONV_WIDTH - 1]
    conv_ref[CONV_WIDTH - 2] = glu
    conv = glu * cw_ref[CONV_WIDTH - 1:CONV_WIDTH, :] + cb_ref[...]
    for k in range(CONV_WIDTH - 1):
        conv = conv + cache_ref[k] * cw_ref[k:k + 1, :]
    b = _ln(conv, clng_ref[...], clnb_ref[...])
    ab_s[:, GM_WIDTH:] = (b * jax.nn.sigmoid(b)).astype(BF16)
    y = x + _dot(ab_s[...], wout_ref[...])
    y_ref[...] = y
    q_ref[...] = _dot(_rms(y, gxa_ref[...]).astype(BF16), wq_ref[...]) * (XA_HEAD_DIM ** -0.5)


def _sample_mix(x, gmix, win, glng, glnb, ws0, bs0, cache_t, cw, cb, clng, clnb, wout, gxa, wq):
    ns = x.shape[0]
    return pl.pallas_call(
        _sample_mix_kernel,
        out_shape=[jax.ShapeDtypeStruct((ns, D_MODEL), F32), jax.ShapeDtypeStruct(cache_t.shape, F32),
                   jax.ShapeDtypeStruct((ns, GM_WIDTH), F32), jax.ShapeDtypeStruct((ns, D_MODEL), F32)],
        scratch_shapes=[pltpu.VMEM((ns, D_MODEL), BF16)],
        compiler_params=pltpu.CompilerParams(vmem_limit_bytes=VMEM_LIMIT),
        name="sample_mix",
    )(x, gmix, win, glng, glnb, ws0, bs0, cache_t, cw, cb, clng, clnb, wout, gxa, wq)


def _kv_copies(k_hbm, v_hbm, k_buf, v_buf, sem, block, slot):
    nb = k_buf.shape[1]
    seqs = pl.ds(block * nb, nb)
    return (pltpu.make_async_copy(k_hbm.at[seqs], k_buf.at[slot], sem.at[slot, 0]),
            pltpu.make_async_copy(v_hbm.at[seqs], v_buf.at[slot], sem.at[slot, 1]))


def _sample_attn_kernel(q_ref, k_hbm, v_hbm, o_ref, k_buf, v_buf, sem, *, first_block, n_steps):
    s = pl.program_id(0)
    ahead = KV_RING - 1

    @pl.when(s == 0)
    def _():
        for j in range(min(ahead, n_steps)):
            for c in _kv_copies(k_hbm, v_hbm, k_buf, v_buf, sem, first_block + j, j):
                c.start()

    @pl.when(s + ahead < n_steps)
    def _():
        for c in _kv_copies(k_hbm, v_hbm, k_buf, v_buf, sem, first_block + s + ahead, (s + ahead) % KV_RING):
            c.start()

    slot = s % KV_RING
    for c in _kv_copies(k_hbm, v_hbm, k_buf, v_buf, sem, first_block + s, slot):
        c.wait()

    ones = jnp.ones((LANES, LANES), BF16)
    rows = N_MEM * SUBLANES
    for i in range(q_ref.shape[0]):
        prod = (k_buf[slot, i] * q_ref[i][None]).reshape(rows, LANES).astype(BF16)
        part = _dot(prod, ones).reshape(N_MEM, SUBLANES, LANES)
        sc = part + pltpu.roll(part, XA_HEADS, axis=1)
        e = jnp.exp(sc - jnp.max(sc, axis=0, keepdims=True))
        p = e / jnp.sum(e, axis=0, keepdims=True)
        o_ref[i] = jnp.sum(p * v_buf[slot, i], axis=0)


def _split_heads(a):
    halves = XA_HEAD_DIM // LANES
    assert halves * XA_HEADS == SUBLANES
    lead = a.shape[:-2]
    a = a.reshape(*lead, XA_HEADS, halves, LANES)
    return jnp.swapaxes(a, -3, -2).reshape(*lead, SUBLANES, LANES)


def _merge_heads(o):
    ns = o.shape[0]
    o = jnp.swapaxes(o.reshape(ns, XA_HEAD_DIM // LANES, XA_HEADS, LANES), 1, 2)
    return o.reshape(ns, XA_HEADS * XA_HEAD_DIM)


def _sample_attn(q, k, v, first, count):
    nb = SAMPLE_ATTN_BLOCK
    assert first % nb == 0 and count % nb == 0
    off = first // nb
    n_steps = count // nb
    hbm = pl.BlockSpec(memory_space=pl.ANY)
    return pl.pallas_call(
        functools.partial(_sample_attn_kernel, first_block=off, n_steps=n_steps),
        grid=(n_steps,),
        in_specs=[pl.BlockSpec((nb, SUBLANES, LANES), lambda i: (i + off, 0, 0)), hbm, hbm],
        out_specs=pl.BlockSpec((nb, SUBLANES, LANES), lambda i: (i, 0, 0)),
        out_shape=jax.ShapeDtypeStruct((count, SUBLANES, LANES), F32),
        scratch_shapes=[pltpu.VMEM((KV_RING, nb, N_MEM, SUBLANES, LANES), F32),
                        pltpu.VMEM((KV_RING, nb, N_MEM, SUBLANES, LANES), F32),
                        pltpu.SemaphoreType.DMA((KV_RING, 2))],
        compiler_params=pltpu.CompilerParams(dimension_semantics=("arbitrary",), vmem_limit_bytes=VMEM_LIMIT),
        name="sample_attn",
    )(q, k, v)


def _sample_proj_kernel(y_ref, o_ref, wo_ref, h_ref):
    h_ref[...] = y_ref[...] + _dot(o_ref[...].astype(BF16), wo_ref[...])


def _sample_proj(y, o, wo):
    return pl.pallas_call(
        _sample_proj_kernel,
        out_shape=jax.ShapeDtypeStruct(y.shape, F32),
        compiler_params=pltpu.CompilerParams(vmem_limit_bytes=VMEM_LIMIT),
        name="sample_proj",
    )(y, o, wo)


def _moe_kernel(h_ref, gffn_ref, wrt_ref, rb_ref, w1_ref, w3_ref, w2_ref, gfin_ref,
                o_ref, xt_s, u_s, gidx_s, slot_s, comb_s, act_s, p2_s, y2_s, cnt_s):
    i = pl.program_id(0)
    g = pl.program_id(1)
    rt = h_ref.shape[0]
    sb = act_s.shape[0]

    @pl.when(jnp.logical_and(i == 0, g == 0))
    def _():
        u_s[...] = (lax.broadcasted_iota(jnp.int32, (rt, rt), 0)
                    < lax.broadcasted_iota(jnp.int32, (rt, rt), 1)).astype(F32).astype(BF16)

    @pl.when(g == 0)
    def _route():
        h = h_ref[...]
        o_ref[...] = h
        xt = _rms(h, gffn_ref[...]).astype(BF16)
        xt_s[...] = xt
        lt = _dot_nt(wrt_ref[...], xt) + rb_ref[...]
        gl = [lt[k:k + 1, :] for k in range(N_GROUPS)]
        gmax = jnp.maximum(jnp.maximum(gl[0], gl[1]), jnp.maximum(gl[2], gl[3]))
        gidx = jnp.where(gl[0] == gmax, 0, jnp.where(gl[1] == gmax, 1, jnp.where(gl[2] == gmax, 2, 3)))
        gidx = gidx.astype(jnp.int32)
        sumexp = (jnp.exp(gl[0] - gmax) + jnp.exp(gl[1] - gmax)) + (jnp.exp(gl[2] - gmax) + jnp.exp(gl[3] - gmax))
        p_g = 1.0 / sumexp
        esel = lt[SUBLANES + 3 * EXPERTS_PER_GROUP:SUBLANES + 4 * EXPERTS_PER_GROUP, :]
        for k in (2, 1, 0):
            esel = jnp.where(gidx == k, lt[SUBLANES + k * EXPERTS_PER_GROUP:SUBLANES + (k + 1) * EXPERTS_PER_GROUP, :],
                             esel)
        eidx = lax.broadcasted_iota(jnp.int32, (EXPERTS_PER_GROUP, rt), 0)
        m1 = jnp.max(esel, axis=0, keepdims=True)
        i1 = jnp.min(jnp.where(esel == m1, eidx, EXPERTS_PER_GROUP), axis=0, keepdims=True)
        rest = jnp.where(eidx == i1, -jnp.inf, esel)
        m2 = jnp.max(rest, axis=0, keepdims=True)
        i2 = jnp.min(jnp.where(rest == m2, eidx, EXPERTS_PER_GROUP), axis=0, keepdims=True)
        t2 = jnp.exp(m2 - m1)
        den = 1.0 + t2
        w_top1 = (1.0 / den) * p_g
        w_top2 = (t2 / den) * p_g
        within = jnp.where(eidx == i1, w_top1, 0.0) + jnp.where(eidx == i2, w_top2, 0.0)
        c_hi = within.astype(BF16).astype(F32)
        r1 = within - c_hi
        c_mid = r1.astype(BF16).astype(F32)
        c_lo = (r1 - c_mid).astype(BF16).astype(F32)
        comb_s[0:8, :] = c_hi
        comb_s[8:16, :] = c_mid
        comb_s[16:24, :] = c_lo
        comb_s[24:32, :] = jnp.zeros((8, rt), F32)
        onehot = (eidx == gidx).astype(F32)
        rank = _dot(onehot.astype(BF16), u_s[...])
        slot_s[...] = jnp.sum(onehot * rank, axis=0, keepdims=True).astype(jnp.int32)
        gidx_s[...] = gidx
        for k in range(N_GROUPS):
            cnt_s[k] = jnp.sum(onehot[k:k + 1, :]).astype(jnp.int32)

    n_blk = (cnt_s[g] + sb - 1) // sb

    def sub_block(j, half):
        half_rows = slice(half * sb, (half + 1) * sb)
        rows = lax.broadcasted_iota(jnp.int32, (sb, rt), 0) + j * sb
        hit = jnp.logical_and(rows == slot_s[...], gidx_s[...] == g)
        p = jnp.where(hit, 1.0, 0.0).astype(BF16)
        p2_s[half_rows, :] = p
        xc = _dot(p, xt_s[...]).astype(BF16)
        cexp = _dot_nt(p, comb_s[...].astype(BF16))
        cw = (cexp[:, 0:8] + cexp[:, 8:16]) + cexp[:, 16:24]
        for e in range(EXPERTS_PER_GROUP):
            h1 = _dot(xc, w1_ref[e])
            h3 = _dot(xc, w3_ref[e])
            a = (h1 * jax.nn.sigmoid(h1)) * h3 * cw[:, e:e + 1]
            act_s[:, e * EXPERT_FF:(e + 1) * EXPERT_FF] = a.astype(BF16)
        y2_s[half_rows, :] = _dot(act_s[...], w2_ref[...]).astype(BF16)

    def body(jj, carry):
        sub_block(2 * jj, 0)

        @pl.when(2 * jj + 1 < n_blk)
        def _():
            sub_block(2 * jj + 1, 1)

        @pl.when(2 * jj + 1 >= n_blk)
        def _():
            p2_s[sb:, :] = jnp.zeros((sb, rt), BF16)
            y2_s[sb:, :] = jnp.zeros((sb, D_MODEL), BF16)

        o_ref[...] += _dot_tn(p2_s[...], y2_s[...])
        return carry

    lax.fori_loop(0, (n_blk + 1) // 2, body, 0)

    @pl.when(g == N_GROUPS - 1)
    def _():
        o_ref[...] = _rms(o_ref[...], gfin_ref[...])


def _moe(h, gffn, wrt, rbias, w1, w3, w2, gfin, *, tile):
    tokens = h.shape[0]
    sb = min(MOE_SUB, tile)
    ff = EXPERTS_PER_GROUP * EXPERT_FF
    full = lambda shape: pl.BlockSpec(shape, lambda i, g: (0,) * len(shape))
    row_spec = pl.BlockSpec((tile, D_MODEL), lambda i, g: (i, 0))
    return pl.pallas_call(
        _moe_kernel,
        grid=(tokens // tile, N_GROUPS),
        in_specs=[row_spec, full((1, D_MODEL)), full((ROUTER_ROWS, D_MODEL)), full((ROUTER_ROWS, 1)),
                  pl.BlockSpec((None, EXPERTS_PER_GROUP, D_MODEL, EXPERT_FF), lambda i, g: (g, 0, 0, 0)),
                  pl.BlockSpec((None, EXPERTS_PER_GROUP, D_MODEL, EXPERT_FF), lambda i, g: (g, 0, 0, 0)),
                  pl.BlockSpec((None, ff, D_MODEL), lambda i, g: (g, 0, 0)),
                  full((1, D_MODEL))],
        out_specs=row_spec,
        out_shape=jax.ShapeDtypeStruct((tokens, D_MODEL), F32),
        scratch_shapes=[pltpu.VMEM((tile, D_MODEL), BF16), pltpu.VMEM((tile, tile), BF16),
                        pltpu.VMEM((1, tile), jnp.int32), pltpu.VMEM((1, tile), jnp.int32),
                        pltpu.VMEM((4 * SUBLANES, tile), F32), pltpu.VMEM((sb, ff), BF16),
                        pltpu.VMEM((2 * sb, tile), BF16), pltpu.VMEM((2 * sb, D_MODEL), BF16),
                        pltpu.SMEM((N_GROUPS,), jnp.int32)],
        compiler_params=pltpu.CompilerParams(dimension_semantics=("arbitrary", "arbitrary"),
                                             vmem_limit_bytes=VMEM_LIMIT),
        name="moe",
    )(h, gffn, wrt, rbias, w1, w3, w2, gfin)


HALF = D_MODEL // 2
SUB_PER_ROW = HALF // LANES
N_EXPERTS = N_GROUPS * EXPERTS_PER_GROUP


def _pack_bf16_pairs(x):
    bits = pltpu.bitcast(x.astype(BF16).astype(F32), jnp.uint32)
    return (bits[:, HALF:] & jnp.uint32(0xFFFF0000)) | (bits[:, :HALF] >> 16)


def _unpack_bf16_pairs(w):
    lo = pltpu.bitcast(w << 16, F32)
    hi = pltpu.bitcast(w & jnp.uint32(0xFFFF0000), F32)
    return lo, hi


def _route_kernel(h_ref, gffn_ref, wrt_ref, rb_ref, xp_ref, e_ref, w_ref, r_ref, cnt_ref, u_s, run_s):
    i = pl.program_id(0)
    rt = u_s.shape[0]

    @pl.when(i == 0)
    def _():
        u_s[...] = (lax.broadcasted_iota(jnp.int32, (rt, rt), 0)
                    < lax.broadcasted_iota(jnp.int32, (rt, rt), 1)).astype(F32).astype(BF16)
        run_s[...] = jnp.zeros(run_s.shape, F32)

    for t in range(h_ref.shape[0] // rt):
        tok = pl.ds(t * rt, rt)
        _route_tile(h_ref.at[tok, :], gffn_ref, wrt_ref, rb_ref, xp_ref.at[:, tok, :], e_ref.at[:, tok],
                    w_ref.at[:, tok], r_ref.at[:, tok], u_s, run_s)
    cnt_ref[...] = run_s[...].astype(jnp.int32)


def _route_tile(h_ref, gffn_ref, wrt_ref, rb_ref, xp_ref, e_ref, w_ref, r_ref, u_s, run_s):
    rt = h_ref.shape[0]
    xt = _rms(h_ref[...], gffn_ref[...])
    _store_planes(xp_ref, _pack_bf16_pairs(xt))
    lt = _dot_nt(wrt_ref[...], xt.astype(BF16)) + rb_ref[...]
    gl = [lt[k:k + 1, :] for k in range(N_GROUPS)]
    gmax = jnp.maximum(jnp.maximum(gl[0], gl[1]), jnp.maximum(gl[2], gl[3]))
    gidx = jnp.where(gl[0] == gmax, 0, jnp.where(gl[1] == gmax, 1, jnp.where(gl[2] == gmax, 2, 3)))
    gidx = gidx.astype(jnp.int32)
    sumexp = (jnp.exp(gl[0] - gmax) + jnp.exp(gl[1] - gmax)) + (jnp.exp(gl[2] - gmax) + jnp.exp(gl[3] - gmax))
    p_g = 1.0 / sumexp
    esel = lt[SUBLANES + 3 * EXPERTS_PER_GROUP:SUBLANES + 4 * EXPERTS_PER_GROUP, :]
    for k in (2, 1, 0):
        esel = jnp.where(gidx == k, lt[SUBLANES + k * EXPERTS_PER_GROUP:SUBLANES + (k + 1) * EXPERTS_PER_GROUP, :],
                         esel)
    eidx = lax.broadcasted_iota(jnp.int32, (EXPERTS_PER_GROUP, rt), 0)
    m1 = jnp.max(esel, axis=0, keepdims=True)
    i1 = jnp.min(jnp.where(esel == m1, eidx, EXPERTS_PER_GROUP), axis=0, keepdims=True)
    rest = jnp.where(eidx == i1, -jnp.inf, esel)
    m2 = jnp.max(rest, axis=0, keepdims=True)
    i2 = jnp.min(jnp.where(rest == m2, eidx, EXPERTS_PER_GROUP), axis=0, keepdims=True)
    t2 = jnp.exp(m2 - m1)
    den = 1.0 + t2
    w_ref[...] = jnp.zeros(w_ref.shape, F32)
    w_ref[0:1, :] = (1.0 / den) * p_g
    w_ref[1:2, :] = (t2 / den) * p_g
    e1 = gidx * EXPERTS_PER_GROUP + i1
    e2 = gidx * EXPERTS_PER_GROUP + i2
    e_ref[0:1, :] = e1
    e_ref[1:2, :] = e2
    xid = lax.broadcasted_iota(jnp.int32, (N_EXPERTS, rt), 0)
    oh1 = (xid == e1).astype(F32)
    oh2 = (xid == e2).astype(F32)
    both = oh1 + oh2
    before = _dot(both.astype(BF16), u_s[...]) + run_s[:, 0:1]
    r_ref[0:1, :] = jnp.sum(oh1 * before, axis=0, keepdims=True).astype(jnp.int32)
    r_ref[1:2, :] = jnp.sum(oh2 * before, axis=0, keepdims=True).astype(jnp.int32)
    run_s[...] = run_s[...] + jnp.sum(both, axis=1, keepdims=True)


def _route(h, gffn, wrt, rbias, *, tile):
    tokens = h.shape[0]
    step = tile * ROUTE_TILES_PER_STEP
    assert tokens % step == 0
    full = lambda shape: pl.BlockSpec(shape, lambda i: (0,) * len(shape))
    lanes = lambda rows: pl.BlockSpec((rows, step), lambda i: (0, i))
    return pl.pallas_call(
        _route_kernel,
        grid=(tokens // step,),
        in_specs=[pl.BlockSpec((step, D_MODEL), lambda i: (i, 0)), full((1, D_MODEL)),
                  full((ROUTER_ROWS, D_MODEL)), full((ROUTER_ROWS, 1))],
        out_specs=[pl.BlockSpec((SUB_PER_ROW, step, LANES), lambda i: (0, i, 0)), lanes(2), lanes(SUBLANES), lanes(2),
                   full((N_EXPERTS, LANES))],
        out_shape=[jax.ShapeDtypeStruct((SUB_PER_ROW, tokens, LANES), jnp.uint32),
                   jax.ShapeDtypeStruct((2, tokens), jnp.int32),
                   jax.ShapeDtypeStruct((SUBLANES, tokens), F32), jax.ShapeDtypeStruct((2, tokens), jnp.int32),
                   jax.ShapeDtypeStruct((N_EXPERTS, LANES), jnp.int32)],
        scratch_shapes=[pltpu.VMEM((tile, tile), BF16), pltpu.VMEM((N_EXPERTS, LANES), F32)],
        compiler_params=pltpu.CompilerParams(dimension_semantics=("arbitrary",), vmem_limit_bytes=VMEM_LIMIT),
        name="moe_route",
    )(h, gffn, wrt, rbias)


def _store_planes(ref, words):
    for j in range(SUB_PER_ROW):
        ref[j] = words[:, j * LANES:(j + 1) * LANES]


def _load_planes(ref):
    return jnp.concatenate([ref[j] for j in range(SUB_PER_ROW)], axis=1)


def _sub_row_index(row_of_token, n_rows):
    plane = jnp.arange(SUB_PER_ROW, dtype=jnp.int32)[:, None] * n_rows
    return (row_of_token[None, :] + plane).reshape(1, -1)


def _sc_mesh():
    return plsc.VectorSubcoreMesh(core_axis_name="core", subcore_axis_name="subcore")


def _sc_scatter_two(x_sub, idx_a, idx_b, n_out):
    n_in = x_sub.shape[0]

    @pl.kernel(out_type=jax.ShapeDtypeStruct((n_out, LANES), x_sub.dtype), mesh=_sc_mesh(), scratch_types=[])
    def scatter(x_hbm, a_hbm, b_hbm, o_hbm):
        def body(x_vmem, a_vmem, b_vmem):
            pltpu.sync_copy(x_vmem, o_hbm.at[a_vmem.at[0]])
            pltpu.sync_copy(x_vmem, o_hbm.at[b_vmem.at[0]])

        pltpu.emit_pipeline(
            body, grid=(n_in // SC_WINDOW,),
            in_specs=[pl.BlockSpec((SC_WINDOW, LANES), lambda i: (i, 0)),
                      pl.BlockSpec((1, SC_WINDOW), lambda i: (0, i)),
                      pl.BlockSpec((1, SC_WINDOW), lambda i: (0, i))],
            out_specs=[],
            core_axis_name=("core", "subcore"), dimension_semantics=(pltpu.PARALLEL,),
        )(x_hbm, a_hbm, b_hbm)

    return scatter(x_sub, idx_a, idx_b)


def _sc_gather(table, idx):
    n_out = idx.shape[1]

    @pl.kernel(out_type=jax.ShapeDtypeStruct((n_out, LANES), table.dtype), mesh=_sc_mesh())
    def gather(t_hbm, i_hbm, o_hbm):
        def body(i_vmem, o_vmem):
            pltpu.sync_copy(t_hbm.at[i_vmem.at[0]], o_vmem)

        pltpu.emit_pipeline(
            body, grid=(n_out // SC_WINDOW,),
            in_specs=[pl.BlockSpec((1, SC_WINDOW), lambda i: (0, i))],
            out_specs=[pl.BlockSpec((SC_WINDOW, LANES), lambda i: (i, 0))],
            core_axis_name=("core", "subcore"), dimension_semantics=(pltpu.PARALLEL,),
        )(i_hbm, o_hbm)

    return gather(table, idx)


def _expert_ffn_kernel(blk_e_ref, n_valid_ref, x_ref, *refs):
    del blk_e_ref
    y_ref = refs[-1]
    for i in range(FFN_PER_STEP):
        w1_ref, w3_ref, w2_ref = refs[3 * i:3 * i + 3]
        rows = slice(i * FFN_BLOCK, (i + 1) * FFN_BLOCK)

        @pl.when(pl.program_id(0) * FFN_PER_STEP + i < n_valid_ref[0])
        def _(w1_ref=w1_ref, w3_ref=w3_ref, w2_ref=w2_ref, rows=rows):
            words = jnp.concatenate([x_ref[j, rows, :] for j in range(SUB_PER_ROW)], axis=1)
            lo, hi = _unpack_bf16_pairs(words)
            xc = jnp.concatenate([lo.astype(BF16), hi.astype(BF16)], axis=1)
            h1 = _dot(xc, w1_ref[...])
            h3 = _dot(xc, w3_ref[...])
            act = ((h1 * jax.nn.sigmoid(h1)) * h3).astype(BF16)
            packed = _pack_bf16_pairs(_dot(act, w2_ref[...]))
            for j in range(SUB_PER_ROW):
                y_ref[j, rows, :] = packed[:, j * LANES:(j + 1) * LANES]


def _expert_ffn(xs, blk_e, n_valid, w1, w3, w2):
    rows = xs.shape[1]
    step_rows = FFN_BLOCK * FFN_PER_STEP
    assert rows % step_rows == 0
    live = lambda s, be, nv: jnp.minimum(s, (nv[0] + FFN_PER_STEP - 1) // FFN_PER_STEP - 1)
    x_spec = pl.BlockSpec((SUB_PER_ROW, step_rows, LANES), lambda s, be, nv: (0, live(s, be, nv), 0))
    w_specs = []
    for i in range(FFN_PER_STEP):
        expert = lambda s, be, nv, i=i: (be[s * FFN_PER_STEP + i], 0, 0)
        w_specs += [pl.BlockSpec((None, D_MODEL, EXPERT_FF), expert), pl.BlockSpec((None, D_MODEL, EXPERT_FF), expert),
                    pl.BlockSpec((None, EXPERT_FF, D_MODEL), expert)]
    grid_spec = pltpu.PrefetchScalarGridSpec(
        num_scalar_prefetch=2, grid=(rows // step_rows,), in_specs=[x_spec] + w_specs, out_specs=x_spec)
    return pl.pallas_call(
        _expert_ffn_kernel, grid_spec=grid_spec,
        out_shape=jax.ShapeDtypeStruct((SUB_PER_ROW, rows, LANES), jnp.uint32),
        compiler_params=pltpu.CompilerParams(dimension_semantics=("arbitrary",), vmem_limit_bytes=VMEM_LIMIT),
        name="moe_ffn",
    )(blk_e, n_valid, xs, *([w1, w3, w2] * FFN_PER_STEP))


def _combine_kernel(h_ref, ya_ref, yb_ref, wt_ref, gfin_ref, o_ref):
    a_lo, a_hi = _unpack_bf16_pairs(_load_planes(ya_ref))
    b_lo, b_hi = _unpack_bf16_pairs(_load_planes(yb_ref))
    wt = jnp.transpose(wt_ref[...])
    wa = wt[:, 0:1]
    wb = wt[:, 1:2]
    o_ref[:, :HALF] = h_ref[:, :HALF] + (wa * a_lo + wb * b_lo)
    o_ref[:, HALF:] = h_ref[:, HALF:] + (wa * a_hi + wb * b_hi)
    o_ref[...] = _rms(o_ref[...], gfin_ref[...])


def _combine(h, ya, yb, wt, gfin, *, tile):
    tokens = h.shape[0]
    return pl.pallas_call(
        _combine_kernel,
        grid=(tokens // tile,),
        in_specs=[pl.BlockSpec((tile, D_MODEL), lambda i: (i, 0)),
                  pl.BlockSpec((SUB_PER_ROW, tile, LANES), lambda i: (0, i, 0)),
                  pl.BlockSpec((SUB_PER_ROW, tile, LANES), lambda i: (0, i, 0)), pl.BlockSpec((SUBLANES, tile), lambda i: (0, i)),
                  pl.BlockSpec((1, D_MODEL), lambda i: (0, 0))],
        out_specs=pl.BlockSpec((tile, D_MODEL), lambda i: (i, 0)),
        out_shape=jax.ShapeDtypeStruct((tokens, D_MODEL), F32),
        compiler_params=pltpu.CompilerParams(dimension_semantics=("arbitrary",), vmem_limit_bytes=VMEM_LIMIT),
        name="moe_combine",
    )(h, ya, yb, wt, gfin)


def _moe_dispatch(h, gffn, wrt, rbias):
    tokens = h.shape[0]
    xp, e12, w12, r12, cnt = _route(h, gffn, wrt, rbias, tile=MOE_TILE)
    count = cnt[:, 0]
    padded = (count + FFN_BLOCK - 1) // FFN_BLOCK * FFN_BLOCK
    end = jnp.cumsum(padded)
    start = end - padded
    n_rows = 2 * tokens + N_EXPERTS * FFN_BLOCK
    n_blocks = n_rows // FFN_BLOCK
    n_valid = (end[-1:] // FFN_BLOCK).astype(jnp.int32)
    first_row = jnp.minimum(jnp.arange(n_blocks, dtype=jnp.int32), n_valid - 1) * FFN_BLOCK
    blk_e = jnp.sum((end[None, :] <= first_row[:, None]).astype(jnp.int32), axis=1)
    experts = jnp.arange(N_EXPERTS, dtype=jnp.int32)
    start_of = jnp.sum(jnp.where(e12[:, :, None] == experts, start.astype(jnp.int32), 0), axis=-1)
    rows_ab = start_of + r12
    idx_a, idx_b = _sub_row_index(rows_ab[0], n_rows), _sub_row_index(rows_ab[1], n_rows)
    xs = _sc_scatter_two(xp.reshape(-1, LANES), idx_a, idx_b, n_rows * SUB_PER_ROW)
    return xs.reshape(SUB_PER_ROW, n_rows, LANES), (blk_e, n_valid, idx_a, idx_b, w12)


def _moe_finish(h, xs, plan, w1, w3, w2, gfin):
    blk_e, n_valid, idx_a, idx_b, w12 = plan
    ys = _expert_ffn(xs, blk_e, n_valid, w1, w3, w2).reshape(-1, LANES)
    ya = _sc_gather(ys, idx_a).reshape(SUB_PER_ROW, -1, LANES)
    yb = _sc_gather(ys, idx_b).reshape(SUB_PER_ROW, -1, LANES)
    return _combine(h, ya, yb, w12, gfin, tile=MOE_TILE)


def kernel(x_prompt, x_sample, mem_prompt, cache_conv, cache_mem_k, cache_mem_v, norm_mix_g, w_in, gm_ln_g, gm_ln_b, gm_ws, gm_bs, conv_w, conv_b, cv_ln_g, cv_ln_b, w_out, norm_mem_g, norm_xa_g, xa_wq, xa_wk, xa_wv, xa_wo, norm_ffn_g, router_g, router_g_b, router_e, router_e_b, exp_w1, exp_w3, exp_w2, final_norm_g):
    depth = w_in.shape[0]
    assert depth == 1, "single-layer trunk"
    nb, seq, _ = x_prompt.shape
    ns = x_sample.shape[0]
    row = lambda a: a.reshape(1, -1)

    l = 0
    gmix, gxa, gffn, gmem = row(norm_mix_g[l]), row(norm_xa_g[l]), row(norm_ffn_g[l]), row(norm_mem_g[l])
    gfin = row(final_norm_g)
    win, wout = w_in[l].astype(BF16), w_out[l].astype(BF16)
    wq, wo = xa_wq[l].astype(BF16), xa_wo[l].astype(BF16)
    wk, wv = xa_wk[l], xa_wv[l]
    glng, glnb = row(gm_ln_g[l]), row(gm_ln_b[l])
    cw, cb, clng, clnb = conv_w[l], row(conv_b[l]), row(cv_ln_g[l]), row(cv_ln_b[l])
    ws, bst = gm_ws[l], gm_bs[l].T
    ws0 = jnp.repeat(gm_ws[l][:, 0, 0], GM_HEAD_DIM).reshape(1, GM_WIDTH)
    bs0 = jnp.repeat(gm_bs[l][:, 0], GM_HEAD_DIM).reshape(1, GM_WIDTH)

    n_exp = N_GROUPS * EXPERTS_PER_GROUP
    pad_g = SUBLANES - N_GROUPS
    pad_t = ROUTER_ROWS - SUBLANES - n_exp
    wrt = jnp.concatenate([router_g[l].T, jnp.zeros((pad_g, D_MODEL), F32),
                           router_e[l].reshape(D_MODEL, n_exp).T, jnp.zeros((pad_t, D_MODEL), F32)], axis=0).astype(BF16)
    rbias = jnp.concatenate([router_g_b[l], jnp.zeros((pad_g,), F32), router_e_b[l].reshape(n_exp),
                             jnp.zeros((pad_t,), F32)]).reshape(ROUTER_ROWS, 1)

    mk, mv, kb, vb = _memkv(mem_prompt.reshape(nb * N_MEM, D_MODEL), gmem, wk, wv)
    kb, vb = kb.reshape(nb, N_MEM, D_MODEL), vb.reshape(nb, N_MEM, D_MODEL)
    hp, conv_tail, gmv_p, w1, w3, w2 = _prompt_mix_attn(
        x_prompt, gmix, win, glng, glnb, ws, bst, cw, cb, clng, clnb, wout, gxa, wq, kb, vb, wo,
        exp_w1[l].reshape(-1, EXPERT_FF), exp_w3[l].reshape(-1, EXPERT_FF), exp_w2[l].reshape(-1, D_MODEL))
    h2d = hp.reshape(nb * seq, D_MODEL)
    xs, plan = _moe_dispatch(h2d, gffn, wrt, rbias)
    w1e, w3e, w2e = (w1.reshape(N_EXPERTS, D_MODEL, EXPERT_FF), w3.reshape(N_EXPERTS, D_MODEL, EXPERT_FF),
                     w2.reshape(N_EXPERTS, EXPERT_FF, D_MODEL))
    w1 = w1.reshape(N_GROUPS, EXPERTS_PER_GROUP, D_MODEL, EXPERT_FF)
    w3 = w3.reshape(N_GROUPS, EXPERTS_PER_GROUP, D_MODEL, EXPERT_FF)
    w2 = w2.reshape(N_GROUPS, EXPERTS_PER_GROUP * EXPERT_FF, D_MODEL)

    cache_t = jnp.transpose(cache_conv[l], (1, 0, 2))
    ys, conv_t, gmv_s, q_s = _sample_mix(x_sample.reshape(ns, D_MODEL), gmix, win, glng, glnb, ws0, bs0,
                                        cache_t, cw, cb, clng, clnb, wout, gxa, wq)
    qh, kh, vh = (_split_heads(q_s.reshape(ns, XA_HEADS, XA_HEAD_DIM)), _split_heads(cache_mem_k[l]),
                  _split_heads(cache_mem_v[l]))
    half = ns // 2
    o_first = _sample_attn(qh, kh, vh, 0, half)
    xs, o_first = lax.optimization_barrier((xs, o_first))
    y_prompt = _moe_finish(h2d, xs, plan, w1e, w3e, w2e, gfin)
    o_second = _sample_attn(qh, kh, vh, half, ns - half)
    o_s = _merge_heads(jnp.concatenate([o_first, o_second], axis=0))
    hs = _sample_proj(ys, o_s, wo)
    y_sample = _moe(hs, gffn, wrt, rbias, w1, w3, w2, gfin, tile=ns)

    conv_prompt = conv_tail[:, HALO - (CONV_WIDTH - 1):, :][None]
    conv_sample = jnp.transpose(conv_t, (1, 0, 2))[None]
    return (y_prompt.reshape(nb, seq, D_MODEL), y_sample.reshape(ns, 1, D_MODEL), conv_prompt, conv_sample,
            gmv_p[None], gmv_s.reshape(1, ns, 1, GM_WIDTH),
            mk.reshape(1, nb, N_MEM, XA_HEADS, XA_HEAD_DIM), mv.reshape(1, nb, N_MEM, XA_HEADS, XA_HEAD_DIM))
```

```python
import functools

import jax
import jax.numpy as jnp
from jax import lax
from jax.experimental import pallas as pl
from jax.experimental.pallas import tpu as pltpu
from jax.experimental.pallas import tpu_sc as plsc

F32 = jnp.float32
BF16 = jnp.bfloat16

D_MODEL = 1024
GM_WIDTH = 512
CV_WIDTH = 512
GM_HEADS = 4
GM_HEAD_DIM = 128
CHUNK = 128
CONV_WIDTH = 31
IN_COLS = 2 * GM_WIDTH + 2 * CV_WIDTH
N_MEM = 256
XA_HEADS = 4
XA_HEAD_DIM = 256
N_GROUPS = 4
EXPERTS_PER_GROUP = 8
EXPERT_FF = 256
EPS = 1e-6

LANES = 128
SUBLANES = 8
HALO = 32
PROMPT_TILE = 512
MOE_TILE = 1024
MOE_SUB = 128
ROUTE_TILES_PER_STEP = 2
FFN_BLOCK = 512
FFN_PER_STEP = 2
SC_WINDOW = 128
SAMPLE_ATTN_BLOCK = 4
KV_RING = 3
ROUTER_ROWS = 128
VMEM_LIMIT = 56 * 1024 * 1024


def _rms(x, g):
    return x * lax.rsqrt(jnp.mean(x * x, axis=-1, keepdims=True) + EPS) * g


def _ln(x, g, b):
    mu = jnp.mean(x, axis=-1, keepdims=True)
    xc = x - mu
    var = jnp.mean(xc * xc, axis=-1, keepdims=True)
    return xc * lax.rsqrt(var + EPS) * g + b


def _dot(a, b):
    return jnp.dot(a, b, preferred_element_type=F32)


def _dot_nt(a, b):
    return lax.dot_general(a, b, (((1,), (1,)), ((), ())), preferred_element_type=F32)


def _dot_tn(a, b):
    return lax.dot_general(a, b, (((0,), (0,)), ((), ())), preferred_element_type=F32)


def _memkv_kernel(mem_ref, g_ref, wk_ref, wv_ref, *refs):
    k_ref, v_ref, kb_ref, vb_ref = refs[-4:]
    n_dense = (len(refs) - 4) // 2
    for src, dst in zip(refs[:n_dense], refs[n_dense:2 * n_dense]):
        dst[...] = src[...].astype(BF16)
    mn = _rms(mem_ref[...], g_ref[...]).astype(BF16)
    k = _dot(mn, wk_ref[...].astype(BF16))
    v = _dot(mn, wv_ref[...].astype(BF16))
    for h in range(XA_HEADS):
        cols = slice(h * XA_HEAD_DIM, (h + 1) * XA_HEAD_DIM)
        k_ref[:, h, :] = k[:, cols]
        v_ref[:, h, :] = v[:, cols]
    kb_ref[...] = k.astype(BF16)
    vb_ref[...] = v.astype(BF16)


def _memkv(mem2d, g, wk, wv, dense):
    rows = mem2d.shape[0]
    tile = 512
    steps = rows // tile
    row_spec = pl.BlockSpec((tile, D_MODEL), lambda i: (i, 0))
    head_spec = pl.BlockSpec((tile, XA_HEADS, XA_HEAD_DIM), lambda i: (i, 0, 0))
    full = lambda shape: pl.BlockSpec(shape, lambda i: (0,) * len(shape))
    assert all(w.ndim == 2 and w.shape[0] % (16 * steps) == 0 for w in dense)
    slabs = [pl.BlockSpec((w.shape[0] // steps, w.shape[1]), lambda i: (i, 0)) for w in dense]
    return pl.pallas_call(
        _memkv_kernel,
        grid=(steps,),
        in_specs=[row_spec, full((1, D_MODEL)), full((D_MODEL, D_MODEL)), full((D_MODEL, D_MODEL))] + slabs,
        out_specs=slabs + [head_spec, head_spec, row_spec, row_spec],
        out_shape=[jax.ShapeDtypeStruct(w.shape, BF16) for w in dense] + [
                   jax.ShapeDtypeStruct((rows, XA_HEADS, XA_HEAD_DIM), F32),
                   jax.ShapeDtypeStruct((rows, XA_HEADS, XA_HEAD_DIM), F32),
                   jax.ShapeDtypeStruct((rows, D_MODEL), BF16), jax.ShapeDtypeStruct((rows, D_MODEL), BF16)],
        compiler_params=pltpu.CompilerParams(dimension_semantics=("arbitrary",), vmem_limit_bytes=VMEM_LIMIT),
        name="memkv",
    )(mem2d, g, wk, wv, *dense)


def _conv_block(g_s, cw_ref, start, lanes):
    assert start % SUBLANES == 0
    rows = CHUNK + HALO
    win = g_s[start:start + rows, lanes]
    off = HALO - (CONV_WIDTH - 1)
    acc = None
    for b in range(SUBLANES):
        shifted = pltpu.roll(win, rows - (off + b), axis=0) if off + b else win
        for k in range(b, CONV_WIDTH, SUBLANES):
            assert k - b + CHUNK + off + b <= rows
            term = shifted[k - b:k - b + CHUNK] * cw_ref[k:k + 1, lanes]
            acc = term if acc is None else acc + term
    return acc


def _cast_in_copies(src_refs, in_bufs, sem_in, chunk, slot):
    copies = []
    for i, (src, ibuf) in enumerate(zip(src_refs, in_bufs)):
        rows = ibuf.shape[1]
        start = pl.multiple_of(chunk * rows, rows)
        copies.append(pltpu.make_async_copy(src.at[pl.ds(start, rows), :], ibuf.at[slot], sem_in.at[slot, i]))
    return copies


def _cast_out_copies(dst_refs, out_bufs, sem_out, chunk):
    copies = []
    for i, (dst, obuf) in enumerate(zip(dst_refs, out_bufs)):
        rows = obuf.shape[0]
        start = pl.multiple_of(chunk * rows, rows)
        copies.append(pltpu.make_async_copy(obuf, dst.at[pl.ds(start, rows), :], sem_out.at[i]))
    return copies


def _prompt_kernel(x_ref, gmix_ref, win_ref, glng_ref, glnb_ref, ws_ref, bst_ref, cw_ref, cb_ref,
                   clng_ref, clnb_ref, wout_ref, gxa_ref, wq_ref, kb_ref, vb_ref, wo_ref,
                   w1f_ref, w3f_ref, w2f_ref,
                   h_ref, conv_ref, gmv_ref, w1b_ref, w3b_ref, w2b_ref,
                   z_s, g_s, c_s, ab_s, o_s, y_prev, y_cur, in1, in3, in2, st1, st3, st2, sem_in, sem_out,
                   *, tiles_per_seq, n_cast_chunks):
    s = pl.program_id(0)
    n_steps = pl.num_programs(0)
    tb = x_ref.shape[0]
    n_chunks = tb // CHUNK

    srcs, dsts = (w1f_ref, w3f_ref, w2f_ref), (w1b_ref, w3b_ref, w2b_ref)
    ins, sts = (in1, in3, in2), (st1, st3, st2)
    slot = s % 2
    chunk_of = lambda step: jnp.minimum(step, n_cast_chunks - 1)

    @pl.when(s == 0)
    def _():
        for c in _cast_in_copies(srcs, ins, sem_in, chunk_of(s), slot):
            c.start()

    @pl.when(s + 1 < n_steps)
    def _():
        for c in _cast_in_copies(srcs, ins, sem_in, chunk_of(s + 1), 1 - slot):
            c.start()

    cast_in = _cast_in_copies(srcs, ins, sem_in, chunk_of(s), slot)
    cast_out = _cast_out_copies(dsts, sts, sem_out, chunk_of(s))
    for c in cast_in:
        c.wait()
    for ibuf, obuf in zip(ins, sts):
        obuf[...] = ibuf[slot].astype(BF16)
    for c in cast_out:
        c.start()

    @pl.when(s == 0)
    def _():
        y_prev[...] = jnp.zeros(y_prev.shape, F32)

    @pl.when(s % tiles_per_seq == 0)
    def _():
        g_s[0:HALO, :] = jnp.zeros((HALO, CV_WIDTH), F32)

    z_s[...] = _dot(_rms(x_ref[...], gmix_ref[...]).astype(BF16), win_ref[...])

    y = y_prev[...]
    qn = _rms(y, gxa_ref[...]).astype(BF16)
    q = (_dot(qn, wq_ref[...]) * (XA_HEAD_DIM ** -0.5)).astype(BF16)
    for h in range(XA_HEADS):
        cols = slice(h * XA_HEAD_DIM, (h + 1) * XA_HEAD_DIM)
        sc = _dot_nt(q[:, cols], kb_ref[:, cols])
        e = jnp.exp(sc - jnp.max(sc, axis=-1, keepdims=True))
        p = (e / jnp.sum(e, axis=-1, keepdims=True)).astype(BF16)
        o_s[:, cols] = _dot(p, vb_ref[:, cols]).astype(BF16)
    h_ref[...] = y + _dot(o_s[...], wo_ref[...])

    tri = (lax.broadcasted_iota(jnp.int32, (CHUNK, CHUNK), 0)
           >= lax.broadcasted_iota(jnp.int32, (CHUNK, CHUNK), 1))
    wm = [jnp.where(tri, ws_ref[h], 0.0).astype(BF16) for h in range(GM_HEADS)]

    for c in range(n_chunks):
        rows = slice(c * CHUNK, (c + 1) * CHUNK)
        for h in range(GM_HEADS):
            cu = slice(h * GM_HEAD_DIM, (h + 1) * GM_HEAD_DIM)
            cv = slice(GM_WIDTH + h * GM_HEAD_DIM, GM_WIDTH + (h + 1) * GM_HEAD_DIM)
            v = _ln(jax.nn.gelu(z_s[rows, cv]), glng_ref[:, cu], glnb_ref[:, cu])
            if c == n_chunks - 1:
                gmv_ref[:, cu] = v
            mixed = _dot(wm[h], v.astype(BF16)) + bst_ref[:, h:h + 1]
            ab_s[rows, cu] = (jax.nn.gelu(z_s[rows, cu]) * mixed).astype(BF16)
        ca = slice(2 * GM_WIDTH, 2 * GM_WIDTH + CV_WIDTH)
        cg = slice(2 * GM_WIDTH + CV_WIDTH, IN_COLS)
        g_s[HALO + c * CHUNK:HALO + (c + 1) * CHUNK, :] = z_s[rows, ca] * jax.nn.sigmoid(z_s[rows, cg])

    for c in range(n_chunks):
        rows = slice(c * CHUNK, (c + 1) * CHUNK)
        for cb in range(CV_WIDTH // LANES):
            lanes = slice(cb * LANES, (cb + 1) * LANES)
            c_s[rows, lanes] = _conv_block(g_s, cw_ref, c * CHUNK, lanes) + cb_ref[:, lanes]
        b = _ln(c_s[rows, :], clng_ref[...], clnb_ref[...])
        ab_s[rows, GM_WIDTH:] = (b * jax.nn.sigmoid(b)).astype(BF16)

    conv_ref[...] = g_s[tb:tb + HALO, :]
    g_s[0:HALO, :] = g_s[tb:tb + HALO, :]
    y_cur[...] = x_ref[...] + _dot(ab_s[...], wout_ref[...])
    y_prev[...] = y_cur[...]
    for c in cast_out:
        c.wait()


def _prompt_mix_attn(x, gmix, win, glng, glnb, ws, bst, cw, cb, clng, clnb, wout, gxa, wq, kb, vb, wo,
                     w1f, w3f, w2f):
    nb, seq, _ = x.shape
    tb = PROMPT_TILE
    tps = seq // tb
    last = nb * tps - 1
    n_cast = nb * tps
    r13, r2 = w1f.shape[0] // n_cast, w2f.shape[0] // n_cast
    assert w1f.shape == w3f.shape and w1f.shape[0] % n_cast == 0 and w2f.shape[0] % n_cast == 0
    assert r13 % 16 == 0 and r2 % 16 == 0
    hbm = pl.BlockSpec(memory_space=pl.ANY)
    tile = lambda s, lag: jnp.clip(s - lag, 0, last)
    full = lambda shape: pl.BlockSpec(shape, lambda s: (0,) * len(shape))
    rows = lambda lag: pl.BlockSpec((None, tb, D_MODEL), lambda s: (tile(s, lag) // tps, tile(s, lag) % tps, 0))
    kv_spec = pl.BlockSpec((None, N_MEM, D_MODEL), lambda s: (tile(s, 1) // tps, 0, 0))
    seq_out = lambda r, c: pl.BlockSpec((None, r, c), lambda s: (tile(s, 0) // tps, 0, 0))
    return pl.pallas_call(
        functools.partial(_prompt_kernel, tiles_per_seq=tps, n_cast_chunks=n_cast),
        grid=(nb * tps + 1,),
        in_specs=[rows(0), full((1, D_MODEL)), full((D_MODEL, IN_COLS)), full((1, GM_WIDTH)), full((1, GM_WIDTH)),
                  full((GM_HEADS, CHUNK, CHUNK)), full((CHUNK, GM_HEADS)), full((CONV_WIDTH, CV_WIDTH)),
                  full((1, CV_WIDTH)), full((1, CV_WIDTH)), full((1, CV_WIDTH)), full((D_MODEL, D_MODEL)),
                  full((1, D_MODEL)), full((D_MODEL, D_MODEL)), kv_spec, kv_spec, full((D_MODEL, D_MODEL)),
                  hbm, hbm, hbm],
        out_specs=[rows(1), seq_out(HALO, CV_WIDTH), seq_out(CHUNK, GM_WIDTH), hbm, hbm, hbm],
        out_shape=[jax.ShapeDtypeStruct((nb, seq, D_MODEL), F32),
                   jax.ShapeDtypeStruct((nb, HALO, CV_WIDTH), F32),
                   jax.ShapeDtypeStruct((nb, CHUNK, GM_WIDTH), F32),
                   jax.ShapeDtypeStruct(w1f.shape, BF16), jax.ShapeDtypeStruct(w3f.shape, BF16),
                   jax.ShapeDtypeStruct(w2f.shape, BF16)],
        scratch_shapes=[pltpu.VMEM((tb, IN_COLS), F32), pltpu.VMEM((HALO + tb, CV_WIDTH), F32),
                        pltpu.VMEM((tb, CV_WIDTH), F32), pltpu.VMEM((tb, D_MODEL), BF16),
                        pltpu.VMEM((tb, D_MODEL), BF16), pltpu.VMEM((tb, D_MODEL), F32),
                        pltpu.VMEM((tb, D_MODEL), F32),
                        pltpu.VMEM((2, r13, EXPERT_FF), F32), pltpu.VMEM((2, r13, EXPERT_FF), F32),
                        pltpu.VMEM((2, r2, D_MODEL), F32),
                        pltpu.VMEM((r13, EXPERT_FF), BF16), pltpu.VMEM((r13, EXPERT_FF), BF16),
                        pltpu.VMEM((r2, D_MODEL), BF16),
                        pltpu.SemaphoreType.DMA((2, 3)), pltpu.SemaphoreType.DMA((3,))],
        compiler_params=pltpu.CompilerParams(dimension_semantics=("arbitrary",), vmem_limit_bytes=VMEM_LIMIT),
        name="prompt_mix_attn",
    )(x, gmix, win, glng, glnb, ws, bst, cw, cb, clng, clnb, wout, gxa, wq, kb, vb, wo, w1f, w3f, w2f)


def _sample_mix_kernel(x_ref, gmix_ref, win_ref, glng_ref, glnb_ref, ws0_ref, bs0_ref, cache_ref, cw_ref, cb_ref,
                       clng_ref, clnb_ref, wout_ref, gxa_ref, wq_ref,
                       y_ref, conv_ref, v_ref, q_ref, ab_s):
    x = x_ref[...]
    z = _dot(_rms(x, gmix_ref[...]).astype(BF16), win_ref[...])
    for h in range(GM_HEADS):
        cu = slice(h * GM_HEAD_DIM, (h + 1) * GM_HEAD_DIM)
        cv = slice(GM_WIDTH + h * GM_HEAD_DIM, GM_WIDTH + (h + 1) * GM_HEAD_DIM)
        v = _ln(jax.nn.gelu(z[:, cv]), glng_ref[:, cu], glnb_ref[:, cu])
        v_ref[:, cu] = v
        ab_s[:, cu] = (jax.nn.gelu(z[:, cu]) * (v * ws0_ref[:, cu] + bs0_ref[:, cu])).astype(BF16)
    glu = z[:, 2 * GM_WIDTH:2 * GM_WIDTH + CV_WIDTH] * jax.nn.sigmoid(z[:, 2 * GM_WIDTH + CV_WIDTH:])
    conv_ref[0:CONV_WIDTH - 2] = cache_ref[1:CONV_WIDTH - 1]
    conv_ref[CONV_WIDTH - 2] = glu
    conv = glu * cw_ref[CONV_WIDTH - 1:CONV_WIDTH, :] + cb_ref[...]
    for k in range(CONV_WIDTH - 1):
        conv = conv + cache_ref[k] * cw_ref[k:k + 1, :]
    b = _ln(conv, clng_ref[...], clnb_ref[...])
    ab_s[:, GM_WIDTH:] = (b * jax.nn.sigmoid(b)).astype(BF16)
    y = x + _dot(ab_s[...], wout_ref[...])
    y_ref[...] = y
    q = _dot(_rms(y, gxa_ref[...]).astype(BF16), wq_ref[...]) * (XA_HEAD_DIM ** -0.5)
    for r in range(SUBLANES):
        q_ref[:, r, :] = q[:, _head_cols(r)]


def _head_cols(r):
    half, head = divmod(r, XA_HEADS)
    start = head * XA_HEAD_DIM + half * LANES
    return slice(start, start + LANES)


def _sample_mix(x, gmix, win, glng, glnb, ws0, bs0, cache_t, cw, cb, clng, clnb, wout, gxa, wq):
    ns = x.shape[0]
    return pl.pallas_call(
        _sample_mix_kernel,
        out_shape=[jax.ShapeDtypeStruct((ns, D_MODEL), F32), jax.ShapeDtypeStruct(cache_t.shape, F32),
                   jax.ShapeDtypeStruct((ns, GM_WIDTH), F32), jax.ShapeDtypeStruct((ns, SUBLANES, LANES), F32)],
        scratch_shapes=[pltpu.VMEM((ns, D_MODEL), BF16)],
        compiler_params=pltpu.CompilerParams(vmem_limit_bytes=VMEM_LIMIT),
        name="sample_mix",
    )(x, gmix, win, glng, glnb, ws0, bs0, cache_t, cw, cb, clng, clnb, wout, gxa, wq)


def _kv_copies(k_hbm, v_hbm, k_buf, v_buf, sem, block, slot):
    nb = k_buf.shape[1]
    seqs = pl.ds(block * nb, nb)
    return (pltpu.make_async_copy(k_hbm.at[seqs], k_buf.at[slot], sem.at[slot, 0]),
            pltpu.make_async_copy(v_hbm.at[seqs], v_buf.at[slot], sem.at[slot, 1]))


def _sample_attn_kernel(q_ref, k_hbm, v_hbm, o_ref, k_buf, v_buf, sem, *, first_block, n_steps):
    s = pl.program_id(0)
    ahead = KV_RING - 1

    @pl.when(s == 0)
    def _():
        for j in range(min(ahead, n_steps)):
            for c in _kv_copies(k_hbm, v_hbm, k_buf, v_buf, sem, first_block + j, j):
                c.start()

    @pl.when(s + ahead < n_steps)
    def _():
        for c in _kv_copies(k_hbm, v_hbm, k_buf, v_buf, sem, first_block + s + ahead, (s + ahead) % KV_RING):
            c.start()

    slot = s % KV_RING
    for c in _kv_copies(k_hbm, v_hbm, k_buf, v_buf, sem, first_block + s, slot):
        c.wait()

    ones = jnp.ones((LANES, LANES), BF16)
    rows = N_MEM * SUBLANES
    for i in range(q_ref.shape[0]):
        prod = (k_buf[slot, i] * q_ref[i][None]).reshape(rows, LANES).astype(BF16)
        part = _dot(prod, ones).reshape(N_MEM, SUBLANES, LANES)
        sc = part + pltpu.roll(part, XA_HEADS, axis=1)
        e = jnp.exp(sc - jnp.max(sc, axis=0, keepdims=True))
        p = e / jnp.sum(e, axis=0, keepdims=True)
        o_ref[i] = jnp.sum(p * v_buf[slot, i], axis=0)


def _split_heads(a):
    halves = XA_HEAD_DIM // LANES
    assert halves * XA_HEADS == SUBLANES
    lead = a.shape[:-2]
    a = a.reshape(*lead, XA_HEADS, halves, LANES)
    return jnp.swapaxes(a, -3, -2).reshape(*lead, SUBLANES, LANES)


def _sample_attn(q, k, v, first, count):
    nb = SAMPLE_ATTN_BLOCK
    assert first % nb == 0 and count % nb == 0
    off = first // nb
    n_steps = count // nb
    hbm = pl.BlockSpec(memory_space=pl.ANY)
    return pl.pallas_call(
        functools.partial(_sample_attn_kernel, first_block=off, n_steps=n_steps),
        grid=(n_steps,),
        in_specs=[pl.BlockSpec((nb, SUBLANES, LANES), lambda i: (i + off, 0, 0)), hbm, hbm],
        out_specs=pl.BlockSpec((nb, SUBLANES, LANES), lambda i: (i, 0, 0)),
        out_shape=jax.ShapeDtypeStruct((count, SUBLANES, LANES), F32),
        scratch_shapes=[pltpu.VMEM((KV_RING, nb, N_MEM, SUBLANES, LANES), F32),
                        pltpu.VMEM((KV_RING, nb, N_MEM, SUBLANES, LANES), F32),
                        pltpu.SemaphoreType.DMA((KV_RING, 2))],
        compiler_params=pltpu.CompilerParams(dimension_semantics=("arbitrary",), vmem_limit_bytes=VMEM_LIMIT),
        name="sample_attn",
    )(q, k, v)


def _sample_proj_kernel(y_ref, oa_ref, ob_ref, wo_ref, h_ref):
    halves = XA_HEAD_DIM // LANES
    first = 0
    for o_ref in (oa_ref, ob_ref):
        n = o_ref.shape[0]
        o = jnp.concatenate([o_ref[:, (j % halves) * XA_HEADS + j // halves, :] for j in range(SUBLANES)], axis=1)
        h_ref[first:first + n, :] = y_ref[first:first + n, :] + _dot(o.astype(BF16), wo_ref[...])
        first += n


def _sample_proj(y, oa, ob, wo):
    assert oa.shape[0] + ob.shape[0] == y.shape[0] and oa.shape[0] % SUBLANES == 0
    return pl.pallas_call(
        _sample_proj_kernel,
        out_shape=jax.ShapeDtypeStruct(y.shape, F32),
        compiler_params=pltpu.CompilerParams(vmem_limit_bytes=VMEM_LIMIT),
        name="sample_proj",
    )(y, oa, ob, wo)


def _moe_kernel(h_ref, gffn_ref, wrt_ref, rb_ref, w1_ref, w3_ref, w2_ref, gfin_ref,
                o_ref, xt_s, u_s, gidx_s, slot_s, comb_s, act_s, p2_s, y2_s, cnt_s):
    i = pl.program_id(0)
    g = pl.program_id(1)
    rt = h_ref.shape[0]
    sb = act_s.shape[0]

    @pl.when(jnp.logical_and(i == 0, g == 0))
    def _():
        u_s[...] = (lax.broadcasted_iota(jnp.int32, (rt, rt), 0)
                    < lax.broadcasted_iota(jnp.int32, (rt, rt), 1)).astype(F32).astype(BF16)

    @pl.when(g == 0)
    def _route():
        h = h_ref[...]
        o_ref[...] = h
        xt = _rms(h, gffn_ref[...]).astype(BF16)
        xt_s[...] = xt
        lt = _dot_nt(wrt_ref[...], xt) + rb_ref[...]
        gl = [lt[k:k + 1, :] for k in range(N_GROUPS)]
        gmax = jnp.maximum(jnp.maximum(gl[0], gl[1]), jnp.maximum(gl[2], gl[3]))
        gidx = jnp.where(gl[0] == gmax, 0, jnp.where(gl[1] == gmax, 1, jnp.where(gl[2] == gmax, 2, 3)))
        gidx = gidx.astype(jnp.int32)
        sumexp = (jnp.exp(gl[0] - gmax) + jnp.exp(gl[1] - gmax)) + (jnp.exp(gl[2] - gmax) + jnp.exp(gl[3] - gmax))
        p_g = 1.0 / sumexp
        esel = lt[SUBLANES + 3 * EXPERTS_PER_GROUP:SUBLANES + 4 * EXPERTS_PER_GROUP, :]
        for k in (2, 1, 0):
            esel = jnp.where(gidx == k, lt[SUBLANES + k * EXPERTS_PER_GROUP:SUBLANES + (k + 1) * EXPERTS_PER_GROUP, :],
                             esel)
        eidx = lax.broadcasted_iota(jnp.int32, (EXPERTS_PER_GROUP, rt), 0)
        m1 = jnp.max(esel, axis=0, keepdims=True)
        i1 = jnp.min(jnp.where(esel == m1, eidx, EXPERTS_PER_GROUP), axis=0, keepdims=True)
        rest = jnp.where(eidx == i1, -jnp.inf, esel)
        m2 = jnp.max(rest, axis=0, keepdims=True)
        i2 = jnp.min(jnp.where(rest == m2, eidx, EXPERTS_PER_GROUP), axis=0, keepdims=True)
        t2 = jnp.exp(m2 - m1)
        den = 1.0 + t2
        w_top1 = (1.0 / den) * p_g
        w_top2 = (t2 / den) * p_g
        within = jnp.where(eidx == i1, w_top1, 0.0) + jnp.where(eidx == i2, w_top2, 0.0)
        c_hi = within.astype(BF16).astype(F32)
        r1 = within - c_hi
        c_mid = r1.astype(BF16).astype(F32)
        c_lo = (r1 - c_mid).astype(BF16).astype(F32)
        comb_s[0:8, :] = c_hi
        comb_s[8:16, :] = c_mid
        comb_s[16:24, :] = c_lo
        comb_s[24:32, :] = jnp.zeros((8, rt), F32)
        onehot = (eidx == gidx).astype(F32)
        rank = _dot(onehot.astype(BF16), u_s[...])
        slot_s[...] = jnp.sum(onehot * rank, axis=0, keepdims=True).astype(jnp.int32)
        gidx_s[...] = gidx
        for k in range(N_GROUPS):
            cnt_s[k] = jnp.sum(onehot[k:k + 1, :]).astype(jnp.int32)

    n_blk = (cnt_s[g] + sb - 1) // sb

    def sub_block(j, half):
        half_rows = slice(half * sb, (half + 1) * sb)
        rows = lax.broadcasted_iota(jnp.int32, (sb, rt), 0) + j * sb
        hit = jnp.logical_and(rows == slot_s[...], gidx_s[...] == g)
        p = jnp.where(hit, 1.0, 0.0).astype(BF16)
        p2_s[half_rows, :] = p
        xc = _dot(p, xt_s[...]).astype(BF16)
        cexp = _dot_nt(p, comb_s[...].astype(BF16))
        cw = (cexp[:, 0:8] + cexp[:, 8:16]) + cexp[:, 16:24]
        for e in range(EXPERTS_PER_GROUP):
            h1 = _dot(xc, w1_ref[e])
            h3 = _dot(xc, w3_ref[e])
            a = (h1 * jax.nn.sigmoid(h1)) * h3 * cw[:, e:e + 1]
            act_s[:, e * EXPERT_FF:(e + 1) * EXPERT_FF] = a.astype(BF16)
        y2_s[half_rows, :] = _dot(act_s[...], w2_ref[...]).astype(BF16)

    def body(jj, carry):
        sub_block(2 * jj, 0)

        @pl.when(2 * jj + 1 < n_blk)
        def _():
            sub_block(2 * jj + 1, 1)

        @pl.when(2 * jj + 1 >= n_blk)
        def _():
            p2_s[sb:, :] = jnp.zeros((sb, rt), BF16)
            y2_s[sb:, :] = jnp.zeros((sb, D_MODEL), BF16)

        o_ref[...] += _dot_tn(p2_s[...], y2_s[...])
        return carry

    lax.fori_loop(0, (n_blk + 1) // 2, body, 0)

    @pl.when(g == N_GROUPS - 1)
    def _():
        o_ref[...] = _rms(o_ref[...], gfin_ref[...])


def _moe(h, gffn, wrt, rbias, w1, w3, w2, gfin, *, tile):
    tokens = h.shape[0]
    sb = min(MOE_SUB, tile)
    ff = EXPERTS_PER_GROUP * EXPERT_FF
    full = lambda shape: pl.BlockSpec(shape, lambda i, g: (0,) * len(shape))
    row_spec = pl.BlockSpec((tile, D_MODEL), lambda i, g: (i, 0))
    return pl.pallas_call(
        _moe_kernel,
        grid=(tokens // tile, N_GROUPS),
        in_specs=[row_spec, full((1, D_MODEL)), full((ROUTER_ROWS, D_MODEL)), full((ROUTER_ROWS, 1)),
                  pl.BlockSpec((None, EXPERTS_PER_GROUP, D_MODEL, EXPERT_FF), lambda i, g: (g, 0, 0, 0)),
                  pl.BlockSpec((None, EXPERTS_PER_GROUP, D_MODEL, EXPERT_FF), lambda i, g: (g, 0, 0, 0)),
                  pl.BlockSpec((None, ff, D_MODEL), lambda i, g: (g, 0, 0)),
                  full((1, D_MODEL))],
        out_specs=row_spec,
        out_shape=jax.ShapeDtypeStruct((tokens, D_MODEL), F32),
        scratch_shapes=[pltpu.VMEM((tile, D_MODEL), BF16), pltpu.VMEM((tile, tile), BF16),
                        pltpu.VMEM((1, tile), jnp.int32), pltpu.VMEM((1, tile), jnp.int32),
                        pltpu.VMEM((4 * SUBLANES, tile), F32), pltpu.VMEM((sb, ff), BF16),
                        pltpu.VMEM((2 * sb, tile), BF16), pltpu.VMEM((2 * sb, D_MODEL), BF16),
                        pltpu.SMEM((N_GROUPS,), jnp.int32)],
        compiler_params=pltpu.CompilerParams(dimension_semantics=("arbitrary", "arbitrary"),
                                             vmem_limit_bytes=VMEM_LIMIT),
        name="moe",
    )(h, gffn, wrt, rbias, w1, w3, w2, gfin)


HALF = D_MODEL // 2
SUB_PER_ROW = HALF // LANES
N_EXPERTS = N_GROUPS * EXPERTS_PER_GROUP


def _pack_bf16_pairs(x):
    bits = pltpu.bitcast(x.astype(BF16).astype(F32), jnp.uint32)
    return (bits[:, HALF:] & jnp.uint32(0xFFFF0000)) | (bits[:, :HALF] >> 16)


def _unpack_bf16_pairs(w):
    lo = pltpu.bitcast(w << 16, F32)
    hi = pltpu.bitcast(w & jnp.uint32(0xFFFF0000), F32)
    return lo, hi


def _route_kernel(h_ref, gffn_ref, wrt_ref, rb_ref, xp_ref, e_ref, w_ref, r_ref, cnt_ref, u_s, run_s):
    i = pl.program_id(0)
    rt = u_s.shape[0]

    @pl.when(i == 0)
    def _():
        u_s[...] = (lax.broadcasted_iota(jnp.int32, (rt, rt), 0)
                    < lax.broadcasted_iota(jnp.int32, (rt, rt), 1)).astype(F32).astype(BF16)
        run_s[...] = jnp.zeros(run_s.shape, F32)

    for t in range(h_ref.shape[0] // rt):
        tok = pl.ds(t * rt, rt)
        _route_tile(h_ref.at[tok, :], gffn_ref, wrt_ref, rb_ref, xp_ref.at[:, tok, :], e_ref.at[:, tok],
                    w_ref.at[:, tok], r_ref.at[:, tok], u_s, run_s)
    cnt_ref[...] = run_s[...].astype(jnp.int32)


def _route_tile(h_ref, gffn_ref, wrt_ref, rb_ref, xp_ref, e_ref, w_ref, r_ref, u_s, run_s):
    rt = h_ref.shape[0]
    xt = _rms(h_ref[...], gffn_ref[...])
    _store_planes(xp_ref, _pack_bf16_pairs(xt))
    lt = _dot_nt(wrt_ref[...], xt.astype(BF16)) + rb_ref[...]
    gl = [lt[k:k + 1, :] for k in range(N_GROUPS)]
    gmax = jnp.maximum(jnp.maximum(gl[0], gl[1]), jnp.maximum(gl[2], gl[3]))
    gidx = jnp.where(gl[0] == gmax, 0, jnp.where(gl[1] == gmax, 1, jnp.where(gl[2] == gmax, 2, 3)))
    gidx = gidx.astype(jnp.int32)
    sumexp = (jnp.exp(gl[0] - gmax) + jnp.exp(gl[1] - gmax)) + (jnp.exp(gl[2] - gmax) + jnp.exp(gl[3] - gmax))
    p_g = 1.0 / sumexp
    esel = lt[SUBLANES + 3 * EXPERTS_PER_GROUP:SUBLANES + 4 * EXPERTS_PER_GROUP, :]
    for k in (2, 1, 0):
        esel = jnp.where(gidx == k, lt[SUBLANES + k * EXPERTS_PER_GROUP:SUBLANES + (k + 1) * EXPERTS_PER_GROUP, :],
                         esel)
    eidx = lax.broadcasted_iota(jnp.int32, (EXPERTS_PER_GROUP, rt), 0)
    m1 = jnp.max(esel, axis=0, keepdims=True)
    i1 = jnp.min(jnp.where(esel == m1, eidx, EXPERTS_PER_GROUP), axis=0, keepdims=True)
    rest = jnp.where(eidx == i1, -jnp.inf, esel)
    m2 = jnp.max(rest, axis=0, keepdims=True)
    i2 = jnp.min(jnp.where(rest == m2, eidx, EXPERTS_PER_GROUP), axis=0, keepdims=True)
    t2 = jnp.exp(m2 - m1)
    den = 1.0 + t2
    w_ref[...] = jnp.zeros(w_ref.shape, F32)
    w_ref[0:1, :] = (1.0 / den) * p_g
    w_ref[1:2, :] = (t2 / den) * p_g
    e1 = gidx * EXPERTS_PER_GROUP + i1
    e2 = gidx * EXPERTS_PER_GROUP + i2
    e_ref[0:1, :] = e1
    e_ref[1:2, :] = e2
    xid = lax.broadcasted_iota(jnp.int32, (N_EXPERTS, rt), 0)
    oh1 = (xid == e1).astype(F32)
    oh2 = (xid == e2).astype(F32)
    both = oh1 + oh2
    before = _dot(both.astype(BF16), u_s[...]) + run_s[:, 0:1]
    r_ref[0:1, :] = jnp.sum(oh1 * before, axis=0, keepdims=True).astype(jnp.int32)
    r_ref[1:2, :] = jnp.sum(oh2 * before, axis=0, keepdims=True).astype(jnp.int32)
    run_s[...] = run_s[...] + jnp.sum(both, axis=1, keepdims=True)


def _route(h, gffn, wrt, rbias, *, tile):
    tokens = h.shape[0]
    step = tile * ROUTE_TILES_PER_STEP
    assert tokens % step == 0
    full = lambda shape: pl.BlockSpec(shape, lambda i: (0,) * len(shape))
    lanes = lambda rows: pl.BlockSpec((rows, step), lambda i: (0, i))
    return pl.pallas_call(
        _route_kernel,
        grid=(tokens // step,),
        in_specs=[pl.BlockSpec((step, D_MODEL), lambda i: (i, 0)), full((1, D_MODEL)),
                  full((ROUTER_ROWS, D_MODEL)), full((ROUTER_ROWS, 1))],
        out_specs=[pl.BlockSpec((SUB_PER_ROW, step, LANES), lambda i: (0, i, 0)), lanes(2), lanes(SUBLANES), lanes(2),
                   full((N_EXPERTS, LANES))],
        out_shape=[jax.ShapeDtypeStruct((SUB_PER_ROW, tokens, LANES), jnp.uint32),
                   jax.ShapeDtypeStruct((2, tokens), jnp.int32),
                   jax.ShapeDtypeStruct((SUBLANES, tokens), F32), jax.ShapeDtypeStruct((2, tokens), jnp.int32),
                   jax.ShapeDtypeStruct((N_EXPERTS, LANES), jnp.int32)],
        scratch_shapes=[pltpu.VMEM((tile, tile), BF16), pltpu.VMEM((N_EXPERTS, LANES), F32)],
        compiler_params=pltpu.CompilerParams(dimension_semantics=("arbitrary",), vmem_limit_bytes=VMEM_LIMIT),
        name="moe_route",
    )(h, gffn, wrt, rbias)


def _store_planes(ref, words):
    for j in range(SUB_PER_ROW):
        ref[j] = words[:, j * LANES:(j + 1) * LANES]


def _load_planes(ref):
    return jnp.concatenate([ref[j] for j in range(SUB_PER_ROW)], axis=1)


def _sub_row_index(row_of_token, n_rows):
    plane = jnp.arange(SUB_PER_ROW, dtype=jnp.int32)[:, None] * n_rows
    return (row_of_token[None, :] + plane).reshape(1, -1)


def _sc_mesh():
    return plsc.VectorSubcoreMesh(core_axis_name="core", subcore_axis_name="subcore")


def _sc_scatter_two(x_sub, idx_a, idx_b, n_out):
    n_in = x_sub.shape[0]

    @pl.kernel(out_type=jax.ShapeDtypeStruct((n_out, LANES), x_sub.dtype), mesh=_sc_mesh(), scratch_types=[])
    def scatter(x_hbm, a_hbm, b_hbm, o_hbm):
        def body(x_vmem, a_vmem, b_vmem):
            pltpu.sync_copy(x_vmem, o_hbm.at[a_vmem.at[0]])
            pltpu.sync_copy(x_vmem, o_hbm.at[b_vmem.at[0]])

        pltpu.emit_pipeline(
            body, grid=(n_in // SC_WINDOW,),
            in_specs=[pl.BlockSpec((SC_WINDOW, LANES), lambda i: (i, 0)),
                      pl.BlockSpec((1, SC_WINDOW), lambda i: (0, i)),
                      pl.BlockSpec((1, SC_WINDOW), lambda i: (0, i))],
            out_specs=[],
            core_axis_name=("core", "subcore"), dimension_semantics=(pltpu.PARALLEL,),
        )(x_hbm, a_hbm, b_hbm)

    return scatter(x_sub, idx_a, idx_b)


def _sc_gather(table, idx):
    n_out = idx.shape[1]

    @pl.kernel(out_type=jax.ShapeDtypeStruct((n_out, LANES), table.dtype), mesh=_sc_mesh())
    def gather(t_hbm, i_hbm, o_hbm):
        def body(i_vmem, o_vmem):
            pltpu.sync_copy(t_hbm.at[i_vmem.at[0]], o_vmem)

        pltpu.emit_pipeline(
            body, grid=(n_out // SC_WINDOW,),
            in_specs=[pl.BlockSpec((1, SC_WINDOW), lambda i: (0, i))],
            out_specs=[pl.BlockSpec((SC_WINDOW, LANES), lambda i: (i, 0))],
            core_axis_name=("core", "subcore"), dimension_semantics=(pltpu.PARALLEL,),
        )(i_hbm, o_hbm)

    return gather(table, idx)


def _expert_ffn_kernel(blk_e_ref, n_valid_ref, x_ref, *refs):
    del blk_e_ref
    y_ref = refs[-1]
    for i in range(FFN_PER_STEP):
        w1_ref, w3_ref, w2_ref = refs[3 * i:3 * i + 3]
        rows = slice(i * FFN_BLOCK, (i + 1) * FFN_BLOCK)

        @pl.when(pl.program_id(0) * FFN_PER_STEP + i < n_valid_ref[0])
        def _(w1_ref=w1_ref, w3_ref=w3_ref, w2_ref=w2_ref, rows=rows):
            words = jnp.concatenate([x_ref[j, rows, :] for j in range(SUB_PER_ROW)], axis=1)
            lo, hi = _unpack_bf16_pairs(words)
            xc = jnp.concatenate([lo.astype(BF16), hi.astype(BF16)], axis=1)
            h1 = _dot(xc, w1_ref[...])
            h3 = _dot(xc, w3_ref[...])
            act = ((h1 * jax.nn.sigmoid(h1)) * h3).astype(BF16)
            packed = _pack_bf16_pairs(_dot(act, w2_ref[...]))
            for j in range(SUB_PER_ROW):
                y_ref[j, rows, :] = packed[:, j * LANES:(j + 1) * LANES]


def _expert_ffn(xs, blk_e, n_valid, w1, w3, w2):
    rows = xs.shape[1]
    step_rows = FFN_BLOCK * FFN_PER_STEP
    assert rows % step_rows == 0
    live = lambda s, be, nv: jnp.minimum(s, (nv[0] + FFN_PER_STEP - 1) // FFN_PER_STEP - 1)
    x_spec = pl.BlockSpec((SUB_PER_ROW, step_rows, LANES), lambda s, be, nv: (0, live(s, be, nv), 0))
    w_specs = []
    for i in range(FFN_PER_STEP):
        expert = lambda s, be, nv, i=i: (be[s * FFN_PER_STEP + i], 0, 0)
        w_specs += [pl.BlockSpec((None, D_MODEL, EXPERT_FF), expert), pl.BlockSpec((None, D_MODEL, EXPERT_FF), expert),
                    pl.BlockSpec((None, EXPERT_FF, D_MODEL), expert)]
    grid_spec = pltpu.PrefetchScalarGridSpec(
        num_scalar_prefetch=2, grid=(rows // step_rows,), in_specs=[x_spec] + w_specs, out_specs=x_spec)
    return pl.pallas_call(
        _expert_ffn_kernel, grid_spec=grid_spec,
        out_shape=jax.ShapeDtypeStruct((SUB_PER_ROW, rows, LANES), jnp.uint32),
        compiler_params=pltpu.CompilerParams(dimension_semantics=("arbitrary",), vmem_limit_bytes=VMEM_LIMIT),
        name="moe_ffn",
    )(blk_e, n_valid, xs, *([w1, w3, w2] * FFN_PER_STEP))


def _combine_kernel(h_ref, ya_ref, yb_ref, wt_ref, gfin_ref, o_ref):
    a_lo, a_hi = _unpack_bf16_pairs(_load_planes(ya_ref))
    b_lo, b_hi = _unpack_bf16_pairs(_load_planes(yb_ref))
    wt = jnp.transpose(wt_ref[...])
    wa = wt[:, 0:1]
    wb = wt[:, 1:2]
    o_ref[:, :HALF] = h_ref[:, :HALF] + (wa * a_lo + wb * b_lo)
    o_ref[:, HALF:] = h_ref[:, HALF:] + (wa * a_hi + wb * b_hi)
    o_ref[...] = _rms(o_ref[...], gfin_ref[...])


def _combine(h, ya, yb, wt, gfin, *, tile):
    tokens = h.shape[0]
    return pl.pallas_call(
        _combine_kernel,
        grid=(tokens // tile,),
        in_specs=[pl.BlockSpec((tile, D_MODEL), lambda i: (i, 0)),
                  pl.BlockSpec((SUB_PER_ROW, tile, LANES), lambda i: (0, i, 0)),
                  pl.BlockSpec((SUB_PER_ROW, tile, LANES), lambda i: (0, i, 0)), pl.BlockSpec((SUBLANES, tile), lambda i: (0, i)),
                  pl.BlockSpec((1, D_MODEL), lambda i: (0, 0))],
        out_specs=pl.BlockSpec((tile, D_MODEL), lambda i: (i, 0)),
        out_shape=jax.ShapeDtypeStruct((tokens, D_MODEL), F32),
        compiler_params=pltpu.CompilerParams(dimension_semantics=("arbitrary",), vmem_limit_bytes=VMEM_LIMIT),
        name="moe_combine",
    )(h, ya, yb, wt, gfin)


def _moe_dispatch(h, gffn, wrt, rbias):
    tokens = h.shape[0]
    xp, e12, w12, r12, cnt = _route(h, gffn, wrt, rbias, tile=MOE_TILE)
    count = cnt[:, 0]
    padded = (count + FFN_BLOCK - 1) // FFN_BLOCK * FFN_BLOCK
    end = jnp.cumsum(padded)
    start = end - padded
    n_rows = 2 * tokens + N_EXPERTS * FFN_BLOCK
    n_blocks = n_rows // FFN_BLOCK
    n_valid = (end[-1:] // FFN_BLOCK).astype(jnp.int32)
    first_row = jnp.minimum(jnp.arange(n_blocks, dtype=jnp.int32), n_valid - 1) * FFN_BLOCK
    blk_e = jnp.sum((end[None, :] <= first_row[:, None]).astype(jnp.int32), axis=1)
    experts = jnp.arange(N_EXPERTS, dtype=jnp.int32)
    start_of = jnp.sum(jnp.where(e12[:, :, None] == experts, start.astype(jnp.int32), 0), axis=-1)
    rows_ab = start_of + r12
    idx_a, idx_b = _sub_row_index(rows_ab[0], n_rows), _sub_row_index(rows_ab[1], n_rows)
    xs = _sc_scatter_two(xp.reshape(-1, LANES), idx_a, idx_b, n_rows * SUB_PER_ROW)
    return xs.reshape(SUB_PER_ROW, n_rows, LANES), (blk_e, n_valid, idx_a, idx_b, w12)


def _moe_finish(h, xs, plan, w1, w3, w2, gfin):
    blk_e, n_valid, idx_a, idx_b, w12 = plan
    ys = _expert_ffn(xs, blk_e, n_valid, w1, w3, w2).reshape(-1, LANES)
    ya = _sc_gather(ys, idx_a).reshape(SUB_PER_ROW, -1, LANES)
    yb = _sc_gather(ys, idx_b).reshape(SUB_PER_ROW, -1, LANES)
    return _combine(h, ya, yb, w12, gfin, tile=MOE_TILE)


def kernel(x_prompt, x_sample, mem_prompt, cache_conv, cache_mem_k, cache_mem_v, norm_mix_g, w_in, gm_ln_g, gm_ln_b, gm_ws, gm_bs, conv_w, conv_b, cv_ln_g, cv_ln_b, w_out, norm_mem_g, norm_xa_g, xa_wq, xa_wk, xa_wv, xa_wo, norm_ffn_g, router_g, router_g_b, router_e, router_e_b, exp_w1, exp_w3, exp_w2, final_norm_g):
    depth = w_in.shape[0]
    assert depth == 1, "single-layer trunk"
    nb, seq, _ = x_prompt.shape
    ns = x_sample.shape[0]
    row = lambda a: a.reshape(1, -1)

    l = 0
    gmix, gxa, gffn, gmem = row(norm_mix_g[l]), row(norm_xa_g[l]), row(norm_ffn_g[l]), row(norm_mem_g[l])
    gfin = row(final_norm_g)
    wk, wv = xa_wk[l], xa_wv[l]
    glng, glnb = row(gm_ln_g[l]), row(gm_ln_b[l])
    cw, cb, clng, clnb = conv_w[l], row(conv_b[l]), row(cv_ln_g[l]), row(cv_ln_b[l])
    ws, bst = gm_ws[l], gm_bs[l].T
    ws0 = jnp.repeat(gm_ws[l][:, 0, 0], GM_HEAD_DIM).reshape(1, GM_WIDTH)
    bs0 = jnp.repeat(gm_bs[l][:, 0], GM_HEAD_DIM).reshape(1, GM_WIDTH)

    n_exp = N_GROUPS * EXPERTS_PER_GROUP
    pad_g = SUBLANES - N_GROUPS
    pad_t = ROUTER_ROWS - SUBLANES - n_exp
    wrt = jnp.concatenate([router_g[l].T, jnp.zeros((pad_g, D_MODEL), F32),
                           router_e[l].reshape(D_MODEL, n_exp).T, jnp.zeros((pad_t, D_MODEL), F32)], axis=0).astype(BF16)
    rbias = jnp.concatenate([router_g_b[l], jnp.zeros((pad_g,), F32), router_e_b[l].reshape(n_exp),
                             jnp.zeros((pad_t,), F32)]).reshape(ROUTER_ROWS, 1)

    win, wout, wq, wo, mk, mv, kb, vb = _memkv(mem_prompt.reshape(nb * N_MEM, D_MODEL), gmem, wk, wv,
                                                (w_in[l], w_out[l], xa_wq[l], xa_wo[l]))
    kb, vb = kb.reshape(nb, N_MEM, D_MODEL), vb.reshape(nb, N_MEM, D_MODEL)
    hp, conv_tail, gmv_p, w1, w3, w2 = _prompt_mix_attn(
        x_prompt, gmix, win, glng, glnb, ws, bst, cw, cb, clng, clnb, wout, gxa, wq, kb, vb, wo,
        exp_w1[l].reshape(-1, EXPERT_FF), exp_w3[l].reshape(-1, EXPERT_FF), exp_w2[l].reshape(-1, D_MODEL))
    h2d = hp.reshape(nb * seq, D_MODEL)
    xs, plan = _moe_dispatch(h2d, gffn, wrt, rbias)
    w1e, w3e, w2e = (w1.reshape(N_EXPERTS, D_MODEL, EXPERT_FF), w3.reshape(N_EXPERTS, D_MODEL, EXPERT_FF),
                     w2.reshape(N_EXPERTS, EXPERT_FF, D_MODEL))
    w1 = w1.reshape(N_GROUPS, EXPERTS_PER_GROUP, D_MODEL, EXPERT_FF)
    w3 = w3.reshape(N_GROUPS, EXPERTS_PER_GROUP, D_MODEL, EXPERT_FF)
    w2 = w2.reshape(N_GROUPS, EXPERTS_PER_GROUP * EXPERT_FF, D_MODEL)

    cache_t = jnp.transpose(cache_conv[l], (1, 0, 2))
    ys, conv_t, gmv_s, qh = _sample_mix(x_sample.reshape(ns, D_MODEL), gmix, win, glng, glnb, ws0, bs0,
                                       cache_t, cw, cb, clng, clnb, wout, gxa, wq)
    kh, vh = _split_heads(cache_mem_k[l]), _split_heads(cache_mem_v[l])
    half = ns // 2
    o_first = _sample_attn(qh, kh, vh, 0, half)
    xs, o_first = lax.optimization_barrier((xs, o_first))
    y_prompt = _moe_finish(h2d, xs, plan, w1e, w3e, w2e, gfin)
    o_second = _sample_attn(qh, kh, vh, half, ns - half)
    hs = _sample_proj(ys, o_first, o_second, wo)
    y_sample = _moe(hs, gffn, wrt, rbias, w1, w3, w2, gfin, tile=ns)

    conv_prompt = conv_tail[:, HALO - (CONV_WIDTH - 1):, :][None]
    conv_sample = jnp.transpose(conv_t, (1, 0, 2))[None]
    return (y_prompt.reshape(nb, seq, D_MODEL), y_sample.reshape(ns, 1, D_MODEL), conv_prompt, conv_sample,
            gmv_p[None], gmv_s.reshape(1, ns, 1, GM_WIDTH),
            mk.reshape(1, nb, N_MEM, XA_HEADS, XA_HEAD_DIM), mv.reshape(1, nb, N_MEM, XA_HEADS, XA_HEAD_DIM))
```

```python
import functools

import jax
import jax.numpy as jnp
from jax import lax
from jax.experimental import pallas as pl
from jax.experimental.pallas import tpu as pltpu
from jax.experimental.pallas import tpu_sc as plsc

F32 = jnp.float32
BF16 = jnp.bfloat16

D_MODEL = 1024
GM_WIDTH = 512
CV_WIDTH = 512
GM_HEADS = 4
GM_HEAD_DIM = 128
CHUNK = 128
CONV_WIDTH = 31
IN_COLS = 2 * GM_WIDTH + 2 * CV_WIDTH
N_MEM = 256
XA_HEADS = 4
XA_HEAD_DIM = 256
N_GROUPS = 4
EXPERTS_PER_GROUP = 8
EXPERT_FF = 256
EPS = 1e-6

LANES = 128
SUBLANES = 8
HALO = 32
PROMPT_TILE = 512
MOE_TILE = 1024
MOE_SUB = 128
ROUTE_TILES_PER_STEP = 2
FFN_BLOCK = 512
FFN_PER_STEP = 2
SC_WINDOW = 128
SAMPLE_ATTN_BLOCK = 4
KV_RING = 3
ROUTER_ROWS = 128
VMEM_LIMIT = 56 * 1024 * 1024


def _rms(x, g):
    return x * lax.rsqrt(jnp.mean(x * x, axis=-1, keepdims=True) + EPS) * g


def _ln(x, g, b):
    mu = jnp.mean(x, axis=-1, keepdims=True)
    xc = x - mu
    var = jnp.mean(xc * xc, axis=-1, keepdims=True)
    return xc * lax.rsqrt(var + EPS) * g + b


def _dot(a, b):
    return jnp.dot(a, b, preferred_element_type=F32)


def _dot_nt(a, b):
    return lax.dot_general(a, b, (((1,), (1,)), ((), ())), preferred_element_type=F32)


def _dot_tn(a, b):
    return lax.dot_general(a, b, (((0,), (0,)), ((), ())), preferred_element_type=F32)


def _memkv_kernel(mem_ref, g_ref, wk_ref, wv_ref, *refs):
    k_ref, v_ref, kb_ref, vb_ref = refs[-4:]
    n_dense = (len(refs) - 4) // 2
    for src, dst in zip(refs[:n_dense], refs[n_dense:2 * n_dense]):
        dst[...] = src[...].astype(BF16)
    mn = _rms(mem_ref[...], g_ref[...]).astype(BF16)
    k = _dot(mn, wk_ref[...].astype(BF16))
    v = _dot(mn, wv_ref[...].astype(BF16))
    for h in range(XA_HEADS):
        cols = slice(h * XA_HEAD_DIM, (h + 1) * XA_HEAD_DIM)
        k_ref[:, h, :] = k[:, cols]
        v_ref[:, h, :] = v[:, cols]
    kb_ref[...] = k.astype(BF16)
    vb_ref[...] = v.astype(BF16)


def _memkv(mem2d, g, wk, wv, dense):
    rows = mem2d.shape[0]
    tile = 256
    steps = rows // tile
    row_spec = pl.BlockSpec((tile, D_MODEL), lambda i: (i, 0))
    head_spec = pl.BlockSpec((tile, XA_HEADS, XA_HEAD_DIM), lambda i: (i, 0, 0))
    full = lambda shape: pl.BlockSpec(shape, lambda i: (0,) * len(shape))
    assert all(w.ndim == 2 and w.shape[0] % (16 * steps) == 0 for w in dense)
    slabs = [pl.BlockSpec((w.shape[0] // steps, w.shape[1]), lambda i: (i, 0)) for w in dense]
    return pl.pallas_call(
        _memkv_kernel,
        grid=(steps,),
        in_specs=[row_spec, full((1, D_MODEL)), full((D_MODEL, D_MODEL)), full((D_MODEL, D_MODEL))] + slabs,
        out_specs=slabs + [head_spec, head_spec, row_spec, row_spec],
        out_shape=[jax.ShapeDtypeStruct(w.shape, BF16) for w in dense] + [
                   jax.ShapeDtypeStruct((rows, XA_HEADS, XA_HEAD_DIM), F32),
                   jax.ShapeDtypeStruct((rows, XA_HEADS, XA_HEAD_DIM), F32),
                   jax.ShapeDtypeStruct((rows, D_MODEL), BF16), jax.ShapeDtypeStruct((rows, D_MODEL), BF16)],
        compiler_params=pltpu.CompilerParams(dimension_semantics=("arbitrary",), vmem_limit_bytes=VMEM_LIMIT),
        name="memkv",
    )(mem2d, g, wk, wv, *dense)


def _conv_block(g_s, cw_ref, start, lanes):
    assert start % SUBLANES == 0
    rows = CHUNK + HALO
    win = g_s[start:start + rows, lanes]
    off = HALO - (CONV_WIDTH - 1)
    acc = None
    for b in range(SUBLANES):
        shifted = pltpu.roll(win, rows - (off + b), axis=0) if off + b else win
        for k in range(b, CONV_WIDTH, SUBLANES):
            assert k - b + CHUNK + off + b <= rows
            term = shifted[k - b:k - b + CHUNK] * cw_ref[k:k + 1, lanes]
            acc = term if acc is None else acc + term
    return acc


def _cast_in_copies(src_refs, in_bufs, sem_in, chunk, slot):
    copies = []
    for i, (src, ibuf) in enumerate(zip(src_refs, in_bufs)):
        rows = ibuf.shape[1]
        start = pl.multiple_of(chunk * rows, rows)
        copies.append(pltpu.make_async_copy(src.at[pl.ds(start, rows), :], ibuf.at[slot], sem_in.at[slot, i]))
    return copies


def _cast_out_copies(dst_refs, out_bufs, sem_out, chunk):
    copies = []
    for i, (dst, obuf) in enumerate(zip(dst_refs, out_bufs)):
        rows = obuf.shape[0]
        start = pl.multiple_of(chunk * rows, rows)
        copies.append(pltpu.make_async_copy(obuf, dst.at[pl.ds(start, rows), :], sem_out.at[i]))
    return copies


def _prompt_kernel(x_ref, gmix_ref, win_ref, glng_ref, glnb_ref, ws_ref, bst_ref, cw_ref, cb_ref,
                   clng_ref, clnb_ref, wout_ref, gxa_ref, wq_ref, kb_ref, vb_ref, wo_ref,
                   w1f_ref, w3f_ref, w2f_ref,
                   h_ref, conv_ref, gmv_ref, w1b_ref, w3b_ref, w2b_ref,
                   z_s, g_s, c_s, ab_s, o_s, y_prev, y_cur, in1, in3, in2, st1, st3, st2, sem_in, sem_out,
                   *, tiles_per_seq, n_cast_chunks):
    s = pl.program_id(0)
    n_steps = pl.num_programs(0)
    tb = x_ref.shape[0]
    n_chunks = tb // CHUNK

    srcs, dsts = (w1f_ref, w3f_ref, w2f_ref), (w1b_ref, w3b_ref, w2b_ref)
    ins, sts = (in1, in3, in2), (st1, st3, st2)
    slot = s % 2
    chunk_of = lambda step: jnp.minimum(step, n_cast_chunks - 1)

    @pl.when(s == 0)
    def _():
        for c in _cast_in_copies(srcs, ins, sem_in, chunk_of(s), slot):
            c.start()

    @pl.when(s + 1 < n_steps)
    def _():
        for c in _cast_in_copies(srcs, ins, sem_in, chunk_of(s + 1), 1 - slot):
            c.start()

    cast_in = _cast_in_copies(srcs, ins, sem_in, chunk_of(s), slot)
    cast_out = _cast_out_copies(dsts, sts, sem_out, chunk_of(s))
    for c in cast_in:
        c.wait()
    for ibuf, obuf in zip(ins, sts):
        obuf[...] = ibuf[slot].astype(BF16)
    for c in cast_out:
        c.start()

    @pl.when(s == 0)
    def _():
        y_prev[...] = jnp.zeros(y_prev.shape, F32)

    @pl.when(s % tiles_per_seq == 0)
    def _():
        g_s[0:HALO, :] = jnp.zeros((HALO, CV_WIDTH), F32)

    z_s[...] = _dot(_rms(x_ref[...], gmix_ref[...]).astype(BF16), win_ref[...])

    y = y_prev[...]
    qn = _rms(y, gxa_ref[...]).astype(BF16)
    q = (_dot(qn, wq_ref[...]) * (XA_HEAD_DIM ** -0.5)).astype(BF16)
    for h in range(XA_HEADS):
        cols = slice(h * XA_HEAD_DIM, (h + 1) * XA_HEAD_DIM)
        sc = _dot_nt(q[:, cols], kb_ref[:, cols])
        e = jnp.exp(sc - jnp.max(sc, axis=-1, keepdims=True))
        p = (e / jnp.sum(e, axis=-1, keepdims=True)).astype(BF16)
        o_s[:, cols] = _dot(p, vb_ref[:, cols]).astype(BF16)
    h_ref[...] = y + _dot(o_s[...], wo_ref[...])

    tri = (lax.broadcasted_iota(jnp.int32, (CHUNK, CHUNK), 0)
           >= lax.broadcasted_iota(jnp.int32, (CHUNK, CHUNK), 1))
    wm = [jnp.where(tri, ws_ref[h], 0.0).astype(BF16) for h in range(GM_HEADS)]

    for c in range(n_chunks):
        rows = slice(c * CHUNK, (c + 1) * CHUNK)
        for h in range(GM_HEADS):
            cu = slice(h * GM_HEAD_DIM, (h + 1) * GM_HEAD_DIM)
            cv = slice(GM_WIDTH + h * GM_HEAD_DIM, GM_WIDTH + (h + 1) * GM_HEAD_DIM)
            v = _ln(jax.nn.gelu(z_s[rows, cv]), glng_ref[:, cu], glnb_ref[:, cu])
            if c == n_chunks - 1:
                gmv_ref[:, cu] = v
            mixed = _dot(wm[h], v.astype(BF16)) + bst_ref[:, h:h + 1]
            ab_s[rows, cu] = (jax.nn.gelu(z_s[rows, cu]) * mixed).astype(BF16)
        ca = slice(2 * GM_WIDTH, 2 * GM_WIDTH + CV_WIDTH)
        cg = slice(2 * GM_WIDTH + CV_WIDTH, IN_COLS)
        g_s[HALO + c * CHUNK:HALO + (c + 1) * CHUNK, :] = z_s[rows, ca] * jax.nn.sigmoid(z_s[rows, cg])

    for c in range(n_chunks):
        rows = slice(c * CHUNK, (c + 1) * CHUNK)
        for cb in range(CV_WIDTH // LANES):
            lanes = slice(cb * LANES, (cb + 1) * LANES)
            c_s[rows, lanes] = _conv_block(g_s, cw_ref, c * CHUNK, lanes) + cb_ref[:, lanes]
        b = _ln(c_s[rows, :], clng_ref[...], clnb_ref[...])
        ab_s[rows, GM_WIDTH:] = (b * jax.nn.sigmoid(b)).astype(BF16)

    conv_ref[...] = g_s[tb:tb + HALO, :]
    g_s[0:HALO, :] = g_s[tb:tb + HALO, :]
    y_cur[...] = x_ref[...] + _dot(ab_s[...], wout_ref[...])
    y_prev[...] = y_cur[...]
    for c in cast_out:
        c.wait()


def _prompt_mix_attn(x, gmix, win, glng, glnb, ws, bst, cw, cb, clng, clnb, wout, gxa, wq, kb, vb, wo,
                     w1f, w3f, w2f):
    nb, seq, _ = x.shape
    tb = PROMPT_TILE
    tps = seq // tb
    last = nb * tps - 1
    n_cast = nb * tps
    r13, r2 = w1f.shape[0] // n_cast, w2f.shape[0] // n_cast
    assert w1f.shape == w3f.shape and w1f.shape[0] % n_cast == 0 and w2f.shape[0] % n_cast == 0
    assert r13 % 16 == 0 and r2 % 16 == 0
    hbm = pl.BlockSpec(memory_space=pl.ANY)
    tile = lambda s, lag: jnp.clip(s - lag, 0, last)
    full = lambda shape: pl.BlockSpec(shape, lambda s: (0,) * len(shape))
    rows = lambda lag: pl.BlockSpec((None, tb, D_MODEL), lambda s: (tile(s, lag) // tps, tile(s, lag) % tps, 0))
    kv_spec = pl.BlockSpec((None, N_MEM, D_MODEL), lambda s: (tile(s, 1) // tps, 0, 0))
    seq_out = lambda r, c: pl.BlockSpec((None, r, c), lambda s: (tile(s, 0) // tps, 0, 0))
    return pl.pallas_call(
        functools.partial(_prompt_kernel, tiles_per_seq=tps, n_cast_chunks=n_cast),
        grid=(nb * tps + 1,),
        in_specs=[rows(0), full((1, D_MODEL)), full((D_MODEL, IN_COLS)), full((1, GM_WIDTH)), full((1, GM_WIDTH)),
                  full((GM_HEADS, CHUNK, CHUNK)), full((CHUNK, GM_HEADS)), full((CONV_WIDTH, CV_WIDTH)),
                  full((1, CV_WIDTH)), full((1, CV_WIDTH)), full((1, CV_WIDTH)), full((D_MODEL, D_MODEL)),
                  full((1, D_MODEL)), full((D_MODEL, D_MODEL)), kv_spec, kv_spec, full((D_MODEL, D_MODEL)),
                  hbm, hbm, hbm],
        out_specs=[rows(1), seq_out(HALO, CV_WIDTH), seq_out(CHUNK, GM_WIDTH), hbm, hbm, hbm],
        out_shape=[jax.ShapeDtypeStruct((nb, seq, D_MODEL), F32),
                   jax.ShapeDtypeStruct((nb, HALO, CV_WIDTH), F32),
                   jax.ShapeDtypeStruct((nb, CHUNK, GM_WIDTH), F32),
                   jax.ShapeDtypeStruct(w1f.shape, BF16), jax.ShapeDtypeStruct(w3f.shape, BF16),
                   jax.ShapeDtypeStruct(w2f.shape, BF16)],
        scratch_shapes=[pltpu.VMEM((tb, IN_COLS), F32), pltpu.VMEM((HALO + tb, CV_WIDTH), F32),
                        pltpu.VMEM((tb, CV_WIDTH), F32), pltpu.VMEM((tb, D_MODEL), BF16),
                        pltpu.VMEM((tb, D_MODEL), BF16), pltpu.VMEM((tb, D_MODEL), F32),
                        pltpu.VMEM((tb, D_MODEL), F32),
                        pltpu.VMEM((2, r13, EXPERT_FF), F32), pltpu.VMEM((2, r13, EXPERT_FF), F32),
                        pltpu.VMEM((2, r2, D_MODEL), F32),
                        pltpu.VMEM((r13, EXPERT_FF), BF16), pltpu.VMEM((r13, EXPERT_FF), BF16),
                        pltpu.VMEM((r2, D_MODEL), BF16),
                        pltpu.SemaphoreType.DMA((2, 3)), pltpu.SemaphoreType.DMA((3,))],
        compiler_params=pltpu.CompilerParams(dimension_semantics=("arbitrary",), vmem_limit_bytes=VMEM_LIMIT),
        name="prompt_mix_attn",
    )(x, gmix, win, glng, glnb, ws, bst, cw, cb, clng, clnb, wout, gxa, wq, kb, vb, wo, w1f, w3f, w2f)


def _sample_mix_kernel(x_ref, gmix_ref, win_ref, glng_ref, glnb_ref, ws0_ref, bs0_ref, cache_ref, cw_ref, cb_ref,
                       clng_ref, clnb_ref, wout_ref, gxa_ref, wq_ref,
                       y_ref, conv_ref, v_ref, q_ref, ab_s):
    x = x_ref[...]
    z = _dot(_rms(x, gmix_ref[...]).astype(BF16), win_ref[...])
    for h in range(GM_HEADS):
        cu = slice(h * GM_HEAD_DIM, (h + 1) * GM_HEAD_DIM)
        cv = slice(GM_WIDTH + h * GM_HEAD_DIM, GM_WIDTH + (h + 1) * GM_HEAD_DIM)
        v = _ln(jax.nn.gelu(z[:, cv]), glng_ref[:, cu], glnb_ref[:, cu])
        v_ref[:, cu] = v
        ab_s[:, cu] = (jax.nn.gelu(z[:, cu]) * (v * ws0_ref[:, cu] + bs0_ref[:, cu])).astype(BF16)
    glu = z[:, 2 * GM_WIDTH:2 * GM_WIDTH + CV_WIDTH] * jax.nn.sigmoid(z[:, 2 * GM_WIDTH + CV_WIDTH:])
    conv_ref[0:CONV_WIDTH - 2] = cache_ref[1:CONV_WIDTH - 1]
    conv_ref[CONV_WIDTH - 2] = glu
    conv = glu * cw_ref[CONV_WIDTH - 1:CONV_WIDTH, :] + cb_ref[...]
    for k in range(CONV_WIDTH - 1):
        conv = conv + cache_ref[k] * cw_ref[k:k + 1, :]
    b = _ln(conv, clng_ref[...], clnb_ref[...])
    ab_s[:, GM_WIDTH:] = (b * jax.nn.sigmoid(b)).astype(BF16)
    y = x + _dot(ab_s[...], wout_ref[...])
    y_ref[...] = y
    q = _dot(_rms(y, gxa_ref[...]).astype(BF16), wq_ref[...]) * (XA_HEAD_DIM ** -0.5)
    for r in range(SUBLANES):
        q_ref[:, r, :] = q[:, _head_cols(r)]


def _head_cols(r):
    half, head = divmod(r, XA_HEADS)
    start = head * XA_HEAD_DIM + half * LANES
    return slice(start, start + LANES)


def _sample_mix(x, gmix, win, glng, glnb, ws0, bs0, cache_t, cw, cb, clng, clnb, wout, gxa, wq):
    ns = x.shape[0]
    return pl.pallas_call(
        _sample_mix_kernel,
        out_shape=[jax.ShapeDtypeStruct((ns, D_MODEL), F32), jax.ShapeDtypeStruct(cache_t.shape, F32),
                   jax.ShapeDtypeStruct((ns, GM_WIDTH), F32), jax.ShapeDtypeStruct((ns, SUBLANES, LANES), F32)],
        scratch_shapes=[pltpu.VMEM((ns, D_MODEL), BF16)],
        compiler_params=pltpu.CompilerParams(vmem_limit_bytes=VMEM_LIMIT),
        name="sample_mix",
    )(x, gmix, win, glng, glnb, ws0, bs0, cache_t, cw, cb, clng, clnb, wout, gxa, wq)


def _kv_copies(k_hbm, v_hbm, k_buf, v_buf, sem, block, slot):
    nb = k_buf.shape[1]
    seqs = pl.ds(block * nb, nb)
    return (pltpu.make_async_copy(k_hbm.at[seqs], k_buf.at[slot], sem.at[slot, 0]),
            pltpu.make_async_copy(v_hbm.at[seqs], v_buf.at[slot], sem.at[slot, 1]))


def _sample_attn_kernel(q_ref, k_hbm, v_hbm, o_ref, k_buf, v_buf, sem, *, first_block, n_steps):
    s = pl.program_id(0)
    ahead = KV_RING - 1

    @pl.when(s == 0)
    def _():
        for j in range(min(ahead, n_steps)):
            for c in _kv_copies(k_hbm, v_hbm, k_buf, v_buf, sem, first_block + j, j):
                c.start()

    @pl.when(s + ahead < n_steps)
    def _():
        for c in _kv_copies(k_hbm, v_hbm, k_buf, v_buf, sem, first_block + s + ahead, (s + ahead) % KV_RING):
            c.start()

    slot = s % KV_RING
    for c in _kv_copies(k_hbm, v_hbm, k_buf, v_buf, sem, first_block + s, slot):
        c.wait()

    ones = jnp.ones((LANES, LANES), BF16)
    rows = N_MEM * SUBLANES
    for i in range(q_ref.shape[0]):
        prod = (k_buf[slot, i] * q_ref[i][None]).reshape(rows, LANES).astype(BF16)
        part = _dot(prod, ones).reshape(N_MEM, SUBLANES, LANES)
        sc = part + pltpu.roll(part, XA_HEADS, axis=1)
        e = jnp.exp(sc - jnp.max(sc, axis=0, keepdims=True))
        p = e / jnp.sum(e, axis=0, keepdims=True)
        o_ref[i] = jnp.sum(p * v_buf[slot, i], axis=0)


def _split_heads(a):
    halves = XA_HEAD_DIM // LANES
    assert halves * XA_HEADS == SUBLANES
    lead = a.shape[:-2]
    a = a.reshape(*lead, XA_HEADS, halves, LANES)
    return jnp.swapaxes(a, -3, -2).reshape(*lead, SUBLANES, LANES)


def _sample_attn(q, k, v, first, count):
    nb = SAMPLE_ATTN_BLOCK
    assert first % nb == 0 and count % nb == 0
    off = first // nb
    n_steps = count // nb
    hbm = pl.BlockSpec(memory_space=pl.ANY)
    return pl.pallas_call(
        functools.partial(_sample_attn_kernel, first_block=off, n_steps=n_steps),
        grid=(n_steps,),
        in_specs=[pl.BlockSpec((nb, SUBLANES, LANES), lambda i: (i + off, 0, 0)), hbm, hbm],
        out_specs=pl.BlockSpec((nb, SUBLANES, LANES), lambda i: (i, 0, 0)),
        out_shape=jax.ShapeDtypeStruct((count, SUBLANES, LANES), F32),
        scratch_shapes=[pltpu.VMEM((KV_RING, nb, N_MEM, SUBLANES, LANES), F32),
                        pltpu.VMEM((KV_RING, nb, N_MEM, SUBLANES, LANES), F32),
                        pltpu.SemaphoreType.DMA((KV_RING, 2))],
        compiler_params=pltpu.CompilerParams(dimension_semantics=("arbitrary",), vmem_limit_bytes=VMEM_LIMIT),
        name="sample_attn",
    )(q, k, v)


def _sample_proj_kernel(y_ref, oa_ref, ob_ref, wo_ref, h_ref):
    halves = XA_HEAD_DIM // LANES
    first = 0
    for o_ref in (oa_ref, ob_ref):
        n = o_ref.shape[0]
        o = jnp.concatenate([o_ref[:, (j % halves) * XA_HEADS + j // halves, :] for j in range(SUBLANES)], axis=1)
        h_ref[first:first + n, :] = y_ref[first:first + n, :] + _dot(o.astype(BF16), wo_ref[...])
        first += n


def _sample_proj(y, oa, ob, wo):
    assert oa.shape[0] + ob.shape[0] == y.shape[0] and oa.shape[0] % SUBLANES == 0
    return pl.pallas_call(
        _sample_proj_kernel,
        out_shape=jax.ShapeDtypeStruct(y.shape, F32),
        compiler_params=pltpu.CompilerParams(vmem_limit_bytes=VMEM_LIMIT),
        name="sample_proj",
    )(y, oa, ob, wo)


def _moe_kernel(h_ref, gffn_ref, wrt_ref, rb_ref, w1_ref, w3_ref, w2_ref, gfin_ref,
                o_ref, xt_s, u_s, gidx_s, slot_s, comb_s, act_s, p2_s, y2_s, cnt_s):
    i = pl.program_id(0)
    g = pl.program_id(1)
    rt = h_ref.shape[0]
    sb = act_s.shape[0]

    @pl.when(jnp.logical_and(i == 0, g == 0))
    def _():
        u_s[...] = (lax.broadcasted_iota(jnp.int32, (rt, rt), 0)
                    < lax.broadcasted_iota(jnp.int32, (rt, rt), 1)).astype(F32).astype(BF16)

    @pl.when(g == 0)
    def _route():
        h = h_ref[...]
        o_ref[...] = h
        xt = _rms(h, gffn_ref[...]).astype(BF16)
        xt_s[...] = xt
        lt = _dot_nt(wrt_ref[...], xt) + rb_ref[...]
        gl = [lt[k:k + 1, :] for k in range(N_GROUPS)]
        gmax = jnp.maximum(jnp.maximum(gl[0], gl[1]), jnp.maximum(gl[2], gl[3]))
        gidx = jnp.where(gl[0] == gmax, 0, jnp.where(gl[1] == gmax, 1, jnp.where(gl[2] == gmax, 2, 3)))
        gidx = gidx.astype(jnp.int32)
        sumexp = (jnp.exp(gl[0] - gmax) + jnp.exp(gl[1] - gmax)) + (jnp.exp(gl[2] - gmax) + jnp.exp(gl[3] - gmax))
        p_g = 1.0 / sumexp
        esel = lt[SUBLANES + 3 * EXPERTS_PER_GROUP:SUBLANES + 4 * EXPERTS_PER_GROUP, :]
        for k in (2, 1, 0):
            esel = jnp.where(gidx == k, lt[SUBLANES + k * EXPERTS_PER_GROUP:SUBLANES + (k + 1) * EXPERTS_PER_GROUP, :],
                             esel)
        eidx = lax.broadcasted_iota(jnp.int32, (EXPERTS_PER_GROUP, rt), 0)
        m1 = jnp.max(esel, axis=0, keepdims=True)
        i1 = jnp.min(jnp.where(esel == m1, eidx, EXPERTS_PER_GROUP), axis=0, keepdims=True)
        rest = jnp.where(eidx == i1, -jnp.inf, esel)
        m2 = jnp.max(rest, axis=0, keepdims=True)
        i2 = jnp.min(jnp.where(rest == m2, eidx, EXPERTS_PER_GROUP), axis=0, keepdims=True)
        t2 = jnp.exp(m2 - m1)
        den = 1.0 + t2
        w_top1 = (1.0 / den) * p_g
        w_top2 = (t2 / den) * p_g
        within = jnp.where(eidx == i1, w_top1, 0.0) + jnp.where(eidx == i2, w_top2, 0.0)
        c_hi = within.astype(BF16).astype(F32)
        r1 = within - c_hi
        c_mid = r1.astype(BF16).astype(F32)
        c_lo = (r1 - c_mid).astype(BF16).astype(F32)
        comb_s[0:8, :] = c_hi
        comb_s[8:16, :] = c_mid
        comb_s[16:24, :] = c_lo
        comb_s[24:32, :] = jnp.zeros((8, rt), F32)
        onehot = (eidx == gidx).astype(F32)
        rank = _dot(onehot.astype(BF16), u_s[...])
        slot_s[...] = jnp.sum(onehot * rank, axis=0, keepdims=True).astype(jnp.int32)
        gidx_s[...] = gidx
        for k in range(N_GROUPS):
            cnt_s[k] = jnp.sum(onehot[k:k + 1, :]).astype(jnp.int32)

    n_blk = (cnt_s[g] + sb - 1) // sb

    def sub_block(j, half):
        half_rows = slice(half * sb, (half + 1) * sb)
        rows = lax.broadcasted_iota(jnp.int32, (sb, rt), 0) + j * sb
        hit = jnp.logical_and(rows == slot_s[...], gidx_s[...] == g)
        p = jnp.where(hit, 1.0, 0.0).astype(BF16)
        p2_s[half_rows, :] = p
        xc = _dot(p, xt_s[...]).astype(BF16)
        cexp = _dot_nt(p, comb_s[...].astype(BF16))
        cw = (cexp[:, 0:8] + cexp[:, 8:16]) + cexp[:, 16:24]
        for e in range(EXPERTS_PER_GROUP):
            h1 = _dot(xc, w1_ref[e])
            h3 = _dot(xc, w3_ref[e])
            a = (h1 * jax.nn.sigmoid(h1)) * h3 * cw[:, e:e + 1]
            act_s[:, e * EXPERT_FF:(e + 1) * EXPERT_FF] = a.astype(BF16)
        y2_s[half_rows, :] = _dot(act_s[...], w2_ref[...]).astype(BF16)

    def body(jj, carry):
        sub_block(2 * jj, 0)

        @pl.when(2 * jj + 1 < n_blk)
        def _():
            sub_block(2 * jj + 1, 1)

        @pl.when(2 * jj + 1 >= n_blk)
        def _():
            p2_s[sb:, :] = jnp.zeros((sb, rt), BF16)
            y2_s[sb:, :] = jnp.zeros((sb, D_MODEL), BF16)

        o_ref[...] += _dot_tn(p2_s[...], y2_s[...])
        return carry

    lax.fori_loop(0, (n_blk + 1) // 2, body, 0)

    @pl.when(g == N_GROUPS - 1)
    def _():
        o_ref[...] = _rms(o_ref[...], gfin_ref[...])


def _moe(h, gffn, wrt, rbias, w1, w3, w2, gfin, *, tile):
    tokens = h.shape[0]
    sb = min(MOE_SUB, tile)
    ff = EXPERTS_PER_GROUP * EXPERT_FF
    full = lambda shape: pl.BlockSpec(shape, lambda i, g: (0,) * len(shape))
    row_spec = pl.BlockSpec((tile, D_MODEL), lambda i, g: (i, 0))
    return pl.pallas_call(
        _moe_kernel,
        grid=(tokens // tile, N_GROUPS),
        in_specs=[row_spec, full((1, D_MODEL)), full((ROUTER_ROWS, D_MODEL)), full((ROUTER_ROWS, 1)),
                  pl.BlockSpec((None, EXPERTS_PER_GROUP, D_MODEL, EXPERT_FF), lambda i, g: (g, 0, 0, 0)),
                  pl.BlockSpec((None, EXPERTS_PER_GROUP, D_MODEL, EXPERT_FF), lambda i, g: (g, 0, 0, 0)),
                  pl.BlockSpec((None, ff, D_MODEL), lambda i, g: (g, 0, 0)),
                  full((1, D_MODEL))],
        out_specs=row_spec,
        out_shape=jax.ShapeDtypeStruct((tokens, D_MODEL), F32),
        scratch_shapes=[pltpu.VMEM((tile, D_MODEL), BF16), pltpu.VMEM((tile, tile), BF16),
                        pltpu.VMEM((1, tile), jnp.int32), pltpu.VMEM((1, tile), jnp.int32),
                        pltpu.VMEM((4 * SUBLANES, tile), F32), pltpu.VMEM((sb, ff), BF16),
                        pltpu.VMEM((2 * sb, tile), BF16), pltpu.VMEM((2 * sb, D_MODEL), BF16),
                        pltpu.SMEM((N_GROUPS,), jnp.int32)],
        compiler_params=pltpu.CompilerParams(dimension_semantics=("arbitrary", "arbitrary"),
                                             vmem_limit_bytes=VMEM_LIMIT),
        name="moe",
    )(h, gffn, wrt, rbias, w1, w3, w2, gfin)


HALF = D_MODEL // 2
SUB_PER_ROW = HALF // LANES
N_EXPERTS = N_GROUPS * EXPERTS_PER_GROUP


def _pack_bf16_pairs(x):
    bits = pltpu.bitcast(x.astype(BF16).astype(F32), jnp.uint32)
    return (bits[:, HALF:] & jnp.uint32(0xFFFF0000)) | (bits[:, :HALF] >> 16)


def _unpack_bf16_pairs(w):
    lo = pltpu.bitcast(w << 16, F32)
    hi = pltpu.bitcast(w & jnp.uint32(0xFFFF0000), F32)
    return lo, hi


def _route_kernel(h_ref, gffn_ref, wrt_ref, rb_ref, xp_ref, e_ref, w_ref, r_ref, cnt_ref, u_s, run_s):
    i = pl.program_id(0)
    rt = u_s.shape[0]

    @pl.when(i == 0)
    def _():
        u_s[...] = (lax.broadcasted_iota(jnp.int32, (rt, rt), 0)
                    < lax.broadcasted_iota(jnp.int32, (rt, rt), 1)).astype(F32).astype(BF16)
        run_s[...] = jnp.zeros(run_s.shape, F32)

    for t in range(h_ref.shape[0] // rt):
        tok = pl.ds(t * rt, rt)
        _route_tile(h_ref.at[tok, :], gffn_ref, wrt_ref, rb_ref, xp_ref.at[:, tok, :], e_ref.at[:, tok],
                    w_ref.at[:, tok], r_ref.at[:, tok], u_s, run_s)
    cnt_ref[...] = run_s[...].astype(jnp.int32)


def _route_tile(h_ref, gffn_ref, wrt_ref, rb_ref, xp_ref, e_ref, w_ref, r_ref, u_s, run_s):
    rt = h_ref.shape[0]
    xt = _rms(h_ref[...], gffn_ref[...])
    _store_planes(xp_ref, _pack_bf16_pairs(xt))
    lt = _dot_nt(wrt_ref[...], xt.astype(BF16)) + rb_ref[...]
    gl = [lt[k:k + 1, :] for k in range(N_GROUPS)]
    gmax = jnp.maximum(jnp.maximum(gl[0], gl[1]), jnp.maximum(gl[2], gl[3]))
    gidx = jnp.where(gl[0] == gmax, 0, jnp.where(gl[1] == gmax, 1, jnp.where(gl[2] == gmax, 2, 3)))
    gidx = gidx.astype(jnp.int32)
    sumexp = (jnp.exp(gl[0] - gmax) + jnp.exp(gl[1] - gmax)) + (jnp.exp(gl[2] - gmax) + jnp.exp(gl[3] - gmax))
    p_g = 1.0 / sumexp
    esel = lt[SUBLANES + 3 * EXPERTS_PER_GROUP:SUBLANES + 4 * EXPERTS_PER_GROUP, :]
    for k in (2, 1, 0):
        esel = jnp.where(gidx == k, lt[SUBLANES + k * EXPERTS_PER_GROUP:SUBLANES + (k + 1) * EXPERTS_PER_GROUP, :],
                         esel)
    eidx = lax.broadcasted_iota(jnp.int32, (EXPERTS_PER_GROUP, rt), 0)
    m1 = jnp.max(esel, axis=0, keepdims=True)
    i1 = jnp.min(jnp.where(esel == m1, eidx, EXPERTS_PER_GROUP), axis=0, keepdims=True)
    rest = jnp.where(eidx == i1, -jnp.inf, esel)
    m2 = jnp.max(rest, axis=0, keepdims=True)
    i2 = jnp.min(jnp.where(rest == m2, eidx, EXPERTS_PER_GROUP), axis=0, keepdims=True)
    t2 = jnp.exp(m2 - m1)
    den = 1.0 + t2
    w_ref[...] = jnp.zeros(w_ref.shape, F32)
    w_ref[0:1, :] = (1.0 / den) * p_g
    w_ref[1:2, :] = (t2 / den) * p_g
    e1 = gidx * EXPERTS_PER_GROUP + i1
    e2 = gidx * EXPERTS_PER_GROUP + i2
    e_ref[0:1, :] = e1
    e_ref[1:2, :] = e2
    xid = lax.broadcasted_iota(jnp.int32, (N_EXPERTS, rt), 0)
    oh1 = (xid == e1).astype(F32)
    oh2 = (xid == e2).astype(F32)
    both = oh1 + oh2
    before = _dot(both.astype(BF16), u_s[...]) + run_s[:, 0:1]
    r_ref[0:1, :] = jnp.sum(oh1 * before, axis=0, keepdims=True).astype(jnp.int32)
    r_ref[1:2, :] = jnp.sum(oh2 * before, axis=0, keepdims=True).astype(jnp.int32)
    run_s[...] = run_s[...] + jnp.sum(both, axis=1, keepdims=True)


def _route(h, gffn, wrt, rbias, *, tile):
    tokens = h.shape[0]
    step = tile * ROUTE_TILES_PER_STEP
    assert tokens % step == 0
    full = lambda shape: pl.BlockSpec(shape, lambda i: (0,) * len(shape))
    lanes = lambda rows: pl.BlockSpec((rows, step), lambda i: (0, i))
    return pl.pallas_call(
        _route_kernel,
        grid=(tokens // step,),
        in_specs=[pl.BlockSpec((step, D_MODEL), lambda i: (i, 0)), full((1, D_MODEL)),
                  full((ROUTER_ROWS, D_MODEL)), full((ROUTER_ROWS, 1))],
        out_specs=[pl.BlockSpec((SUB_PER_ROW, step, LANES), lambda i: (0, i, 0)), lanes(2), lanes(SUBLANES), lanes(2),
                   full((N_EXPERTS, LANES))],
        out_shape=[jax.ShapeDtypeStruct((SUB_PER_ROW, tokens, LANES), jnp.uint32),
                   jax.ShapeDtypeStruct((2, tokens), jnp.int32),
                   jax.ShapeDtypeStruct((SUBLANES, tokens), F32), jax.ShapeDtypeStruct((2, tokens), jnp.int32),
                   jax.ShapeDtypeStruct((N_EXPERTS, LANES), jnp.int32)],
        scratch_shapes=[pltpu.VMEM((tile, tile), BF16), pltpu.VMEM((N_EXPERTS, LANES), F32)],
        compiler_params=pltpu.CompilerParams(dimension_semantics=("arbitrary",), vmem_limit_bytes=VMEM_LIMIT),
        name="moe_route",
    )(h, gffn, wrt, rbias)


def _store_planes(ref, words):
    for j in range(SUB_PER_ROW):
        ref[j] = words[:, j * LANES:(j + 1) * LANES]


def _load_planes(ref):
    return jnp.concatenate([ref[j] for j in range(SUB_PER_ROW)], axis=1)


def _sub_row_index(row_of_token, n_rows):
    plane = jnp.arange(SUB_PER_ROW, dtype=jnp.int32)[:, None] * n_rows
    return (row_of_token[None, :] + plane).reshape(1, -1)


def _sc_mesh():
    return plsc.VectorSubcoreMesh(core_axis_name="core", subcore_axis_name="subcore")


def _sc_scatter_two(x_sub, idx_a, idx_b, n_out):
    n_in = x_sub.shape[0]

    @pl.kernel(out_type=jax.ShapeDtypeStruct((n_out, LANES), x_sub.dtype), mesh=_sc_mesh(), scratch_types=[])
    def scatter(x_hbm, a_hbm, b_hbm, o_hbm):
        def body(x_vmem, a_vmem, b_vmem):
            pltpu.sync_copy(x_vmem, o_hbm.at[a_vmem.at[0]])
            pltpu.sync_copy(x_vmem, o_hbm.at[b_vmem.at[0]])

        pltpu.emit_pipeline(
            body, grid=(n_in // SC_WINDOW,),
            in_specs=[pl.BlockSpec((SC_WINDOW, LANES), lambda i: (i, 0)),
                      pl.BlockSpec((1, SC_WINDOW), lambda i: (0, i)),
                      pl.BlockSpec((1, SC_WINDOW), lambda i: (0, i))],
            out_specs=[],
            core_axis_name=("core", "subcore"), dimension_semantics=(pltpu.PARALLEL,),
        )(x_hbm, a_hbm, b_hbm)

    return scatter(x_sub, idx_a, idx_b)


def _sc_gather(table, idx):
    n_out = idx.shape[1]

    @pl.kernel(out_type=jax.ShapeDtypeStruct((n_out, LANES), table.dtype), mesh=_sc_mesh())
    def gather(t_hbm, i_hbm, o_hbm):
        def body(i_vmem, o_vmem):
            pltpu.sync_copy(t_hbm.at[i_vmem.at[0]], o_vmem)

        pltpu.emit_pipeline(
            body, grid=(n_out // SC_WINDOW,),
            in_specs=[pl.BlockSpec((1, SC_WINDOW), lambda i: (0, i))],
            out_specs=[pl.BlockSpec((SC_WINDOW, LANES), lambda i: (i, 0))],
            core_axis_name=("core", "subcore"), dimension_semantics=(pltpu.PARALLEL,),
        )(i_hbm, o_hbm)

    return gather(table, idx)


def _expert_ffn_kernel(blk_e_ref, n_valid_ref, x_ref, *refs):
    del blk_e_ref
    y_ref = refs[-1]
    for i in range(FFN_PER_STEP):
        w1_ref, w3_ref, w2_ref = refs[3 * i:3 * i + 3]
        rows = slice(i * FFN_BLOCK, (i + 1) * FFN_BLOCK)

        @pl.when(pl.program_id(0) * FFN_PER_STEP + i < n_valid_ref[0])
        def _(w1_ref=w1_ref, w3_ref=w3_ref, w2_ref=w2_ref, rows=rows):
            words = jnp.concatenate([x_ref[j, rows, :] for j in range(SUB_PER_ROW)], axis=1)
            lo, hi = _unpack_bf16_pairs(words)
            xc = jnp.concatenate([lo.astype(BF16), hi.astype(BF16)], axis=1)
            h1 = _dot(xc, w1_ref[...])
            h3 = _dot(xc, w3_ref[...])
            act = ((h1 * jax.nn.sigmoid(h1)) * h3).astype(BF16)
            packed = _pack_bf16_pairs(_dot(act, w2_ref[...]))
            for j in range(SUB_PER_ROW):
                y_ref[j, rows, :] = packed[:, j * LANES:(j + 1) * LANES]


def _expert_ffn(xs, blk_e, n_valid, w1, w3, w2):
    rows = xs.shape[1]
    step_rows = FFN_BLOCK * FFN_PER_STEP
    assert rows % step_rows == 0
    live = lambda s, be, nv: jnp.minimum(s, (nv[0] + FFN_PER_STEP - 1) // FFN_PER_STEP - 1)
    x_spec = pl.BlockSpec((SUB_PER_ROW, step_rows, LANES), lambda s, be, nv: (0, live(s, be, nv), 0))
    w_specs = []
    for i in range(FFN_PER_STEP):
        expert = lambda s, be, nv, i=i: (be[s * FFN_PER_STEP + i], 0, 0)
        w_specs += [pl.BlockSpec((None, D_MODEL, EXPERT_FF), expert), pl.BlockSpec((None, D_MODEL, EXPERT_FF), expert),
                    pl.BlockSpec((None, EXPERT_FF, D_MODEL), expert)]
    grid_spec = pltpu.PrefetchScalarGridSpec(
        num_scalar_prefetch=2, grid=(rows // step_rows,), in_specs=[x_spec] + w_specs, out_specs=x_spec)
    return pl.pallas_call(
        _expert_ffn_kernel, grid_spec=grid_spec,
        out_shape=jax.ShapeDtypeStruct((SUB_PER_ROW, rows, LANES), jnp.uint32),
        compiler_params=pltpu.CompilerParams(dimension_semantics=("arbitrary",), vmem_limit_bytes=VMEM_LIMIT),
        name="moe_ffn",
    )(blk_e, n_valid, xs, *([w1, w3, w2] * FFN_PER_STEP))


def _combine_kernel(h_ref, ya_ref, yb_ref, wt_ref, gfin_ref, o_ref):
    a_lo, a_hi = _unpack_bf16_pairs(_load_planes(ya_ref))
    b_lo, b_hi = _unpack_bf16_pairs(_load_planes(yb_ref))
    wt = jnp.transpose(wt_ref[...])
    wa = wt[:, 0:1]
    wb = wt[:, 1:2]
    o_ref[:, :HALF] = h_ref[:, :HALF] + (wa * a_lo + wb * b_lo)
    o_ref[:, HALF:] = h_ref[:, HALF:] + (wa * a_hi + wb * b_hi)
    o_ref[...] = _rms(o_ref[...], gfin_ref[...])


def _combine(h, ya, yb, wt, gfin, *, tile):
    tokens = h.shape[0]
    return pl.pallas_call(
        _combine_kernel,
        grid=(tokens // tile,),
        in_specs=[pl.BlockSpec((tile, D_MODEL), lambda i: (i, 0)),
                  pl.BlockSpec((SUB_PER_ROW, tile, LANES), lambda i: (0, i, 0)),
                  pl.BlockSpec((SUB_PER_ROW, tile, LANES), lambda i: (0, i, 0)), pl.BlockSpec((SUBLANES, tile), lambda i: (0, i)),
                  pl.BlockSpec((1, D_MODEL), lambda i: (0, 0))],
        out_specs=pl.BlockSpec((tile, D_MODEL), lambda i: (i, 0)),
        out_shape=jax.ShapeDtypeStruct((tokens, D_MODEL), F32),
        compiler_params=pltpu.CompilerParams(dimension_semantics=("arbitrary",), vmem_limit_bytes=VMEM_LIMIT),
        name="moe_combine",
    )(h, ya, yb, wt, gfin)


def _moe_dispatch(h, gffn, wrt, rbias):
    tokens = h.shape[0]
    xp, e12, w12, r12, cnt = _route(h, gffn, wrt, rbias, tile=MOE_TILE)
    count = cnt[:, 0]
    padded = (count + FFN_BLOCK - 1) // FFN_BLOCK * FFN_BLOCK
    end = jnp.cumsum(padded)
    start = end - padded
    n_rows = 2 * tokens + N_EXPERTS * FFN_BLOCK
    n_blocks = n_rows // FFN_BLOCK
    n_valid = (end[-1:] // FFN_BLOCK).astype(jnp.int32)
    first_row = jnp.minimum(jnp.arange(n_blocks, dtype=jnp.int32), n_valid - 1) * FFN_BLOCK
    blk_e = jnp.sum((end[None, :] <= first_row[:, None]).astype(jnp.int32), axis=1)
    experts = jnp.arange(N_EXPERTS, dtype=jnp.int32)
    start_of = jnp.sum(jnp.where(e12[:, :, None] == experts, start.astype(jnp.int32), 0), axis=-1)
    rows_ab = start_of + r12
    idx_a, idx_b = _sub_row_index(rows_ab[0], n_rows), _sub_row_index(rows_ab[1], n_rows)
    xs = _sc_scatter_two(xp.reshape(-1, LANES), idx_a, idx_b, n_rows * SUB_PER_ROW)
    return xs.reshape(SUB_PER_ROW, n_rows, LANES), (blk_e, n_valid, idx_a, idx_b, w12)


def _moe_finish(h, xs, plan, w1, w3, w2, gfin):
    blk_e, n_valid, idx_a, idx_b, w12 = plan
    ys = _expert_ffn(xs, blk_e, n_valid, w1, w3, w2).reshape(-1, LANES)
    ya = _sc_gather(ys, idx_a).reshape(SUB_PER_ROW, -1, LANES)
    yb = _sc_gather(ys, idx_b).reshape(SUB_PER_ROW, -1, LANES)
    return _combine(h, ya, yb, w12, gfin, tile=MOE_TILE)


def kernel(x_prompt, x_sample, mem_prompt, cache_conv, cache_mem_k, cache_mem_v, norm_mix_g, w_in, gm_ln_g, gm_ln_b, gm_ws, gm_bs, conv_w, conv_b, cv_ln_g, cv_ln_b, w_out, norm_mem_g, norm_xa_g, xa_wq, xa_wk, xa_wv, xa_wo, norm_ffn_g, router_g, router_g_b, router_e, router_e_b, exp_w1, exp_w3, exp_w2, final_norm_g):
    depth = w_in.shape[0]
    assert depth == 1, "single-layer trunk"
    nb, seq, _ = x_prompt.shape
    ns = x_sample.shape[0]
    row = lambda a: a.reshape(1, -1)

    l = 0
    gmix, gxa, gffn, gmem = row(norm_mix_g[l]), row(norm_xa_g[l]), row(norm_ffn_g[l]), row(norm_mem_g[l])
    gfin = row(final_norm_g)
    wk, wv = xa_wk[l], xa_wv[l]
    glng, glnb = row(gm_ln_g[l]), row(gm_ln_b[l])
    cw, cb, clng, clnb = conv_w[l], row(conv_b[l]), row(cv_ln_g[l]), row(cv_ln_b[l])
    ws, bst = gm_ws[l], gm_bs[l].T
    ws0 = jnp.repeat(gm_ws[l][:, 0, 0], GM_HEAD_DIM).reshape(1, GM_WIDTH)
    bs0 = jnp.repeat(gm_bs[l][:, 0], GM_HEAD_DIM).reshape(1, GM_WIDTH)

    n_exp = N_GROUPS * EXPERTS_PER_GROUP
    pad_g = SUBLANES - N_GROUPS
    pad_t = ROUTER_ROWS - SUBLANES - n_exp
    wrt = jnp.concatenate([router_g[l].T, jnp.zeros((pad_g, D_MODEL), F32),
                           router_e[l].reshape(D_MODEL, n_exp).T, jnp.zeros((pad_t, D_MODEL), F32)], axis=0).astype(BF16)
    rbias = jnp.concatenate([router_g_b[l], jnp.zeros((pad_g,), F32), router_e_b[l].reshape(n_exp),
                             jnp.zeros((pad_t,), F32)]).reshape(ROUTER_ROWS, 1)

    win, wout, wq, wo, mk, mv, kb, vb = _memkv(mem_prompt.reshape(nb * N_MEM, D_MODEL), gmem, wk, wv,
                                                (w_in[l], w_out[l], xa_wq[l], xa_wo[l]))
    kb, vb = kb.reshape(nb, N_MEM, D_MODEL), vb.reshape(nb, N_MEM, D_MODEL)
    hp, conv_tail, gmv_p, w1, w3, w2 = _prompt_mix_attn(
        x_prompt, gmix, win, glng, glnb, ws, bst, cw, cb, clng, clnb, wout, gxa, wq, kb, vb, wo,
        exp_w1[l].reshape(-1, EXPERT_FF), exp_w3[l].reshape(-1, EXPERT_FF), exp_w2[l].reshape(-1, D_MODEL))
    h2d = hp.reshape(nb * seq, D_MODEL)
    xs, plan = _moe_dispatch(h2d, gffn, wrt, rbias)
    w1e, w3e, w2e = (w1.reshape(N_EXPERTS, D_MODEL, EXPERT_FF), w3.reshape(N_EXPERTS, D_MODEL, EXPERT_FF),
                     w2.reshape(N_EXPERTS, EXPERT_FF, D_MODEL))
    w1 = w1.reshape(N_GROUPS, EXPERTS_PER_GROUP, D_MODEL, EXPERT_FF)
    w3 = w3.reshape(N_GROUPS, EXPERTS_PER_GROUP, D_MODEL, EXPERT_FF)
    w2 = w2.reshape(N_GROUPS, EXPERTS_PER_GROUP * EXPERT_FF, D_MODEL)

    cache_t = jnp.transpose(cache_conv[l], (1, 0, 2))
    ys, conv_t, gmv_s, qh = _sample_mix(x_sample.reshape(ns, D_MODEL), gmix, win, glng, glnb, ws0, bs0,
                                       cache_t, cw, cb, clng, clnb, wout, gxa, wq)
    kh, vh = _split_heads(cache_mem_k[l]), _split_heads(cache_mem_v[l])
    half = ns // 2
    o_first = _sample_attn(qh, kh, vh, 0, half)
    xs, o_first = lax.optimization_barrier((xs, o_first))
    y_prompt = _moe_finish(h2d, xs, plan, w1e, w3e, w2e, gfin)
    o_second = _sample_attn(qh, kh, vh, half, ns - half)
    hs = _sample_proj(ys, o_first, o_second, wo)
    y_sample = _moe(hs, gffn, wrt, rbias, w1, w3, w2, gfin, tile=ns)

    conv_prompt = conv_tail[:, HALO - (CONV_WIDTH - 1):, :][None]
    conv_sample = jnp.transpose(conv_t, (1, 0, 2))[None]
    return (y_prompt.reshape(nb, seq, D_MODEL), y_sample.reshape(ns, 1, D_MODEL), conv_prompt, conv_sample,
            gmv_p[None], gmv_s.reshape(1, ns, 1, GM_WIDTH),
            mk.reshape(1, nb, N_MEM, XA_HEADS, XA_HEAD_DIM), mv.reshape(1, nb, N_MEM, XA_HEADS, XA_HEAD_DIM))
```
